```python
import jax, jax.numpy as jnp
from jax import lax
import numpy as np

D_MODEL = 1024
BATCH = 16
SEQ = 256
DEPTH = 2
DEC_BATCH = 4
DEC_SEQ = 4096
PAST_LEN = 512

GRID_W = 64
HEAD_DIM = 64
POOL_WIDTH = 512
POOL_WINDOWS = (2, 4, 8, 16)
POOL_GROUPS = 4
POOL_GROUP_WIDTH = POOL_WIDTH // POOL_GROUPS
CONV_WIDTH = 512
CONV_K = 3
EVEN_IN = POOL_WIDTH + 3 * CONV_WIDTH
EVEN_OUT = POOL_WIDTH + CONV_WIDTH
NA_HEADS = 8
NA_ROWS = 8
NA_COLS = 16
SW_HEADS = 8
SW_KV_HEADS = 2
SW_GROUP = SW_HEADS // SW_KV_HEADS
SW_WINDOW = 128
ATTN_BLOCK = 128
ODD_IN = 3 * NA_HEADS * HEAD_DIM + (SW_HEADS + 2 * SW_KV_HEADS) * HEAD_DIM
ODD_OUT = (NA_HEADS + SW_HEADS) * HEAD_DIM
ROPE_BASE = 10000.0
N_EXPERTS = 16
N_EXPERT_GROUPS = 4
EXPERTS_PER_GROUP = N_EXPERTS // N_EXPERT_GROUPS
TOP_K = 2
D_EXPERT = 512
N_EVEN = (DEPTH + 1) // 2
N_ODD = DEPTH // 2
RMS_EPS = 1e-6
NEG_INF = -1e30

kernel_name = 'hybrid_flow_trunk_step'


def _rmsnorm(x, g):
    x32 = x.astype(jnp.float32)
    y = x32 * lax.rsqrt(jnp.mean(x32 * x32, axis=-1, keepdims=True) + RMS_EPS)
    return (y * g.astype(jnp.float32)).astype(x.dtype)


def _modulation(cvec, mod_w, mod_b):
    m = jax.nn.silu(cvec) @ mod_w + mod_b
    return jnp.split(m[..., None, :], 6, axis=-1)


def _rope_2d(x):
    T = x.shape[1]
    t = jnp.arange(T)
    quarter = HEAD_DIM // 4
    inv = 1.0 / (ROPE_BASE ** (jnp.arange(quarter, dtype=jnp.float32) / quarter))

    def cos_sin(pos):
        ang = pos.astype(jnp.float32)[:, None] * inv[None, :]
        ang = jnp.concatenate([ang, ang], axis=-1)
        return jnp.cos(ang), jnp.sin(ang)

    cr, sr = cos_sin(t // GRID_W)
    cc, sc = cos_sin(t % GRID_W)
    shape = (1, T) + (1,) * (x.ndim - 3) + (HEAD_DIM,)
    cos = jnp.concatenate([cr, cc], axis=-1).reshape(shape)
    sin = jnp.concatenate([sr, sc], axis=-1).reshape(shape)

    def rot_half(u):
        u1, u2 = jnp.split(u, 2, axis=-1)
        return jnp.concatenate([-u2, u1], axis=-1)

    x32 = x.astype(jnp.float32)
    xr, xc = jnp.split(x32, 2, axis=-1)
    rot = jnp.concatenate([rot_half(xr), rot_half(xc)], axis=-1)
    return (x32 * cos + rot * sin).astype(x.dtype)


def _pool_mixer(p, pool_w, pool_scale):
    B, T, _ = p.shape
    t = jnp.arange(T)
    p32 = p.astype(jnp.float32)
    csum = jnp.concatenate([jnp.zeros((B, 1, POOL_WIDTH), jnp.float32), jnp.cumsum(p32, axis=1)], axis=1)
    groups = []
    for g, w in enumerate(POOL_WINDOWS):
        lo = jnp.maximum(t - w // 2, 0)
        hi = jnp.minimum(t + (w - 1 - w // 2), T - 1)
        sl = slice(g * POOL_GROUP_WIDTH, (g + 1) * POOL_GROUP_WIDTH)
        cs = csum[..., sl]
        cnt = (hi - lo + 1).astype(jnp.float32)[None, :, None]
        mean = (jnp.take(cs, hi + 1, axis=1) - jnp.take(cs, lo, axis=1)) / cnt
        groups.append(mean - p32[..., sl])
    d = jnp.stack(groups, axis=2).astype(p.dtype)
    a = jnp.einsum('btgc,gcd->btgd', d, pool_w).reshape(B, T, POOL_WIDTH)
    return a * pool_scale


def _short_conv_mixer(xin, gate_b, gate_c, conv_w):
    u = gate_c * xin
    T = u.shape[1]
    half = CONV_K // 2
    up = jnp.pad(u, ((0, 0), (half, half), (0, 0)))
    conv = sum(up[:, k:k + T] * conv_w[k] for k in range(CONV_K))
    return gate_b * conv


def _even_mixer(h, w_in, pool_w, pool_scale, conv_w, w_out):
    z = h @ w_in
    p, xin, gb, gc = jnp.split(z, [POOL_WIDTH, POOL_WIDTH + CONV_WIDTH, POOL_WIDTH + 2 * CONV_WIDTH], axis=-1)
    a = _pool_mixer(p, pool_w, pool_scale)
    b = _short_conv_mixer(xin, gb, gc, conv_w)
    return jnp.concatenate([a, b], axis=-1) @ w_out


def _odd_projections(h, w_in):
    B, T, _ = h.shape
    z = h @ w_in
    na = NA_HEADS * HEAD_DIM
    sq = SW_HEADS * HEAD_DIM
    skv = SW_KV_HEADS * HEAD_DIM
    cuts = [na, 2 * na, 3 * na, 3 * na + sq, 3 * na + sq + skv]
    qc, kc, vc, qd, kd, vd = jnp.split(z, cuts, axis=-1)
    qc = qc.reshape(B, T, NA_HEADS, 1, HEAD_DIM)
    kc = kc.reshape(B, T, NA_HEADS, HEAD_DIM)
    vc = vc.reshape(B, T, NA_HEADS, HEAD_DIM)
    qd = qd.reshape(B, T, SW_KV_HEADS, SW_GROUP, HEAD_DIM)
    kd = kd.reshape(B, T, SW_KV_HEADS, HEAD_DIM)
    vd = vd.reshape(B, T, SW_KV_HEADS, HEAD_DIM)
    return qc, kc, vc, qd, kd, vd


def _attend(q, segments, sink):
    scale = HEAD_DIM ** -0.5
    logits = []
    for k, v, bias in segments:
        s = jnp.einsum('bqgrd,bkgd->bgrqk', q, k).astype(jnp.float32) * scale
        if bias is not None:
            s = s + bias
        logits.append(s)
    if sink is not None:
        B, Q, G, R, _ = q.shape
        logits.append(jnp.broadcast_to(sink.astype(jnp.float32).reshape(1, G, R, 1, 1), (B, G, R, Q, 1)))
    p = jax.nn.softmax(jnp.concatenate(logits, axis=-1), axis=-1)
    out = 0
    start = 0
    for k, v, _ in segments:
        n = k.shape[1]
        out = out + jnp.einsum('bgrqk,bkgd->bqgrd', p[..., start:start + n].astype(v.dtype), v)
        start += n
    return out


def _context_attention(q, k, v, sink):
    B, T, G, R, dh = q.shape
    nb = T // ATTN_BLOCK
    qb = jnp.moveaxis(q.reshape(B, nb, ATTN_BLOCK, G, R, dh), 1, 0)
    out = lax.map(lambda qi: _attend(qi, [(k, v, None)], sink), qb)
    return jnp.moveaxis(out, 0, 1).reshape(B, T, G * R * dh)


def _neighbourhood_attention(q, k, v, k_ctx, v_ctx, rpb):
    B, T, H, _, dh = q.shape
    rows = T // GRID_W
    kr = min(NA_ROWS, rows)
    qg = q.reshape(B, rows, GRID_W, H, 1, dh)
    kg = k.reshape(B, rows, GRID_W, H, dh)
    vg = v.reshape(B, rows, GRID_W, H, dh)
    col = jnp.arange(GRID_W)
    cstart = jnp.clip(col - NA_COLS // 2, 0, GRID_W - NA_COLS)
    col_ok = (col[None, :] >= cstart[:, None]) & (col[None, :] < cstart[:, None] + NA_COLS)
    dc_idx = jnp.clip(col[None, :] - col[:, None] + NA_COLS - 1, 0, 2 * NA_COLS - 2)
    rpb32 = rpb.astype(jnp.float32)

    def row_block(r):
        rs = jnp.clip(r - kr // 2, 0, rows - kr)
        qr = lax.dynamic_index_in_dim(qg, r, axis=1, keepdims=False)
        kb = lax.dynamic_slice_in_dim(kg, rs, kr, axis=1).reshape(B, kr * GRID_W, H, dh)
        vb = lax.dynamic_slice_in_dim(vg, rs, kr, axis=1).reshape(B, kr * GRID_W, H, dh)
        dr_idx = rs + jnp.arange(kr) - r + NA_ROWS - 1
        bias = rpb32[:, dr_idx[None, :, None], dc_idx[:, None, :]]
        bias = jnp.where(col_ok[:, None, :], bias, NEG_INF).reshape(H, GRID_W, kr * GRID_W)
        return _attend(qr, [(kb, vb, bias[None, :, None]), (k_ctx, v_ctx, None)], None)

    out = lax.map(row_block, jnp.arange(rows))
    return jnp.moveaxis(out, 0, 1).reshape(B, T, H * dh)


def _window_attention(q, k, v, k_ctx, v_ctx, sink):
    B, T, G, R, dh = q.shape
    nb = T // ATTN_BLOCK
    pad = ((0, 0), (ATTN_BLOCK, ATTN_BLOCK), (0, 0), (0, 0))
    kp = jnp.pad(k, pad)
    vp = jnp.pad(v, pad)
    offs_q = jnp.arange(ATTN_BLOCK)
    offs_k = jnp.arange(3 * ATTN_BLOCK) - ATTN_BLOCK

    def block(j):
        s0 = j * ATTN_BLOCK
        qb = lax.dynamic_slice_in_dim(q, s0, ATTN_BLOCK, axis=1)
        kb = lax.dynamic_slice_in_dim(kp, s0, 3 * ATTN_BLOCK, axis=1)
        vb = lax.dynamic_slice_in_dim(vp, s0, 3 * ATTN_BLOCK, axis=1)
        kpos = s0 + offs_k
        ok = (jnp.abs(offs_q[:, None] - offs_k[None, :]) <= SW_WINDOW) & (kpos >= 0)[None, :] & (kpos < T)[None, :]
        bias = jnp.where(ok, 0.0, NEG_INF).astype(jnp.float32)[None, None, None]
        return _attend(qb, [(kb, vb, bias), (k_ctx, v_ctx, None)], sink)

    out = lax.map(block, jnp.arange(nb))
    return jnp.moveaxis(out, 0, 1).reshape(B, T, G * R * dh)


def _moe(x, router_w, router_b, w_gate, w_up, w_down):
    shp = x.shape
    xt = x.reshape(-1, D_MODEL)
    n = xt.shape[0]
    probs = jax.nn.softmax((xt @ router_w).astype(jnp.float32), axis=-1)
    sel = probs + router_b.astype(jnp.float32)
    grp = lax.top_k(sel.reshape(n, N_EXPERT_GROUPS, EXPERTS_PER_GROUP), TOP_K)[0].sum(-1)
    best = jnp.argmax(grp, axis=-1)
    in_grp = (jnp.arange(N_EXPERTS) // EXPERTS_PER_GROUP)[None, :] == best[:, None]
    _, idx = lax.top_k(jnp.where(in_grp, sel, -jnp.inf), TOP_K)
    w = jnp.take_along_axis(probs, idx, axis=-1)
    w = w / jnp.sum(w, axis=-1, keepdims=True)
    gates = jnp.einsum('nk,nke->ne', w, jax.nn.one_hot(idx, N_EXPERTS, dtype=jnp.float32)).astype(x.dtype)
    y = jnp.zeros_like(xt)
    for e in range(N_EXPERTS):
        h = jax.nn.silu(xt @ w_gate[e]) * (xt @ w_up[e])
        y = y + gates[:, e:e + 1] * (h @ w_down[e])
    return y.reshape(shp)


def setup_inputs(seed: int = 0) -> dict:
    key = jax.random.key(seed)
    ks = jax.random.split(key, 32)
    f32 = jnp.float32

    def nrm(k, shape, s):
        return jax.random.normal(k, shape, f32) * s

    d = D_MODEL
    return {
        'x_prompt': nrm(ks[0], (BATCH, SEQ, d), 1.0),
        'x_sample': nrm(ks[1], (DEC_BATCH, DEC_SEQ, d), 1.0),
        'cache_na_k': nrm(ks[2], (DEC_BATCH, N_ODD, PAST_LEN, NA_HEADS, HEAD_DIM), 1.0),
        'cache_na_v': nrm(ks[3], (DEC_BATCH, N_ODD, PAST_LEN, NA_HEADS, HEAD_DIM), 1.0),
        'cache_sw_k': nrm(ks[4], (DEC_BATCH, N_ODD, PAST_LEN, SW_KV_HEADS, HEAD_DIM), 1.0),
        'cache_sw_v': nrm(ks[5], (DEC_BATCH, N_ODD, PAST_LEN, SW_KV_HEADS, HEAD_DIM), 1.0),
        'c': nrm(ks[6], (DEC_BATCH, d), 1.0),
        'c_ctx': nrm(ks[7], (d,), 1.0),
        'mod_w': nrm(ks[8], (DEPTH, d, 6 * d), 0.5 * d ** -0.5),
        'mod_b': nrm(ks[9], (DEPTH, 6 * d), 0.02),
        'norm_mix_g': 1.0 + nrm(ks[10], (DEPTH, d), 0.05),
        'norm_ffn_g': 1.0 + nrm(ks[11], (DEPTH, d), 0.05),
        'ev_w_in': nrm(ks[12], (N_EVEN, d, EVEN_IN), d ** -0.5),
        'ev_pool_w': nrm(ks[13], (N_EVEN, POOL_GROUPS, POOL_GROUP_WIDTH, POOL_GROUP_WIDTH), POOL_GROUP_WIDTH ** -0.5),
        'ev_pool_scale': 1.0 + nrm(ks[14], (N_EVEN, POOL_WIDTH), 0.1),
        'ev_conv_w': nrm(ks[15], (N_EVEN, CONV_K, CONV_WIDTH), CONV_K ** -0.5),
        'ev_w_out': nrm(ks[16], (N_EVEN, EVEN_OUT, d), EVEN_OUT ** -0.5),
        'od_w_in': nrm(ks[17], (N_ODD, d, ODD_IN), d ** -0.5),
        'od_rpb': nrm(ks[18], (N_ODD, NA_HEADS, 2 * NA_ROWS - 1, 2 * NA_COLS - 1), 0.1),
        'od_sink': nrm(ks[19], (N_ODD, SW_HEADS), 0.5),
        'od_w_out': nrm(ks[20], (N_ODD, ODD_OUT, d), ODD_OUT ** -0.5),
        'router_w': nrm(ks[21], (d, N_EXPERTS), d ** -0.5),
        'router_b': nrm(ks[22], (N_EXPERTS,), 0.01),
        'moe_w_gate': nrm(ks[23], (DEPTH, N_EXPERTS, d, D_EXPERT), d ** -0.5),
        'moe_w_up': nrm(ks[24], (DEPTH, N_EXPERTS, d, D_EXPERT), d ** -0.5),
        'moe_w_down': nrm(ks[25], (DEPTH, N_EXPERTS, D_EXPERT, d), D_EXPERT ** -0.5),
        'final_norm_g': 1.0 + nrm(ks[26], (d,), 0.05),
    }


def reference(x_prompt, x_sample, cache_na_k, cache_na_v, cache_sw_k, cache_sw_v, c, c_ctx,
              mod_w, mod_b, norm_mix_g, norm_ffn_g,
              ev_w_in, ev_pool_w, ev_pool_scale, ev_conv_w, ev_w_out,
              od_w_in, od_rpb, od_sink, od_w_out,
              router_w, router_b, moe_w_gate, moe_w_up, moe_w_down, final_norm_g):
    xp, xs = x_prompt, x_sample
    Bp, Tp, _ = xp.shape
    na_k, na_v, sw_k, sw_v = [], [], [], []
    for l in range(DEPTH):
        i = l // 2
        sh1p, sc1p, g1p, sh2p, sc2p, g2p = _modulation(c_ctx, mod_w[l], mod_b[l])
        sh1s, sc1s, g1s, sh2s, sc2s, g2s = _modulation(c, mod_w[l], mod_b[l])
        hp = _rmsnorm(xp, norm_mix_g[l]) * (1 + sc1p) + sh1p
        hs = _rmsnorm(xs, norm_mix_g[l]) * (1 + sc1s) + sh1s
        if l % 2 == 0:
            yp = _even_mixer(hp, ev_w_in[i], ev_pool_w[i], ev_pool_scale[i], ev_conv_w[i], ev_w_out[i])
            ys = _even_mixer(hs, ev_w_in[i], ev_pool_w[i], ev_pool_scale[i], ev_conv_w[i], ev_w_out[i])
        else:
            qc, kc, vc, qd, kd, vd = _odd_projections(hp, od_w_in[i])
            oc = _context_attention(qc, kc, vc, None)
            od = _context_attention(qd, kd, vd, od_sink[i])
            yp = jnp.concatenate([oc, od], axis=-1) @ od_w_out[i]
            na_k.append(kc)
            na_v.append(vc)
            sw_k.append(kd)
            sw_v.append(vd)
            qc, kc, vc, qd, kd, vd = _odd_projections(hs, od_w_in[i])
            oc = _neighbourhood_attention(qc, kc, vc, cache_na_k[:, i], cache_na_v[:, i], od_rpb[i])
            od = _window_attention(_rope_2d(qd), _rope_2d(kd), vd, cache_sw_k[:, i], cache_sw_v[:, i], od_sink[i])
            ys = jnp.concatenate([oc, od], axis=-1) @ od_w_out[i]
        xp = xp + g1p * yp
        xs = xs + g1s * ys
        hp = _rmsnorm(xp, norm_ffn_g[l]) * (1 + sc2p) + sh2p
        hs = _rmsnorm(xs, norm_ffn_g[l]) * (1 + sc2s) + sh2s
        xp = xp + g2p * _moe(hp, router_w, router_b, moe_w_gate[l], moe_w_up[l], moe_w_down[l])
        xs = xs + g2s * _moe(hs, router_w, router_b, moe_w_gate[l], moe_w_up[l], moe_w_down[l])
    y_prompt = _rmsnorm(xp, final_norm_g)
    y_sample = _rmsnorm(xs, final_norm_g)
    new_na_k = jnp.stack(na_k, axis=1)
    new_na_v = jnp.stack(na_v, axis=1)
    new_sw_k = jnp.stack(sw_k, axis=1)
    new_sw_v = jnp.stack(sw_v, axis=1)
    return (y_prompt, y_sample, new_na_k, new_na_v, new_sw_k, new_sw_v)
```

```python
import functools

import jax
import jax.numpy as jnp
from jax import lax
from jax.experimental import pallas as pl
from jax.experimental.pallas import tpu as pltpu

D = 1024
BATCH = 16
SEQ = 256
DEC_BATCH = 4
DEC_SEQ = 4096
PAST = 512
GRID_W = 64
ROWS = DEC_SEQ // GRID_W
DH = 64
POOL_W = 512
POOL_WINDOWS = (2, 4, 8, 16)
POOL_GW = 128
CONV_W = 512
EVEN_IN = POOL_W + 3 * CONV_W
NA_H = 8
NA_ROWS = 8
NA_COLS = 16
SW_H = 8
SW_KV = 2
SW_G = SW_H // SW_KV
SW_WIN = 128
ABLK = 128
NA_W = NA_H * DH
SWQ_W = SW_H * DH
SWKV_W = SW_KV * DH
ODD_IN = 3 * NA_W + SWQ_W + 2 * SWKV_W
N_EXP = 16
N_GRP = 4
EPG = 4
D_EXP = 512
EPS = 1e-6
NEG = -1e30
ROPE_BASE = 10000.0

N_P = BATCH * SEQ
N_S = DEC_BATCH * DEC_SEQ
N_TOK = N_P + N_S
MOD_ROWS = 8

TM = 256
NPT = N_P // TM
TPS = DEC_SEQ // TM
NT = N_TOK // TM
HALO = 8
TM_MOE = 1024

F32 = jnp.float32
BF16 = jnp.bfloat16
VMEM_LIMIT = 56 * 1024 * 1024


def _cparams(*sem):
    return pltpu.CompilerParams(dimension_semantics=sem, vmem_limit_bytes=VMEM_LIMIT)


def _mod_row(i):
    return jnp.where(i < NPT, 0, 1 + (i - NPT) // TPS)


def _rms_mod(x, g, shift, scale):
    ms = jnp.mean(x * x, axis=-1, keepdims=True)
    y = x * lax.rsqrt(ms + EPS) * g
    return y * (1.0 + scale) + shift


def _dot(a, b):
    return jnp.dot(a, b, preferred_element_type=F32)


def _dot_nt(a, b):
    return lax.dot_general(a, b, (((1,), (1,)), ((), ())), preferred_element_type=F32)


def _mod_kernel(cv_ref, w_ref, b_ref, o_ref):
    cv = cv_ref[...]
    a = cv / (1.0 + jnp.exp(-cv))
    o_ref[...] = jnp.dot(a, w_ref[...], preferred_element_type=F32,
                         precision=lax.Precision.HIGHEST) + b_ref[...]


def _modulation(cvec, mod_w, mod_b):
    depth = mod_w.shape[0]
    return pl.pallas_call(
        _mod_kernel,
        grid=(depth, 6),
        in_specs=[
            pl.BlockSpec((MOD_ROWS, D), lambda l, j: (0, 0)),
            pl.BlockSpec((None, D, D), lambda l, j: (l, 0, j)),
            pl.BlockSpec((None, None, 1, D), lambda l, j: (l, j, 0, 0)),
        ],
        out_specs=pl.BlockSpec((None, None, MOD_ROWS, D), lambda l, j: (l, j, 0, 0)),
        out_shape=jax.ShapeDtypeStruct((depth, 6, MOD_ROWS, D), F32),
        compiler_params=_cparams("arbitrary", "arbitrary"),
        name="modulation",
    )(cvec, mod_w, mod_b.reshape(depth, 6, 1, D))


def _dual_specs(tm, width, npt):
    return [
        pl.BlockSpec((tm, width), lambda i: (jnp.minimum(i, npt - 1), 0)),
        pl.BlockSpec((tm, width), lambda i: (jnp.maximum(i - npt, 0), 0)),
    ]


def _even_in_kernel(xp_ref, xs_ref, mod_ref, g_ref, w_ref, z_ref):
    i = pl.program_id(0)
    r = _mod_row(i)
    x = jnp.where(i < NPT, xp_ref[...], xs_ref[...])
    h = _rms_mod(x, g_ref[...], mod_ref[0, pl.ds(r, 1), :], mod_ref[1, pl.ds(r, 1), :])
    z_ref[...] = _dot(h.astype(BF16), w_ref[...])


def _even_in(xp, xs, mod_l, g, w):
    return pl.pallas_call(
        _even_in_kernel,
        grid=(NT,),
        in_specs=_dual_specs(TM, D, NPT) + [
            pl.BlockSpec((6, MOD_ROWS, D), lambda i: (0, 0, 0)),
            pl.BlockSpec((1, D), lambda i: (0, 0)),
            pl.BlockSpec((D, EVEN_IN), lambda i: (0, 0)),
        ],
        out_specs=pl.BlockSpec((TM, EVEN_IN), lambda i: (i, 0)),
        out_shape=jax.ShapeDtypeStruct((N_TOK, EVEN_IN), F32),
        compiler_params=_cparams("parallel"),
        name="even_in",
    )(xp, xs, mod_l, g, w)


def _even_mix_kernel(zc_ref, zp_ref, zn_ref, xp_ref, xs_ref, mod_ref, pw_ref, ps_ref, cw_ref,
                     wo_ref, o_ref, pext, uext):
    i = pl.program_id(0)
    r = _mod_row(i)
    is_p = i < NPT
    t0 = jnp.where(is_p, 0, ((i - NPT) % TPS) * TM)
    seq_len = jnp.where(is_p, SEQ, DEC_SEQ)
    first = t0 == 0
    last = t0 + TM == seq_len

    z = zc_ref[...]
    zp = jnp.where(first, 0.0, zp_ref[...])
    zn = jnp.where(last, 0.0, zn_ref[...])

    def split(v):
        return (v[:, :POOL_W], v[:, POOL_W:POOL_W + CONV_W],
                v[:, POOL_W + CONV_W:POOL_W + 2 * CONV_W], v[:, POOL_W + 2 * CONV_W:])

    p, xin, gb, gc = split(z)
    pp, xinp, _, gcp = split(zp)
    pn, xinn, _, gcn = split(zn)
    pext[0:HALO, :] = pp
    pext[HALO:HALO + TM, :] = p
    pext[HALO + TM:, :] = pn
    uext[0:HALO, :] = gcp * xinp
    uext[HALO:HALO + TM, :] = gc * xin
    uext[HALO + TM:, :] = gcn * xinn

    t = t0 + lax.broadcasted_iota(jnp.int32, (TM, 1), 0)
    y = jnp.zeros((TM, D), F32)
    for g, w in enumerate(POOL_WINDOWS):
        cols = slice(g * POOL_GW, (g + 1) * POOL_GW)
        acc = jnp.zeros((TM, POOL_GW), F32)
        for k in range(-(w // 2), w - (w // 2)):
            acc = acc + pext[pl.ds(HALO + k, TM), cols]
        lo = jnp.maximum(t - w // 2, 0)
        hi = jnp.minimum(t + (w - 1 - w // 2), seq_len - 1)
        cnt = (hi - lo + 1).astype(F32)
        d = acc / cnt - p[:, cols]
        a = _dot(d.astype(BF16), pw_ref[g]) * ps_ref[:, cols]
        y = y + _dot(a.astype(BF16), wo_ref[cols, :])
    conv = (uext[pl.ds(HALO - 1, TM), :] * cw_ref[0:1, :] + uext[pl.ds(HALO, TM), :] * cw_ref[1:2, :]
            + uext[pl.ds(HALO + 1, TM), :] * cw_ref[2:3, :])
    b = gb * conv
    y = y + _dot(b.astype(BF16), wo_ref[POOL_W:, :])
    x = jnp.where(is_p, xp_ref[...], xs_ref[...])
    o_ref[...] = x + mod_ref[2, pl.ds(r, 1), :] * y


def _even_mix(z, xp, xs, mod_l, pool_w, pool_scale, conv_w, w_out):
    hb = TM // HALO
    nhb = N_TOK // HALO
    return pl.pallas_call(
        _even_mix_kernel,
        grid=(NT,),
        in_specs=[
            pl.BlockSpec((TM, EVEN_IN), lambda i: (i, 0)),
            pl.BlockSpec((HALO, EVEN_IN), lambda i: (jnp.maximum(i * hb - 1, 0), 0)),
            pl.BlockSpec((HALO, EVEN_IN), lambda i: (jnp.minimum((i + 1) * hb, nhb - 1), 0)),
        ] + _dual_specs(TM, D, NPT) + [
            pl.BlockSpec((6, MOD_ROWS, D), lambda i: (0, 0, 0)),
            pl.BlockSpec((4, POOL_GW, POOL_GW), lambda i: (0, 0, 0)),
            pl.BlockSpec((1, POOL_W), lambda i: (0, 0)),
            pl.BlockSpec((8, CONV_W), lambda i: (0, 0)),
            pl.BlockSpec((D, D), lambda i: (0, 0)),
        ],
        out_specs=pl.BlockSpec((TM, D), lambda i: (i, 0)),
        out_shape=jax.ShapeDtypeStruct((N_TOK, D), F32),
        scratch_shapes=[pltpu.VMEM((TM + 2 * HALO, POOL_W), F32),
                        pltpu.VMEM((TM + 2 * HALO, CONV_W), F32)],
        compiler_params=_cparams("parallel"),
        name="even_mix",
    )(z, z, z, xp, xs, mod_l, pool_w, pool_scale, conv_w, w_out)


def _rope(x, cos, sin_signed):
    n = x.shape[1] // 128
    cosf = jnp.concatenate([cos] * n, axis=1) if n > 1 else cos
    sinf = jnp.concatenate([sin_signed] * n, axis=1) if n > 1 else sin_signed
    w = x.shape[1]
    lane = lax.broadcasted_iota(jnp.int32, x.shape, 1)
    up = pltpu.roll(x, w - DH // 4, 1)
    dn = pltpu.roll(x, DH // 4, 1)
    rot = jnp.where((lane % (DH // 2)) < DH // 4, up, dn)
    return x * cosf + rot * sinf


def _odd_in_kernel(x_ref, mod_ref, g_ref, w_ref, cos_ref, sin_ref, *out_refs, tile0, rope, kv_f32):
    i = pl.program_id(0) + tile0
    r = _mod_row(i)
    h = _rms_mod(x_ref[...], g_ref[...], mod_ref[0, pl.ds(r, 1), :], mod_ref[1, pl.ds(r, 1), :])
    z = _dot(h.astype(BF16), w_ref[...])
    qna_ref, kna_ref, vna_ref, qsw_ref, ksw_ref, vsw_ref = out_refs[:6]
    c0 = 0
    qna = z[:, 0:NA_W]
    kna = z[:, NA_W:2 * NA_W]
    vna = z[:, 2 * NA_W:3 * NA_W]
    c0 = 3 * NA_W
    qsw = z[:, c0:c0 + SWQ_W]
    ksw = z[:, c0 + SWQ_W:c0 + SWQ_W + SWKV_W]
    vsw = z[:, c0 + SWQ_W + SWKV_W:]
    if kv_f32:
        out_refs[6][...] = kna
        out_refs[7][...] = vna
        out_refs[8][...] = ksw
        out_refs[9][...] = vsw
    if rope:
        cos = cos_ref[...]
        sin = sin_ref[...]
        qsw = _rope(qsw, cos, sin)
        ksw = _rope(ksw, cos, sin)
    qna_ref[...] = qna.astype(BF16)
    kna_ref[...] = kna.astype(BF16)
    vna_ref[...] = vna.astype(BF16)
    qsw_ref[...] = qsw.astype(BF16)
    ksw_ref[...] = ksw.astype(BF16)
    vsw_ref[...] = vsw.astype(BF16)


def _odd_in(x, mod_l, g, w, cos, sin, *, prompt):
    tile0 = 0 if prompt else NPT
    nt = NPT if prompt else NT - NPT
    n = nt * TM
    widths = [NA_W, NA_W, NA_W, SWQ_W, SWKV_W, SWKV_W]
    out_shape = [jax.ShapeDtypeStruct((n, w_), BF16) for w_ in widths]
    out_specs = [pl.BlockSpec((TM, w_), lambda i: (i, 0)) for w_ in widths]
    if prompt:
        for w_ in (NA_W, NA_W, SWKV_W, SWKV_W):
            out_shape.append(jax.ShapeDtypeStruct((n, w_), F32))
            out_specs.append(pl.BlockSpec((TM, w_), lambda i: (i, 0)))
    return pl.pallas_call(
        functools.partial(_odd_in_kernel, tile0=tile0, rope=not prompt, kv_f32=prompt),
        grid=(nt,),
        in_specs=[
            pl.BlockSpec((TM, D), lambda i: (i + tile0, 0)),
            pl.BlockSpec((6, MOD_ROWS, D), lambda i: (0, 0, 0)),
            pl.BlockSpec((1, D), lambda i: (0, 0)),
            pl.BlockSpec((D, ODD_IN), lambda i: (0, 0)),
            pl.BlockSpec((TM, 128), lambda i: (i % TPS, 0)),
            pl.BlockSpec((TM, 128), lambda i: (i % TPS, 0)),
        ],
        out_specs=out_specs,
        out_shape=out_shape,
        compiler_params=_cparams("parallel"),
        name="odd_in_prompt" if prompt else "odd_in_latent",
    )(x, mod_l, g, w, cos, sin)


def _rope_tables():
    t = jnp.arange(DEC_SEQ)
    quarter = DH // 4
    inv = 1.0 / (ROPE_BASE ** (jnp.arange(quarter, dtype=F32) / quarter))

    def cos_sin(pos):
        ang = pos.astype(F32)[:, None] * inv[None, :]
        ang = jnp.concatenate([ang, ang], axis=-1)
        return jnp.cos(ang), jnp.sin(ang)

    cr, sr = cos_sin(t // GRID_W)
    cc, sc = cos_sin(t % GRID_W)
    cos = jnp.concatenate([cr, cc], axis=-1)
    sin = jnp.concatenate([sr, sc], axis=-1)
    sign = jnp.where((jnp.arange(DH) % (DH // 2)) < DH // 4, -1.0, 1.0).astype(F32)
    sin = sin * sign[None, :]
    return jnp.concatenate([cos, cos], axis=-1), jnp.concatenate([sin, sin], axis=-1)


def _softmax_pv(segs, sink=None):
    m = None
    for s, _ in segs:
        sm = jnp.max(s, axis=-1, keepdims=True)
        m = sm if m is None else jnp.maximum(m, sm)
    if sink is not None:
        m = jnp.maximum(m, sink)
    den = None
    acc = None
    for s, v in segs:
        p = jnp.exp(s - m)
        ps = jnp.sum(p, axis=-1, keepdims=True)
        den = ps if den is None else den + ps
        pv = _dot(p.astype(BF16), v)
        acc = pv if acc is None else acc + pv
    if sink is not None:
        den = den + jnp.exp(sink - m)
    return acc / den


def _sink_col(sink_ref, g, rows_per_head):
    row = lax.broadcasted_iota(jnp.int32, (SW_G * rows_per_head, 1), 0)
    col = jnp.zeros((SW_G * rows_per_head, 1), F32)
    for r in range(SW_G):
        col = jnp.where(row // rows_per_head == r, sink_ref[g * SW_G + r], col)
    return col


def _ctx_attn_kernel(sink_ref, qna_ref, kna_ref, vna_ref, qsw_ref, ksw_ref, vsw_ref, ona_ref, osw_ref):
    scale = DH ** -0.5
    outs = []
    for h in range(NA_H):
        cols = slice(h * DH, (h + 1) * DH)
        s = _dot_nt(qna_ref[:, cols], kna_ref[:, cols]) * scale
        outs.append(_softmax_pv([(s, vna_ref[:, cols])]))
    ona_ref[...] = jnp.concatenate(outs, axis=1).astype(BF16)
    outs = []
    for g in range(SW_KV):
        kc = slice(g * DH, (g + 1) * DH)
        q = jnp.concatenate([qsw_ref[:, (g * SW_G + r) * DH:(g * SW_G + r + 1) * DH] for r in range(SW_G)], axis=0)
        s = _dot_nt(q, ksw_ref[:, kc]) * scale
        o = _softmax_pv([(s, vsw_ref[:, kc])], sink=_sink_col(sink_ref, g, SEQ))
        outs.extend(o[r * SEQ:(r + 1) * SEQ, :] for r in range(SW_G))
    osw_ref[...] = jnp.concatenate(outs, axis=1).astype(BF16)


def _ctx_attn(sink, qna, kna, vna, qsw, ksw, vsw):
    def spec(w):
        return pl.BlockSpec((SEQ, w), lambda b: (b, 0))

    return pl.pallas_call(
        _ctx_attn_kernel,
        grid=(BATCH,),
        in_specs=[pl.BlockSpec(memory_space=pltpu.SMEM), spec(NA_W), spec(NA_W), spec(NA_W),
                  spec(SWQ_W), spec(SWKV_W), spec(SWKV_W)],
        out_specs=[spec(NA_W), spec(SWQ_W)],
        out_shape=[jax.ShapeDtypeStruct((N_P, NA_W), BF16), jax.ShapeDtypeStruct((N_P, SWQ_W), BF16)],
        compiler_params=_cparams("parallel"),
        name="ctx_attn",
    )(sink, qna, kna, vna, qsw, ksw, vsw)


def _na_bias_kernel(rpb_ref, o_ref):
    h = pl.program_id(0)
    cq = lax.broadcasted_iota(jnp.int32, (GRID_W, GRID_W), 0)
    ck = lax.broadcasted_iota(jnp.int32, (GRID_W, GRID_W), 1)
    cstart = jnp.clip(cq - NA_COLS // 2, 0, GRID_W - NA_COLS)
    ok = (ck >= cstart) & (ck < cstart + NA_COLS)
    dc = jnp.clip(ck - cq + NA_COLS - 1, 0, 2 * NA_COLS - 2)
    ndc = 2 * NA_COLS - 1
    for d in range(2 * NA_ROWS - 1):
        def body(e, acc):
            return jnp.where(dc == e, rpb_ref[h, d * ndc + e], acc)
        b = lax.fori_loop(0, ndc, body, jnp.zeros((GRID_W, GRID_W), F32))
        o_ref[d] = jnp.where(ok, b, NEG)


def _na_bias(rpb):
    nd = 2 * NA_ROWS - 1
    return pl.pallas_call(
        _na_bias_kernel,
        grid=(NA_H,),
        in_specs=[pl.BlockSpec(memory_space=pltpu.SMEM)],
        out_specs=pl.BlockSpec((None, nd, GRID_W, GRID_W), lambda h: (h, 0, 0, 0)),
        out_shape=jax.ShapeDtypeStruct((NA_H, nd, GRID_W, GRID_W), F32),
        compiler_params=_cparams("parallel"),
        name="na_bias",
    )(rpb.reshape(NA_H, nd * (2 * NA_COLS - 1)))


def _na_row_start(r):
    return jnp.clip(r - NA_ROWS // 2, 0, ROWS - NA_ROWS)


def _na_kernel(q_ref, k_ref, v_ref, kc_ref, vc_ref, bias_ref, o_ref):
    scale = DH ** -0.5
    r = pl.program_id(1)
    rs = _na_row_start(r)
    start = pl.multiple_of(rs * GRID_W, GRID_W)
    outs = []
    for h in range(NA_H):
        cols = slice(h * DH, (h + 1) * DH)
        q = q_ref[:, cols]
        kl = k_ref[pl.ds(start, NA_ROWS * GRID_W), cols]
        vl = v_ref[pl.ds(start, NA_ROWS * GRID_W), cols]
        s_loc = _dot_nt(q, kl) * scale + bias_ref[h]
        s_ctx = _dot_nt(q, kc_ref[:, cols]) * scale
        outs.append(_softmax_pv([(s_loc, vl), (s_ctx, vc_ref[:, cols])]))
    o_ref[...] = jnp.concatenate(outs, axis=1).astype(BF16)


def _na_attn(q, k, v, kc, vc, bias):
    return pl.pallas_call(
        _na_kernel,
        grid=(DEC_BATCH, ROWS),
        in_specs=[
            pl.BlockSpec((GRID_W, NA_W), lambda b, r: (b * ROWS + r, 0)),
            pl.BlockSpec((DEC_SEQ, NA_W), lambda b, r: (b, 0)),
            pl.BlockSpec((DEC_SEQ, NA_W), lambda b, r: (b, 0)),
            pl.BlockSpec((None, PAST, NA_W), lambda b, r: (b, 0, 0)),
            pl.BlockSpec((None, PAST, NA_W), lambda b, r: (b, 0, 0)),
            pl.BlockSpec((NA_H, None, GRID_W, NA_ROWS * GRID_W),
                         lambda b, r: (0, _na_row_start(r) - r + NA_ROWS - 1, 0, 0)),
        ],
        out_specs=pl.BlockSpec((GRID_W, NA_W), lambda b, r: (b * ROWS + r, 0)),
        out_shape=jax.ShapeDtypeStruct((N_S, NA_W), BF16),
        compiler_params=_cparams("parallel", "arbitrary"),
        name="na_attn",
    )(q, k, v, kc, vc, bias)


def _sw_kernel(sink_ref, q_ref, k_ref, v_ref, kc_ref, vc_ref, o_ref):
    scale = DH ** -0.5
    j = pl.program_id(1)
    nk = 3 * ABLK
    start = pl.multiple_of(jnp.clip((j - 1) * ABLK, 0, DEC_SEQ - nk), ABLK)
    qpos = j * ABLK + lax.broadcasted_iota(jnp.int32, (SW_G * ABLK, nk), 0) % ABLK
    kpos = start + lax.broadcasted_iota(jnp.int32, (SW_G * ABLK, nk), 1)
    ok = jnp.abs(qpos - kpos) <= SW_WIN
    outs = []
    for g in range(SW_KV):
        kcols = slice(g * DH, (g + 1) * DH)
        q = jnp.concatenate([q_ref[:, (g * SW_G + r) * DH:(g * SW_G + r + 1) * DH] for r in range(SW_G)], axis=0)
        kw = k_ref[pl.ds(start, nk), kcols]
        vw = v_ref[pl.ds(start, nk), kcols]
        s_w = jnp.where(ok, _dot_nt(q, kw) * scale, NEG)
        s_c = _dot_nt(q, kc_ref[:, kcols]) * scale
        o = _softmax_pv([(s_w, vw), (s_c, vc_ref[:, kcols])], sink=_sink_col(sink_ref, g, ABLK))
        outs.extend(o[r * ABLK:(r + 1) * ABLK, :] for r in range(SW_G))
    o_ref[...] = jnp.concatenate(outs, axis=1).astype(BF16)


def _sw_attn(sink, q, k, v, kc, vc):
    nb = DEC_SEQ // ABLK
    return pl.pallas_call(
        _sw_kernel,
        grid=(DEC_BATCH, nb),
        in_specs=[
            pl.BlockSpec(memory_space=pltpu.SMEM),
            pl.BlockSpec((ABLK, SWQ_W), lambda b, j: (b * nb + j, 0)),
            pl.BlockSpec((DEC_SEQ, SWKV_W), lambda b, j: (b, 0)),
            pl.BlockSpec((DEC_SEQ, SWKV_W), lambda b, j: (b, 0)),
            pl.BlockSpec((None, PAST, SWKV_W), lambda b, j: (b, 0, 0)),
            pl.BlockSpec((None, PAST, SWKV_W), lambda b, j: (b, 0, 0)),
        ],
        out_specs=pl.BlockSpec((ABLK, SWQ_W), lambda b, j: (b * nb + j, 0)),
        out_shape=jax.ShapeDtypeStruct((N_S, SWQ_W), BF16),
        compiler_params=_cparams("parallel", "arbitrary"),
        name="sw_attn",
    )(sink, q, k, v, kc, vc)


def _odd_out_kernel(nap_ref, nas_ref, swp_ref, sws_ref, x_ref, mod_ref, wo_ref, o_ref):
    i = pl.program_id(0)
    r = _mod_row(i)
    is_p = i < NPT
    ona = jnp.where(is_p, nap_ref[...], nas_ref[...])
    osw = jnp.where(is_p, swp_ref[...], sws_ref[...])
    y = _dot(ona, wo_ref[:NA_W, :]) + _dot(osw, wo_ref[NA_W:, :])
    o_ref[...] = x_ref[...] + mod_ref[2, pl.ds(r, 1), :] * y


def _odd_out(nap, nas, swp, sws, x, mod_l, w_out):
    return pl.pallas_call(
        _odd_out_kernel,
        grid=(NT,),
        in_specs=_dual_specs(TM, NA_W, NPT) + _dual_specs(TM, SWQ_W, NPT) + [
            pl.BlockSpec((TM, D), lambda i: (i, 0)),
            pl.BlockSpec((6, MOD_ROWS, D), lambda i: (0, 0, 0)),
            pl.BlockSpec((D, D), lambda i: (0, 0)),
        ],
        out_specs=pl.BlockSpec((TM, D), lambda i: (i, 0)),
        out_shape=jax.ShapeDtypeStruct((N_TOK, D), F32),
        compiler_params=_cparams("parallel"),
        name="odd_out",
    )(nap, nas, swp, sws, x, mod_l, w_out)


def _route(probs, sel):
    picked = []
    score = []
    for g in range(N_GRP):
        ids = range(g * EPG, (g + 1) * EPG)
        tot = None
        for e in ids:
            rank = jnp.zeros_like(sel[e])
            for j in ids:
                if j < e:
                    rank = rank + (sel[j] >= sel[e]).astype(F32)
                elif j > e:
                    rank = rank + (sel[j] > sel[e]).astype(F32)
            pick = rank < 2.0
            picked.append(pick)
            contrib = jnp.where(pick, sel[e], 0.0)
            tot = contrib if tot is None else tot + contrib
        score.append(tot)
    gates = []
    for g in range(N_GRP):
        best = None
        for j in range(N_GRP):
            if j == g:
                continue
            c = (score[g] > score[j]) if j < g else (score[g] >= score[j])
            best = c if best is None else jnp.logical_and(best, c)
        for e in range(g * EPG, (g + 1) * EPG):
            gates.append(jnp.where(jnp.logical_and(best, picked[e]), probs[e], 0.0))
    den = gates[0]
    for e in range(1, N_EXP):
        den = den + gates[e]
    return [gt / den for gt in gates]


def _ffn_pre_kernel(x_ref, mod_ref, g_ref, rw_ref, rb_ref, h_ref, gate_ref):
    i = pl.program_id(0)
    r = _mod_row(i)
    h = _rms_mod(x_ref[...], g_ref[...], mod_ref[3, pl.ds(r, 1), :], mod_ref[4, pl.ds(r, 1), :])
    h_ref[...] = h.astype(BF16)
    logits = jnp.dot(h, rw_ref[...], preferred_element_type=F32, precision=lax.Precision.HIGHEST)
    lt = logits.T[:N_EXP, :]
    m = jnp.max(lt, axis=0, keepdims=True)
    ex = jnp.exp(lt - m)
    pr = ex / jnp.sum(ex, axis=0, keepdims=True)
    se = pr + rb_ref[:N_EXP, :]
    probs = [pr[e:e + 1, :] for e in range(N_EXP)]
    sel = [se[e:e + 1, :] for e in range(N_EXP)]
    gates = _route(probs, sel)
    gt = jnp.concatenate(gates + [jnp.zeros((128 - N_EXP, TM), F32)], axis=0)
    gate_ref[...] = gt.T


def _ffn_pre(x, mod_l, g, rw_pad, rb_col):
    return pl.pallas_call(
        _ffn_pre_kernel,
        grid=(NT,),
        in_specs=[
            pl.BlockSpec((TM, D), lambda i: (i, 0)),
            pl.BlockSpec((6, MOD_ROWS, D), lambda i: (0, 0, 0)),
            pl.BlockSpec((1, D), lambda i: (0, 0)),
            pl.BlockSpec((D, 128), lambda i: (0, 0)),
            pl.BlockSpec((128, 1), lambda i: (0, 0)),
        ],
        out_specs=[pl.BlockSpec((TM, D), lambda i: (i, 0)), pl.BlockSpec((TM, 128), lambda i: (i, 0))],
        out_shape=[jax.ShapeDtypeStruct((N_TOK, D), BF16), jax.ShapeDtypeStruct((N_TOK, 128), F32)],
        compiler_params=_cparams("parallel"),
        name="ffn_pre",
    )(x, mod_l, g, rw_pad, rb_col)


def _moe_kernel(h_ref, gate_ref, wg_ref, wu_ref, wd_ref, x_ref, mod_ref, o_ref, acc_ref):
    i = pl.program_id(0)
    e = pl.program_id(1)

    @pl.when(e == 0)
    def _():
        acc_ref[...] = jnp.zeros_like(acc_ref)

    h = h_ref[...]
    a = _dot(h, wg_ref[...])
    b = _dot(h, wu_ref[...])
    hid = (a / (1.0 + jnp.exp(-a))) * b
    out = _dot(hid.astype(BF16), wd_ref[...])
    lane = lax.broadcasted_iota(jnp.int32, (TM_MOE, 128), 1)
    gcol = jnp.sum(jnp.where(lane == e, gate_ref[...], 0.0), axis=1, keepdims=True)
    acc_ref[...] += gcol * out

    @pl.when(e == N_EXP - 1)
    def _():
        r = _mod_row(i * (TM_MOE // TM))
        o_ref[...] = x_ref[...] + mod_ref[5, pl.ds(r, 1), :] * acc_ref[...]


def _moe(h, gates, wg, wu, wd, x, mod_l):
    return pl.pallas_call(
        _moe_kernel,
        grid=(N_TOK // TM_MOE, N_EXP),
        in_specs=[
            pl.BlockSpec((TM_MOE, D), lambda i, e: (i, 0)),
            pl.BlockSpec((TM_MOE, 128), lambda i, e: (i, 0)),
            pl.BlockSpec((None, D, D_EXP), lambda i, e: (e, 0, 0)),
            pl.BlockSpec((None, D, D_EXP), lambda i, e: (e, 0, 0)),
            pl.BlockSpec((None, D_EXP, D), lambda i, e: (e, 0, 0)),
            pl.BlockSpec((TM_MOE, D), lambda i, e: (i, 0)),
            pl.BlockSpec((6, MOD_ROWS, D), lambda i, e: (0, 0, 0)),
        ],
        out_specs=pl.BlockSpec((TM_MOE, D), lambda i, e: (i, 0)),
        out_shape=jax.ShapeDtypeStruct((N_TOK, D), F32),
        scratch_shapes=[pltpu.VMEM((TM_MOE, D), F32)],
        compiler_params=_cparams("parallel", "arbitrary"),
        name="moe",
    )(h, gates, wg, wu, wd, x, mod_l)


def _final_kernel(x_ref, g_ref, o_ref):
    x = x_ref[...]
    ms = jnp.mean(x * x, axis=-1, keepdims=True)
    o_ref[...] = x * lax.rsqrt(ms + EPS) * g_ref[...]


def _final_norm(x, g, *, prompt):
    tile0 = 0 if prompt else NPT
    nt = NPT if prompt else NT - NPT
    return pl.pallas_call(
        _final_kernel,
        grid=(nt,),
        in_specs=[pl.BlockSpec((TM, D), lambda i: (i + tile0, 0)), pl.BlockSpec((1, D), lambda i: (0, 0))],
        out_specs=pl.BlockSpec((TM, D), lambda i: (i, 0)),
        out_shape=jax.ShapeDtypeStruct((nt * TM, D), F32),
        compiler_params=_cparams("parallel"),
        name="final_norm_prompt" if prompt else "final_norm_latent",
    )(x, g)


def kernel(x_prompt, x_sample, cache_na_k, cache_na_v, cache_sw_k, cache_sw_v, c, c_ctx, mod_w, mod_b, norm_mix_g, norm_ffn_g, ev_w_in, ev_pool_w, ev_pool_scale, ev_conv_w, ev_w_out, od_w_in, od_rpb, od_sink, od_w_out, router_w, router_b, moe_w_gate, moe_w_up, moe_w_down, final_norm_g):
    xp = x_prompt.reshape(N_P, D)
    xs = x_sample.reshape(N_S, D)
    cvec = jnp.concatenate([c_ctx[None, :], c, jnp.zeros((MOD_ROWS - 1 - DEC_BATCH, D), F32)], axis=0)
    mod = _modulation(cvec, mod_w, mod_b)

    rw_pad = jnp.pad(router_w, ((0, 0), (0, 128 - N_EXP)))
    rb_col = jnp.pad(router_b, (0, 128 - N_EXP)).reshape(128, 1)

    z = _even_in(xp, xs, mod[0], norm_mix_g[0:1], ev_w_in[0].astype(BF16))
    conv_w = jnp.pad(ev_conv_w[0], ((0, 8 - ev_conv_w.shape[1]), (0, 0)))
    x = _even_mix(z, xp, xs, mod[0], ev_pool_w[0].astype(BF16), ev_pool_scale[0:1], conv_w,
                  ev_w_out[0].astype(BF16))
    h, gates = _ffn_pre(x, mod[0], norm_ffn_g[0:1], rw_pad, rb_col)
    x = _moe(h, gates, moe_w_gate[0].astype(BF16), moe_w_up[0].astype(BF16), moe_w_down[0].astype(BF16),
             x, mod[0])

    cos, sin = _rope_tables()
    w_in = od_w_in[0].astype(BF16)
    g1 = norm_mix_g[1:2]
    qna_p, kna_p, vna_p, qsw_p, ksw_p, vsw_p, nak, nav, swk, swv = _odd_in(x, mod[1], g1, w_in, cos, sin, prompt=True)
    qna_s, kna_s, vna_s, qsw_s, ksw_s, vsw_s = _odd_in(x, mod[1], g1, w_in, cos, sin, prompt=False)
    sink = od_sink[0]
    ona_p, osw_p = _ctx_attn(sink, qna_p, kna_p, vna_p, qsw_p, ksw_p, vsw_p)
    nd = 2 * NA_ROWS - 1
    bias = _na_bias(od_rpb[0])
    bias = jnp.stack([bias[:, cidx:cidx + NA_ROWS] for cidx in range(NA_ROWS)], axis=1)
    bias = bias.transpose(0, 1, 3, 2, 4).reshape(NA_H, NA_ROWS, GRID_W, NA_ROWS * GRID_W)
    ona_s = _na_attn(qna_s, kna_s, vna_s,
                     cache_na_k[:, 0].reshape(DEC_BATCH, PAST, NA_W).astype(BF16),
                     cache_na_v[:, 0].reshape(DEC_BATCH, PAST, NA_W).astype(BF16), bias)
    osw_s = _sw_attn(sink, qsw_s, ksw_s, vsw_s,
                     cache_sw_k[:, 0].reshape(DEC_BATCH, PAST, SWKV_W).astype(BF16),
                     cache_sw_v[:, 0].reshape(DEC_BATCH, PAST, SWKV_W).astype(BF16))
    x = _odd_out(ona_p, ona_s, osw_p, osw_s, x, mod[1], od_w_out[0].astype(BF16))
    h, gates = _ffn_pre(x, mod[1], norm_ffn_g[1:2], rw_pad, rb_col)
    x = _moe(h, gates, moe_w_gate[1].astype(BF16), moe_w_up[1].astype(BF16), moe_w_down[1].astype(BF16),
             x, mod[1])

    fg = final_norm_g.reshape(1, D)
    y_prompt = _final_norm(x, fg, prompt=True).reshape(BATCH, SEQ, D)
    y_sample = _final_norm(x, fg, prompt=False).reshape(DEC_BATCH, DEC_SEQ, D)
    new_na_k = nak.reshape(BATCH, 1, SEQ, NA_H, DH)
    new_na_v = nav.reshape(BATCH, 1, SEQ, NA_H, DH)
    new_sw_k = swk.reshape(BATCH, 1, SEQ, SW_KV, DH)
    new_sw_v = swv.reshape(BATCH, 1, SEQ, SW_KV, DH)
    return (y_prompt, y_sample, new_na_k, new_na_v, new_sw_k, new_sw_v)
```

```python
import functools

import jax
import jax.numpy as jnp
from jax import lax
from jax.experimental import pallas as pl
from jax.experimental.pallas import tpu as pltpu

D = 1024
BATCH = 16
SEQ = 256
DEC_BATCH = 4
DEC_SEQ = 4096
PAST = 512
GRID_W = 64
ROWS = DEC_SEQ // GRID_W
DH = 64
POOL_W = 512
POOL_WINDOWS = (2, 4, 8, 16)
POOL_GW = 128
CONV_W = 512
EVEN_IN = POOL_W + 3 * CONV_W
NA_H = 8
NA_ROWS = 8
NA_COLS = 16
SW_H = 8
SW_KV = 2
SW_G = SW_H // SW_KV
SW_WIN = 128
ABLK = 128
NA_W = NA_H * DH
SWQ_W = SW_H * DH
SWKV_W = SW_KV * DH
ODD_IN = 3 * NA_W + SWQ_W + 2 * SWKV_W
N_EXP = 16
N_GRP = 4
EPG = 4
D_EXP = 512
EPS = 1e-6
NEG = -1e30
ROPE_BASE = 10000.0

N_P = BATCH * SEQ
N_S = DEC_BATCH * DEC_SEQ
N_TOK = N_P + N_S
MOD_ROWS = 8

TM = 256
NPT = N_P // TM
TPS = DEC_SEQ // TM
NT = N_TOK // TM
HALO = 8

F32 = jnp.float32
BF16 = jnp.bfloat16
VMEM_LIMIT = 56 * 1024 * 1024


def _cparams(*sem):
    return pltpu.CompilerParams(dimension_semantics=sem, vmem_limit_bytes=VMEM_LIMIT)


def _mod_row(i):
    return jnp.where(i < NPT, 0, 1 + (i - NPT) // TPS)


def _rms_mod(x, g, shift, scale):
    ms = jnp.mean(x * x, axis=-1, keepdims=True)
    y = x * lax.rsqrt(ms + EPS) * g
    return y * (1.0 + scale) + shift


def _dot(a, b):
    return jnp.dot(a, b, preferred_element_type=F32)


def _dot_nt(a, b):
    return lax.dot_general(a, b, (((1,), (1,)), ((), ())), preferred_element_type=F32)


def _mod_kernel(cv_ref, w_ref, b_ref, o_ref):
    cv = cv_ref[...]
    a = cv / (1.0 + jnp.exp(-cv))
    o_ref[...] = jnp.dot(a, w_ref[...], preferred_element_type=F32,
                         precision=lax.Precision.HIGHEST) + b_ref[...]


def _modulation(cvec, mod_w, mod_b):
    depth = mod_w.shape[0]
    return pl.pallas_call(
        _mod_kernel,
        grid=(depth, 6),
        in_specs=[
            pl.BlockSpec((MOD_ROWS, D), lambda l, j: (0, 0)),
            pl.BlockSpec((None, D, D), lambda l, j: (l, 0, j)),
            pl.BlockSpec((None, None, 1, D), lambda l, j: (l, j, 0, 0)),
        ],
        out_specs=pl.BlockSpec((None, None, MOD_ROWS, D), lambda l, j: (l, j, 0, 0)),
        out_shape=jax.ShapeDtypeStruct((depth, 6, MOD_ROWS, D), F32),
        compiler_params=_cparams("arbitrary", "arbitrary"),
        name="modulation",
    )(cvec, mod_w, mod_b.reshape(depth, 6, 1, D))


def _dual_specs(tm, width, npt):
    return [
        pl.BlockSpec((tm, width), lambda i: (jnp.minimum(i, npt - 1), 0)),
        pl.BlockSpec((tm, width), lambda i: (jnp.maximum(i - npt, 0), 0)),
    ]


def _even_in_kernel(xp_ref, xs_ref, mod_ref, g_ref, w_ref, z_ref):
    i = pl.program_id(0)
    r = _mod_row(i)
    x = jnp.where(i < NPT, xp_ref[...], xs_ref[...])
    h = _rms_mod(x, g_ref[...], mod_ref[0, pl.ds(r, 1), :], mod_ref[1, pl.ds(r, 1), :])
    z_ref[...] = _dot(h.astype(BF16), w_ref[...])


def _even_in(xp, xs, mod_l, g, w):
    return pl.pallas_call(
        _even_in_kernel,
        grid=(NT,),
        in_specs=_dual_specs(TM, D, NPT) + [
            pl.BlockSpec((6, MOD_ROWS, D), lambda i: (0, 0, 0)),
            pl.BlockSpec((1, D), lambda i: (0, 0)),
            pl.BlockSpec((D, EVEN_IN), lambda i: (0, 0)),
        ],
        out_specs=pl.BlockSpec((TM, EVEN_IN), lambda i: (i, 0)),
        out_shape=jax.ShapeDtypeStruct((N_TOK, EVEN_IN), F32),
        compiler_params=_cparams("parallel"),
        name="even_in",
    )(xp, xs, mod_l, g, w)


def _even_mix_kernel(zc_ref, zp_ref, zn_ref, xp_ref, xs_ref, mod_ref, pw_ref, ps_ref, cw_ref,
                     wo_ref, o_ref, pext, uext):
    i = pl.program_id(0)
    r = _mod_row(i)
    is_p = i < NPT
    t0 = jnp.where(is_p, 0, ((i - NPT) % TPS) * TM)
    seq_len = jnp.where(is_p, SEQ, DEC_SEQ)
    first = t0 == 0
    last = t0 + TM == seq_len

    z = zc_ref[...]
    zp = jnp.where(first, 0.0, zp_ref[...])
    zn = jnp.where(last, 0.0, zn_ref[...])

    def split(v):
        return (v[:, :POOL_W], v[:, POOL_W:POOL_W + CONV_W],
                v[:, POOL_W + CONV_W:POOL_W + 2 * CONV_W], v[:, POOL_W + 2 * CONV_W:])

    p, xin, gb, gc = split(z)
    pp, xinp, _, gcp = split(zp)
    pn, xinn, _, gcn = split(zn)
    pext[0:HALO, :] = pp
    pext[HALO:HALO + TM, :] = p
    pext[HALO + TM:, :] = pn
    uext[0:HALO, :] = gcp * xinp
    uext[HALO:HALO + TM, :] = gc * xin
    uext[HALO + TM:, :] = gcn * xinn

    t = t0 + lax.broadcasted_iota(jnp.int32, (TM, 1), 0)
    y = jnp.zeros((TM, D), F32)
    for g, w in enumerate(POOL_WINDOWS):
        cols = slice(g * POOL_GW, (g + 1) * POOL_GW)
        acc = jnp.zeros((TM, POOL_GW), F32)
        for k in range(-(w // 2), w - (w // 2)):
            acc = acc + pext[pl.ds(HALO + k, TM), cols]
        lo = jnp.maximum(t - w // 2, 0)
        hi = jnp.minimum(t + (w - 1 - w // 2), seq_len - 1)
        cnt = (hi - lo + 1).astype(F32)
        d = acc / cnt - p[:, cols]
        a = _dot(d.astype(BF16), pw_ref[g]) * ps_ref[:, cols]
        y = y + _dot(a.astype(BF16), wo_ref[cols, :])
    conv = (uext[pl.ds(HALO - 1, TM), :] * cw_ref[0:1, :] + uext[pl.ds(HALO, TM), :] * cw_ref[1:2, :]
            + uext[pl.ds(HALO + 1, TM), :] * cw_ref[2:3, :])
    b = gb * conv
    y = y + _dot(b.astype(BF16), wo_ref[POOL_W:, :])
    x = jnp.where(is_p, xp_ref[...], xs_ref[...])
    o_ref[...] = x + mod_ref[2, pl.ds(r, 1), :] * y


def _even_mix(z, xp, xs, mod_l, pool_w, pool_scale, conv_w, w_out):
    hb = TM // HALO
    nhb = N_TOK // HALO
    return pl.pallas_call(
        _even_mix_kernel,
        grid=(NT,),
        in_specs=[
            pl.BlockSpec((TM, EVEN_IN), lambda i: (i, 0)),
            pl.BlockSpec((HALO, EVEN_IN), lambda i: (jnp.maximum(i * hb - 1, 0), 0)),
            pl.BlockSpec((HALO, EVEN_IN), lambda i: (jnp.minimum((i + 1) * hb, nhb - 1), 0)),
        ] + _dual_specs(TM, D, NPT) + [
            pl.BlockSpec((6, MOD_ROWS, D), lambda i: (0, 0, 0)),
            pl.BlockSpec((4, POOL_GW, POOL_GW), lambda i: (0, 0, 0)),
            pl.BlockSpec((1, POOL_W), lambda i: (0, 0)),
            pl.BlockSpec((8, CONV_W), lambda i: (0, 0)),
            pl.BlockSpec((D, D), lambda i: (0, 0)),
        ],
        out_specs=pl.BlockSpec((TM, D), lambda i: (i, 0)),
        out_shape=jax.ShapeDtypeStruct((N_TOK, D), F32),
        scratch_shapes=[pltpu.VMEM((TM + 2 * HALO, POOL_W), F32),
                        pltpu.VMEM((TM + 2 * HALO, CONV_W), F32)],
        compiler_params=_cparams("parallel"),
        name="even_mix",
    )(z, z, z, xp, xs, mod_l, pool_w, pool_scale, conv_w, w_out)


def _rope(x, cos, sin_signed):
    n = x.shape[1] // 128
    cosf = jnp.concatenate([cos] * n, axis=1) if n > 1 else cos
    sinf = jnp.concatenate([sin_signed] * n, axis=1) if n > 1 else sin_signed
    w = x.shape[1]
    lane = lax.broadcasted_iota(jnp.int32, x.shape, 1)
    up = pltpu.roll(x, w - DH // 4, 1)
    dn = pltpu.roll(x, DH // 4, 1)
    rot = jnp.where((lane % (DH // 2)) < DH // 4, up, dn)
    return x * cosf + rot * sinf


def _odd_in_kernel(x_ref, mod_ref, g_ref, w_ref, cos_ref, sin_ref, *out_refs, tile0, rope, kv_f32):
    i = pl.program_id(0) + tile0
    r = _mod_row(i)
    h = _rms_mod(x_ref[...], g_ref[...], mod_ref[0, pl.ds(r, 1), :], mod_ref[1, pl.ds(r, 1), :])
    z = _dot(h.astype(BF16), w_ref[...])
    qna_ref, kna_ref, vna_ref, qsw_ref, ksw_ref, vsw_ref = out_refs[:6]
    c0 = 0
    qna = z[:, 0:NA_W]
    kna = z[:, NA_W:2 * NA_W]
    vna = z[:, 2 * NA_W:3 * NA_W]
    c0 = 3 * NA_W
    qsw = z[:, c0:c0 + SWQ_W]
    ksw = z[:, c0 + SWQ_W:c0 + SWQ_W + SWKV_W]
    vsw = z[:, c0 + SWQ_W + SWKV_W:]
    if kv_f32:
        out_refs[6][...] = kna
        out_refs[7][...] = vna
        out_refs[8][...] = ksw
        out_refs[9][...] = vsw
    if rope:
        cos = cos_ref[...]
        sin = sin_ref[...]
        qsw = _rope(qsw, cos, sin)
        ksw = _rope(ksw, cos, sin)
    qna_ref[...] = qna.astype(BF16)
    kna_ref[...] = kna.astype(BF16)
    vna_ref[...] = vna.astype(BF16)
    qsw_ref[...] = qsw.astype(BF16)
    ksw_ref[...] = ksw.astype(BF16)
    vsw_ref[...] = vsw.astype(BF16)


def _odd_in(x, mod_l, g, w, cos, sin, *, prompt):
    tile0 = 0 if prompt else NPT
    nt = NPT if prompt else NT - NPT
    n = nt * TM
    widths = [NA_W, NA_W, NA_W, SWQ_W, SWKV_W, SWKV_W]
    out_shape = [jax.ShapeDtypeStruct((n, w_), BF16) for w_ in widths]
    out_specs = [pl.BlockSpec((TM, w_), lambda i: (i, 0)) for w_ in widths]
    if prompt:
        for w_ in (NA_W, NA_W, SWKV_W, SWKV_W):
            out_shape.append(jax.ShapeDtypeStruct((n, w_), F32))
            out_specs.append(pl.BlockSpec((TM, w_), lambda i: (i, 0)))
    return pl.pallas_call(
        functools.partial(_odd_in_kernel, tile0=tile0, rope=not prompt, kv_f32=prompt),
        grid=(nt,),
        in_specs=[
            pl.BlockSpec((TM, D), lambda i: (i + tile0, 0)),
            pl.BlockSpec((6, MOD_ROWS, D), lambda i: (0, 0, 0)),
            pl.BlockSpec((1, D), lambda i: (0, 0)),
            pl.BlockSpec((D, ODD_IN), lambda i: (0, 0)),
            pl.BlockSpec((TM, 128), lambda i: (i % TPS, 0)),
            pl.BlockSpec((TM, 128), lambda i: (i % TPS, 0)),
        ],
        out_specs=out_specs,
        out_shape=out_shape,
        compiler_params=_cparams("parallel"),
        name="odd_in_prompt" if prompt else "odd_in_latent",
    )(x, mod_l, g, w, cos, sin)


def _rope_tables():
    t = jnp.arange(DEC_SEQ)
    quarter = DH // 4
    inv = 1.0 / (ROPE_BASE ** (jnp.arange(quarter, dtype=F32) / quarter))

    def cos_sin(pos):
        ang = pos.astype(F32)[:, None] * inv[None, :]
        ang = jnp.concatenate([ang, ang], axis=-1)
        return jnp.cos(ang), jnp.sin(ang)

    cr, sr = cos_sin(t // GRID_W)
    cc, sc = cos_sin(t % GRID_W)
    cos = jnp.concatenate([cr, cc], axis=-1)
    sin = jnp.concatenate([sr, sc], axis=-1)
    sign = jnp.where((jnp.arange(DH) % (DH // 2)) < DH // 4, -1.0, 1.0).astype(F32)
    sin = sin * sign[None, :]
    return jnp.concatenate([cos, cos], axis=-1), jnp.concatenate([sin, sin], axis=-1)


def _softmax_pv(segs, sink=None):
    m = None
    for s, _ in segs:
        sm = jnp.max(s, axis=-1, keepdims=True)
        m = sm if m is None else jnp.maximum(m, sm)
    if sink is not None:
        m = jnp.maximum(m, sink)
    den = None
    acc = None
    for s, v in segs:
        p = jnp.exp(s - m)
        ps = jnp.sum(p, axis=-1, keepdims=True)
        den = ps if den is None else den + ps
        pv = _dot(p.astype(BF16), v)
        acc = pv if acc is None else acc + pv
    if sink is not None:
        den = den + jnp.exp(sink - m)
    return acc / den


def _sink_col(sink_ref, g, rows_per_head):
    row = lax.broadcasted_iota(jnp.int32, (SW_G * rows_per_head, 1), 0)
    col = jnp.zeros((SW_G * rows_per_head, 1), F32)
    for r in range(SW_G):
        col = jnp.where(row // rows_per_head == r, sink_ref[g * SW_G + r], col)
    return col


def _ctx_attn_kernel(sink_ref, qna_ref, kna_ref, vna_ref, qsw_ref, ksw_ref, vsw_ref, ona_ref, osw_ref):
    scale = DH ** -0.5
    outs = []
    for h in range(NA_H):
        cols = slice(h * DH, (h + 1) * DH)
        s = _dot_nt(qna_ref[:, cols], kna_ref[:, cols]) * scale
        outs.append(_softmax_pv([(s, vna_ref[:, cols])]))
    ona_ref[...] = jnp.concatenate(outs, axis=1).astype(BF16)
    outs = []
    for g in range(SW_KV):
        kc = slice(g * DH, (g + 1) * DH)
        q = jnp.concatenate([qsw_ref[:, (g * SW_G + r) * DH:(g * SW_G + r + 1) * DH] for r in range(SW_G)], axis=0)
        s = _dot_nt(q, ksw_ref[:, kc]) * scale
        o = _softmax_pv([(s, vsw_ref[:, kc])], sink=_sink_col(sink_ref, g, SEQ))
        outs.extend(o[r * SEQ:(r + 1) * SEQ, :] for r in range(SW_G))
    osw_ref[...] = jnp.concatenate(outs, axis=1).astype(BF16)


def _ctx_attn(sink, qna, kna, vna, qsw, ksw, vsw):
    def spec(w):
        return pl.BlockSpec((SEQ, w), lambda b: (b, 0))

    return pl.pallas_call(
        _ctx_attn_kernel,
        grid=(BATCH,),
        in_specs=[pl.BlockSpec(memory_space=pltpu.SMEM), spec(NA_W), spec(NA_W), spec(NA_W),
                  spec(SWQ_W), spec(SWKV_W), spec(SWKV_W)],
        out_specs=[spec(NA_W), spec(SWQ_W)],
        out_shape=[jax.ShapeDtypeStruct((N_P, NA_W), BF16), jax.ShapeDtypeStruct((N_P, SWQ_W), BF16)],
        compiler_params=_cparams("parallel"),
        name="ctx_attn",
    )(sink, qna, kna, vna, qsw, ksw, vsw)


def _na_bias_kernel(rpb_ref, o_ref):
    h = pl.program_id(0)
    cq = lax.broadcasted_iota(jnp.int32, (GRID_W, GRID_W), 0)
    ck = lax.broadcasted_iota(jnp.int32, (GRID_W, GRID_W), 1)
    cstart = jnp.clip(cq - NA_COLS // 2, 0, GRID_W - NA_COLS)
    ok = (ck >= cstart) & (ck < cstart + NA_COLS)
    dc = jnp.clip(ck - cq + NA_COLS - 1, 0, 2 * NA_COLS - 2)
    ndc = 2 * NA_COLS - 1
    for d in range(2 * NA_ROWS - 1):
        def body(e, acc):
            return jnp.where(dc == e, rpb_ref[h, d * ndc + e], acc)
        b = lax.fori_loop(0, ndc, body, jnp.zeros((GRID_W, GRID_W), F32))
        o_ref[d] = jnp.where(ok, b, NEG)


def _na_bias(rpb):
    nd = 2 * NA_ROWS - 1
    return pl.pallas_call(
        _na_bias_kernel,
        grid=(NA_H,),
        in_specs=[pl.BlockSpec(memory_space=pltpu.SMEM)],
        out_specs=pl.BlockSpec((None, nd, GRID_W, GRID_W), lambda h: (h, 0, 0, 0)),
        out_shape=jax.ShapeDtypeStruct((NA_H, nd, GRID_W, GRID_W), F32),
        compiler_params=_cparams("parallel"),
        name="na_bias",
    )(rpb.reshape(NA_H, nd * (2 * NA_COLS - 1)))


def _na_row_start(r):
    return jnp.clip(r - NA_ROWS // 2, 0, ROWS - NA_ROWS)


def _na_kernel(q_ref, k_ref, v_ref, kc_ref, vc_ref, bias_ref, o_ref):
    scale = DH ** -0.5
    r = pl.program_id(1)
    rs = _na_row_start(r)
    start = pl.multiple_of(rs * GRID_W, GRID_W)
    outs = []
    for h in range(NA_H):
        cols = slice(h * DH, (h + 1) * DH)
        q = q_ref[:, cols]
        kl = k_ref[pl.ds(start, NA_ROWS * GRID_W), cols]
        vl = v_ref[pl.ds(start, NA_ROWS * GRID_W), cols]
        s_loc = _dot_nt(q, kl) * scale + bias_ref[h]
        s_ctx = _dot_nt(q, kc_ref[:, cols]) * scale
        outs.append(_softmax_pv([(s_loc, vl), (s_ctx, vc_ref[:, cols])]))
    o_ref[...] = jnp.concatenate(outs, axis=1).astype(BF16)


def _na_attn(q, k, v, kc, vc, bias):
    return pl.pallas_call(
        _na_kernel,
        grid=(DEC_BATCH, ROWS),
        in_specs=[
            pl.BlockSpec((GRID_W, NA_W), lambda b, r: (b * ROWS + r, 0)),
            pl.BlockSpec((DEC_SEQ, NA_W), lambda b, r: (b, 0)),
            pl.BlockSpec((DEC_SEQ, NA_W), lambda b, r: (b, 0)),
            pl.BlockSpec((None, PAST, NA_W), lambda b, r: (b, 0, 0)),
            pl.BlockSpec((None, PAST, NA_W), lambda b, r: (b, 0, 0)),
            pl.BlockSpec((NA_H, None, GRID_W, NA_ROWS * GRID_W),
                         lambda b, r: (0, _na_row_start(r) - r + NA_ROWS - 1, 0, 0)),
        ],
        out_specs=pl.BlockSpec((GRID_W, NA_W), lambda b, r: (b * ROWS + r, 0)),
        out_shape=jax.ShapeDtypeStruct((N_S, NA_W), BF16),
        compiler_params=_cparams("parallel", "arbitrary"),
        name="na_attn",
    )(q, k, v, kc, vc, bias)


def _sw_kernel(sink_ref, q_ref, k_ref, v_ref, kc_ref, vc_ref, o_ref):
    scale = DH ** -0.5
    j = pl.program_id(1)
    nk = 3 * ABLK
    start = pl.multiple_of(jnp.clip((j - 1) * ABLK, 0, DEC_SEQ - nk), ABLK)
    qpos = j * ABLK + lax.broadcasted_iota(jnp.int32, (SW_G * ABLK, nk), 0) % ABLK
    kpos = start + lax.broadcasted_iota(jnp.int32, (SW_G * ABLK, nk), 1)
    ok = jnp.abs(qpos - kpos) <= SW_WIN
    outs = []
    for g in range(SW_KV):
        kcols = slice(g * DH, (g + 1) * DH)
        q = jnp.concatenate([q_ref[:, (g * SW_G + r) * DH:(g * SW_G + r + 1) * DH] for r in range(SW_G)], axis=0)
        kw = k_ref[pl.ds(start, nk), kcols]
        vw = v_ref[pl.ds(start, nk), kcols]
        s_w = jnp.where(ok, _dot_nt(q, kw) * scale, NEG)
        s_c = _dot_nt(q, kc_ref[:, kcols]) * scale
        o = _softmax_pv([(s_w, vw), (s_c, vc_ref[:, kcols])], sink=_sink_col(sink_ref, g, ABLK))
        outs.extend(o[r * ABLK:(r + 1) * ABLK, :] for r in range(SW_G))
    o_ref[...] = jnp.concatenate(outs, axis=1).astype(BF16)


def _sw_attn(sink, q, k, v, kc, vc):
    nb = DEC_SEQ // ABLK
    return pl.pallas_call(
        _sw_kernel,
        grid=(DEC_BATCH, nb),
        in_specs=[
            pl.BlockSpec(memory_space=pltpu.SMEM),
            pl.BlockSpec((ABLK, SWQ_W), lambda b, j: (b * nb + j, 0)),
            pl.BlockSpec((DEC_SEQ, SWKV_W), lambda b, j: (b, 0)),
            pl.BlockSpec((DEC_SEQ, SWKV_W), lambda b, j: (b, 0)),
            pl.BlockSpec((None, PAST, SWKV_W), lambda b, j: (b, 0, 0)),
            pl.BlockSpec((None, PAST, SWKV_W), lambda b, j: (b, 0, 0)),
        ],
        out_specs=pl.BlockSpec((ABLK, SWQ_W), lambda b, j: (b * nb + j, 0)),
        out_shape=jax.ShapeDtypeStruct((N_S, SWQ_W), BF16),
        compiler_params=_cparams("parallel", "arbitrary"),
        name="sw_attn",
    )(sink, q, k, v, kc, vc)


def _odd_out_kernel(nap_ref, nas_ref, swp_ref, sws_ref, x_ref, mod_ref, wo_ref, o_ref):
    i = pl.program_id(0)
    r = _mod_row(i)
    is_p = i < NPT
    ona = jnp.where(is_p, nap_ref[...], nas_ref[...])
    osw = jnp.where(is_p, swp_ref[...], sws_ref[...])
    y = _dot(ona, wo_ref[:NA_W, :]) + _dot(osw, wo_ref[NA_W:, :])
    o_ref[...] = x_ref[...] + mod_ref[2, pl.ds(r, 1), :] * y


def _odd_out(nap, nas, swp, sws, x, mod_l, w_out):
    return pl.pallas_call(
        _odd_out_kernel,
        grid=(NT,),
        in_specs=_dual_specs(TM, NA_W, NPT) + _dual_specs(TM, SWQ_W, NPT) + [
            pl.BlockSpec((TM, D), lambda i: (i, 0)),
            pl.BlockSpec((6, MOD_ROWS, D), lambda i: (0, 0, 0)),
            pl.BlockSpec((D, D), lambda i: (0, 0)),
        ],
        out_specs=pl.BlockSpec((TM, D), lambda i: (i, 0)),
        out_shape=jax.ShapeDtypeStruct((N_TOK, D), F32),
        compiler_params=_cparams("parallel"),
        name="odd_out",
    )(nap, nas, swp, sws, x, mod_l, w_out)


BT = 2048
NB = N_TOK // BT
TPB = BT // TM
CH = 256
XO_ROWS = 2 * BT + CH
ROW = 8
assert D == ROW * 128


def _route_sparse(probs, sel):
    rank = []
    score = []
    for g in range(N_GRP):
        ids = range(g * EPG, (g + 1) * EPG)
        tot = None
        for e in ids:
            rk = jnp.zeros_like(sel[e])
            for j in ids:
                if j < e:
                    rk = rk + (sel[j] >= sel[e]).astype(F32)
                elif j > e:
                    rk = rk + (sel[j] > sel[e]).astype(F32)
            rank.append(rk)
            contrib = jnp.where(rk < 2.0, sel[e], 0.0)
            tot = contrib if tot is None else tot + contrib
        score.append(tot)
    top1, top2 = [], []
    for g in range(N_GRP):
        best = None
        for j in range(N_GRP):
            if j == g:
                continue
            c = (score[g] > score[j]) if j < g else (score[g] >= score[j])
            best = c if best is None else jnp.logical_and(best, c)
        for e in range(g * EPG, (g + 1) * EPG):
            top1.append(jnp.logical_and(best, rank[e] == 0.0))
            top2.append(jnp.logical_and(best, rank[e] == 1.0))
    return top1, top2


def _pick(masks, rows):
    acc = None
    for m, r in zip(masks, rows):
        v = jnp.where(m, r, 0.0)
        acc = v if acc is None else acc + v
    return acc


def _router_kernel(x_ref, mod_ref, g_ref, rw_ref, rb_ref, h_ref, pos_ref, w_ref, cnt_ref, off_ref,
                   meta, carry):
    i = pl.program_id(0)
    j = i % TPB
    r = _mod_row(i)
    h = _rms_mod(x_ref[...], g_ref[...], mod_ref[3, pl.ds(r, 1), :], mod_ref[4, pl.ds(r, 1), :])
    h_ref[...] = h.astype(BF16)
    logits = jnp.dot(h, rw_ref[...], preferred_element_type=F32, precision=lax.Precision.HIGHEST)
    lt = logits.T[:N_EXP, :]
    m = jnp.max(lt, axis=0, keepdims=True)
    ex = jnp.exp(lt - m)
    pr = ex / jnp.sum(ex, axis=0, keepdims=True)
    se = pr + rb_ref[:N_EXP, :]
    probs = [pr[e:e + 1, :] for e in range(N_EXP)]
    sel = [se[e:e + 1, :] for e in range(N_EXP)]
    top1, top2 = _route_sparse(probs, sel)

    @pl.when(j == 0)
    def _():
        carry[...] = jnp.zeros_like(carry)

    member = jnp.concatenate([jnp.logical_or(a, b).astype(F32) for a, b in zip(top1, top2)], axis=0)
    s_idx = lax.broadcasted_iota(jnp.int32, (TM, TM), 0)
    t_idx = lax.broadcasted_iota(jnp.int32, (TM, TM), 1)
    before = jnp.where(s_idx < t_idx, 1.0, 0.0).astype(BF16)
    seen = _dot(member.astype(BF16), before) + carry[:, 0:1]
    seen_rows = [seen[e:e + 1, :] for e in range(N_EXP)]
    ids = [jnp.full((1, TM), float(e), F32) for e in range(N_EXP)]
    p1 = _pick(top1, probs)
    p2 = _pick(top2, probs)
    den = p1 + p2
    w_ref[j, 0:1, :] = p1 / den
    w_ref[j, 1:2, :] = p2 / den
    meta[j, 0:1, :] = _pick(top1, ids)
    meta[j, 1:2, :] = _pick(top2, ids)
    meta[j, 2:3, :] = _pick(top1, seen_rows)
    meta[j, 3:4, :] = _pick(top2, seen_rows)
    carry[...] = carry[...] + jnp.sum(member, axis=1, keepdims=True)

    @pl.when(j == TPB - 1)
    def _():
        cnt = carry[...]
        offs = [jnp.zeros((1, 128), F32)]
        for e in range(1, N_EXP):
            offs.append(offs[-1] + cnt[e - 1:e, :])
        cnt_ref[...] = cnt
        off_ref[...] = jnp.concatenate(offs, axis=0)
        for jj in range(TPB):
            for k in range(2):
                eid = meta[jj, k:k + 1, :]
                pos = meta[jj, 2 + k:3 + k, :]
                for e in range(1, N_EXP):
                    pos = pos + jnp.where(eid == float(e), offs[e][:, 0:1], 0.0)
                pos_ref[jj, k:k + 1, :] = pos.astype(jnp.int32)


def _router(x, mod_l, g, rw_pad, rb_col):
    blk = lambda i: (i // TPB, 0, 0, 0)
    return pl.pallas_call(
        _router_kernel,
        grid=(NT,),
        in_specs=[
            pl.BlockSpec((TM, D), lambda i: (i, 0)),
            pl.BlockSpec((6, MOD_ROWS, D), lambda i: (0, 0, 0)),
            pl.BlockSpec((1, D), lambda i: (0, 0)),
            pl.BlockSpec((D, 128), lambda i: (0, 0)),
            pl.BlockSpec((128, 1), lambda i: (0, 0)),
        ],
        out_specs=[
            pl.BlockSpec((TM, D), lambda i: (i, 0)),
            pl.BlockSpec((None, TPB, 2, TM), blk),
            pl.BlockSpec((None, TPB, 2, TM), blk),
            pl.BlockSpec((None, N_EXP, 128), lambda i: (i // TPB, 0, 0)),
            pl.BlockSpec((None, N_EXP, 128), lambda i: (i // TPB, 0, 0)),
        ],
        out_shape=[
            jax.ShapeDtypeStruct((N_TOK, D), BF16),
            jax.ShapeDtypeStruct((NB, TPB, 2, TM), jnp.int32),
            jax.ShapeDtypeStruct((NB, TPB, 2, TM), F32),
            jax.ShapeDtypeStruct((NB, N_EXP, 128), F32),
            jax.ShapeDtypeStruct((NB, N_EXP, 128), F32),
        ],
        scratch_shapes=[pltpu.VMEM((TPB, 4, TM), F32), pltpu.VMEM((N_EXP, 128), F32)],
        compiler_params=_cparams("arbitrary"),
        name="router",
    )(x, mod_l, g, rw_pad, rb_col)


def _row(p):
    return pl.ds(pl.multiple_of(p * ROW, ROW), ROW)


def _experts_kernel(cnt_ref, off_ref, pos_ref, w_ref, h_ref, wg_ref, wu_ref, wd_ref, x_ref, mod_ref, o_ref,
                    xo, stg):
    b = pl.program_id(0)
    e = pl.program_id(1)

    @pl.when(e == 0)
    def _():
        xo[pl.ds(2 * BT * ROW, CH * ROW), :] = jnp.zeros((CH * ROW, 128), F32)

        def sub(j, carry):
            hs = h_ref[pl.ds(pl.multiple_of(j * TM, TM), TM), :].astype(F32)
            for c in range(ROW):
                stg[pl.ds(c, TM, stride=ROW), :] = hs[:, c * 128:(c + 1) * 128]
            for t in range(TM):
                v = stg[t * ROW:(t + 1) * ROW, :]
                xo[_row(pos_ref[j, 0, t]), :] = v
                xo[_row(pos_ref[j, 1, t]), :] = v
            return carry

        lax.fori_loop(0, TPB, sub, 0)

    n = cnt_ref[b, e]
    base = off_ref[b, e]

    def chunk(jc, carry):
        row0 = base + jc * CH
        x = jnp.concatenate([xo[pl.ds(row0 * ROW + c, CH, stride=ROW), :] for c in range(ROW)], axis=1)
        xb = x.astype(BF16)
        a = _dot(xb, wg_ref[...])
        u = _dot(xb, wu_ref[...])
        hid = (a / (1.0 + jnp.exp(-a))) * u
        out = _dot(hid.astype(BF16), wd_ref[...])
        valid = lax.broadcasted_iota(jnp.int32, (CH, 1), 0) < n - jc * CH
        res = jnp.where(valid, out, x)
        for c in range(ROW):
            xo[pl.ds(row0 * ROW + c, CH, stride=ROW), :] = res[:, c * 128:(c + 1) * 128]
        return carry

    lax.fori_loop(0, (n + CH - 1) // CH, chunk, 0)

    @pl.when(e == N_EXP - 1)
    def _():
        g2 = mod_ref[5, pl.ds(_mod_row(b * TPB), 1), :]

        def sub(j, carry):
            for t in range(TM):
                a = xo[_row(pos_ref[j, 0, t]), :]
                u = xo[_row(pos_ref[j, 1, t]), :]
                stg[t * ROW:(t + 1) * ROW, :] = w_ref[j, 0, t] * a + w_ref[j, 1, t] * u
            y = jnp.concatenate([stg[pl.ds(c, TM, stride=ROW), :] for c in range(ROW)], axis=1)
            rows = pl.ds(pl.multiple_of(j * TM, TM), TM)
            o_ref[rows, :] = x_ref[rows, :] + g2 * y
            return carry

        lax.fori_loop(0, TPB, sub, 0)


def _experts(cnt, off, pos, wts, h, wg, wu, wd, x, mod_l):
    smem_blk = pl.BlockSpec((None, TPB, 2, TM), lambda b, e, *_: (b, 0, 0, 0), memory_space=pltpu.SMEM)
    grid_spec = pltpu.PrefetchScalarGridSpec(
        num_scalar_prefetch=2,
        grid=(NB, N_EXP),
        in_specs=[
            smem_blk,
            smem_blk,
            pl.BlockSpec((BT, D), lambda b, e, *_: (b, 0)),
            pl.BlockSpec((None, D, D_EXP), lambda b, e, *_: (e, 0, 0)),
            pl.BlockSpec((None, D, D_EXP), lambda b, e, *_: (e, 0, 0)),
            pl.BlockSpec((None, D_EXP, D), lambda b, e, *_: (e, 0, 0)),
            pl.BlockSpec((BT, D), lambda b, e, *_: (b, 0), pipeline_mode=pl.Buffered(1)),
            pl.BlockSpec((6, MOD_ROWS, D), lambda b, e, *_: (0, 0, 0)),
        ],
        out_specs=pl.BlockSpec((BT, D), lambda b, e, *_: (b, 0), pipeline_mode=pl.Buffered(1)),
        scratch_shapes=[pltpu.VMEM((XO_ROWS * ROW, 128), F32), pltpu.VMEM((TM * ROW, 128), F32)],
    )
    return pl.pallas_call(
        _experts_kernel,
        grid_spec=grid_spec,
        out_shape=jax.ShapeDtypeStruct((N_TOK, D), F32),
        compiler_params=_cparams("arbitrary", "arbitrary"),
        name="experts",
    )(cnt, off, pos, wts, h, wg, wu, wd, x, mod_l)


def _moe_sparse(x, mod_l, g, rw_pad, rb_col, wg, wu, wd):
    h, pos, wts, cnt, off = _router(x, mod_l, g, rw_pad, rb_col)
    cnt = cnt[:, :, 0].astype(jnp.int32)
    off = off[:, :, 0].astype(jnp.int32)
    return _experts(cnt, off, pos, wts, h, wg, wu, wd, x, mod_l)


def _final_kernel(x_ref, g_ref, o_ref):
    x = x_ref[...]
    ms = jnp.mean(x * x, axis=-1, keepdims=True)
    o_ref[...] = x * lax.rsqrt(ms + EPS) * g_ref[...]


def _final_norm(x, g, *, prompt):
    tile0 = 0 if prompt else NPT
    nt = NPT if prompt else NT - NPT
    return pl.pallas_call(
        _final_kernel,
        grid=(nt,),
        in_specs=[pl.BlockSpec((TM, D), lambda i: (i + tile0, 0)), pl.BlockSpec((1, D), lambda i: (0, 0))],
        out_specs=pl.BlockSpec((TM, D), lambda i: (i, 0)),
        out_shape=jax.ShapeDtypeStruct((nt * TM, D), F32),
        compiler_params=_cparams("parallel"),
        name="final_norm_prompt" if prompt else "final_norm_latent",
    )(x, g)


def kernel(x_prompt, x_sample, cache_na_k, cache_na_v, cache_sw_k, cache_sw_v, c, c_ctx, mod_w, mod_b, norm_mix_g, norm_ffn_g, ev_w_in, ev_pool_w, ev_pool_scale, ev_conv_w, ev_w_out, od_w_in, od_rpb, od_sink, od_w_out, router_w, router_b, moe_w_gate, moe_w_up, moe_w_down, final_norm_g):
    xp = x_prompt.reshape(N_P, D)
    xs = x_sample.reshape(N_S, D)
    cvec = jnp.concatenate([c_ctx[None, :], c, jnp.zeros((MOD_ROWS - 1 - DEC_BATCH, D), F32)], axis=0)
    mod = _modulation(cvec, mod_w, mod_b)

    rw_pad = jnp.pad(router_w, ((0, 0), (0, 128 - N_EXP)))
    rb_col = jnp.pad(router_b, (0, 128 - N_EXP)).reshape(128, 1)

    z = _even_in(xp, xs, mod[0], norm_mix_g[0:1], ev_w_in[0].astype(BF16))
    conv_w = jnp.pad(ev_conv_w[0], ((0, 8 - ev_conv_w.shape[1]), (0, 0)))
    x = _even_mix(z, xp, xs, mod[0], ev_pool_w[0].astype(BF16), ev_pool_scale[0:1], conv_w,
                  ev_w_out[0].astype(BF16))
    x = _moe_sparse(x, mod[0], norm_ffn_g[0:1], rw_pad, rb_col, moe_w_gate[0].astype(BF16),
                    moe_w_up[0].astype(BF16), moe_w_down[0].astype(BF16))

    cos, sin = _rope_tables()
    w_in = od_w_in[0].astype(BF16)
    g1 = norm_mix_g[1:2]
    qna_p, kna_p, vna_p, qsw_p, ksw_p, vsw_p, nak, nav, swk, swv = _odd_in(x, mod[1], g1, w_in, cos, sin, prompt=True)
    qna_s, kna_s, vna_s, qsw_s, ksw_s, vsw_s = _odd_in(x, mod[1], g1, w_in, cos, sin, prompt=False)
    sink = od_sink[0]
    ona_p, osw_p = _ctx_attn(sink, qna_p, kna_p, vna_p, qsw_p, ksw_p, vsw_p)
    nd = 2 * NA_ROWS - 1
    bias = _na_bias(od_rpb[0])
    bias = jnp.stack([bias[:, cidx:cidx + NA_ROWS] for cidx in range(NA_ROWS)], axis=1)
    bias = bias.transpose(0, 1, 3, 2, 4).reshape(NA_H, NA_ROWS, GRID_W, NA_ROWS * GRID_W)
    ona_s = _na_attn(qna_s, kna_s, vna_s,
                     cache_na_k[:, 0].reshape(DEC_BATCH, PAST, NA_W).astype(BF16),
                     cache_na_v[:, 0].reshape(DEC_BATCH, PAST, NA_W).astype(BF16), bias)
    osw_s = _sw_attn(sink, qsw_s, ksw_s, vsw_s,
                     cache_sw_k[:, 0].reshape(DEC_BATCH, PAST, SWKV_W).astype(BF16),
                     cache_sw_v[:, 0].reshape(DEC_BATCH, PAST, SWKV_W).astype(BF16))
    x = _odd_out(ona_p, ona_s, osw_p, osw_s, x, mod[1], od_w_out[0].astype(BF16))
    x = _moe_sparse(x, mod[1], norm_ffn_g[1:2], rw_pad, rb_col, moe_w_gate[1].astype(BF16),
                    moe_w_up[1].astype(BF16), moe_w_down[1].astype(BF16))

    fg = final_norm_g.reshape(1, D)
    y_prompt = _final_norm(x, fg, prompt=True).reshape(BATCH, SEQ, D)
    y_sample = _final_norm(x, fg, prompt=False).reshape(DEC_BATCH, DEC_SEQ, D)
    new_na_k = nak.reshape(BATCH, 1, SEQ, NA_H, DH)
    new_na_v = nav.reshape(BATCH, 1, SEQ, NA_H, DH)
    new_sw_k = swk.reshape(BATCH, 1, SEQ, SW_KV, DH)
    new_sw_v = swv.reshape(BATCH, 1, SEQ, SW_KV, DH)
    return (y_prompt, y_sample, new_na_k, new_na_v, new_sw_k, new_sw_v)
```

```python
import functools

import jax
import jax.numpy as jnp
from jax import lax
from jax.experimental import pallas as pl
from jax.experimental.pallas import tpu as pltpu

D = 1024
BATCH = 16
SEQ = 256
DEC_BATCH = 4
DEC_SEQ = 4096
PAST = 512
GRID_W = 64
ROWS = DEC_SEQ // GRID_W
DH = 64
POOL_W = 512
POOL_WINDOWS = (2, 4, 8, 16)
POOL_GW = 128
CONV_W = 512
EVEN_IN = POOL_W + 3 * CONV_W
NA_H = 8
NA_ROWS = 8
NA_COLS = 16
SW_H = 8
SW_KV = 2
SW_G = SW_H // SW_KV
SW_WIN = 128
ABLK = 128
NA_W = NA_H * DH
SWQ_W = SW_H * DH
SWKV_W = SW_KV * DH
ODD_IN = 3 * NA_W + SWQ_W + 2 * SWKV_W
N_EXP = 16
N_GRP = 4
EPG = 4
D_EXP = 512
EPS = 1e-6
NEG = -1e30
ROPE_BASE = 10000.0
QK_SCALE = DH ** -0.5
assert QK_SCALE == 0.125

N_P = BATCH * SEQ
N_S = DEC_BATCH * DEC_SEQ
N_TOK = N_P + N_S
MOD_ROWS = 8

TM = 256
NPT = N_P // TM
TPS = DEC_SEQ // TM
NT = N_TOK // TM
HALO = 8

F32 = jnp.float32
BF16 = jnp.bfloat16
VMEM_LIMIT = 56 * 1024 * 1024


def _cparams(*sem):
    return pltpu.CompilerParams(dimension_semantics=sem, vmem_limit_bytes=VMEM_LIMIT)


def _mod_row(i):
    return jnp.where(i < NPT, 0, 1 + (i - NPT) // TPS)


def _rms_mod(x, g, shift, scale):
    ms = jnp.mean(x * x, axis=-1, keepdims=True)
    y = x * lax.rsqrt(ms + EPS) * g
    return y * (1.0 + scale) + shift


def _dot(a, b):
    return jnp.dot(a, b, preferred_element_type=F32)


def _dot_nt(a, b):
    return lax.dot_general(a, b, (((1,), (1,)), ((), ())), preferred_element_type=F32)


def _mod_kernel(cv_ref, w_ref, b_ref, o_ref):
    cv = cv_ref[...]
    a = cv / (1.0 + jnp.exp(-cv))
    o_ref[...] = jnp.dot(a, w_ref[...], preferred_element_type=F32,
                         precision=lax.Precision.HIGHEST) + b_ref[...]


def _modulation(cvec, mod_w, mod_b):
    depth = mod_w.shape[0]
    return pl.pallas_call(
        _mod_kernel,
        grid=(depth, 6),
        in_specs=[
            pl.BlockSpec((MOD_ROWS, D), lambda l, j: (0, 0)),
            pl.BlockSpec((None, D, D), lambda l, j: (l, 0, j)),
            pl.BlockSpec((None, None, 1, D), lambda l, j: (l, j, 0, 0)),
        ],
        out_specs=pl.BlockSpec((None, None, MOD_ROWS, D), lambda l, j: (l, j, 0, 0)),
        out_shape=jax.ShapeDtypeStruct((depth, 6, MOD_ROWS, D), F32),
        compiler_params=_cparams("arbitrary", "arbitrary"),
        name="modulation",
    )(cvec, mod_w, mod_b.reshape(depth, 6, 1, D))


def _dual_specs(tm, width, npt):
    return [
        pl.BlockSpec((tm, width), lambda i: (jnp.minimum(i, npt - 1), 0)),
        pl.BlockSpec((tm, width), lambda i: (jnp.maximum(i - npt, 0), 0)),
    ]


def _even_kernel(xp_ref, xs_ref, xprev_ref, xnext_ref, mod_ref, g_ref, wi_ref, pw_ref, ps_ref, cw_ref,
                 wo_ref, o_ref, pext, uext):
    i = pl.program_id(0)
    r = _mod_row(i)
    is_p = i < NPT
    t0 = jnp.where(is_p, 0, ((i - NPT) % TPS) * TM)
    seq_len = jnp.where(is_p, SEQ, DEC_SEQ)
    first = t0 == 0
    last = t0 + TM == seq_len

    x = jnp.where(is_p, xp_ref[...], xs_ref[...])
    xe = jnp.concatenate([xprev_ref[...], x, xnext_ref[...]], axis=0)
    h = _rms_mod(xe, g_ref[...], mod_ref[0, pl.ds(r, 1), :], mod_ref[1, pl.ds(r, 1), :])
    ze = _dot(h.astype(BF16), wi_ref[...])
    row = lax.broadcasted_iota(jnp.int32, (TM + 2 * HALO, 1), 0)
    outside = jnp.logical_or(jnp.logical_and(first, row < HALO), jnp.logical_and(last, row >= HALO + TM))
    ze = jnp.where(outside, 0.0, ze)
    pext[...] = ze[:, :POOL_W]
    uext[...] = ze[:, POOL_W + 2 * CONV_W:] * ze[:, POOL_W:POOL_W + CONV_W]
    p = ze[HALO:HALO + TM, :POOL_W]
    gb = ze[HALO:HALO + TM, POOL_W + CONV_W:POOL_W + 2 * CONV_W]

    t = t0 + lax.broadcasted_iota(jnp.int32, (TM, 1), 0)
    y = jnp.zeros((TM, D), F32)
    for g, w in enumerate(POOL_WINDOWS):
        cols = slice(g * POOL_GW, (g + 1) * POOL_GW)
        acc = jnp.zeros((TM, POOL_GW), F32)
        for k in range(-(w // 2), w - (w // 2)):
            acc = acc + pext[pl.ds(HALO + k, TM), cols]
        lo = jnp.maximum(t - w // 2, 0)
        hi = jnp.minimum(t + (w - 1 - w // 2), seq_len - 1)
        cnt = (hi - lo + 1).astype(F32)
        d = acc / cnt - p[:, cols]
        a = _dot(d.astype(BF16), pw_ref[g]) * ps_ref[:, cols]
        y = y + _dot(a.astype(BF16), wo_ref[cols, :])
    conv = (uext[pl.ds(HALO - 1, TM), :] * cw_ref[0:1, :] + uext[pl.ds(HALO, TM), :] * cw_ref[1:2, :]
            + uext[pl.ds(HALO + 1, TM), :] * cw_ref[2:3, :])
    b = gb * conv
    y = y + _dot(b.astype(BF16), wo_ref[POOL_W:, :])
    o_ref[...] = x + mod_ref[2, pl.ds(r, 1), :] * y


def _even_layer(xp, xs, mod_l, g, w_in, pool_w, pool_scale, conv_w, w_out):
    hb = TM // HALO
    nhb = N_S // HALO
    return pl.pallas_call(
        _even_kernel,
        grid=(NT,),
        in_specs=_dual_specs(TM, D, NPT) + [
            pl.BlockSpec((HALO, D), lambda i: (jnp.maximum((i - NPT) * hb - 1, 0), 0)),
            pl.BlockSpec((HALO, D), lambda i: (jnp.clip((i - NPT + 1) * hb, 0, nhb - 1), 0)),
            pl.BlockSpec((6, MOD_ROWS, D), lambda i: (0, 0, 0)),
            pl.BlockSpec((1, D), lambda i: (0, 0)),
            pl.BlockSpec((D, EVEN_IN), lambda i: (0, 0)),
            pl.BlockSpec((4, POOL_GW, POOL_GW), lambda i: (0, 0, 0)),
            pl.BlockSpec((1, POOL_W), lambda i: (0, 0)),
            pl.BlockSpec((8, CONV_W), lambda i: (0, 0)),
            pl.BlockSpec((D, D), lambda i: (0, 0)),
        ],
        out_specs=pl.BlockSpec((TM, D), lambda i: (i, 0)),
        out_shape=jax.ShapeDtypeStruct((N_TOK, D), F32),
        scratch_shapes=[pltpu.VMEM((TM + 2 * HALO, POOL_W), F32),
                        pltpu.VMEM((TM + 2 * HALO, CONV_W), F32)],
        compiler_params=_cparams("parallel"),
        name="even_layer",
    )(xp, xs, xs, xs, mod_l, g, w_in, pool_w, pool_scale, conv_w, w_out)


def _rope(x, cos, sin_signed):
    n = x.shape[1] // 128
    cosf = jnp.concatenate([cos] * n, axis=1) if n > 1 else cos
    sinf = jnp.concatenate([sin_signed] * n, axis=1) if n > 1 else sin_signed
    w = x.shape[1]
    lane = lax.broadcasted_iota(jnp.int32, x.shape, 1)
    up = pltpu.roll(x, w - DH // 4, 1)
    dn = pltpu.roll(x, DH // 4, 1)
    rot = jnp.where((lane % (DH // 2)) < DH // 4, up, dn)
    return x * cosf + rot * sinf


def _odd_in_kernel(x_ref, mod_ref, g_ref, w_ref, cos_ref, sin_ref, *out_refs, tile0, rope, kv_f32):
    i = pl.program_id(0) + tile0
    r = _mod_row(i)
    h = _rms_mod(x_ref[...], g_ref[...], mod_ref[0, pl.ds(r, 1), :], mod_ref[1, pl.ds(r, 1), :])
    z = _dot(h.astype(BF16), w_ref[...])
    qna_ref, kna_ref, vna_ref, qsw_ref, ksw_ref, vsw_ref = out_refs[:6]
    c0 = 0
    qna = z[:, 0:NA_W]
    kna = z[:, NA_W:2 * NA_W]
    vna = z[:, 2 * NA_W:3 * NA_W]
    c0 = 3 * NA_W
    qsw = z[:, c0:c0 + SWQ_W]
    ksw = z[:, c0 + SWQ_W:c0 + SWQ_W + SWKV_W]
    vsw = z[:, c0 + SWQ_W + SWKV_W:]
    if kv_f32:
        out_refs[6][...] = kna
        out_refs[7][...] = vna
        out_refs[8][...] = ksw
        out_refs[9][...] = vsw
    if rope:
        cos = cos_ref[...]
        sin = sin_ref[...]
        qsw = _rope(qsw, cos, sin)
        ksw = _rope(ksw, cos, sin)
    qna_ref[...] = (qna * QK_SCALE).astype(BF16)
    kna_ref[...] = kna.astype(BF16)
    vna_ref[...] = vna.astype(BF16)
    qsw_ref[...] = (qsw * QK_SCALE).astype(BF16)
    ksw_ref[...] = ksw.astype(BF16)
    vsw_ref[...] = vsw.astype(BF16)


def _odd_in(x, mod_l, g, w, cos, sin, *, prompt):
    tile0 = 0 if prompt else NPT
    nt = NPT if prompt else NT - NPT
    n = nt * TM
    widths = [NA_W, NA_W, NA_W, SWQ_W, SWKV_W, SWKV_W]
    out_shape = [jax.ShapeDtypeStruct((n, w_), BF16) for w_ in widths]
    out_specs = [pl.BlockSpec((TM, w_), lambda i: (i, 0)) for w_ in widths]
    if prompt:
        for w_ in (NA_W, NA_W, SWKV_W, SWKV_W):
            out_shape.append(jax.ShapeDtypeStruct((n, w_), F32))
            out_specs.append(pl.BlockSpec((TM, w_), lambda i: (i, 0)))
    return pl.pallas_call(
        functools.partial(_odd_in_kernel, tile0=tile0, rope=not prompt, kv_f32=prompt),
        grid=(nt,),
        in_specs=[
            pl.BlockSpec((TM, D), lambda i: (i + tile0, 0)),
            pl.BlockSpec((6, MOD_ROWS, D), lambda i: (0, 0, 0)),
            pl.BlockSpec((1, D), lambda i: (0, 0)),
            pl.BlockSpec((D, ODD_IN), lambda i: (0, 0)),
            pl.BlockSpec((TM, 128), lambda i: (i % TPS, 0)),
            pl.BlockSpec((TM, 128), lambda i: (i % TPS, 0)),
        ],
        out_specs=out_specs,
        out_shape=out_shape,
        compiler_params=_cparams("parallel"),
        name="odd_in_prompt" if prompt else "odd_in_latent",
    )(x, mod_l, g, w, cos, sin)


def _rope_tables():
    t = jnp.arange(DEC_SEQ)
    quarter = DH // 4
    inv = 1.0 / (ROPE_BASE ** (jnp.arange(quarter, dtype=F32) / quarter))

    def cos_sin(pos):
        ang = pos.astype(F32)[:, None] * inv[None, :]
        ang = jnp.concatenate([ang, ang], axis=-1)
        return jnp.cos(ang), jnp.sin(ang)

    cr, sr = cos_sin(t // GRID_W)
    cc, sc = cos_sin(t % GRID_W)
    cos = jnp.concatenate([cr, cc], axis=-1)
    sin = jnp.concatenate([sr, sc], axis=-1)
    sign = jnp.where((jnp.arange(DH) % (DH // 2)) < DH // 4, -1.0, 1.0).astype(F32)
    sin = sin * sign[None, :]
    return jnp.concatenate([cos, cos], axis=-1), jnp.concatenate([sin, sin], axis=-1)


def _softmax_pv(segs, sink=None):
    m = None
    for s, _ in segs:
        sm = jnp.max(s, axis=-1, keepdims=True)
        m = sm if m is None else jnp.maximum(m, sm)
    if sink is not None:
        m = jnp.maximum(m, sink)
    den = None
    acc = None
    for s, v in segs:
        p = jnp.exp(s - m)
        ps = jnp.sum(p, axis=-1, keepdims=True)
        den = ps if den is None else den + ps
        pv = _dot(p.astype(BF16), v)
        acc = pv if acc is None else acc + pv
    if sink is not None:
        den = den + jnp.exp(sink - m)
    return acc / den


def _sink_col(sink_ref, g, rows_per_head):
    row = lax.broadcasted_iota(jnp.int32, (SW_G * rows_per_head, 1), 0)
    col = jnp.zeros((SW_G * rows_per_head, 1), F32)
    for r in range(SW_G):
        col = jnp.where(row // rows_per_head == r, sink_ref[g * SW_G + r], col)
    return col


def _ctx_attn_kernel(sink_ref, qna_ref, kna_ref, vna_ref, qsw_ref, ksw_ref, vsw_ref, ona_ref, osw_ref):
    outs = []
    for h in range(NA_H):
        cols = slice(h * DH, (h + 1) * DH)
        s = _dot_nt(qna_ref[:, cols], kna_ref[:, cols])
        outs.append(_softmax_pv([(s, vna_ref[:, cols])]))
    ona_ref[...] = jnp.concatenate(outs, axis=1).astype(BF16)
    outs = []
    for g in range(SW_KV):
        kc = slice(g * DH, (g + 1) * DH)
        q = jnp.concatenate([qsw_ref[:, (g * SW_G + r) * DH:(g * SW_G + r + 1) * DH] for r in range(SW_G)], axis=0)
        s = _dot_nt(q, ksw_ref[:, kc])
        o = _softmax_pv([(s, vsw_ref[:, kc])], sink=_sink_col(sink_ref, g, SEQ))
        outs.extend(o[r * SEQ:(r + 1) * SEQ, :] for r in range(SW_G))
    osw_ref[...] = jnp.concatenate(outs, axis=1).astype(BF16)


def _ctx_attn(sink, qna, kna, vna, qsw, ksw, vsw):
    def spec(w):
        return pl.BlockSpec((SEQ, w), lambda b: (b, 0))

    return pl.pallas_call(
        _ctx_attn_kernel,
        grid=(BATCH,),
        in_specs=[pl.BlockSpec(memory_space=pltpu.SMEM), spec(NA_W), spec(NA_W), spec(NA_W),
                  spec(SWQ_W), spec(SWKV_W), spec(SWKV_W)],
        out_specs=[spec(NA_W), spec(SWQ_W)],
        out_shape=[jax.ShapeDtypeStruct((N_P, NA_W), BF16), jax.ShapeDtypeStruct((N_P, SWQ_W), BF16)],
        compiler_params=_cparams("parallel"),
        name="ctx_attn",
    )(sink, qna, kna, vna, qsw, ksw, vsw)


def _na_bias_kernel(rpb_ref, o_ref):
    h = pl.program_id(0)
    cq = lax.broadcasted_iota(jnp.int32, (GRID_W, GRID_W), 0)
    ck = lax.broadcasted_iota(jnp.int32, (GRID_W, GRID_W), 1)
    cstart = jnp.clip(cq - NA_COLS // 2, 0, GRID_W - NA_COLS)
    ok = (ck >= cstart) & (ck < cstart + NA_COLS)
    dc = jnp.clip(ck - cq + NA_COLS - 1, 0, 2 * NA_COLS - 2)
    ndc = 2 * NA_COLS - 1
    for d in range(2 * NA_ROWS - 1):
        def body(e, acc):
            return jnp.where(dc == e, rpb_ref[h, d * ndc + e], acc)
        b = lax.fori_loop(0, ndc, body, jnp.zeros((GRID_W, GRID_W), F32))
        o_ref[d] = jnp.where(ok, b, NEG)


def _na_bias(rpb):
    nd = 2 * NA_ROWS - 1
    return pl.pallas_call(
        _na_bias_kernel,
        grid=(NA_H,),
        in_specs=[pl.BlockSpec(memory_space=pltpu.SMEM)],
        out_specs=pl.BlockSpec((None, nd, GRID_W, GRID_W), lambda h: (h, 0, 0, 0)),
        out_shape=jax.ShapeDtypeStruct((NA_H, nd, GRID_W, GRID_W), F32),
        compiler_params=_cparams("parallel"),
        name="na_bias",
    )(rpb.reshape(NA_H, nd * (2 * NA_COLS - 1)))


NA_QB = 4
NA_KR = 12
NA_NQB = ROWS // NA_QB


def _na_bias_blocks(bias):
    neg = jnp.full((NA_H, GRID_W, GRID_W), NEG, F32)
    cases = []
    for case in range(3):
        rows = []
        for i in range(NA_QB):
            blocks = []
            for j in range(NA_KR):
                if case == 0:
                    valid, dr = j < NA_ROWS, j - i + NA_ROWS - 1
                elif case == 1:
                    valid, dr = i <= j < i + NA_ROWS, j - i + NA_ROWS // 2 - 1
                else:
                    valid, dr = j >= NA_KR - NA_ROWS, j - i + NA_ROWS - 1 - (NA_KR - NA_QB)
                blocks.append(bias[:, dr] if valid else neg)
            rows.append(jnp.concatenate(blocks, axis=2))
        cases.append(jnp.concatenate(rows, axis=1))
    return jnp.stack(cases)


def _na_kernel(q_ref, k_ref, v_ref, kc_ref, vc_ref, bias_ref, o_ref):
    r0 = pl.program_id(1) * NA_QB
    start = pl.multiple_of(jnp.clip(r0 - NA_ROWS // 2, 0, ROWS - NA_KR) * GRID_W, GRID_W)
    nq = NA_QB * GRID_W
    lane = lax.broadcasted_iota(jnp.int32, (nq, 2 * DH), 1)
    for p in range(NA_H // 2):
        cols = slice(p * 2 * DH, (p + 1) * 2 * DH)
        q = q_ref[:, cols]
        kl = k_ref[pl.ds(start, NA_KR * GRID_W), cols]
        vl = v_ref[pl.ds(start, NA_KR * GRID_W), cols]
        kc = kc_ref[:, cols]
        vc = vc_ref[:, cols]
        outs = []
        for half in range(2):
            mine = (lane < DH) if half == 0 else (lane >= DH)
            qm = jnp.where(mine, q, jnp.zeros_like(q))
            s_loc = _dot_nt(qm, kl) + bias_ref[2 * p + half]
            s_ctx = _dot_nt(qm, kc)
            outs.append(_softmax_pv([(s_loc, vl), (s_ctx, vc)]))
        o_ref[:, cols] = jnp.where(lane < DH, outs[0], outs[1]).astype(BF16)


def _na_attn(q, k, v, kc, vc, bias):
    nq = NA_QB * GRID_W

    def bias_case(b, rb):
        return (jnp.where(rb == 0, 0, jnp.where(rb == NA_NQB - 1, 2, 1)), 0, 0, 0)

    return pl.pallas_call(
        _na_kernel,
        grid=(DEC_BATCH, NA_NQB),
        in_specs=[
            pl.BlockSpec((nq, NA_W), lambda b, rb: (b * NA_NQB + rb, 0)),
            pl.BlockSpec((DEC_SEQ, NA_W), lambda b, rb: (b, 0)),
            pl.BlockSpec((DEC_SEQ, NA_W), lambda b, rb: (b, 0)),
            pl.BlockSpec((None, PAST, NA_W), lambda b, rb: (b, 0, 0)),
            pl.BlockSpec((None, PAST, NA_W), lambda b, rb: (b, 0, 0)),
            pl.BlockSpec((None, NA_H, nq, NA_KR * GRID_W), bias_case),
        ],
        out_specs=pl.BlockSpec((nq, NA_W), lambda b, rb: (b * NA_NQB + rb, 0)),
        out_shape=jax.ShapeDtypeStruct((N_S, NA_W), BF16),
        compiler_params=_cparams("parallel", "arbitrary"),
        name="na_attn",
    )(q, k, v, kc, vc, bias)


def _sw_kernel(sink_ref, q_ref, k_ref, v_ref, kc_ref, vc_ref, o_ref):
    j = pl.program_id(1)
    nk = 3 * ABLK
    start = pl.multiple_of(jnp.clip((j - 1) * ABLK, 0, DEC_SEQ - nk), ABLK)
    qpos = j * ABLK + lax.broadcasted_iota(jnp.int32, (SW_G * ABLK, nk), 0) % ABLK
    kpos = start + lax.broadcasted_iota(jnp.int32, (SW_G * ABLK, nk), 1)
    ok = jnp.abs(qpos - kpos) <= SW_WIN
    outs = []
    for g in range(SW_KV):
        kcols = slice(g * DH, (g + 1) * DH)
        q = jnp.concatenate([q_ref[:, (g * SW_G + r) * DH:(g * SW_G + r + 1) * DH] for r in range(SW_G)], axis=0)
        kw = k_ref[pl.ds(start, nk), kcols]
        vw = v_ref[pl.ds(start, nk), kcols]
        s_w = jnp.where(ok, _dot_nt(q, kw), NEG)
        s_c = _dot_nt(q, kc_ref[:, kcols])
        o = _softmax_pv([(s_w, vw), (s_c, vc_ref[:, kcols])], sink=_sink_col(sink_ref, g, ABLK))
        outs.extend(o[r * ABLK:(r + 1) * ABLK, :] for r in range(SW_G))
    o_ref[...] = jnp.concatenate(outs, axis=1).astype(BF16)


def _sw_attn(sink, q, k, v, kc, vc):
    nb = DEC_SEQ // ABLK
    return pl.pallas_call(
        _sw_kernel,
        grid=(DEC_BATCH, nb),
        in_specs=[
            pl.BlockSpec(memory_space=pltpu.SMEM),
            pl.BlockSpec((ABLK, SWQ_W), lambda b, j: (b * nb + j, 0)),
            pl.BlockSpec((DEC_SEQ, SWKV_W), lambda b, j: (b, 0)),
            pl.BlockSpec((DEC_SEQ, SWKV_W), lambda b, j: (b, 0)),
            pl.BlockSpec((None, PAST, SWKV_W), lambda b, j: (b, 0, 0)),
            pl.BlockSpec((None, PAST, SWKV_W), lambda b, j: (b, 0, 0)),
        ],
        out_specs=pl.BlockSpec((ABLK, SWQ_W), lambda b, j: (b * nb + j, 0)),
        out_shape=jax.ShapeDtypeStruct((N_S, SWQ_W), BF16),
        compiler_params=_cparams("parallel", "arbitrary"),
        name="sw_attn",
    )(sink, q, k, v, kc, vc)


def _odd_out_kernel(nap_ref, nas_ref, swp_ref, sws_ref, x_ref, mod_ref, wo_ref, o_ref):
    i = pl.program_id(0)
    r = _mod_row(i)
    is_p = i < NPT
    ona = jnp.where(is_p, nap_ref[...], nas_ref[...])
    osw = jnp.where(is_p, swp_ref[...], sws_ref[...])
    y = _dot(ona, wo_ref[:NA_W, :]) + _dot(osw, wo_ref[NA_W:, :])
    o_ref[...] = x_ref[...] + mod_ref[2, pl.ds(r, 1), :] * y


def _odd_out(nap, nas, swp, sws, x, mod_l, w_out):
    return pl.pallas_call(
        _odd_out_kernel,
        grid=(NT,),
        in_specs=_dual_specs(TM, NA_W, NPT) + _dual_specs(TM, SWQ_W, NPT) + [
            pl.BlockSpec((TM, D), lambda i: (i, 0)),
            pl.BlockSpec((6, MOD_ROWS, D), lambda i: (0, 0, 0)),
            pl.BlockSpec((D, D), lambda i: (0, 0)),
        ],
        out_specs=pl.BlockSpec((TM, D), lambda i: (i, 0)),
        out_shape=jax.ShapeDtypeStruct((N_TOK, D), F32),
        compiler_params=_cparams("parallel"),
        name="odd_out",
    )(nap, nas, swp, sws, x, mod_l, w_out)


BT = 2560
NB = N_TOK // BT
TPB = BT // TM
CH = 256
XO_ROWS = 2 * BT + CH
ROW = 8
assert D == ROW * 128


def _route_sparse(probs, sel):
    rank = []
    score = []
    for g in range(N_GRP):
        ids = range(g * EPG, (g + 1) * EPG)
        tot = None
        for e in ids:
            rk = jnp.zeros_like(sel[e])
            for j in ids:
                if j < e:
                    rk = rk + (sel[j] >= sel[e]).astype(F32)
                elif j > e:
                    rk = rk + (sel[j] > sel[e]).astype(F32)
            rank.append(rk)
            contrib = jnp.where(rk < 2.0, sel[e], 0.0)
            tot = contrib if tot is None else tot + contrib
        score.append(tot)
    top1, top2 = [], []
    for g in range(N_GRP):
        best = None
        for j in range(N_GRP):
            if j == g:
                continue
            c = (score[g] > score[j]) if j < g else (score[g] >= score[j])
            best = c if best is None else jnp.logical_and(best, c)
        for e in range(g * EPG, (g + 1) * EPG):
            top1.append(jnp.logical_and(best, rank[e] == 0.0))
            top2.append(jnp.logical_and(best, rank[e] == 1.0))
    return top1, top2


def _pick(masks, rows):
    acc = None
    for m, r in zip(masks, rows):
        v = jnp.where(m, r, 0.0)
        acc = v if acc is None else acc + v
    return acc


def _router_kernel(x_ref, mod_ref, g_ref, rw_ref, rb_ref, h_ref, pos_ref, w_ref, cnt_ref, off_ref,
                   meta, carry):
    i = pl.program_id(0)
    j = i % TPB
    r = _mod_row(i)
    h = _rms_mod(x_ref[...], g_ref[...], mod_ref[3, pl.ds(r, 1), :], mod_ref[4, pl.ds(r, 1), :])
    h_ref[...] = h.astype(BF16)
    logits = jnp.dot(h, rw_ref[...], preferred_element_type=F32, precision=lax.Precision.HIGHEST)
    lt = logits.T[:N_EXP, :]
    m = jnp.max(lt, axis=0, keepdims=True)
    ex = jnp.exp(lt - m)
    pr = ex / jnp.sum(ex, axis=0, keepdims=True)
    se = pr + rb_ref[:N_EXP, :]
    probs = [pr[e:e + 1, :] for e in range(N_EXP)]
    sel = [se[e:e + 1, :] for e in range(N_EXP)]
    top1, top2 = _route_sparse(probs, sel)

    @pl.when(j == 0)
    def _():
        carry[...] = jnp.zeros_like(carry)

    member = jnp.concatenate([jnp.logical_or(a, b).astype(F32) for a, b in zip(top1, top2)], axis=0)
    s_idx = lax.broadcasted_iota(jnp.int32, (TM, TM), 0)
    t_idx = lax.broadcasted_iota(jnp.int32, (TM, TM), 1)
    before = jnp.where(s_idx < t_idx, 1.0, 0.0).astype(BF16)
    seen = _dot(member.astype(BF16), before) + carry[:, 0:1]
    seen_rows = [seen[e:e + 1, :] for e in range(N_EXP)]
    ids = [jnp.full((1, TM), float(e), F32) for e in range(N_EXP)]
    p1 = _pick(top1, probs)
    p2 = _pick(top2, probs)
    den = p1 + p2
    w_ref[j, 0:1, :] = p1 / den
    w_ref[j, 1:2, :] = p2 / den
    meta[j, 0:1, :] = _pick(top1, ids)
    meta[j, 1:2, :] = _pick(top2, ids)
    meta[j, 2:3, :] = _pick(top1, seen_rows)
    meta[j, 3:4, :] = _pick(top2, seen_rows)
    carry[...] = carry[...] + jnp.sum(member, axis=1, keepdims=True)

    @pl.when(j == TPB - 1)
    def _():
        cnt = carry[...]
        offs = [jnp.zeros((1, 128), F32)]
        for e in range(1, N_EXP):
            offs.append(offs[-1] + cnt[e - 1:e, :])
        cnt_ref[...] = cnt
        off_ref[...] = jnp.concatenate(offs, axis=0)
        for jj in range(TPB):
            for k in range(2):
                eid = meta[jj, k:k + 1, :]
                pos = meta[jj, 2 + k:3 + k, :]
                for e in range(1, N_EXP):
                    pos = pos + jnp.where(eid == float(e), offs[e][:, 0:1], 0.0)
                pos_ref[jj, k:k + 1, :] = pos.astype(jnp.int32)


def _router(x, mod_l, g, rw_pad, rb_col):
    blk = lambda i: (i // TPB, 0, 0, 0)
    return pl.pallas_call(
        _router_kernel,
        grid=(NT,),
        in_specs=[
            pl.BlockSpec((TM, D), lambda i: (i, 0)),
            pl.BlockSpec((6, MOD_ROWS, D), lambda i: (0, 0, 0)),
            pl.BlockSpec((1, D), lambda i: (0, 0)),
            pl.BlockSpec((D, 128), lambda i: (0, 0)),
            pl.BlockSpec((128, 1), lambda i: (0, 0)),
        ],
        out_specs=[
            pl.BlockSpec((TM, D), lambda i: (i, 0)),
            pl.BlockSpec((None, TPB, 2, TM), blk),
            pl.BlockSpec((None, TPB, 2, TM), blk),
            pl.BlockSpec((None, N_EXP, 128), lambda i: (i // TPB, 0, 0)),
            pl.BlockSpec((None, N_EXP, 128), lambda i: (i // TPB, 0, 0)),
        ],
        out_shape=[
            jax.ShapeDtypeStruct((N_TOK, D), BF16),
            jax.ShapeDtypeStruct((NB, TPB, 2, TM), jnp.int32),
            jax.ShapeDtypeStruct((NB, TPB, 2, TM), F32),
            jax.ShapeDtypeStruct((NB, N_EXP, 128), F32),
            jax.ShapeDtypeStruct((NB, N_EXP, 128), F32),
        ],
        scratch_shapes=[pltpu.VMEM((TPB, 4, TM), F32), pltpu.VMEM((N_EXP, 128), F32)],
        compiler_params=_cparams("arbitrary"),
        name="router",
    )(x, mod_l, g, rw_pad, rb_col)


def _row(p):
    return pl.ds(pl.multiple_of(p * ROW, ROW), ROW)


def _experts_kernel(cnt_ref, off_ref, pos_ref, w_ref, h_ref, wg_ref, wu_ref, wd_ref, x_ref, mod_ref, fg_ref,
                    *rest, final):
    if final:
        op_ref, os_ref, xo, stg = rest
    else:
        o_ref, xo, stg = rest
    b = pl.program_id(0)
    s = pl.program_id(1)

    @pl.when(s == 0)
    def _():
        xo[pl.ds(2 * BT * ROW, CH * ROW), :] = jnp.zeros((CH * ROW, 128), F32)

        def sub(j, carry):
            hs = h_ref[pl.ds(pl.multiple_of(j * TM, TM), TM), :].astype(F32)
            for c in range(ROW):
                stg[pl.ds(c, TM, stride=ROW), :] = hs[:, c * 128:(c + 1) * 128]
            for t in range(TM):
                v = stg[t * ROW:(t + 1) * ROW, :]
                xo[_row(pos_ref[j, 0, t]), :] = v
                xo[_row(pos_ref[j, 1, t]), :] = v
            return carry

        lax.fori_loop(0, TPB, sub, 0)

    @pl.when(s < N_EXP)
    def _():
        n = cnt_ref[b, s]
        base = off_ref[b, s]

        def chunk(jc, carry):
            row0 = base + jc * CH
            x = jnp.concatenate([xo[pl.ds(row0 * ROW + c, CH, stride=ROW), :] for c in range(ROW)], axis=1)
            xb = x.astype(BF16)
            a = _dot(xb, wg_ref[...])
            u = _dot(xb, wu_ref[...])
            hid = (a / (1.0 + jnp.exp(-a))) * u
            out = _dot(hid.astype(BF16), wd_ref[...])
            valid = lax.broadcasted_iota(jnp.int32, (CH, 1), 0) < n - jc * CH
            res = jnp.where(valid, out, x)
            for c in range(ROW):
                xo[pl.ds(row0 * ROW + c, CH, stride=ROW), :] = res[:, c * 128:(c + 1) * 128]
            return carry

        lax.fori_loop(0, (n + CH - 1) // CH, chunk, 0)

    @pl.when(s >= N_EXP - 1)
    def _():
        j = s - (N_EXP - 1)
        tile = b * TPB + j
        g2 = mod_ref[5, pl.ds(_mod_row(tile), 1), :]
        for t in range(TM):
            a = xo[_row(pos_ref[j, 0, t]), :]
            u = xo[_row(pos_ref[j, 1, t]), :]
            stg[t * ROW:(t + 1) * ROW, :] = w_ref[j, 0, t] * a + w_ref[j, 1, t] * u
        y = jnp.concatenate([stg[pl.ds(c, TM, stride=ROW), :] for c in range(ROW)], axis=1)
        res = x_ref[...] + g2 * y
        if final:
            ms = jnp.mean(res * res, axis=-1, keepdims=True)
            res = res * lax.rsqrt(ms + EPS) * fg_ref[...]

            @pl.when(tile < NPT)
            def _():
                op_ref[...] = res

            @pl.when(tile >= NPT)
            def _():
                os_ref[...] = res
        else:
            o_ref[...] = res


def _experts(cnt, off, pos, wts, h, wg, wu, wd, x, mod_l, fg, *, final):
    def tile_of(b, s):
        return b * TPB + jnp.clip(s - (N_EXP - 1), 0, TPB - 1)

    def expert_of(b, s, *_):
        return (jnp.minimum(s, N_EXP - 1), 0, 0)

    smem_blk = pl.BlockSpec((None, TPB, 2, TM), lambda b, s, *_: (b, 0, 0, 0), memory_space=pltpu.SMEM)
    if final:
        out_specs = [pl.BlockSpec((TM, D), lambda b, s, *_: (jnp.minimum(tile_of(b, s), NPT - 1), 0)),
                     pl.BlockSpec((TM, D), lambda b, s, *_: (jnp.maximum(tile_of(b, s) - NPT, 0), 0))]
        out_shape = [jax.ShapeDtypeStruct((N_P, D), F32), jax.ShapeDtypeStruct((N_S, D), F32)]
    else:
        out_specs = pl.BlockSpec((TM, D), lambda b, s, *_: (tile_of(b, s), 0))
        out_shape = jax.ShapeDtypeStruct((N_TOK, D), F32)
    grid_spec = pltpu.PrefetchScalarGridSpec(
        num_scalar_prefetch=2,
        grid=(NB, N_EXP + TPB - 1),
        in_specs=[
            smem_blk,
            smem_blk,
            pl.BlockSpec((BT, D), lambda b, s, *_: (b, 0)),
            pl.BlockSpec((None, D, D_EXP), expert_of),
            pl.BlockSpec((None, D, D_EXP), expert_of),
            pl.BlockSpec((None, D_EXP, D), expert_of),
            pl.BlockSpec((TM, D), lambda b, s, *_: (tile_of(b, s), 0)),
            pl.BlockSpec((6, MOD_ROWS, D), lambda b, s, *_: (0, 0, 0)),
            pl.BlockSpec((1, D), lambda b, s, *_: (0, 0)),
        ],
        out_specs=out_specs,
        scratch_shapes=[pltpu.VMEM((XO_ROWS * ROW, 128), F32), pltpu.VMEM((TM * ROW, 128), F32)],
    )
    return pl.pallas_call(
        functools.partial(_experts_kernel, final=final),
        grid_spec=grid_spec,
        out_shape=out_shape,
        compiler_params=_cparams("arbitrary", "arbitrary"),
        name="experts_final" if final else "experts",
    )(cnt, off, pos, wts, h, wg, wu, wd, x, mod_l, fg)


def _moe_sparse(x, mod_l, g, rw_pad, rb_col, wg, wu, wd, fg, *, final):
    h, pos, wts, cnt, off = _router(x, mod_l, g, rw_pad, rb_col)
    cnt = cnt[:, :, 0].astype(jnp.int32)
    off = off[:, :, 0].astype(jnp.int32)
    return _experts(cnt, off, pos, wts, h, wg, wu, wd, x, mod_l, fg, final=final)


def kernel(x_prompt, x_sample, cache_na_k, cache_na_v, cache_sw_k, cache_sw_v, c, c_ctx, mod_w, mod_b, norm_mix_g, norm_ffn_g, ev_w_in, ev_pool_w, ev_pool_scale, ev_conv_w, ev_w_out, od_w_in, od_rpb, od_sink, od_w_out, router_w, router_b, moe_w_gate, moe_w_up, moe_w_down, final_norm_g):
    xp = x_prompt.reshape(N_P, D)
    xs = x_sample.reshape(N_S, D)
    cvec = jnp.concatenate([c_ctx[None, :], c, jnp.zeros((MOD_ROWS - 1 - DEC_BATCH, D), F32)], axis=0)
    mod = _modulation(cvec, mod_w, mod_b)

    rw_pad = jnp.pad(router_w, ((0, 0), (0, 128 - N_EXP)))
    rb_col = jnp.pad(router_b, (0, 128 - N_EXP)).reshape(128, 1)

    conv_w = jnp.pad(ev_conv_w[0], ((0, 8 - ev_conv_w.shape[1]), (0, 0)))
    x = _even_layer(xp, xs, mod[0], norm_mix_g[0:1], ev_w_in[0].astype(BF16), ev_pool_w[0].astype(BF16),
                    ev_pool_scale[0:1], conv_w, ev_w_out[0].astype(BF16))
    fg = final_norm_g.reshape(1, D)
    x = _moe_sparse(x, mod[0], norm_ffn_g[0:1], rw_pad, rb_col, moe_w_gate[0].astype(BF16),
                    moe_w_up[0].astype(BF16), moe_w_down[0].astype(BF16), fg, final=False)

    cos, sin = _rope_tables()
    w_in = od_w_in[0].astype(BF16)
    g1 = norm_mix_g[1:2]
    qna_p, kna_p, vna_p, qsw_p, ksw_p, vsw_p, nak, nav, swk, swv = _odd_in(x, mod[1], g1, w_in, cos, sin, prompt=True)
    qna_s, kna_s, vna_s, qsw_s, ksw_s, vsw_s = _odd_in(x, mod[1], g1, w_in, cos, sin, prompt=False)
    sink = od_sink[0]
    ona_p, osw_p = _ctx_attn(sink, qna_p, kna_p, vna_p, qsw_p, ksw_p, vsw_p)
    bias = _na_bias_blocks(_na_bias(od_rpb[0]))
    ona_s = _na_attn(qna_s, kna_s, vna_s,
                     cache_na_k[:, 0].reshape(DEC_BATCH, PAST, NA_W).astype(BF16),
                     cache_na_v[:, 0].reshape(DEC_BATCH, PAST, NA_W).astype(BF16), bias)
    osw_s = _sw_attn(sink, qsw_s, ksw_s, vsw_s,
                     cache_sw_k[:, 0].reshape(DEC_BATCH, PAST, SWKV_W).astype(BF16),
                     cache_sw_v[:, 0].reshape(DEC_BATCH, PAST, SWKV_W).astype(BF16))
    x = _odd_out(ona_p, ona_s, osw_p, osw_s, x, mod[1], od_w_out[0].astype(BF16))
    y_prompt, y_sample = _moe_sparse(x, mod[1], norm_ffn_g[1:2], rw_pad, rb_col, moe_w_gate[1].astype(BF16),
                                     moe_w_up[1].astype(BF16), moe_w_down[1].astype(BF16), fg, final=True)
    y_prompt = y_prompt.reshape(BATCH, SEQ, D)
    y_sample = y_sample.reshape(DEC_BATCH, DEC_SEQ, D)
    new_na_k = nak.reshape(BATCH, 1, SEQ, NA_H, DH)
    new_na_v = nav.reshape(BATCH, 1, SEQ, NA_H, DH)
    new_sw_k = swk.reshape(BATCH, 1, SEQ, SW_KV, DH)
    new_sw_v = swv.reshape(BATCH, 1, SEQ, SW_KV, DH)
    return (y_prompt, y_sample, new_na_k, new_na_v, new_sw_k, new_sw_v)
```

```python
import functools

import jax
import jax.numpy as jnp
import numpy as np
from jax import lax
from jax.experimental import pallas as pl
from jax.experimental.pallas import tpu as pltpu

D = 1024
BATCH = 16
SEQ = 256
DEC_BATCH = 4
DEC_SEQ = 4096
PAST = 512
GRID_W = 64
ROWS = DEC_SEQ // GRID_W
DH = 64
POOL_W = 512
POOL_WINDOWS = (2, 4, 8, 16)
POOL_GW = 128
CONV_W = 512
EVEN_IN = POOL_W + 3 * CONV_W
NA_H = 8
NA_ROWS = 8
NA_COLS = 16
SW_H = 8
SW_KV = 2
SW_G = SW_H // SW_KV
SW_WIN = 128
ABLK = 128
NA_W = NA_H * DH
SWQ_W = SW_H * DH
SWKV_W = SW_KV * DH
ODD_IN = 3 * NA_W + SWQ_W + 2 * SWKV_W
N_EXP = 16
N_GRP = 4
EPG = 4
D_EXP = 512
EPS = 1e-6
NEG = -1e30
ROPE_BASE = 10000.0
QK_SCALE = DH ** -0.5
assert QK_SCALE == 0.125

N_P = BATCH * SEQ
N_S = DEC_BATCH * DEC_SEQ
N_TOK = N_P + N_S
MOD_ROWS = 8

TM = 256
NPT = N_P // TM
TPS = DEC_SEQ // TM
NT = N_TOK // TM
HALO = 8

F32 = jnp.float32
BF16 = jnp.bfloat16
VMEM_LIMIT = 56 * 1024 * 1024


def _cparams(*sem):
    return pltpu.CompilerParams(dimension_semantics=sem, vmem_limit_bytes=VMEM_LIMIT)


def _mod_row(i):
    return jnp.where(i < NPT, 0, 1 + (i - NPT) // TPS)


def _rms_mod(x, g, shift, scale):
    ms = jnp.mean(x * x, axis=-1, keepdims=True)
    y = x * lax.rsqrt(ms + EPS) * g
    return y * (1.0 + scale) + shift


def _dot(a, b):
    return jnp.dot(a, b, preferred_element_type=F32)


def _dot_nt(a, b):
    return lax.dot_general(a, b, (((1,), (1,)), ((), ())), preferred_element_type=F32)


def _mod_kernel(cv_ref, w_ref, b_ref, o_ref):
    cv = cv_ref[...]
    a = cv / (1.0 + jnp.exp(-cv))
    o_ref[...] = jnp.dot(a, w_ref[...], preferred_element_type=F32,
                         precision=lax.Precision.HIGHEST) + b_ref[...]


def _modulation(cvec, mod_w, mod_b):
    depth = mod_w.shape[0]
    return pl.pallas_call(
        _mod_kernel,
        grid=(depth, 6),
        in_specs=[
            pl.BlockSpec((MOD_ROWS, D), lambda l, j: (0, 0)),
            pl.BlockSpec((None, D, D), lambda l, j: (l, 0, j)),
            pl.BlockSpec((None, None, 1, D), lambda l, j: (l, j, 0, 0)),
        ],
        out_specs=pl.BlockSpec((None, None, MOD_ROWS, D), lambda l, j: (l, j, 0, 0)),
        out_shape=jax.ShapeDtypeStruct((depth, 6, MOD_ROWS, D), F32),
        compiler_params=_cparams("arbitrary", "arbitrary"),
        name="modulation",
    )(cvec, mod_w, mod_b.reshape(depth, 6, 1, D))


def _dual_specs(tm, width, npt):
    return [
        pl.BlockSpec((tm, width), lambda i: (jnp.minimum(i, npt - 1), 0)),
        pl.BlockSpec((tm, width), lambda i: (jnp.maximum(i - npt, 0), 0)),
    ]


def _even_kernel(xp_ref, xs_ref, xprev_ref, xnext_ref, mod_ref, g_ref, wi_ref, pw_ref, ps_ref, cw_ref,
                 wo_ref, o_ref, pext, uext):
    i = pl.program_id(0)
    r = _mod_row(i)
    is_p = i < NPT
    t0 = jnp.where(is_p, 0, ((i - NPT) % TPS) * TM)
    seq_len = jnp.where(is_p, SEQ, DEC_SEQ)
    first = t0 == 0
    last = t0 + TM == seq_len

    x = jnp.where(is_p, xp_ref[...], xs_ref[...])
    xe = jnp.concatenate([xprev_ref[...], x, xnext_ref[...]], axis=0)
    h = _rms_mod(xe, g_ref[...], mod_ref[0, pl.ds(r, 1), :], mod_ref[1, pl.ds(r, 1), :])
    ze = _dot(h.astype(BF16), wi_ref[...])
    row = lax.broadcasted_iota(jnp.int32, (TM + 2 * HALO, 1), 0)
    outside = jnp.logical_or(jnp.logical_and(first, row < HALO), jnp.logical_and(last, row >= HALO + TM))
    ze = jnp.where(outside, 0.0, ze)
    pext[...] = ze[:, :POOL_W]
    uext[...] = ze[:, POOL_W + 2 * CONV_W:] * ze[:, POOL_W:POOL_W + CONV_W]
    p = ze[HALO:HALO + TM, :POOL_W]
    gb = ze[HALO:HALO + TM, POOL_W + CONV_W:POOL_W + 2 * CONV_W]

    t = t0 + lax.broadcasted_iota(jnp.int32, (TM, 1), 0)
    y = jnp.zeros((TM, D), F32)
    for g, w in enumerate(POOL_WINDOWS):
        cols = slice(g * POOL_GW, (g + 1) * POOL_GW)
        acc = jnp.zeros((TM, POOL_GW), F32)
        for k in range(-(w // 2), w - (w // 2)):
            acc = acc + pext[pl.ds(HALO + k, TM), cols]
        lo = jnp.maximum(t - w // 2, 0)
        hi = jnp.minimum(t + (w - 1 - w // 2), seq_len - 1)
        cnt = (hi - lo + 1).astype(F32)
        d = acc / cnt - p[:, cols]
        a = _dot(d.astype(BF16), pw_ref[g]) * ps_ref[:, cols]
        y = y + _dot(a.astype(BF16), wo_ref[cols, :])
    conv = (uext[pl.ds(HALO - 1, TM), :] * cw_ref[0:1, :] + uext[pl.ds(HALO, TM), :] * cw_ref[1:2, :]
            + uext[pl.ds(HALO + 1, TM), :] * cw_ref[2:3, :])
    b = gb * conv
    y = y + _dot(b.astype(BF16), wo_ref[POOL_W:, :])
    o_ref[...] = x + mod_ref[2, pl.ds(r, 1), :] * y


def _even_layer(xp, xs, mod_l, g, w_in, pool_w, pool_scale, conv_w, w_out):
    hb = TM // HALO
    nhb = N_S // HALO
    return pl.pallas_call(
        _even_kernel,
        grid=(NT,),
        in_specs=_dual_specs(TM, D, NPT) + [
            pl.BlockSpec((HALO, D), lambda i: (jnp.maximum((i - NPT) * hb - 1, 0), 0)),
            pl.BlockSpec((HALO, D), lambda i: (jnp.clip((i - NPT + 1) * hb, 0, nhb - 1), 0)),
            pl.BlockSpec((6, MOD_ROWS, D), lambda i: (0, 0, 0)),
            pl.BlockSpec((1, D), lambda i: (0, 0)),
            pl.BlockSpec((D, EVEN_IN), lambda i: (0, 0)),
            pl.BlockSpec((4, POOL_GW, POOL_GW), lambda i: (0, 0, 0)),
            pl.BlockSpec((1, POOL_W), lambda i: (0, 0)),
            pl.BlockSpec((8, CONV_W), lambda i: (0, 0)),
            pl.BlockSpec((D, D), lambda i: (0, 0)),
        ],
        out_specs=pl.BlockSpec((TM, D), lambda i: (i, 0)),
        out_shape=jax.ShapeDtypeStruct((N_TOK, D), F32),
        scratch_shapes=[pltpu.VMEM((TM + 2 * HALO, POOL_W), F32),
                        pltpu.VMEM((TM + 2 * HALO, CONV_W), F32)],
        compiler_params=_cparams("parallel"),
        name="even_layer",
    )(xp, xs, xs, xs, mod_l, g, w_in, pool_w, pool_scale, conv_w, w_out)


def _rope(x, cos, sin_signed):
    n = x.shape[1] // 128
    cosf = jnp.concatenate([cos] * n, axis=1) if n > 1 else cos
    sinf = jnp.concatenate([sin_signed] * n, axis=1) if n > 1 else sin_signed
    w = x.shape[1]
    lane = lax.broadcasted_iota(jnp.int32, x.shape, 1)
    up = pltpu.roll(x, w - DH // 4, 1)
    dn = pltpu.roll(x, DH // 4, 1)
    rot = jnp.where((lane % (DH // 2)) < DH // 4, up, dn)
    return x * cosf + rot * sinf


def _odd_in_kernel(x_ref, mod_ref, g_ref, w_ref, cos_ref, sin_ref, *out_refs, tile0, rope, kv_f32):
    i = pl.program_id(0) + tile0
    r = _mod_row(i)
    h = _rms_mod(x_ref[...], g_ref[...], mod_ref[0, pl.ds(r, 1), :], mod_ref[1, pl.ds(r, 1), :])
    z = _dot(h.astype(BF16), w_ref[...])
    qna_ref, kna_ref, vna_ref, qsw_ref, ksw_ref, vsw_ref = out_refs[:6]
    c0 = 0
    qna = z[:, 0:NA_W]
    kna = z[:, NA_W:2 * NA_W]
    vna = z[:, 2 * NA_W:3 * NA_W]
    c0 = 3 * NA_W
    qsw = z[:, c0:c0 + SWQ_W]
    ksw = z[:, c0 + SWQ_W:c0 + SWQ_W + SWKV_W]
    vsw = z[:, c0 + SWQ_W + SWKV_W:]
    if kv_f32:
        out_refs[6][...] = kna
        out_refs[7][...] = vna
        out_refs[8][...] = ksw
        out_refs[9][...] = vsw
    if rope:
        cos = cos_ref[...]
        sin = sin_ref[...]
        qsw = _rope(qsw, cos, sin)
        ksw = _rope(ksw, cos, sin)
    qna_ref[...] = (qna * QK_SCALE).astype(BF16)
    kna_ref[...] = kna.astype(BF16)
    vna_ref[...] = vna.astype(BF16)
    qsw_ref[...] = (qsw * QK_SCALE).astype(BF16)
    ksw_ref[...] = ksw.astype(BF16)
    vsw_ref[...] = vsw.astype(BF16)


def _odd_in(x, mod_l, g, w, cos, sin, *, prompt):
    tile0 = 0 if prompt else NPT
    nt = NPT if prompt else NT - NPT
    n = nt * TM
    widths = [NA_W, NA_W, NA_W, SWQ_W, SWKV_W, SWKV_W]
    out_shape = [jax.ShapeDtypeStruct((n, w_), BF16) for w_ in widths]
    out_specs = [pl.BlockSpec((TM, w_), lambda i: (i, 0)) for w_ in widths]
    if prompt:
        for w_ in (NA_W, NA_W, SWKV_W, SWKV_W):
            out_shape.append(jax.ShapeDtypeStruct((n, w_), F32))
            out_specs.append(pl.BlockSpec((TM, w_), lambda i: (i, 0)))
    return pl.pallas_call(
        functools.partial(_odd_in_kernel, tile0=tile0, rope=not prompt, kv_f32=prompt),
        grid=(nt,),
        in_specs=[
            pl.BlockSpec((TM, D), lambda i: (i + tile0, 0)),
            pl.BlockSpec((6, MOD_ROWS, D), lambda i: (0, 0, 0)),
            pl.BlockSpec((1, D), lambda i: (0, 0)),
            pl.BlockSpec((D, ODD_IN), lambda i: (0, 0)),
            pl.BlockSpec((TM, 128), lambda i: (i % TPS, 0)),
            pl.BlockSpec((TM, 128), lambda i: (i % TPS, 0)),
        ],
        out_specs=out_specs,
        out_shape=out_shape,
        compiler_params=_cparams("parallel"),
        name="odd_in_prompt" if prompt else "odd_in_latent",
    )(x, mod_l, g, w, cos, sin)


def _rope_tables():
    t = np.arange(DEC_SEQ)
    quarter = DH // 4
    inv = 1.0 / (ROPE_BASE ** (np.arange(quarter, dtype=np.float64) / quarter))

    def cos_sin(pos):
        ang = pos.astype(np.float64)[:, None] * inv[None, :]
        ang = np.concatenate([ang, ang], axis=-1)
        return np.cos(ang), np.sin(ang)

    cr, sr = cos_sin(t // GRID_W)
    cc, sc = cos_sin(t % GRID_W)
    cos = np.concatenate([cr, cc], axis=-1)
    sin = np.concatenate([sr, sc], axis=-1)
    sign = np.where((np.arange(DH) % (DH // 2)) < DH // 4, -1.0, 1.0)
    sin = sin * sign[None, :]
    cos = np.concatenate([cos, cos], axis=-1).astype(np.float32)
    sin = np.concatenate([sin, sin], axis=-1).astype(np.float32)
    return jnp.asarray(cos), jnp.asarray(sin)


def _softmax_pv(segs, sink=None):
    m = None
    for s, _ in segs:
        sm = jnp.max(s, axis=-1, keepdims=True)
        m = sm if m is None else jnp.maximum(m, sm)
    if sink is not None:
        m = jnp.maximum(m, sink)
    den = None
    acc = None
    for s, v in segs:
        p = jnp.exp(s - m)
        ps = jnp.sum(p, axis=-1, keepdims=True)
        den = ps if den is None else den + ps
        pv = _dot(p.astype(BF16), v)
        acc = pv if acc is None else acc + pv
    if sink is not None:
        den = den + jnp.exp(sink - m)
    return acc / den


def _sink_col(sink_ref, g, rows_per_head):
    row = lax.broadcasted_iota(jnp.int32, (SW_G * rows_per_head, 1), 0)
    col = jnp.zeros((SW_G * rows_per_head, 1), F32)
    for r in range(SW_G):
        col = jnp.where(row // rows_per_head == r, sink_ref[g * SW_G + r], col)
    return col


def _ctx_attn_kernel(sink_ref, qna_ref, kna_ref, vna_ref, qsw_ref, ksw_ref, vsw_ref, ona_ref, osw_ref):
    outs = []
    for h in range(NA_H):
        cols = slice(h * DH, (h + 1) * DH)
        s = _dot_nt(qna_ref[:, cols], kna_ref[:, cols])
        outs.append(_softmax_pv([(s, vna_ref[:, cols])]))
    ona_ref[...] = jnp.concatenate(outs, axis=1).astype(BF16)
    outs = []
    for g in range(SW_KV):
        kc = slice(g * DH, (g + 1) * DH)
        q = jnp.concatenate([qsw_ref[:, (g * SW_G + r) * DH:(g * SW_G + r + 1) * DH] for r in range(SW_G)], axis=0)
        s = _dot_nt(q, ksw_ref[:, kc])
        o = _softmax_pv([(s, vsw_ref[:, kc])], sink=_sink_col(sink_ref, g, SEQ))
        outs.extend(o[r * SEQ:(r + 1) * SEQ, :] for r in range(SW_G))
    osw_ref[...] = jnp.concatenate(outs, axis=1).astype(BF16)


def _ctx_attn(sink, qna, kna, vna, qsw, ksw, vsw):
    def spec(w):
        return pl.BlockSpec((SEQ, w), lambda b: (b, 0))

    return pl.pallas_call(
        _ctx_attn_kernel,
        grid=(BATCH,),
        in_specs=[pl.BlockSpec(memory_space=pltpu.SMEM), spec(NA_W), spec(NA_W), spec(NA_W),
                  spec(SWQ_W), spec(SWKV_W), spec(SWKV_W)],
        out_specs=[spec(NA_W), spec(SWQ_W)],
        out_shape=[jax.ShapeDtypeStruct((N_P, NA_W), BF16), jax.ShapeDtypeStruct((N_P, SWQ_W), BF16)],
        compiler_params=_cparams("parallel"),
        name="ctx_attn",
    )(sink, qna, kna, vna, qsw, ksw, vsw)


def _na_bias_kernel(rpb_ref, o_ref):
    h = pl.program_id(0)
    cq = lax.broadcasted_iota(jnp.int32, (GRID_W, GRID_W), 0)
    ck = lax.broadcasted_iota(jnp.int32, (GRID_W, GRID_W), 1)
    cstart = jnp.clip(cq - NA_COLS // 2, 0, GRID_W - NA_COLS)
    ok = (ck >= cstart) & (ck < cstart + NA_COLS)
    dc = jnp.clip(ck - cq + NA_COLS - 1, 0, 2 * NA_COLS - 2)
    ndc = 2 * NA_COLS - 1
    for d in range(2 * NA_ROWS - 1):
        def body(e, acc):
            return jnp.where(dc == e, rpb_ref[h, d * ndc + e], acc)
        b = lax.fori_loop(0, ndc, body, jnp.zeros((GRID_W, GRID_W), F32))
        o_ref[d] = jnp.where(ok, b, NEG)


def _na_bias(rpb):
    nd = 2 * NA_ROWS - 1
    return pl.pallas_call(
        _na_bias_kernel,
        grid=(NA_H,),
        in_specs=[pl.BlockSpec(memory_space=pltpu.SMEM)],
        out_specs=pl.BlockSpec((None, nd, GRID_W, GRID_W), lambda h: (h, 0, 0, 0)),
        out_shape=jax.ShapeDtypeStruct((NA_H, nd, GRID_W, GRID_W), F32),
        compiler_params=_cparams("parallel"),
        name="na_bias",
    )(rpb.reshape(NA_H, nd * (2 * NA_COLS - 1)))


NA_QB = 4
NA_KR = 12
NA_NQB = ROWS // NA_QB


def _na_bias_blocks(bias):
    neg = jnp.full((NA_H, GRID_W, GRID_W), NEG, F32)
    cases = []
    for case in range(3):
        rows = []
        for i in range(NA_QB):
            blocks = []
            for j in range(NA_KR):
                if case == 0:
                    valid, dr = j < NA_ROWS, j - i + NA_ROWS - 1
                elif case == 1:
                    valid, dr = i <= j < i + NA_ROWS, j - i + NA_ROWS // 2 - 1
                else:
                    valid, dr = j >= NA_KR - NA_ROWS, j - i + NA_ROWS - 1 - (NA_KR - NA_QB)
                blocks.append(bias[:, dr] if valid else neg)
            rows.append(jnp.concatenate(blocks, axis=2))
        cases.append(jnp.concatenate(rows, axis=1))
    return jnp.stack(cases)


def _na_kernel(q_ref, k_ref, v_ref, kc_ref, vc_ref, bias_ref, o_ref):
    r0 = pl.program_id(1) * NA_QB
    start = pl.multiple_of(jnp.clip(r0 - NA_ROWS // 2, 0, ROWS - NA_KR) * GRID_W, GRID_W)
    nq = NA_QB * GRID_W
    lane = lax.broadcasted_iota(jnp.int32, (nq, 2 * DH), 1)
    for p in range(NA_H // 2):
        cols = slice(p * 2 * DH, (p + 1) * 2 * DH)
        q = q_ref[:, cols]
        kl = k_ref[pl.ds(start, NA_KR * GRID_W), cols]
        vl = v_ref[pl.ds(start, NA_KR * GRID_W), cols]
        kc = kc_ref[:, cols]
        vc = vc_ref[:, cols]
        outs = []
        for half in range(2):
            mine = (lane < DH) if half == 0 else (lane >= DH)
            qm = jnp.where(mine, q, jnp.zeros_like(q))
            s_loc = _dot_nt(qm, kl) + bias_ref[2 * p + half]
            s_ctx = _dot_nt(qm, kc)
            outs.append(_softmax_pv([(s_loc, vl), (s_ctx, vc)]))
        o_ref[:, cols] = jnp.where(lane < DH, outs[0], outs[1]).astype(BF16)


def _na_attn(q, k, v, kc, vc, bias):
    nq = NA_QB * GRID_W

    def bias_case(b, rb):
        return (jnp.where(rb == 0, 0, jnp.where(rb == NA_NQB - 1, 2, 1)), 0, 0, 0)

    return pl.pallas_call(
        _na_kernel,
        grid=(DEC_BATCH, NA_NQB),
        in_specs=[
            pl.BlockSpec((nq, NA_W), lambda b, rb: (b * NA_NQB + rb, 0)),
            pl.BlockSpec((DEC_SEQ, NA_W), lambda b, rb: (b, 0)),
            pl.BlockSpec((DEC_SEQ, NA_W), lambda b, rb: (b, 0)),
            pl.BlockSpec((None, PAST, NA_W), lambda b, rb: (b, 0, 0)),
            pl.BlockSpec((None, PAST, NA_W), lambda b, rb: (b, 0, 0)),
            pl.BlockSpec((None, NA_H, nq, NA_KR * GRID_W), bias_case),
        ],
        out_specs=pl.BlockSpec((nq, NA_W), lambda b, rb: (b * NA_NQB + rb, 0)),
        out_shape=jax.ShapeDtypeStruct((N_S, NA_W), BF16),
        compiler_params=_cparams("parallel", "arbitrary"),
        name="na_attn",
    )(q, k, v, kc, vc, bias)


def _sw_window_bias():
    q = np.arange(SW_G * ABLK)[:, None] % ABLK
    k = np.arange(3 * ABLK)[None, :]
    tables = [np.where(np.abs(q + lead - k) <= SW_WIN, 0.0, NEG) for lead in (0, ABLK, 2 * ABLK)]
    return jnp.asarray(np.stack(tables).astype(np.float32))


def _sw_kernel(sink_ref, q_ref, k_ref, v_ref, kc_ref, vc_ref, wb_ref, o_ref):
    j = pl.program_id(1)
    nk = 3 * ABLK
    start = pl.multiple_of(jnp.clip((j - 1) * ABLK, 0, DEC_SEQ - nk), ABLK)
    outs = []
    for g in range(SW_KV):
        kcols = slice(g * DH, (g + 1) * DH)
        q = jnp.concatenate([q_ref[:, (g * SW_G + r) * DH:(g * SW_G + r + 1) * DH] for r in range(SW_G)], axis=0)
        kw = k_ref[pl.ds(start, nk), kcols]
        vw = v_ref[pl.ds(start, nk), kcols]
        s_w = _dot_nt(q, kw) + wb_ref[...]
        s_c = _dot_nt(q, kc_ref[:, kcols])
        o = _softmax_pv([(s_w, vw), (s_c, vc_ref[:, kcols])], sink=_sink_col(sink_ref, g, ABLK))
        outs.extend(o[r * ABLK:(r + 1) * ABLK, :] for r in range(SW_G))
    o_ref[...] = jnp.concatenate(outs, axis=1).astype(BF16)


def _sw_attn(sink, q, k, v, kc, vc):
    nb = DEC_SEQ // ABLK
    return pl.pallas_call(
        _sw_kernel,
        grid=(DEC_BATCH, nb),
        in_specs=[
            pl.BlockSpec(memory_space=pltpu.SMEM),
            pl.BlockSpec((ABLK, SWQ_W), lambda b, j: (b * nb + j, 0)),
            pl.BlockSpec((DEC_SEQ, SWKV_W), lambda b, j: (b, 0)),
            pl.BlockSpec((DEC_SEQ, SWKV_W), lambda b, j: (b, 0)),
            pl.BlockSpec((None, PAST, SWKV_W), lambda b, j: (b, 0, 0)),
            pl.BlockSpec((None, PAST, SWKV_W), lambda b, j: (b, 0, 0)),
            pl.BlockSpec((None, SW_G * ABLK, 3 * ABLK),
                         lambda b, j: (jnp.where(j == 0, 0, jnp.where(j == nb - 1, 2, 1)), 0, 0)),
        ],
        out_specs=pl.BlockSpec((ABLK, SWQ_W), lambda b, j: (b * nb + j, 0)),
        out_shape=jax.ShapeDtypeStruct((N_S, SWQ_W), BF16),
        compiler_params=_cparams("parallel", "arbitrary"),
        name="sw_attn",
    )(sink, q, k, v, kc, vc, _sw_window_bias())


def _odd_out_kernel(nap_ref, nas_ref, swp_ref, sws_ref, x_ref, mod_ref, wo_ref, o_ref):
    i = pl.program_id(0)
    r = _mod_row(i)
    is_p = i < NPT
    ona = jnp.where(is_p, nap_ref[...], nas_ref[...])
    osw = jnp.where(is_p, swp_ref[...], sws_ref[...])
    y = _dot(ona, wo_ref[:NA_W, :]) + _dot(osw, wo_ref[NA_W:, :])
    o_ref[...] = x_ref[...] + mod_ref[2, pl.ds(r, 1), :] * y


def _odd_out(nap, nas, swp, sws, x, mod_l, w_out):
    return pl.pallas_call(
        _odd_out_kernel,
        grid=(NT,),
        in_specs=_dual_specs(TM, NA_W, NPT) + _dual_specs(TM, SWQ_W, NPT) + [
            pl.BlockSpec((TM, D), lambda i: (i, 0)),
            pl.BlockSpec((6, MOD_ROWS, D), lambda i: (0, 0, 0)),
            pl.BlockSpec((D, D), lambda i: (0, 0)),
        ],
        out_specs=pl.BlockSpec((TM, D), lambda i: (i, 0)),
        out_shape=jax.ShapeDtypeStruct((N_TOK, D), F32),
        compiler_params=_cparams("parallel"),
        name="odd_out",
    )(nap, nas, swp, sws, x, mod_l, w_out)


BT = 2560
NB = N_TOK // BT
TPB = BT // TM
CH = 256
CH_TAIL = 128
XO_ROWS = 2 * BT + CH
ROW = 8
assert D == ROW * 128


def _route_sparse(probs, sel):
    rank = []
    score = []
    for g in range(N_GRP):
        ids = range(g * EPG, (g + 1) * EPG)
        tot = None
        for e in ids:
            rk = jnp.zeros_like(sel[e])
            for j in ids:
                if j < e:
                    rk = rk + (sel[j] >= sel[e]).astype(F32)
                elif j > e:
                    rk = rk + (sel[j] > sel[e]).astype(F32)
            rank.append(rk)
            contrib = jnp.where(rk < 2.0, sel[e], 0.0)
            tot = contrib if tot is None else tot + contrib
        score.append(tot)
    top1, top2 = [], []
    for g in range(N_GRP):
        best = None
        for j in range(N_GRP):
            if j == g:
                continue
            c = (score[g] > score[j]) if j < g else (score[g] >= score[j])
            best = c if best is None else jnp.logical_and(best, c)
        for e in range(g * EPG, (g + 1) * EPG):
            top1.append(jnp.logical_and(best, rank[e] == 0.0))
            top2.append(jnp.logical_and(best, rank[e] == 1.0))
    return top1, top2


def _pick(masks, rows):
    acc = None
    for m, r in zip(masks, rows):
        v = jnp.where(m, r, 0.0)
        acc = v if acc is None else acc + v
    return acc


def _router_kernel(x_ref, mod_ref, g_ref, rw_ref, rb_ref, h_ref, pos_ref, w_ref, cnt_ref, off_ref,
                   meta, carry):
    i = pl.program_id(0)
    j = i % TPB
    r = _mod_row(i)
    h = _rms_mod(x_ref[...], g_ref[...], mod_ref[3, pl.ds(r, 1), :], mod_ref[4, pl.ds(r, 1), :])
    h_ref[...] = h.astype(BF16)
    h_hi = h.astype(BF16)
    h_lo = (h - h_hi.astype(F32)).astype(BF16)
    logits = _dot(h_lo, rw_ref[0]) + _dot(h_hi, rw_ref[1]) + _dot(h_hi, rw_ref[0])
    lt = logits.T[:N_EXP, :]
    m = jnp.max(lt, axis=0, keepdims=True)
    ex = jnp.exp(lt - m)
    pr = ex / jnp.sum(ex, axis=0, keepdims=True)
    se = pr + rb_ref[:N_EXP, :]
    probs = [pr[e:e + 1, :] for e in range(N_EXP)]
    sel = [se[e:e + 1, :] for e in range(N_EXP)]
    top1, top2 = _route_sparse(probs, sel)

    @pl.when(j == 0)
    def _():
        carry[...] = jnp.zeros_like(carry)

    member = jnp.concatenate([jnp.logical_or(a, b).astype(F32) for a, b in zip(top1, top2)], axis=0)
    s_idx = lax.broadcasted_iota(jnp.int32, (TM, TM), 0)
    t_idx = lax.broadcasted_iota(jnp.int32, (TM, TM), 1)
    before = jnp.where(s_idx < t_idx, 1.0, 0.0).astype(BF16)
    seen = _dot(member.astype(BF16), before) + carry[:, 0:1]
    seen_rows = [seen[e:e + 1, :] for e in range(N_EXP)]
    ids = [jnp.full((1, TM), float(e), F32) for e in range(N_EXP)]
    p1 = _pick(top1, probs)
    p2 = _pick(top2, probs)
    den = p1 + p2
    w_ref[j, 0:1, :] = p1 / den
    w_ref[j, 1:2, :] = p2 / den
    meta[j, 0:1, :] = _pick(top1, ids)
    meta[j, 1:2, :] = _pick(top2, ids)
    meta[j, 2:3, :] = _pick(top1, seen_rows)
    meta[j, 3:4, :] = _pick(top2, seen_rows)
    carry[...] = carry[...] + jnp.sum(member, axis=1, keepdims=True)

    @pl.when(j == TPB - 1)
    def _():
        cnt = carry[...]
        offs = [jnp.zeros((1, 128), F32)]
        for e in range(1, N_EXP):
            offs.append(offs[-1] + cnt[e - 1:e, :])
        cnt_ref[...] = cnt
        off_ref[...] = jnp.concatenate(offs, axis=0)
        for jj in range(TPB):
            for k in range(2):
                eid = meta[jj, k:k + 1, :]
                pos = meta[jj, 2 + k:3 + k, :]
                for e in range(1, N_EXP):
                    pos = pos + jnp.where(eid == float(e), offs[e][:, 0:1], 0.0)
                pos_ref[jj, k:k + 1, :] = pos.astype(jnp.int32)


def _router(x, mod_l, g, rw_pad, rb_col):
    blk = lambda i: (i // TPB, 0, 0, 0)
    return pl.pallas_call(
        _router_kernel,
        grid=(NT,),
        in_specs=[
            pl.BlockSpec((TM, D), lambda i: (i, 0)),
            pl.BlockSpec((6, MOD_ROWS, D), lambda i: (0, 0, 0)),
            pl.BlockSpec((1, D), lambda i: (0, 0)),
            pl.BlockSpec((2, D, 128), lambda i: (0, 0, 0)),
            pl.BlockSpec((128, 1), lambda i: (0, 0)),
        ],
        out_specs=[
            pl.BlockSpec((TM, D), lambda i: (i, 0)),
            pl.BlockSpec((None, TPB, 2, TM), blk),
            pl.BlockSpec((None, TPB, 2, TM), blk),
            pl.BlockSpec((None, N_EXP, 128), lambda i: (i // TPB, 0, 0)),
            pl.BlockSpec((None, N_EXP, 128), lambda i: (i // TPB, 0, 0)),
        ],
        out_shape=[
            jax.ShapeDtypeStruct((N_TOK, D), BF16),
            jax.ShapeDtypeStruct((NB, TPB, 2, TM), jnp.int32),
            jax.ShapeDtypeStruct((NB, TPB, 2, TM), F32),
            jax.ShapeDtypeStruct((NB, N_EXP, 128), F32),
            jax.ShapeDtypeStruct((NB, N_EXP, 128), F32),
        ],
        scratch_shapes=[pltpu.VMEM((TPB, 4, TM), F32), pltpu.VMEM((N_EXP, 128), F32)],
        compiler_params=_cparams("arbitrary"),
        name="router",
    )(x, mod_l, g, rw_pad, rb_col)


def _row(p):
    return pl.ds(pl.multiple_of(p * ROW, ROW), ROW)


def _experts_kernel(cnt_ref, off_ref, pos_ref, w_ref, h_ref, wg_ref, wu_ref, wd_ref, x_ref, mod_ref, fg_ref,
                    *rest, final):
    if final:
        op_ref, os_ref, xo, stg = rest
    else:
        o_ref, xo, stg = rest
    b = pl.program_id(0)
    s = pl.program_id(1)

    @pl.when(s == 0)
    def _():
        xo[pl.ds(2 * BT * ROW, CH * ROW), :] = jnp.zeros((CH * ROW, 128), F32)

        def sub(j, carry):
            hs = h_ref[pl.ds(pl.multiple_of(j * TM, TM), TM), :].astype(F32)
            for c in range(ROW):
                stg[pl.ds(c, TM, stride=ROW), :] = hs[:, c * 128:(c + 1) * 128]
            for t in range(TM):
                v = stg[t * ROW:(t + 1) * ROW, :]
                xo[_row(pos_ref[j, 0, t]), :] = v
                xo[_row(pos_ref[j, 1, t]), :] = v
            return carry

        lax.fori_loop(0, TPB, sub, 0)

    @pl.when(s < N_EXP)
    def _():
        n = cnt_ref[b, s]
        base = off_ref[b, s]

        def ffn_rows(row0, rows):
            x = jnp.concatenate([xo[pl.ds(row0 * ROW + c, rows, stride=ROW), :] for c in range(ROW)], axis=1)
            xb = x.astype(BF16)
            a = _dot(xb, wg_ref[...])
            u = _dot(xb, wu_ref[...])
            hid = (a / (1.0 + jnp.exp(-a))) * u
            out = _dot(hid.astype(BF16), wd_ref[...])
            valid = lax.broadcasted_iota(jnp.int32, (rows, 1), 0) < base + n - row0
            res = jnp.where(valid, out, x)
            for c in range(ROW):
                xo[pl.ds(row0 * ROW + c, rows, stride=ROW), :] = res[:, c * 128:(c + 1) * 128]

        def chunk(jc, carry):
            ffn_rows(base + jc * CH, CH)
            return carry

        nfull = (n + CH - CH_TAIL - 1) // CH
        lax.fori_loop(0, nfull, chunk, 0)

        @pl.when(n > nfull * CH)
        def _():
            ffn_rows(base + nfull * CH, CH_TAIL)

    @pl.when(s >= N_EXP - 1)
    def _():
        j = s - (N_EXP - 1)
        tile = b * TPB + j
        g2 = mod_ref[5, pl.ds(_mod_row(tile), 1), :]
        for t in range(TM):
            a = xo[_row(pos_ref[j, 0, t]), :]
            u = xo[_row(pos_ref[j, 1, t]), :]
            stg[t * ROW:(t + 1) * ROW, :] = w_ref[j, 0, t] * a + w_ref[j, 1, t] * u
        y = jnp.concatenate([stg[pl.ds(c, TM, stride=ROW), :] for c in range(ROW)], axis=1)
        res = x_ref[...] + g2 * y
        if final:
            ms = jnp.mean(res * res, axis=-1, keepdims=True)
            res = res * lax.rsqrt(ms + EPS) * fg_ref[...]

            @pl.when(tile < NPT)
            def _():
                op_ref[...] = res

            @pl.when(tile >= NPT)
            def _():
                os_ref[...] = res
        else:
            o_ref[...] = res


def _experts(cnt, off, pos, wts, h, wg, wu, wd, x, mod_l, fg, *, layer, final):
    def tile_of(b, s):
        return b * TPB + jnp.clip(s - (N_EXP - 1), 0, TPB - 1)

    def expert_of(b, s, *_):
        return (layer, jnp.minimum(s, N_EXP - 1), 0, 0)

    smem_blk = pl.BlockSpec((None, TPB, 2, TM), lambda b, s, *_: (b, 0, 0, 0), memory_space=pltpu.SMEM)
    if final:
        out_specs = [pl.BlockSpec((TM, D), lambda b, s, *_: (jnp.minimum(tile_of(b, s), NPT - 1), 0)),
                     pl.BlockSpec((TM, D), lambda b, s, *_: (jnp.maximum(tile_of(b, s) - NPT, 0), 0))]
        out_shape = [jax.ShapeDtypeStruct((N_P, D), F32), jax.ShapeDtypeStruct((N_S, D), F32)]
    else:
        out_specs = pl.BlockSpec((TM, D), lambda b, s, *_: (tile_of(b, s), 0))
        out_shape = jax.ShapeDtypeStruct((N_TOK, D), F32)
    grid_spec = pltpu.PrefetchScalarGridSpec(
        num_scalar_prefetch=2,
        grid=(NB, N_EXP + TPB - 1),
        in_specs=[
            smem_blk,
            smem_blk,
            pl.BlockSpec((BT, D), lambda b, s, *_: (b, 0)),
            pl.BlockSpec((None, None, D, D_EXP), expert_of),
            pl.BlockSpec((None, None, D, D_EXP), expert_of),
            pl.BlockSpec((None, None, D_EXP, D), expert_of),
            pl.BlockSpec((TM, D), lambda b, s, *_: (tile_of(b, s), 0)),
            pl.BlockSpec((6, MOD_ROWS, D), lambda b, s, *_: (0, 0, 0)),
            pl.BlockSpec((1, D), lambda b, s, *_: (0, 0)),
        ],
        out_specs=out_specs,
        scratch_shapes=[pltpu.VMEM((XO_ROWS * ROW, 128), F32), pltpu.VMEM((TM * ROW, 128), F32)],
    )
    return pl.pallas_call(
        functools.partial(_experts_kernel, final=final),
        grid_spec=grid_spec,
        out_shape=out_shape,
        compiler_params=_cparams("arbitrary", "arbitrary"),
        name="experts_final" if final else "experts",
    )(cnt, off, pos, wts, h, wg, wu, wd, x, mod_l, fg)


def _moe_sparse(x, mod_l, g, rw2, rb_col, wg, wu, wd, fg, *, layer, final):
    h, pos, wts, cnt, off = _router(x, mod_l, g, rw2, rb_col)
    cnt = cnt[:, :, 0].astype(jnp.int32)
    off = off[:, :, 0].astype(jnp.int32)
    return _experts(cnt, off, pos, wts, h, wg, wu, wd, x, mod_l, fg, layer=layer, final=final)


def kernel(x_prompt, x_sample, cache_na_k, cache_na_v, cache_sw_k, cache_sw_v, c, c_ctx, mod_w, mod_b, norm_mix_g, norm_ffn_g, ev_w_in, ev_pool_w, ev_pool_scale, ev_conv_w, ev_w_out, od_w_in, od_rpb, od_sink, od_w_out, router_w, router_b, moe_w_gate, moe_w_up, moe_w_down, final_norm_g):
    xp = x_prompt.reshape(N_P, D)
    xs = x_sample.reshape(N_S, D)
    cvec = jnp.concatenate([c_ctx[None, :], c, jnp.zeros((MOD_ROWS - 1 - DEC_BATCH, D), F32)], axis=0)
    mod = _modulation(cvec, mod_w, mod_b)

    rw_pad = jnp.pad(router_w, ((0, 0), (0, 128 - N_EXP)))
    rw_hi = rw_pad.astype(BF16)
    rw2 = jnp.stack([rw_hi, (rw_pad - rw_hi.astype(F32)).astype(BF16)])
    rb_col = jnp.pad(router_b, (0, 128 - N_EXP)).reshape(128, 1)
    wg, wu, wd = moe_w_gate.astype(BF16), moe_w_up.astype(BF16), moe_w_down.astype(BF16)

    conv_w = jnp.pad(ev_conv_w[0], ((0, 8 - ev_conv_w.shape[1]), (0, 0)))
    x = _even_layer(xp, xs, mod[0], norm_mix_g[0:1], ev_w_in[0].astype(BF16), ev_pool_w[0].astype(BF16),
                    ev_pool_scale[0:1], conv_w, ev_w_out[0].astype(BF16))
    fg = final_norm_g.reshape(1, D)
    x = _moe_sparse(x, mod[0], norm_ffn_g[0:1], rw2, rb_col, wg, wu, wd, fg, layer=0, final=False)

    cos, sin = _rope_tables()
    w_in = od_w_in[0].astype(BF16)
    g1 = norm_mix_g[1:2]
    qna_p, kna_p, vna_p, qsw_p, ksw_p, vsw_p, nak, nav, swk, swv = _odd_in(x, mod[1], g1, w_in, cos, sin, prompt=True)
    qna_s, kna_s, vna_s, qsw_s, ksw_s, vsw_s = _odd_in(x, mod[1], g1, w_in, cos, sin, prompt=False)
    sink = od_sink[0]
    ona_p, osw_p = _ctx_attn(sink, qna_p, kna_p, vna_p, qsw_p, ksw_p, vsw_p)
    bias = _na_bias_blocks(_na_bias(od_rpb[0]))
    ona_s = _na_attn(qna_s, kna_s, vna_s,
                     cache_na_k[:, 0].reshape(DEC_BATCH, PAST, NA_W).astype(BF16),
                     cache_na_v[:, 0].reshape(DEC_BATCH, PAST, NA_W).astype(BF16), bias)
    osw_s = _sw_attn(sink, qsw_s, ksw_s, vsw_s,
                     cache_sw_k[:, 0].reshape(DEC_BATCH, PAST, SWKV_W).astype(BF16),
                     cache_sw_v[:, 0].reshape(DEC_BATCH, PAST, SWKV_W).astype(BF16))
    x = _odd_out(ona_p, ona_s, osw_p, osw_s, x, mod[1], od_w_out[0].astype(BF16))
    y_prompt, y_sample = _moe_sparse(x, mod[1], norm_ffn_g[1:2], rw2, rb_col, wg, wu, wd, fg, layer=1, final=True)
    y_prompt = y_prompt.reshape(BATCH, SEQ, D)
    y_sample = y_sample.reshape(DEC_BATCH, DEC_SEQ, D)
    new_na_k = nak.reshape(BATCH, 1, SEQ, NA_H, DH)
    new_na_v = nav.reshape(BATCH, 1, SEQ, NA_H, DH)
    new_sw_k = swk.reshape(BATCH, 1, SEQ, SW_KV, DH)
    new_sw_v = swv.reshape(BATCH, 1, SEQ, SW_KV, DH)
    return (y_prompt, y_sample, new_na_k, new_na_v, new_sw_k, new_sw_v)
```

```python
import functools

import jax
import jax.numpy as jnp
import numpy as np
from jax import lax
from jax.experimental import pallas as pl
from jax.experimental.pallas import tpu as pltpu

D = 1024
BATCH = 16
SEQ = 256
DEC_BATCH = 4
DEC_SEQ = 4096
PAST = 512
GRID_W = 64
ROWS = DEC_SEQ // GRID_W
DH = 64
POOL_W = 512
POOL_WINDOWS = (2, 4, 8, 16)
POOL_GW = 128
CONV_W = 512
EVEN_IN = POOL_W + 3 * CONV_W
NA_H = 8
NA_ROWS = 8
NA_COLS = 16
SW_H = 8
SW_KV = 2
SW_G = SW_H // SW_KV
SW_WIN = 128
ABLK = 128
NA_W = NA_H * DH
SWQ_W = SW_H * DH
SWKV_W = SW_KV * DH
ODD_IN = 3 * NA_W + SWQ_W + 2 * SWKV_W
N_EXP = 16
N_GRP = 4
EPG = 4
D_EXP = 512
EPS = 1e-6
NEG = -1e30
ROPE_BASE = 10000.0
QK_SCALE = DH ** -0.5
assert QK_SCALE == 0.125

N_P = BATCH * SEQ
N_S = DEC_BATCH * DEC_SEQ
N_TOK = N_P + N_S
MOD_ROWS = 8

TM = 256
NPT = N_P // TM
TPS = DEC_SEQ // TM
NT = N_TOK // TM
HALO = 8
CAST_STEPS = 64

F32 = jnp.float32
BF16 = jnp.bfloat16
VMEM_LIMIT = 56 * 1024 * 1024


def _cparams(*sem):
    return pltpu.CompilerParams(dimension_semantics=sem, vmem_limit_bytes=VMEM_LIMIT)


def _mod_row(i):
    return jnp.where(i < NPT, 0, 1 + (i - NPT) // TPS)


def _rms_mod(x, g, shift, scale):
    ms = jnp.mean(x * x, axis=-1, keepdims=True)
    y = x * lax.rsqrt(ms + EPS) * g
    return y * (1.0 + scale) + shift


def _dot(a, b):
    return jnp.dot(a, b, preferred_element_type=F32)


def _dot_nt(a, b):
    return lax.dot_general(a, b, (((1,), (1,)), ((), ())), preferred_element_type=F32)


def _mod_kernel(cv_ref, w_ref, b_ref, o_ref):
    cv = cv_ref[...]
    a = cv / (1.0 + jnp.exp(-cv))
    o_ref[...] = jnp.dot(a, w_ref[...], preferred_element_type=F32,
                         precision=lax.Precision.HIGHEST) + b_ref[...]


def _modulation(cvec, mod_w, mod_b):
    depth = mod_w.shape[0]
    return pl.pallas_call(
        _mod_kernel,
        grid=(depth, 6),
        in_specs=[
            pl.BlockSpec((MOD_ROWS, D), lambda l, j: (0, 0)),
            pl.BlockSpec((None, D, D), lambda l, j: (l, 0, j)),
            pl.BlockSpec((None, None, 1, D), lambda l, j: (l, j, 0, 0)),
        ],
        out_specs=pl.BlockSpec((None, None, MOD_ROWS, D), lambda l, j: (l, j, 0, 0)),
        out_shape=jax.ShapeDtypeStruct((depth, 6, MOD_ROWS, D), F32),
        compiler_params=_cparams("arbitrary", "arbitrary"),
        name="modulation",
    )(cvec, mod_w, mod_b.reshape(depth, 6, 1, D))


def _dual_specs(tm, width, npt):
    return [
        pl.BlockSpec((tm, width), lambda i: (jnp.minimum(i, npt - 1), 0)),
        pl.BlockSpec((tm, width), lambda i: (jnp.maximum(i - npt, 0), 0)),
    ]


def _even_kernel(xp_ref, xs_ref, xprev_ref, xnext_ref, mod_ref, g_ref, wi_ref, pw_ref, ps_ref, cw_ref,
                 wo_ref, wgf_ref, wuf_ref, wdf_ref, o_ref, wgb_ref, wub_ref, wdb_ref, pext, uext):
    i = pl.program_id(0)

    @pl.when(i < CAST_STEPS)
    def _():
        wgb_ref[...] = wgf_ref[...].astype(BF16)
        wub_ref[...] = wuf_ref[...].astype(BF16)
        wdb_ref[...] = wdf_ref[...].astype(BF16)

    r = _mod_row(i)
    is_p = i < NPT
    t0 = jnp.where(is_p, 0, ((i - NPT) % TPS) * TM)
    seq_len = jnp.where(is_p, SEQ, DEC_SEQ)
    first = t0 == 0
    last = t0 + TM == seq_len

    x = jnp.where(is_p, xp_ref[...], xs_ref[...])
    xe = jnp.concatenate([xprev_ref[...], x, xnext_ref[...]], axis=0)
    h = _rms_mod(xe, g_ref[...], mod_ref[0, pl.ds(r, 1), :], mod_ref[1, pl.ds(r, 1), :])
    ze = _dot(h.astype(BF16), wi_ref[...])
    row = lax.broadcasted_iota(jnp.int32, (TM + 2 * HALO, 1), 0)
    outside = jnp.logical_or(jnp.logical_and(first, row < HALO), jnp.logical_and(last, row >= HALO + TM))
    ze = jnp.where(outside, 0.0, ze)
    pext[...] = ze[:, :POOL_W]
    uext[...] = ze[:, POOL_W + 2 * CONV_W:] * ze[:, POOL_W:POOL_W + CONV_W]
    p = ze[HALO:HALO + TM, :POOL_W]
    gb = ze[HALO:HALO + TM, POOL_W + CONV_W:POOL_W + 2 * CONV_W]

    t = t0 + lax.broadcasted_iota(jnp.int32, (TM, 1), 0)
    y = jnp.zeros((TM, D), F32)
    for g, w in enumerate(POOL_WINDOWS):
        cols = slice(g * POOL_GW, (g + 1) * POOL_GW)
        acc = jnp.zeros((TM, POOL_GW), F32)
        for k in range(-(w // 2), w - (w // 2)):
            acc = acc + pext[pl.ds(HALO + k, TM), cols]
        lo = jnp.maximum(t - w // 2, 0)
        hi = jnp.minimum(t + (w - 1 - w // 2), seq_len - 1)
        cnt = (hi - lo + 1).astype(F32)
        d = acc / cnt - p[:, cols]
        a = _dot(d.astype(BF16), pw_ref[g]) * ps_ref[:, cols]
        y = y + _dot(a.astype(BF16), wo_ref[cols, :])
    conv = (uext[pl.ds(HALO - 1, TM), :] * cw_ref[0:1, :] + uext[pl.ds(HALO, TM), :] * cw_ref[1:2, :]
            + uext[pl.ds(HALO + 1, TM), :] * cw_ref[2:3, :])
    b = gb * conv
    y = y + _dot(b.astype(BF16), wo_ref[POOL_W:, :])
    o_ref[...] = x + mod_ref[2, pl.ds(r, 1), :] * y


def _even_layer(xp, xs, mod_l, g, w_in, pool_w, pool_scale, conv_w, w_out, wg, wu, wd):
    hb = TM // HALO
    nhb = N_S // HALO
    wg2, wu2, wd2 = wg.reshape(-1, D_EXP), wu.reshape(-1, D_EXP), wd.reshape(-1, D)
    rows_in, rows_out = wg2.shape[0] // CAST_STEPS, wd2.shape[0] // CAST_STEPS
    assert rows_in * CAST_STEPS == wg2.shape[0] and rows_out * CAST_STEPS == wd2.shape[0] and CAST_STEPS <= NT
    cast_blk = lambda i: (jnp.minimum(i, CAST_STEPS - 1), 0)
    cast_specs = [pl.BlockSpec((rows_in, D_EXP), cast_blk), pl.BlockSpec((rows_in, D_EXP), cast_blk),
                  pl.BlockSpec((rows_out, D), cast_blk)]
    outs = pl.pallas_call(
        _even_kernel,
        grid=(NT,),
        in_specs=_dual_specs(TM, D, NPT) + [
            pl.BlockSpec((HALO, D), lambda i: (jnp.maximum((i - NPT) * hb - 1, 0), 0)),
            pl.BlockSpec((HALO, D), lambda i: (jnp.clip((i - NPT + 1) * hb, 0, nhb - 1), 0)),
            pl.BlockSpec((6, MOD_ROWS, D), lambda i: (0, 0, 0)),
            pl.BlockSpec((1, D), lambda i: (0, 0)),
            pl.BlockSpec((D, EVEN_IN), lambda i: (0, 0)),
            pl.BlockSpec((4, POOL_GW, POOL_GW), lambda i: (0, 0, 0)),
            pl.BlockSpec((1, POOL_W), lambda i: (0, 0)),
            pl.BlockSpec((8, CONV_W), lambda i: (0, 0)),
            pl.BlockSpec((D, D), lambda i: (0, 0)),
        ] + cast_specs,
        out_specs=[pl.BlockSpec((TM, D), lambda i: (i, 0))] + cast_specs,
        out_shape=[jax.ShapeDtypeStruct((N_TOK, D), F32), jax.ShapeDtypeStruct(wg2.shape, BF16),
                   jax.ShapeDtypeStruct(wu2.shape, BF16), jax.ShapeDtypeStruct(wd2.shape, BF16)],
        scratch_shapes=[pltpu.VMEM((TM + 2 * HALO, POOL_W), F32),
                        pltpu.VMEM((TM + 2 * HALO, CONV_W), F32)],
        compiler_params=_cparams("arbitrary"),
        name="even_layer",
    )(xp, xs, xs, xs, mod_l, g, w_in, pool_w, pool_scale, conv_w, w_out, wg2, wu2, wd2)
    return outs[0], outs[1].reshape(wg.shape), outs[2].reshape(wu.shape), outs[3].reshape(wd.shape)


def _rope(x, cos, sin_signed):
    n = x.shape[1] // 128
    cosf = jnp.concatenate([cos] * n, axis=1) if n > 1 else cos
    sinf = jnp.concatenate([sin_signed] * n, axis=1) if n > 1 else sin_signed
    w = x.shape[1]
    lane = lax.broadcasted_iota(jnp.int32, x.shape, 1)
    up = pltpu.roll(x, w - DH // 4, 1)
    dn = pltpu.roll(x, DH // 4, 1)
    rot = jnp.where((lane % (DH // 2)) < DH // 4, up, dn)
    return x * cosf + rot * sinf


def _odd_in_kernel(x_ref, mod_ref, g_ref, w_ref, cos_ref, sin_ref, *out_refs, tile0, rope, kv_f32):
    i = pl.program_id(0) + tile0
    r = _mod_row(i)
    h = _rms_mod(x_ref[...], g_ref[...], mod_ref[0, pl.ds(r, 1), :], mod_ref[1, pl.ds(r, 1), :])
    z = _dot(h.astype(BF16), w_ref[...])
    qna_ref, kna_ref, vna_ref, qsw_ref, ksw_ref, vsw_ref = out_refs[:6]
    c0 = 0
    qna = z[:, 0:NA_W]
    kna = z[:, NA_W:2 * NA_W]
    vna = z[:, 2 * NA_W:3 * NA_W]
    c0 = 3 * NA_W
    qsw = z[:, c0:c0 + SWQ_W]
    ksw = z[:, c0 + SWQ_W:c0 + SWQ_W + SWKV_W]
    vsw = z[:, c0 + SWQ_W + SWKV_W:]
    if kv_f32:
        for ref, val in zip(out_refs[6:10], (kna, vna, ksw, vsw)):
            heads = val.shape[1] // DH
            for hh in range(heads):
                ref[pl.ds(hh, SEQ, stride=heads), :] = val[:, hh * DH:(hh + 1) * DH]
    if rope:
        cos = cos_ref[...]
        sin = sin_ref[...]
        qsw = _rope(qsw, cos, sin)
        ksw = _rope(ksw, cos, sin)
    qna_ref[...] = (qna * QK_SCALE).astype(BF16)
    kna_ref[...] = kna.astype(BF16)
    vna_ref[...] = vna.astype(BF16)
    qsw_ref[...] = (qsw * QK_SCALE).astype(BF16)
    ksw_ref[...] = ksw.astype(BF16)
    vsw_ref[...] = vsw.astype(BF16)


def _odd_in(x, mod_l, g, w, cos, sin, *, prompt):
    tile0 = 0 if prompt else NPT
    nt = NPT if prompt else NT - NPT
    n = nt * TM
    widths = [NA_W, NA_W, NA_W, SWQ_W, SWKV_W, SWKV_W]
    out_shape = [jax.ShapeDtypeStruct((n, w_), BF16) for w_ in widths]
    out_specs = [pl.BlockSpec((TM, w_), lambda i: (i, 0)) for w_ in widths]
    if prompt:
        assert TM == SEQ
        for heads in (NA_H, NA_H, SW_KV, SW_KV):
            out_shape.append(jax.ShapeDtypeStruct((BATCH * SEQ * heads, DH), F32))
            out_specs.append(pl.BlockSpec((SEQ * heads, DH), lambda i: (i, 0)))
    return pl.pallas_call(
        functools.partial(_odd_in_kernel, tile0=tile0, rope=not prompt, kv_f32=prompt),
        grid=(nt,),
        in_specs=[
            pl.BlockSpec((TM, D), lambda i: (i + tile0, 0)),
            pl.BlockSpec((6, MOD_ROWS, D), lambda i: (0, 0, 0)),
            pl.BlockSpec((1, D), lambda i: (0, 0)),
            pl.BlockSpec((D, ODD_IN), lambda i: (0, 0)),
            pl.BlockSpec((TM, 128), lambda i: (i % TPS, 0)),
            pl.BlockSpec((TM, 128), lambda i: (i % TPS, 0)),
        ],
        out_specs=out_specs,
        out_shape=out_shape,
        compiler_params=_cparams("parallel"),
        name="odd_in_prompt" if prompt else "odd_in_latent",
    )(x, mod_l, g, w, cos, sin)


def _rope_tables():
    t = np.arange(DEC_SEQ)
    quarter = DH // 4
    inv = 1.0 / (ROPE_BASE ** (np.arange(quarter, dtype=np.float64) / quarter))

    def cos_sin(pos):
        ang = pos.astype(np.float64)[:, None] * inv[None, :]
        ang = np.concatenate([ang, ang], axis=-1)
        return np.cos(ang), np.sin(ang)

    cr, sr = cos_sin(t // GRID_W)
    cc, sc = cos_sin(t % GRID_W)
    cos = np.concatenate([cr, cc], axis=-1)
    sin = np.concatenate([sr, sc], axis=-1)
    sign = np.where((np.arange(DH) % (DH // 2)) < DH // 4, -1.0, 1.0)
    sin = sin * sign[None, :]
    cos = np.concatenate([cos, cos], axis=-1).astype(np.float32)
    sin = np.concatenate([sin, sin], axis=-1).astype(np.float32)
    return jnp.asarray(cos), jnp.asarray(sin)


def _softmax_pv(segs, sink=None):
    m = None
    for s, _ in segs:
        sm = jnp.max(s, axis=-1, keepdims=True)
        m = sm if m is None else jnp.maximum(m, sm)
    if sink is not None:
        m = jnp.maximum(m, sink)
    den = None
    acc = None
    for s, v in segs:
        p = jnp.exp(s - m)
        ps = jnp.sum(p, axis=-1, keepdims=True)
        den = ps if den is None else den + ps
        pv = _dot(p.astype(BF16), v)
        acc = pv if acc is None else acc + pv
    if sink is not None:
        den = den + jnp.exp(sink - m)
    return acc / den


def _sink_col(sink_ref, g, rows_per_head):
    row = lax.broadcasted_iota(jnp.int32, (SW_G * rows_per_head, 1), 0)
    col = jnp.zeros((SW_G * rows_per_head, 1), F32)
    for r in range(SW_G):
        col = jnp.where(row // rows_per_head == r, sink_ref[g * SW_G + r], col)
    return col


def _ctx_attn_kernel(sink_ref, qna_ref, kna_ref, vna_ref, qsw_ref, ksw_ref, vsw_ref, ona_ref, osw_ref):
    lane = lax.broadcasted_iota(jnp.int32, (SEQ, 2 * DH), 1)
    for p in range(NA_H // 2):
        cols = slice(p * 2 * DH, (p + 1) * 2 * DH)
        q = qna_ref[:, cols]
        k = kna_ref[:, cols]
        v = vna_ref[:, cols]
        outs = []
        for half in range(2):
            mine = (lane < DH) if half == 0 else (lane >= DH)
            qm = jnp.where(mine, q, jnp.zeros_like(q))
            outs.append(_softmax_pv([(_dot_nt(qm, k), v)]))
        ona_ref[:, cols] = jnp.where(lane < DH, outs[0], outs[1]).astype(BF16)
    outs = []
    for g in range(SW_KV):
        kc = slice(g * DH, (g + 1) * DH)
        q = jnp.concatenate([qsw_ref[:, (g * SW_G + r) * DH:(g * SW_G + r + 1) * DH] for r in range(SW_G)], axis=0)
        s = _dot_nt(q, ksw_ref[:, kc])
        o = _softmax_pv([(s, vsw_ref[:, kc])], sink=_sink_col(sink_ref, g, SEQ))
        outs.extend(o[r * SEQ:(r + 1) * SEQ, :] for r in range(SW_G))
    osw_ref[...] = jnp.concatenate(outs, axis=1).astype(BF16)


def _ctx_attn(sink, qna, kna, vna, qsw, ksw, vsw):
    def spec(w):
        return pl.BlockSpec((SEQ, w), lambda b: (b, 0))

    return pl.pallas_call(
        _ctx_attn_kernel,
        grid=(BATCH,),
        in_specs=[pl.BlockSpec(memory_space=pltpu.SMEM), spec(NA_W), spec(NA_W), spec(NA_W),
                  spec(SWQ_W), spec(SWKV_W), spec(SWKV_W)],
        out_specs=[spec(NA_W), spec(SWQ_W)],
        out_shape=[jax.ShapeDtypeStruct((N_P, NA_W), BF16), jax.ShapeDtypeStruct((N_P, SWQ_W), BF16)],
        compiler_params=_cparams("parallel"),
        name="ctx_attn",
    )(sink, qna, kna, vna, qsw, ksw, vsw)


def _na_bias_kernel(rpb_ref, o_ref):
    h = pl.program_id(0)
    cq = lax.broadcasted_iota(jnp.int32, (GRID_W, GRID_W), 0)
    ck = lax.broadcasted_iota(jnp.int32, (GRID_W, GRID_W), 1)
    cstart = jnp.clip(cq - NA_COLS // 2, 0, GRID_W - NA_COLS)
    ok = (ck >= cstart) & (ck < cstart + NA_COLS)
    dc = jnp.clip(ck - cq + NA_COLS - 1, 0, 2 * NA_COLS - 2)
    ndc = 2 * NA_COLS - 1
    for d in range(2 * NA_ROWS - 1):
        b = jnp.zeros((GRID_W, GRID_W), F32)
        for e in range(ndc):
            b = jnp.where(dc == e, rpb_ref[h, d * ndc + e], b)
        o_ref[d] = jnp.where(ok, b, NEG)


def _na_bias(rpb):
    nd = 2 * NA_ROWS - 1
    return pl.pallas_call(
        _na_bias_kernel,
        grid=(NA_H,),
        in_specs=[pl.BlockSpec(memory_space=pltpu.SMEM)],
        out_specs=pl.BlockSpec((None, nd, GRID_W, GRID_W), lambda h: (h, 0, 0, 0)),
        out_shape=jax.ShapeDtypeStruct((NA_H, nd, GRID_W, GRID_W), F32),
        compiler_params=_cparams("parallel"),
        name="na_bias",
    )(rpb.reshape(NA_H, nd * (2 * NA_COLS - 1)))


NA_QB = 4
NA_KR = 12
NA_NQB = ROWS // NA_QB


def _na_bias_blocks(bias):
    neg = jnp.full((NA_H, GRID_W, GRID_W), NEG, F32)
    cases = []
    for case in range(3):
        rows = []
        for i in range(NA_QB):
            blocks = []
            for j in range(NA_KR):
                if case == 0:
                    valid, dr = j < NA_ROWS, j - i + NA_ROWS - 1
                elif case == 1:
                    valid, dr = i <= j < i + NA_ROWS, j - i + NA_ROWS // 2 - 1
                else:
                    valid, dr = j >= NA_KR - NA_ROWS, j - i + NA_ROWS - 1 - (NA_KR - NA_QB)
                blocks.append(bias[:, dr] if valid else neg)
            rows.append(jnp.concatenate(blocks, axis=2))
        cases.append(jnp.concatenate(rows, axis=1))
    return jnp.stack(cases)


def _na_kernel(q_ref, k_ref, v_ref, kc_ref, vc_ref, bias_ref, o_ref):
    r0 = pl.program_id(1) * NA_QB
    start = pl.multiple_of(jnp.clip(r0 - NA_ROWS // 2, 0, ROWS - NA_KR) * GRID_W, GRID_W)
    nq = NA_QB * GRID_W
    lane = lax.broadcasted_iota(jnp.int32, (nq, 2 * DH), 1)
    for p in range(NA_H // 2):
        cols = slice(p * 2 * DH, (p + 1) * 2 * DH)
        q = q_ref[:, cols]
        kl = k_ref[pl.ds(start, NA_KR * GRID_W), cols]
        vl = v_ref[pl.ds(start, NA_KR * GRID_W), cols]
        kc = kc_ref[:, cols]
        vc = vc_ref[:, cols]
        outs = []
        for half in range(2):
            mine = (lane < DH) if half == 0 else (lane >= DH)
            qm = jnp.where(mine, q, jnp.zeros_like(q))
            s_loc = _dot_nt(qm, kl) + bias_ref[2 * p + half]
            s_ctx = _dot_nt(qm, kc)
            outs.append(_softmax_pv([(s_loc, vl), (s_ctx, vc)]))
        o_ref[:, cols] = jnp.where(lane < DH, outs[0], outs[1]).astype(BF16)


def _na_attn(q, k, v, kc, vc, bias):
    nq = NA_QB * GRID_W

    def bias_case(b, rb):
        return (jnp.where(rb == 0, 0, jnp.where(rb == NA_NQB - 1, 2, 1)), 0, 0, 0)

    return pl.pallas_call(
        _na_kernel,
        grid=(DEC_BATCH, NA_NQB),
        in_specs=[
            pl.BlockSpec((nq, NA_W), lambda b, rb: (b * NA_NQB + rb, 0)),
            pl.BlockSpec((DEC_SEQ, NA_W), lambda b, rb: (b, 0)),
            pl.BlockSpec((DEC_SEQ, NA_W), lambda b, rb: (b, 0)),
            pl.BlockSpec((None, PAST, NA_W), lambda b, rb: (b, 0, 0)),
            pl.BlockSpec((None, PAST, NA_W), lambda b, rb: (b, 0, 0)),
            pl.BlockSpec((None, NA_H, nq, NA_KR * GRID_W), bias_case),
        ],
        out_specs=pl.BlockSpec((nq, NA_W), lambda b, rb: (b * NA_NQB + rb, 0)),
        out_shape=jax.ShapeDtypeStruct((N_S, NA_W), BF16),
        compiler_params=_cparams("parallel", "arbitrary"),
        name="na_attn",
    )(q, k, v, kc, vc, bias)


def _sw_window_bias():
    q = np.arange(SW_G * ABLK)[:, None] % ABLK
    k = np.arange(3 * ABLK)[None, :]
    tables = [np.where(np.abs(q + lead - k) <= SW_WIN, 0.0, NEG) for lead in (0, ABLK, 2 * ABLK)]
    return jnp.asarray(np.stack(tables).astype(np.float32))


def _sw_kernel(sink_ref, q_ref, k_ref, v_ref, kc_ref, vc_ref, wb_ref, o_ref):
    j = pl.program_id(1)
    nk = 3 * ABLK
    start = pl.multiple_of(jnp.clip((j - 1) * ABLK, 0, DEC_SEQ - nk), ABLK)
    outs = []
    for g in range(SW_KV):
        kcols = slice(g * DH, (g + 1) * DH)
        q = jnp.concatenate([q_ref[:, (g * SW_G + r) * DH:(g * SW_G + r + 1) * DH] for r in range(SW_G)], axis=0)
        kw = k_ref[pl.ds(start, nk), kcols]
        vw = v_ref[pl.ds(start, nk), kcols]
        s_w = _dot_nt(q, kw) + wb_ref[...]
        s_c = _dot_nt(q, kc_ref[:, kcols])
        o = _softmax_pv([(s_w, vw), (s_c, vc_ref[:, kcols])], sink=_sink_col(sink_ref, g, ABLK))
        outs.extend(o[r * ABLK:(r + 1) * ABLK, :] for r in range(SW_G))
    o_ref[...] = jnp.concatenate(outs, axis=1).astype(BF16)


def _sw_attn(sink, q, k, v, kc, vc):
    nb = DEC_SEQ // ABLK
    return pl.pallas_call(
        _sw_kernel,
        grid=(DEC_BATCH, nb),
        in_specs=[
            pl.BlockSpec(memory_space=pltpu.SMEM),
            pl.BlockSpec((ABLK, SWQ_W), lambda b, j: (b * nb + j, 0)),
            pl.BlockSpec((DEC_SEQ, SWKV_W), lambda b, j: (b, 0)),
            pl.BlockSpec((DEC_SEQ, SWKV_W), lambda b, j: (b, 0)),
            pl.BlockSpec((None, PAST, SWKV_W), lambda b, j: (b, 0, 0)),
            pl.BlockSpec((None, PAST, SWKV_W), lambda b, j: (b, 0, 0)),
            pl.BlockSpec((None, SW_G * ABLK, 3 * ABLK),
                         lambda b, j: (jnp.where(j == 0, 0, jnp.where(j == nb - 1, 2, 1)), 0, 0)),
        ],
        out_specs=pl.BlockSpec((ABLK, SWQ_W), lambda b, j: (b * nb + j, 0)),
        out_shape=jax.ShapeDtypeStruct((N_S, SWQ_W), BF16),
        compiler_params=_cparams("parallel", "arbitrary"),
        name="sw_attn",
    )(sink, q, k, v, kc, vc, _sw_window_bias())


def _odd_out_kernel(nap_ref, nas_ref, swp_ref, sws_ref, x_ref, mod_ref, wo_ref, o_ref):
    i = pl.program_id(0)
    r = _mod_row(i)
    is_p = i < NPT
    ona = jnp.where(is_p, nap_ref[...], nas_ref[...])
    osw = jnp.where(is_p, swp_ref[...], sws_ref[...])
    y = _dot(ona, wo_ref[:NA_W, :]) + _dot(osw, wo_ref[NA_W:, :])
    o_ref[...] = x_ref[...] + mod_ref[2, pl.ds(r, 1), :] * y


def _odd_out(nap, nas, swp, sws, x, mod_l, w_out):
    return pl.pallas_call(
        _odd_out_kernel,
        grid=(NT,),
        in_specs=_dual_specs(TM, NA_W, NPT) + _dual_specs(TM, SWQ_W, NPT) + [
            pl.BlockSpec((TM, D), lambda i: (i, 0)),
            pl.BlockSpec((6, MOD_ROWS, D), lambda i: (0, 0, 0)),
            pl.BlockSpec((D, D), lambda i: (0, 0)),
        ],
        out_specs=pl.BlockSpec((TM, D), lambda i: (i, 0)),
        out_shape=jax.ShapeDtypeStruct((N_TOK, D), F32),
        compiler_params=_cparams("parallel"),
        name="odd_out",
    )(nap, nas, swp, sws, x, mod_l, w_out)


BT = 2560
NB = N_TOK // BT
TPB = BT // TM
CH = 256
CH_TAIL = 128
XO_ROWS = 2 * BT + CH
ROW = 8
assert D == ROW * 128


def _route_sparse(probs, sel):
    rank = []
    score = []
    for g in range(N_GRP):
        ids = range(g * EPG, (g + 1) * EPG)
        tot = None
        for e in ids:
            rk = jnp.zeros_like(sel[e])
            for j in ids:
                if j < e:
                    rk = rk + (sel[j] >= sel[e]).astype(F32)
                elif j > e:
                    rk = rk + (sel[j] > sel[e]).astype(F32)
            rank.append(rk)
            contrib = jnp.where(rk < 2.0, sel[e], 0.0)
            tot = contrib if tot is None else tot + contrib
        score.append(tot)
    top1, top2 = [], []
    for g in range(N_GRP):
        best = None
        for j in range(N_GRP):
            if j == g:
                continue
            c = (score[g] > score[j]) if j < g else (score[g] >= score[j])
            best = c if best is None else jnp.logical_and(best, c)
        for e in range(g * EPG, (g + 1) * EPG):
            top1.append(jnp.logical_and(best, rank[e] == 0.0))
            top2.append(jnp.logical_and(best, rank[e] == 1.0))
    return top1, top2


def _pick(masks, rows):
    acc = None
    for m, r in zip(masks, rows):
        v = jnp.where(m, r, 0.0)
        acc = v if acc is None else acc + v
    return acc


def _router_kernel(x_ref, mod_ref, g_ref, rw_ref, rb_ref, h_ref, pos_ref, w_ref, cnt_ref, off_ref,
                   meta, carry):
    i = pl.program_id(0)
    j = i % TPB
    r = _mod_row(i)
    h = _rms_mod(x_ref[...], g_ref[...], mod_ref[3, pl.ds(r, 1), :], mod_ref[4, pl.ds(r, 1), :])
    h_ref[...] = h.astype(BF16)
    h_hi = h.astype(BF16)
    h_lo = (h - h_hi.astype(F32)).astype(BF16)
    logits = _dot(h_lo, rw_ref[0]) + _dot(h_hi, rw_ref[1]) + _dot(h_hi, rw_ref[0])
    lt = logits.T[:N_EXP, :]
    m = jnp.max(lt, axis=0, keepdims=True)
    ex = jnp.exp(lt - m)
    pr = ex / jnp.sum(ex, axis=0, keepdims=True)
    se = pr + rb_ref[:N_EXP, :]
    probs = [pr[e:e + 1, :] for e in range(N_EXP)]
    sel = [se[e:e + 1, :] for e in range(N_EXP)]
    top1, top2 = _route_sparse(probs, sel)

    @pl.when(j == 0)
    def _():
        carry[...] = jnp.zeros_like(carry)

    member = jnp.concatenate([jnp.logical_or(a, b).astype(F32) for a, b in zip(top1, top2)], axis=0)
    s_idx = lax.broadcasted_iota(jnp.int32, (TM, TM), 0)
    t_idx = lax.broadcasted_iota(jnp.int32, (TM, TM), 1)
    before = jnp.where(s_idx < t_idx, 1.0, 0.0).astype(BF16)
    seen = _dot(member.astype(BF16), before) + carry[:, 0:1]
    seen_rows = [seen[e:e + 1, :] for e in range(N_EXP)]
    ids = [jnp.full((1, TM), float(e), F32) for e in range(N_EXP)]
    p1 = _pick(top1, probs)
    p2 = _pick(top2, probs)
    den = p1 + p2
    w_ref[j, 0:1, :] = p1 / den
    w_ref[j, 1:2, :] = p2 / den
    meta[j, 0:1, :] = _pick(top1, ids)
    meta[j, 1:2, :] = _pick(top2, ids)
    meta[j, 2:3, :] = _pick(top1, seen_rows)
    meta[j, 3:4, :] = _pick(top2, seen_rows)
    carry[...] = carry[...] + jnp.sum(member, axis=1, keepdims=True)

    @pl.when(j == TPB - 1)
    def _():
        cnt = carry[...]
        offs = [jnp.zeros((1, 128), F32)]
        for e in range(1, N_EXP):
            offs.append(offs[-1] + cnt[e - 1:e, :])
        cnt_ref[...] = cnt
        off_ref[...] = jnp.concatenate(offs, axis=0)
        for jj in range(TPB):
            for k in range(2):
                eid = meta[jj, k:k + 1, :]
                pos = meta[jj, 2 + k:3 + k, :]
                for e in range(1, N_EXP):
                    pos = pos + jnp.where(eid == float(e), offs[e][:, 0:1], 0.0)
                pos_ref[jj, k:k + 1, :] = pos.astype(jnp.int32)


def _router(x, mod_l, g, rw_pad, rb_col):
    blk = lambda i: (i // TPB, 0, 0, 0)
    return pl.pallas_call(
        _router_kernel,
        grid=(NT,),
        in_specs=[
            pl.BlockSpec((TM, D), lambda i: (i, 0)),
            pl.BlockSpec((6, MOD_ROWS, D), lambda i: (0, 0, 0)),
            pl.BlockSpec((1, D), lambda i: (0, 0)),
            pl.BlockSpec((2, D, 128), lambda i: (0, 0, 0)),
            pl.BlockSpec((128, 1), lambda i: (0, 0)),
        ],
        out_specs=[
            pl.BlockSpec((TM, D), lambda i: (i, 0)),
            pl.BlockSpec((None, TPB, 2, TM), blk),
            pl.BlockSpec((None, TPB, 2, TM), blk),
            pl.BlockSpec((None, N_EXP, 128), lambda i: (i // TPB, 0, 0)),
            pl.BlockSpec((None, N_EXP, 128), lambda i: (i // TPB, 0, 0)),
        ],
        out_shape=[
            jax.ShapeDtypeStruct((N_TOK, D), BF16),
            jax.ShapeDtypeStruct((NB, TPB, 2, TM), jnp.int32),
            jax.ShapeDtypeStruct((NB, TPB, 2, TM), F32),
            jax.ShapeDtypeStruct((NB, N_EXP, 128), F32),
            jax.ShapeDtypeStruct((NB, N_EXP, 128), F32),
        ],
        scratch_shapes=[pltpu.VMEM((TPB, 4, TM), F32), pltpu.VMEM((N_EXP, 128), F32)],
        compiler_params=_cparams("arbitrary"),
        name="router",
    )(x, mod_l, g, rw_pad, rb_col)


def _row(p):
    return pl.ds(pl.multiple_of(p * ROW, ROW), ROW)


def _experts_kernel(cnt_ref, off_ref, pos_ref, w_ref, h_ref, wg_ref, wu_ref, wd_ref, x_ref, mod_ref, fg_ref,
                    *rest, final):
    if final:
        op_ref, os_ref, xo, stg = rest
    else:
        o_ref, xo, stg = rest
    b = pl.program_id(0)
    s = pl.program_id(1)

    @pl.when(s == 0)
    def _():
        xo[pl.ds(2 * BT * ROW, CH * ROW), :] = jnp.zeros((CH * ROW, 128), F32)

        def sub(j, carry):
            hs = h_ref[pl.ds(pl.multiple_of(j * TM, TM), TM), :].astype(F32)
            for c in range(ROW):
                stg[pl.ds(c, TM, stride=ROW), :] = hs[:, c * 128:(c + 1) * 128]
            for t in range(TM):
                v = stg[t * ROW:(t + 1) * ROW, :]
                xo[_row(pos_ref[j, 0, t]), :] = v
                xo[_row(pos_ref[j, 1, t]), :] = v
            return carry

        lax.fori_loop(0, TPB, sub, 0)

    @pl.when(s < N_EXP)
    def _():
        n = cnt_ref[b, s]
        base = off_ref[b, s]

        def ffn_rows(row0, rows):
            x = jnp.concatenate([xo[pl.ds(row0 * ROW + c, rows, stride=ROW), :] for c in range(ROW)], axis=1)
            xb = x.astype(BF16)
            a = _dot(xb, wg_ref[...])
            u = _dot(xb, wu_ref[...])
            hid = (a / (1.0 + jnp.exp(-a))) * u
            out = _dot(hid.astype(BF16), wd_ref[...])
            valid = lax.broadcasted_iota(jnp.int32, (rows, 1), 0) < base + n - row0
            res = jnp.where(valid, out, x)
            for c in range(ROW):
                xo[pl.ds(row0 * ROW + c, rows, stride=ROW), :] = res[:, c * 128:(c + 1) * 128]

        def chunk(jc, carry):
            ffn_rows(base + jc * CH, CH)
            return carry

        nfull = (n + CH - CH_TAIL - 1) // CH
        lax.fori_loop(0, nfull, chunk, 0)

        @pl.when(n > nfull * CH)
        def _():
            ffn_rows(base + nfull * CH, CH_TAIL)

    @pl.when(s >= N_EXP - 1)
    def _():
        j = s - (N_EXP - 1)
        tile = b * TPB + j
        g2 = mod_ref[5, pl.ds(_mod_row(tile), 1), :]
        for t in range(TM):
            a = xo[_row(pos_ref[j, 0, t]), :]
            u = xo[_row(pos_ref[j, 1, t]), :]
            stg[t * ROW:(t + 1) * ROW, :] = w_ref[j, 0, t] * a + w_ref[j, 1, t] * u
        y = jnp.concatenate([stg[pl.ds(c, TM, stride=ROW), :] for c in range(ROW)], axis=1)
        res = x_ref[...] + g2 * y
        if final:
            ms = jnp.mean(res * res, axis=-1, keepdims=True)
            res = res * lax.rsqrt(ms + EPS) * fg_ref[...]

            @pl.when(tile < NPT)
            def _():
                op_ref[...] = res

            @pl.when(tile >= NPT)
            def _():
                os_ref[...] = res
        else:
            o_ref[...] = res


def _experts(cnt, off, pos, wts, h, wg, wu, wd, x, mod_l, fg, *, layer, final):
    def tile_of(b, s):
        return b * TPB + jnp.clip(s - (N_EXP - 1), 0, TPB - 1)

    def expert_of(b, s, *_):
        return (layer, jnp.minimum(s, N_EXP - 1), 0, 0)

    smem_blk = pl.BlockSpec((None, TPB, 2, TM), lambda b, s, *_: (b, 0, 0, 0), memory_space=pltpu.SMEM)
    if final:
        out_specs = [pl.BlockSpec((TM, D), lambda b, s, *_: (jnp.minimum(tile_of(b, s), NPT - 1), 0)),
                     pl.BlockSpec((TM, D), lambda b, s, *_: (jnp.maximum(tile_of(b, s) - NPT, 0), 0))]
        out_shape = [jax.ShapeDtypeStruct((N_P, D), F32), jax.ShapeDtypeStruct((N_S, D), F32)]
    else:
        out_specs = pl.BlockSpec((TM, D), lambda b, s, *_: (tile_of(b, s), 0))
        out_shape = jax.ShapeDtypeStruct((N_TOK, D), F32)
    grid_spec = pltpu.PrefetchScalarGridSpec(
        num_scalar_prefetch=2,
        grid=(NB, N_EXP + TPB - 1),
        in_specs=[
            smem_blk,
            smem_blk,
            pl.BlockSpec((BT, D), lambda b, s, *_: (b, 0)),
            pl.BlockSpec((None, None, D, D_EXP), expert_of),
            pl.BlockSpec((None, None, D, D_EXP), expert_of),
            pl.BlockSpec((None, None, D_EXP, D), expert_of),
            pl.BlockSpec((TM, D), lambda b, s, *_: (tile_of(b, s), 0)),
            pl.BlockSpec((6, MOD_ROWS, D), lambda b, s, *_: (0, 0, 0)),
            pl.BlockSpec((1, D), lambda b, s, *_: (0, 0)),
        ],
        out_specs=out_specs,
        scratch_shapes=[pltpu.VMEM((XO_ROWS * ROW, 128), F32), pltpu.VMEM((TM * ROW, 128), F32)],
    )
    return pl.pallas_call(
        functools.partial(_experts_kernel, final=final),
        grid_spec=grid_spec,
        out_shape=out_shape,
        compiler_params=_cparams("arbitrary", "arbitrary"),
        name="experts_final" if final else "experts",
    )(cnt, off, pos, wts, h, wg, wu, wd, x, mod_l, fg)


def _moe_sparse(x, mod_l, g, rw2, rb_col, wg, wu, wd, fg, *, layer, final):
    h, pos, wts, cnt, off = _router(x, mod_l, g, rw2, rb_col)
    cnt = cnt[:, :, 0].astype(jnp.int32)
    off = off[:, :, 0].astype(jnp.int32)
    return _experts(cnt, off, pos, wts, h, wg, wu, wd, x, mod_l, fg, layer=layer, final=final)


def kernel(x_prompt, x_sample, cache_na_k, cache_na_v, cache_sw_k, cache_sw_v, c, c_ctx, mod_w, mod_b, norm_mix_g, norm_ffn_g, ev_w_in, ev_pool_w, ev_pool_scale, ev_conv_w, ev_w_out, od_w_in, od_rpb, od_sink, od_w_out, router_w, router_b, moe_w_gate, moe_w_up, moe_w_down, final_norm_g):
    xp = x_prompt.reshape(N_P, D)
    xs = x_sample.reshape(N_S, D)
    cvec = jnp.concatenate([c_ctx[None, :], c, jnp.zeros((MOD_ROWS - 1 - DEC_BATCH, D), F32)], axis=0)
    mod = _modulation(cvec, mod_w, mod_b)

    rw_pad = jnp.pad(router_w, ((0, 0), (0, 128 - N_EXP)))
    rw_hi = rw_pad.astype(BF16)
    rw2 = jnp.stack([rw_hi, (rw_pad - rw_hi.astype(F32)).astype(BF16)])
    rb_col = jnp.pad(router_b, (0, 128 - N_EXP)).reshape(128, 1)

    conv_w = jnp.pad(ev_conv_w[0], ((0, 8 - ev_conv_w.shape[1]), (0, 0)))
    x, wg, wu, wd = _even_layer(xp, xs, mod[0], norm_mix_g[0:1], ev_w_in[0].astype(BF16),
                                ev_pool_w[0].astype(BF16), ev_pool_scale[0:1], conv_w, ev_w_out[0].astype(BF16),
                                moe_w_gate, moe_w_up, moe_w_down)
    fg = final_norm_g.reshape(1, D)
    x = _moe_sparse(x, mod[0], norm_ffn_g[0:1], rw2, rb_col, wg, wu, wd, fg, layer=0, final=False)

    cos, sin = _rope_tables()
    w_in = od_w_in[0].astype(BF16)
    g1 = norm_mix_g[1:2]
    qna_p, kna_p, vna_p, qsw_p, ksw_p, vsw_p, nak, nav, swk, swv = _odd_in(x, mod[1], g1, w_in, cos, sin, prompt=True)
    qna_s, kna_s, vna_s, qsw_s, ksw_s, vsw_s = _odd_in(x, mod[1], g1, w_in, cos, sin, prompt=False)
    sink = od_sink[0]
    ona_p, osw_p = _ctx_attn(sink, qna_p, kna_p, vna_p, qsw_p, ksw_p, vsw_p)
    bias = _na_bias_blocks(_na_bias(od_rpb[0]))
    ona_s = _na_attn(qna_s, kna_s, vna_s,
                     cache_na_k[:, 0].reshape(DEC_BATCH, PAST, NA_W).astype(BF16),
                     cache_na_v[:, 0].reshape(DEC_BATCH, PAST, NA_W).astype(BF16), bias)
    osw_s = _sw_attn(sink, qsw_s, ksw_s, vsw_s,
                     cache_sw_k[:, 0].reshape(DEC_BATCH, PAST, SWKV_W).astype(BF16),
                     cache_sw_v[:, 0].reshape(DEC_BATCH, PAST, SWKV_W).astype(BF16))
    x = _odd_out(ona_p, ona_s, osw_p, osw_s, x, mod[1], od_w_out[0].astype(BF16))
    y_prompt, y_sample = _moe_sparse(x, mod[1], norm_ffn_g[1:2], rw2, rb_col, wg, wu, wd, fg, layer=1, final=True)
    y_prompt = y_prompt.reshape(BATCH, SEQ, D)
    y_sample = y_sample.reshape(DEC_BATCH, DEC_SEQ, D)
    new_na_k = nak.reshape(BATCH, 1, SEQ, NA_H, DH)
    new_na_v = nav.reshape(BATCH, 1, SEQ, NA_H, DH)
    new_sw_k = swk.reshape(BATCH, 1, SEQ, SW_KV, DH)
    new_sw_v = swv.reshape(BATCH, 1, SEQ, SW_KV, DH)
    return (y_prompt, y_sample, new_na_k, new_na_v, new_sw_k, new_sw_v)
```

```python
import functools

import jax
import jax.numpy as jnp
import numpy as np
from jax import lax
from jax.experimental import pallas as pl
from jax.experimental.pallas import tpu as pltpu

D = 1024
BATCH = 16
SEQ = 256
DEC_BATCH = 4
DEC_SEQ = 4096
PAST = 512
GRID_W = 64
ROWS = DEC_SEQ // GRID_W
DH = 64
POOL_W = 512
POOL_WINDOWS = (2, 4, 8, 16)
POOL_GW = 128
CONV_W = 512
EVEN_IN = POOL_W + 3 * CONV_W
NA_H = 8
NA_ROWS = 8
NA_COLS = 16
SW_H = 8
SW_KV = 2
SW_G = SW_H // SW_KV
SW_WIN = 128
ABLK = 128
NA_W = NA_H * DH
SWQ_W = SW_H * DH
SWKV_W = SW_KV * DH
ODD_IN = 3 * NA_W + SWQ_W + 2 * SWKV_W
N_EXP = 16
N_GRP = 4
EPG = 4
D_EXP = 512
EPS = 1e-6
NEG = -1e30
ROPE_BASE = 10000.0
QK_SCALE = DH ** -0.5
assert QK_SCALE == 0.125

N_P = BATCH * SEQ
N_S = DEC_BATCH * DEC_SEQ
N_TOK = N_P + N_S
MOD_ROWS = 8

TM = 256
NPT = N_P // TM
TPS = DEC_SEQ // TM
NT = N_TOK // TM
HALO = 8
CAST_STEPS = 64

F32 = jnp.float32
BF16 = jnp.bfloat16
VMEM_LIMIT = 56 * 1024 * 1024


def _cparams(*sem):
    return pltpu.CompilerParams(dimension_semantics=sem, vmem_limit_bytes=VMEM_LIMIT)


def _mod_row(i):
    return jnp.where(i < NPT, 0, 1 + (i - NPT) // TPS)


def _rms_mod(x, g, shift, scale):
    ms = jnp.mean(x * x, axis=-1, keepdims=True)
    y = x * lax.rsqrt(ms + EPS) * g
    return y * (1.0 + scale) + shift


def _dot(a, b):
    return jnp.dot(a, b, preferred_element_type=F32)


def _dot_nt(a, b):
    return lax.dot_general(a, b, (((1,), (1,)), ((), ())), preferred_element_type=F32)


def _mod_kernel(cv_ref, w_ref, b_ref, o_ref):
    cv = cv_ref[...]
    a = cv / (1.0 + jnp.exp(-cv))
    o_ref[...] = jnp.dot(a, w_ref[...], preferred_element_type=F32,
                         precision=lax.Precision.HIGHEST) + b_ref[...]


def _modulation(cvec, mod_w, mod_b):
    depth = mod_w.shape[0]
    return pl.pallas_call(
        _mod_kernel,
        grid=(depth, 6),
        in_specs=[
            pl.BlockSpec((MOD_ROWS, D), lambda l, j: (0, 0)),
            pl.BlockSpec((None, D, D), lambda l, j: (l, 0, j)),
            pl.BlockSpec((None, None, 1, D), lambda l, j: (l, j, 0, 0)),
        ],
        out_specs=pl.BlockSpec((None, None, MOD_ROWS, D), lambda l, j: (l, j, 0, 0)),
        out_shape=jax.ShapeDtypeStruct((depth, 6, MOD_ROWS, D), F32),
        compiler_params=_cparams("arbitrary", "arbitrary"),
        name="modulation",
    )(cvec, mod_w, mod_b.reshape(depth, 6, 1, D))


def _dual_specs(tm, width, npt):
    return [
        pl.BlockSpec((tm, width), lambda i: (jnp.minimum(i, npt - 1), 0)),
        pl.BlockSpec((tm, width), lambda i: (jnp.maximum(i - npt, 0), 0)),
    ]


def _even_kernel(xp_ref, xs_ref, xprev_ref, xnext_ref, mod_ref, g_ref, wi_ref, pw_ref, ps_ref, cw_ref,
                 wo_ref, wgf_ref, wuf_ref, wdf_ref, o_ref, wgb_ref, wub_ref, wdb_ref, pext, uext):
    i = pl.program_id(0)

    @pl.when(i < CAST_STEPS)
    def _():
        wgb_ref[...] = wgf_ref[...].astype(BF16)
        wub_ref[...] = wuf_ref[...].astype(BF16)
        wdb_ref[...] = wdf_ref[...].astype(BF16)

    r = _mod_row(i)
    is_p = i < NPT
    t0 = jnp.where(is_p, 0, ((i - NPT) % TPS) * TM)
    seq_len = jnp.where(is_p, SEQ, DEC_SEQ)
    first = t0 == 0
    last = t0 + TM == seq_len

    x = jnp.where(is_p, xp_ref[...], xs_ref[...])
    xe = jnp.concatenate([xprev_ref[...], x, xnext_ref[...]], axis=0)
    h = _rms_mod(xe, g_ref[...], mod_ref[0, pl.ds(r, 1), :], mod_ref[1, pl.ds(r, 1), :])
    ze = _dot(h.astype(BF16), wi_ref[...])
    row = lax.broadcasted_iota(jnp.int32, (TM + 2 * HALO, 1), 0)
    outside = jnp.logical_or(jnp.logical_and(first, row < HALO), jnp.logical_and(last, row >= HALO + TM))
    ze = jnp.where(outside, 0.0, ze)
    pext[...] = ze[:, :POOL_W]
    uext[...] = ze[:, POOL_W + 2 * CONV_W:] * ze[:, POOL_W:POOL_W + CONV_W]
    p = ze[HALO:HALO + TM, :POOL_W]
    gb = ze[HALO:HALO + TM, POOL_W + CONV_W:POOL_W + 2 * CONV_W]

    t = t0 + lax.broadcasted_iota(jnp.int32, (TM, 1), 0)
    y = jnp.zeros((TM, D), F32)
    for g, w in enumerate(POOL_WINDOWS):
        cols = slice(g * POOL_GW, (g + 1) * POOL_GW)
        acc = jnp.zeros((TM, POOL_GW), F32)
        for k in range(-(w // 2), w - (w // 2)):
            acc = acc + pext[pl.ds(HALO + k, TM), cols]
        lo = jnp.maximum(t - w // 2, 0)
        hi = jnp.minimum(t + (w - 1 - w // 2), seq_len - 1)
        cnt = (hi - lo + 1).astype(F32)
        d = acc / cnt - p[:, cols]
        a = _dot(d.astype(BF16), pw_ref[g]) * ps_ref[:, cols]
        y = y + _dot(a.astype(BF16), wo_ref[cols, :])
    conv = (uext[pl.ds(HALO - 1, TM), :] * cw_ref[0:1, :] + uext[pl.ds(HALO, TM), :] * cw_ref[1:2, :]
            + uext[pl.ds(HALO + 1, TM), :] * cw_ref[2:3, :])
    b = gb * conv
    y = y + _dot(b.astype(BF16), wo_ref[POOL_W:, :])
    o_ref[...] = x + mod_ref[2, pl.ds(r, 1), :] * y


def _even_layer(xp, xs, mod_l, g, w_in, pool_w, pool_scale, conv_w, w_out, wg, wu, wd):
    hb = TM // HALO
    nhb = N_S // HALO
    wg2, wu2, wd2 = wg.reshape(-1, D_EXP), wu.reshape(-1, D_EXP), wd.reshape(-1, D)
    rows_in, rows_out = wg2.shape[0] // CAST_STEPS, wd2.shape[0] // CAST_STEPS
    assert rows_in * CAST_STEPS == wg2.shape[0] and rows_out * CAST_STEPS == wd2.shape[0] and CAST_STEPS <= NT
    cast_blk = lambda i: (jnp.minimum(i, CAST_STEPS - 1), 0)
    cast_specs = [pl.BlockSpec((rows_in, D_EXP), cast_blk), pl.BlockSpec((rows_in, D_EXP), cast_blk),
                  pl.BlockSpec((rows_out, D), cast_blk)]
    outs = pl.pallas_call(
        _even_kernel,
        grid=(NT,),
        in_specs=_dual_specs(TM, D, NPT) + [
            pl.BlockSpec((HALO, D), lambda i: (jnp.maximum((i - NPT) * hb - 1, 0), 0)),
            pl.BlockSpec((HALO, D), lambda i: (jnp.clip((i - NPT + 1) * hb, 0, nhb - 1), 0)),
            pl.BlockSpec((6, MOD_ROWS, D), lambda i: (0, 0, 0)),
            pl.BlockSpec((1, D), lambda i: (0, 0)),
            pl.BlockSpec((D, EVEN_IN), lambda i: (0, 0)),
            pl.BlockSpec((4, POOL_GW, POOL_GW), lambda i: (0, 0, 0)),
            pl.BlockSpec((1, POOL_W), lambda i: (0, 0)),
            pl.BlockSpec((8, CONV_W), lambda i: (0, 0)),
            pl.BlockSpec((D, D), lambda i: (0, 0)),
        ] + cast_specs,
        out_specs=[pl.BlockSpec((TM, D), lambda i: (i, 0))] + cast_specs,
        out_shape=[jax.ShapeDtypeStruct((N_TOK, D), F32), jax.ShapeDtypeStruct(wg2.shape, BF16),
                   jax.ShapeDtypeStruct(wu2.shape, BF16), jax.ShapeDtypeStruct(wd2.shape, BF16)],
        scratch_shapes=[pltpu.VMEM((TM + 2 * HALO, POOL_W), F32),
                        pltpu.VMEM((TM + 2 * HALO, CONV_W), F32)],
        compiler_params=_cparams("arbitrary"),
        name="even_layer",
    )(xp, xs, xs, xs, mod_l, g, w_in, pool_w, pool_scale, conv_w, w_out, wg2, wu2, wd2)
    return outs[0], outs[1].reshape(wg.shape), outs[2].reshape(wu.shape), outs[3].reshape(wd.shape)


def _rope(x, cos, sin_signed):
    n = x.shape[1] // 128
    cosf = jnp.concatenate([cos] * n, axis=1) if n > 1 else cos
    sinf = jnp.concatenate([sin_signed] * n, axis=1) if n > 1 else sin_signed
    w = x.shape[1]
    lane = lax.broadcasted_iota(jnp.int32, x.shape, 1)
    up = pltpu.roll(x, w - DH // 4, 1)
    dn = pltpu.roll(x, DH // 4, 1)
    rot = jnp.where((lane % (DH // 2)) < DH // 4, up, dn)
    return x * cosf + rot * sinf


def _odd_in_kernel(x_ref, mod_ref, g_ref, w_ref, cos_ref, sin_ref, *out_refs, tile0, rope, kv_f32):
    i = pl.program_id(0) + tile0
    r = _mod_row(i)
    h = _rms_mod(x_ref[...], g_ref[...], mod_ref[0, pl.ds(r, 1), :], mod_ref[1, pl.ds(r, 1), :])
    z = _dot(h.astype(BF16), w_ref[...])
    qna_ref, kna_ref, vna_ref, qsw_ref, ksw_ref, vsw_ref = out_refs[:6]
    c0 = 0
    qna = z[:, 0:NA_W]
    kna = z[:, NA_W:2 * NA_W]
    vna = z[:, 2 * NA_W:3 * NA_W]
    c0 = 3 * NA_W
    qsw = z[:, c0:c0 + SWQ_W]
    ksw = z[:, c0 + SWQ_W:c0 + SWQ_W + SWKV_W]
    vsw = z[:, c0 + SWQ_W + SWKV_W:]
    if kv_f32:
        for ref, val in zip(out_refs[6:10], (kna, vna, ksw, vsw)):
            heads = val.shape[1] // DH
            for hh in range(heads):
                ref[pl.ds(hh, SEQ, stride=heads), :] = val[:, hh * DH:(hh + 1) * DH]
    if rope:
        cos = cos_ref[...]
        sin = sin_ref[...]
        qsw = _rope(qsw, cos, sin)
        ksw = _rope(ksw, cos, sin)
    qna_ref[...] = (qna * QK_SCALE).astype(BF16)
    kna_ref[...] = kna.astype(BF16)
    vna_ref[...] = vna.astype(BF16)
    qsw_ref[...] = (qsw * QK_SCALE).astype(BF16)
    ksw_ref[...] = ksw.astype(BF16)
    vsw_ref[...] = vsw.astype(BF16)


def _odd_in(x, mod_l, g, w, cos, sin, *, prompt):
    tile0 = 0 if prompt else NPT
    nt = NPT if prompt else NT - NPT
    n = nt * TM
    widths = [NA_W, NA_W, NA_W, SWQ_W, SWKV_W, SWKV_W]
    out_shape = [jax.ShapeDtypeStruct((n, w_), BF16) for w_ in widths]
    out_specs = [pl.BlockSpec((TM, w_), lambda i: (i, 0)) for w_ in widths]
    if prompt:
        assert TM == SEQ
        for heads in (NA_H, NA_H, SW_KV, SW_KV):
            out_shape.append(jax.ShapeDtypeStruct((BATCH * SEQ * heads, DH), F32))
            out_specs.append(pl.BlockSpec((SEQ * heads, DH), lambda i: (i, 0)))
    return pl.pallas_call(
        functools.partial(_odd_in_kernel, tile0=tile0, rope=not prompt, kv_f32=prompt),
        grid=(nt,),
        in_specs=[
            pl.BlockSpec((TM, D), lambda i: (i + tile0, 0)),
            pl.BlockSpec((6, MOD_ROWS, D), lambda i: (0, 0, 0)),
            pl.BlockSpec((1, D), lambda i: (0, 0)),
            pl.BlockSpec((D, ODD_IN), lambda i: (0, 0)),
            pl.BlockSpec((TM, 128), lambda i: (i % TPS, 0)),
            pl.BlockSpec((TM, 128), lambda i: (i % TPS, 0)),
        ],
        out_specs=out_specs,
        out_shape=out_shape,
        compiler_params=_cparams("parallel"),
        name="odd_in_prompt" if prompt else "odd_in_latent",
    )(x, mod_l, g, w, cos, sin)


def _rope_tables():
    t = np.arange(DEC_SEQ)
    quarter = DH // 4
    inv = 1.0 / (ROPE_BASE ** (np.arange(quarter, dtype=np.float64) / quarter))

    def cos_sin(pos):
        ang = pos.astype(np.float64)[:, None] * inv[None, :]
        ang = np.concatenate([ang, ang], axis=-1)
        return np.cos(ang), np.sin(ang)

    cr, sr = cos_sin(t // GRID_W)
    cc, sc = cos_sin(t % GRID_W)
    cos = np.concatenate([cr, cc], axis=-1)
    sin = np.concatenate([sr, sc], axis=-1)
    sign = np.where((np.arange(DH) % (DH // 2)) < DH // 4, -1.0, 1.0)
    sin = sin * sign[None, :]
    cos = np.concatenate([cos, cos], axis=-1).astype(np.float32)
    sin = np.concatenate([sin, sin], axis=-1).astype(np.float32)
    return jnp.asarray(cos), jnp.asarray(sin)


def _softmax_pv(segs, sink=None):
    m = None
    for s, _ in segs:
        sm = jnp.max(s, axis=-1, keepdims=True)
        m = sm if m is None else jnp.maximum(m, sm)
    if sink is not None:
        m = jnp.maximum(m, sink)
    den = None
    acc = None
    for s, v in segs:
        p = jnp.exp(s - m)
        ps = jnp.sum(p, axis=-1, keepdims=True)
        den = ps if den is None else den + ps
        pv = _dot(p.astype(BF16), v)
        acc = pv if acc is None else acc + pv
    if sink is not None:
        den = den + jnp.exp(sink - m)
    return acc / den


def _sink_col(sink_ref, g, rows_per_head):
    row = lax.broadcasted_iota(jnp.int32, (SW_G * rows_per_head, 1), 0)
    col = jnp.zeros((SW_G * rows_per_head, 1), F32)
    for r in range(SW_G):
        col = jnp.where(row // rows_per_head == r, sink_ref[g * SW_G + r], col)
    return col


def _ctx_attn_kernel(sink_ref, qna_ref, kna_ref, vna_ref, qsw_ref, ksw_ref, vsw_ref, ona_ref, osw_ref):
    lane = lax.broadcasted_iota(jnp.int32, (SEQ, 2 * DH), 1)
    for p in range(NA_H // 2):
        cols = slice(p * 2 * DH, (p + 1) * 2 * DH)
        q = qna_ref[:, cols]
        k = kna_ref[:, cols]
        v = vna_ref[:, cols]
        outs = []
        for half in range(2):
            mine = (lane < DH) if half == 0 else (lane >= DH)
            qm = jnp.where(mine, q, jnp.zeros_like(q))
            outs.append(_softmax_pv([(_dot_nt(qm, k), v)]))
        ona_ref[:, cols] = jnp.where(lane < DH, outs[0], outs[1]).astype(BF16)
    outs = []
    for g in range(SW_KV):
        kc = slice(g * DH, (g + 1) * DH)
        q = jnp.concatenate([qsw_ref[:, (g * SW_G + r) * DH:(g * SW_G + r + 1) * DH] for r in range(SW_G)], axis=0)
        s = _dot_nt(q, ksw_ref[:, kc])
        o = _softmax_pv([(s, vsw_ref[:, kc])], sink=_sink_col(sink_ref, g, SEQ))
        outs.extend(o[r * SEQ:(r + 1) * SEQ, :] for r in range(SW_G))
    osw_ref[...] = jnp.concatenate(outs, axis=1).astype(BF16)


def _ctx_attn(sink, qna, kna, vna, qsw, ksw, vsw):
    def spec(w):
        return pl.BlockSpec((SEQ, w), lambda b: (b, 0))

    return pl.pallas_call(
        _ctx_attn_kernel,
        grid=(BATCH,),
        in_specs=[pl.BlockSpec(memory_space=pltpu.SMEM), spec(NA_W), spec(NA_W), spec(NA_W),
                  spec(SWQ_W), spec(SWKV_W), spec(SWKV_W)],
        out_specs=[spec(NA_W), spec(SWQ_W)],
        out_shape=[jax.ShapeDtypeStruct((N_P, NA_W), BF16), jax.ShapeDtypeStruct((N_P, SWQ_W), BF16)],
        compiler_params=_cparams("parallel"),
        name="ctx_attn",
    )(sink, qna, kna, vna, qsw, ksw, vsw)


def _na_bias_kernel(rpb_ref, o_ref):
    h = pl.program_id(0)
    cq = lax.broadcasted_iota(jnp.int32, (GRID_W, GRID_W), 0)
    ck = lax.broadcasted_iota(jnp.int32, (GRID_W, GRID_W), 1)
    cstart = jnp.clip(cq - NA_COLS // 2, 0, GRID_W - NA_COLS)
    ok = (ck >= cstart) & (ck < cstart + NA_COLS)
    dc = jnp.clip(ck - cq + NA_COLS - 1, 0, 2 * NA_COLS - 2)
    ndc = 2 * NA_COLS - 1
    for d in range(2 * NA_ROWS - 1):
        b = jnp.zeros((GRID_W, GRID_W), F32)
        for e in range(ndc):
            b = jnp.where(dc == e, rpb_ref[h, d * ndc + e], b)
        o_ref[d] = jnp.where(ok, b, NEG)


def _na_bias(rpb):
    nd = 2 * NA_ROWS - 1
    return pl.pallas_call(
        _na_bias_kernel,
        grid=(NA_H,),
        in_specs=[pl.BlockSpec(memory_space=pltpu.SMEM)],
        out_specs=pl.BlockSpec((None, nd, GRID_W, GRID_W), lambda h: (h, 0, 0, 0)),
        out_shape=jax.ShapeDtypeStruct((NA_H, nd, GRID_W, GRID_W), F32),
        compiler_params=_cparams("parallel"),
        name="na_bias",
    )(rpb.reshape(NA_H, nd * (2 * NA_COLS - 1)))


NA_QB = 4
NA_KR = 12
NA_NQB = ROWS // NA_QB


def _na_bias_blocks(bias):
    neg = jnp.full((NA_H, GRID_W, GRID_W), NEG, F32)
    cases = []
    for case in range(3):
        rows = []
        for i in range(NA_QB):
            blocks = []
            for j in range(NA_KR):
                if case == 0:
                    valid, dr = j < NA_ROWS, j - i + NA_ROWS - 1
                elif case == 1:
                    valid, dr = i <= j < i + NA_ROWS, j - i + NA_ROWS // 2 - 1
                else:
                    valid, dr = j >= NA_KR - NA_ROWS, j - i + NA_ROWS - 1 - (NA_KR - NA_QB)
                blocks.append(bias[:, dr] if valid else neg)
            rows.append(jnp.concatenate(blocks, axis=2))
        cases.append(jnp.concatenate(rows, axis=1))
    return jnp.stack(cases)


def _na_kernel(q_ref, k_ref, v_ref, kc_ref, vc_ref, bias_ref, o_ref):
    r0 = pl.program_id(1) * NA_QB
    start = pl.multiple_of(jnp.clip(r0 - NA_ROWS // 2, 0, ROWS - NA_KR) * GRID_W, GRID_W)
    nq = NA_QB * GRID_W
    lane = lax.broadcasted_iota(jnp.int32, (nq, 2 * DH), 1)
    for p in range(NA_H // 2):
        cols = slice(p * 2 * DH, (p + 1) * 2 * DH)
        q = q_ref[:, cols]
        kl = k_ref[pl.ds(start, NA_KR * GRID_W), cols]
        vl = v_ref[pl.ds(start, NA_KR * GRID_W), cols]
        kc = kc_ref[:, cols]
        vc = vc_ref[:, cols]
        outs = []
        for half in range(2):
            mine = (lane < DH) if half == 0 else (lane >= DH)
            qm = jnp.where(mine, q, jnp.zeros_like(q))
            s_loc = _dot_nt(qm, kl) + bias_ref[2 * p + half]
            s_ctx = _dot_nt(qm, kc)
            outs.append(_softmax_pv([(s_loc, vl), (s_ctx, vc)]))
        o_ref[:, cols] = jnp.where(lane < DH, outs[0], outs[1]).astype(BF16)


def _na_attn(q, k, v, kc, vc, bias):
    nq = NA_QB * GRID_W

    def bias_case(b, rb):
        return (jnp.where(rb == 0, 0, jnp.where(rb == NA_NQB - 1, 2, 1)), 0, 0, 0)

    return pl.pallas_call(
        _na_kernel,
        grid=(DEC_BATCH, NA_NQB),
        in_specs=[
            pl.BlockSpec((nq, NA_W), lambda b, rb: (b * NA_NQB + rb, 0)),
            pl.BlockSpec((DEC_SEQ, NA_W), lambda b, rb: (b, 0)),
            pl.BlockSpec((DEC_SEQ, NA_W), lambda b, rb: (b, 0)),
            pl.BlockSpec((None, PAST, NA_W), lambda b, rb: (b, 0, 0)),
            pl.BlockSpec((None, PAST, NA_W), lambda b, rb: (b, 0, 0)),
            pl.BlockSpec((None, NA_H, nq, NA_KR * GRID_W), bias_case),
        ],
        out_specs=pl.BlockSpec((nq, NA_W), lambda b, rb: (b * NA_NQB + rb, 0)),
        out_shape=jax.ShapeDtypeStruct((N_S, NA_W), BF16),
        compiler_params=_cparams("parallel", "arbitrary"),
        name="na_attn",
    )(q, k, v, kc, vc, bias)


def _sw_window_bias():
    q = np.arange(SW_G * ABLK)[:, None] % ABLK
    k = np.arange(3 * ABLK)[None, :]
    tables = [np.where(np.abs(q + lead - k) <= SW_WIN, 0.0, NEG) for lead in (0, ABLK, 2 * ABLK)]
    return jnp.asarray(np.stack(tables).astype(np.float32))


def _sw_kernel(sink_ref, q_ref, k_ref, v_ref, kc_ref, vc_ref, wb_ref, o_ref):
    j = pl.program_id(1)
    nk = 3 * ABLK
    start = pl.multiple_of(jnp.clip((j - 1) * ABLK, 0, DEC_SEQ - nk), ABLK)
    outs = []
    for g in range(SW_KV):
        kcols = slice(g * DH, (g + 1) * DH)
        q = jnp.concatenate([q_ref[:, (g * SW_G + r) * DH:(g * SW_G + r + 1) * DH] for r in range(SW_G)], axis=0)
        kw = k_ref[pl.ds(start, nk), kcols]
        vw = v_ref[pl.ds(start, nk), kcols]
        s_w = _dot_nt(q, kw) + wb_ref[...]
        s_c = _dot_nt(q, kc_ref[:, kcols])
        o = _softmax_pv([(s_w, vw), (s_c, vc_ref[:, kcols])], sink=_sink_col(sink_ref, g, ABLK))
        outs.extend(o[r * ABLK:(r + 1) * ABLK, :] for r in range(SW_G))
    o_ref[...] = jnp.concatenate(outs, axis=1).astype(BF16)


def _sw_attn(sink, q, k, v, kc, vc):
    nb = DEC_SEQ // ABLK
    return pl.pallas_call(
        _sw_kernel,
        grid=(DEC_BATCH, nb),
        in_specs=[
            pl.BlockSpec(memory_space=pltpu.SMEM),
            pl.BlockSpec((ABLK, SWQ_W), lambda b, j: (b * nb + j, 0)),
            pl.BlockSpec((DEC_SEQ, SWKV_W), lambda b, j: (b, 0)),
            pl.BlockSpec((DEC_SEQ, SWKV_W), lambda b, j: (b, 0)),
            pl.BlockSpec((None, PAST, SWKV_W), lambda b, j: (b, 0, 0)),
            pl.BlockSpec((None, PAST, SWKV_W), lambda b, j: (b, 0, 0)),
            pl.BlockSpec((None, SW_G * ABLK, 3 * ABLK),
                         lambda b, j: (jnp.where(j == 0, 0, jnp.where(j == nb - 1, 2, 1)), 0, 0)),
        ],
        out_specs=pl.BlockSpec((ABLK, SWQ_W), lambda b, j: (b * nb + j, 0)),
        out_shape=jax.ShapeDtypeStruct((N_S, SWQ_W), BF16),
        compiler_params=_cparams("parallel", "arbitrary"),
        name="sw_attn",
    )(sink, q, k, v, kc, vc, _sw_window_bias())


def _odd_out_kernel(nap_ref, nas_ref, swp_ref, sws_ref, x_ref, mod_ref, wo_ref, o_ref):
    i = pl.program_id(0)
    r = _mod_row(i)
    is_p = i < NPT
    ona = jnp.where(is_p, nap_ref[...], nas_ref[...])
    osw = jnp.where(is_p, swp_ref[...], sws_ref[...])
    y = _dot(ona, wo_ref[:NA_W, :]) + _dot(osw, wo_ref[NA_W:, :])
    o_ref[...] = x_ref[...] + mod_ref[2, pl.ds(r, 1), :] * y


def _odd_out(nap, nas, swp, sws, x, mod_l, w_out):
    return pl.pallas_call(
        _odd_out_kernel,
        grid=(NT,),
        in_specs=_dual_specs(TM, NA_W, NPT) + _dual_specs(TM, SWQ_W, NPT) + [
            pl.BlockSpec((TM, D), lambda i: (i, 0)),
            pl.BlockSpec((6, MOD_ROWS, D), lambda i: (0, 0, 0)),
            pl.BlockSpec((D, D), lambda i: (0, 0)),
        ],
        out_specs=pl.BlockSpec((TM, D), lambda i: (i, 0)),
        out_shape=jax.ShapeDtypeStruct((N_TOK, D), F32),
        compiler_params=_cparams("parallel"),
        name="odd_out",
    )(nap, nas, swp, sws, x, mod_l, w_out)


BT = 2560
NB = N_TOK // BT
TPB = BT // TM
CH = 256
CH_TAIL = 128
XPS = 2
NES = N_EXP // XPS
XO_ROWS = 2 * BT + CH
ROW = 8
assert D == ROW * 128


def _route_sparse(probs, sel):
    rank = []
    score = []
    for g in range(N_GRP):
        ids = range(g * EPG, (g + 1) * EPG)
        tot = None
        for e in ids:
            rk = jnp.zeros_like(sel[e])
            for j in ids:
                if j < e:
                    rk = rk + (sel[j] >= sel[e]).astype(F32)
                elif j > e:
                    rk = rk + (sel[j] > sel[e]).astype(F32)
            rank.append(rk)
            contrib = jnp.where(rk < 2.0, sel[e], 0.0)
            tot = contrib if tot is None else tot + contrib
        score.append(tot)
    top1, top2 = [], []
    for g in range(N_GRP):
        best = None
        for j in range(N_GRP):
            if j == g:
                continue
            c = (score[g] > score[j]) if j < g else (score[g] >= score[j])
            best = c if best is None else jnp.logical_and(best, c)
        for e in range(g * EPG, (g + 1) * EPG):
            top1.append(jnp.logical_and(best, rank[e] == 0.0))
            top2.append(jnp.logical_and(best, rank[e] == 1.0))
    return top1, top2


def _pick(masks, rows):
    acc = None
    for m, r in zip(masks, rows):
        v = jnp.where(m, r, 0.0)
        acc = v if acc is None else acc + v
    return acc


def _router_kernel(x_ref, mod_ref, g_ref, rw_ref, rb_ref, h_ref, pos_ref, w_ref, cnt_ref, off_ref,
                   meta, carry):
    i = pl.program_id(0)
    j = i % TPB
    r = _mod_row(i)
    h = _rms_mod(x_ref[...], g_ref[...], mod_ref[3, pl.ds(r, 1), :], mod_ref[4, pl.ds(r, 1), :])
    h_ref[...] = h.astype(BF16)
    h_hi = h.astype(BF16)
    h_lo = (h - h_hi.astype(F32)).astype(BF16)
    logits = _dot(h_lo, rw_ref[0]) + _dot(h_hi, rw_ref[1]) + _dot(h_hi, rw_ref[0])
    lt = logits.T[:N_EXP, :]
    m = jnp.max(lt, axis=0, keepdims=True)
    ex = jnp.exp(lt - m)
    pr = ex / jnp.sum(ex, axis=0, keepdims=True)
    se = pr + rb_ref[:N_EXP, :]
    probs = [pr[e:e + 1, :] for e in range(N_EXP)]
    sel = [se[e:e + 1, :] for e in range(N_EXP)]
    top1, top2 = _route_sparse(probs, sel)

    @pl.when(j == 0)
    def _():
        carry[...] = jnp.zeros_like(carry)

    member = jnp.concatenate([jnp.logical_or(a, b).astype(F32) for a, b in zip(top1, top2)], axis=0)
    s_idx = lax.broadcasted_iota(jnp.int32, (TM, TM), 0)
    t_idx = lax.broadcasted_iota(jnp.int32, (TM, TM), 1)
    before = jnp.where(s_idx < t_idx, 1.0, 0.0).astype(BF16)
    seen = _dot(member.astype(BF16), before) + carry[:, 0:1]
    seen_rows = [seen[e:e + 1, :] for e in range(N_EXP)]
    ids = [jnp.full((1, TM), float(e), F32) for e in range(N_EXP)]
    p1 = _pick(top1, probs)
    p2 = _pick(top2, probs)
    den = p1 + p2
    w_ref[j, 0:1, :] = p1 / den
    w_ref[j, 1:2, :] = p2 / den
    meta[j, 0:1, :] = _pick(top1, ids)
    meta[j, 1:2, :] = _pick(top2, ids)
    meta[j, 2:3, :] = _pick(top1, seen_rows)
    meta[j, 3:4, :] = _pick(top2, seen_rows)
    carry[...] = carry[...] + jnp.sum(member, axis=1, keepdims=True)

    @pl.when(j == TPB - 1)
    def _():
        cnt = carry[...]
        offs = [jnp.zeros((1, 128), F32)]
        for e in range(1, N_EXP):
            offs.append(offs[-1] + cnt[e - 1:e, :])
        cnt_ref[...] = cnt
        off_ref[...] = jnp.concatenate(offs, axis=0)
        for jj in range(TPB):
            for k in range(2):
                eid = meta[jj, k:k + 1, :]
                pos = meta[jj, 2 + k:3 + k, :]
                for e in range(1, N_EXP):
                    pos = pos + jnp.where(eid == float(e), offs[e][:, 0:1], 0.0)
                pos_ref[jj, k:k + 1, :] = pos.astype(jnp.int32)


def _router(x, mod_l, g, rw_pad, rb_col):
    blk = lambda i: (i // TPB, 0, 0, 0)
    return pl.pallas_call(
        _router_kernel,
        grid=(NT,),
        in_specs=[
            pl.BlockSpec((TM, D), lambda i: (i, 0)),
            pl.BlockSpec((6, MOD_ROWS, D), lambda i: (0, 0, 0)),
            pl.BlockSpec((1, D), lambda i: (0, 0)),
            pl.BlockSpec((2, D, 128), lambda i: (0, 0, 0)),
            pl.BlockSpec((128, 1), lambda i: (0, 0)),
        ],
        out_specs=[
            pl.BlockSpec((TM, D), lambda i: (i, 0)),
            pl.BlockSpec((None, TPB, 2, TM), blk),
            pl.BlockSpec((None, TPB, 2, TM), blk),
            pl.BlockSpec((None, N_EXP, 128), lambda i: (i // TPB, 0, 0)),
            pl.BlockSpec((None, N_EXP, 128), lambda i: (i // TPB, 0, 0)),
        ],
        out_shape=[
            jax.ShapeDtypeStruct((N_TOK, D), BF16),
            jax.ShapeDtypeStruct((NB, TPB, 2, TM), jnp.int32),
            jax.ShapeDtypeStruct((NB, TPB, 2, TM), F32),
            jax.ShapeDtypeStruct((NB, N_EXP, 128), F32),
            jax.ShapeDtypeStruct((NB, N_EXP, 128), F32),
        ],
        scratch_shapes=[pltpu.VMEM((TPB, 4, TM), F32), pltpu.VMEM((N_EXP, 128), F32)],
        compiler_params=_cparams("arbitrary"),
        name="router",
    )(x, mod_l, g, rw_pad, rb_col)


def _row(p):
    return pl.ds(pl.multiple_of(p * ROW, ROW), ROW)


def _expert_segment(n, base, xo, wg_ref, wu_ref, wd_ref):
    def ffn_rows(row0, rows):
        x = jnp.concatenate([xo[pl.ds(row0 * ROW + c, rows, stride=ROW), :] for c in range(ROW)], axis=1)
        xb = x.astype(BF16)
        a = _dot(xb, wg_ref[...])
        u = _dot(xb, wu_ref[...])
        hid = (a / (1.0 + jnp.exp(-a))) * u
        out = _dot(hid.astype(BF16), wd_ref[...])
        valid = lax.broadcasted_iota(jnp.int32, (rows, 1), 0) < base + n - row0
        res = jnp.where(valid, out, x)
        for c in range(ROW):
            xo[pl.ds(row0 * ROW + c, rows, stride=ROW), :] = res[:, c * 128:(c + 1) * 128]

    def chunk(jc, carry):
        ffn_rows(base + jc * CH, CH)
        return carry

    nfull = (n + CH - CH_TAIL - 1) // CH
    lax.fori_loop(0, nfull, chunk, 0)

    @pl.when(n > nfull * CH)
    def _():
        ffn_rows(base + nfull * CH, CH_TAIL)


def _experts_kernel(cnt_ref, off_ref, pos_ref, w_ref, h_ref, wg_ref, wu_ref, wd_ref, x_ref, mod_ref, fg_ref,
                    *rest, final):
    if final:
        op_ref, os_ref, xo, stg = rest
    else:
        o_ref, xo, stg = rest
    b = pl.program_id(0)
    s = pl.program_id(1)

    @pl.when(s == 0)
    def _():
        xo[pl.ds(2 * BT * ROW, CH * ROW), :] = jnp.zeros((CH * ROW, 128), F32)

        def sub(j, carry):
            hs = h_ref[pl.ds(pl.multiple_of(j * TM, TM), TM), :].astype(F32)
            for c in range(ROW):
                stg[pl.ds(c, TM, stride=ROW), :] = hs[:, c * 128:(c + 1) * 128]
            for t in range(TM):
                v = stg[t * ROW:(t + 1) * ROW, :]
                xo[_row(pos_ref[j, 0, t]), :] = v
                xo[_row(pos_ref[j, 1, t]), :] = v
            return carry

        lax.fori_loop(0, TPB, sub, 0)

    @pl.when(s < NES)
    def _():
        for k in range(XPS):
            _expert_segment(cnt_ref[b, s * XPS + k], off_ref[b, s * XPS + k], xo, wg_ref.at[k], wu_ref.at[k],
                            wd_ref.at[k])

    @pl.when(s >= NES - 1)
    def _():
        j = s - (NES - 1)
        tile = b * TPB + j
        g2 = mod_ref[5, pl.ds(_mod_row(tile), 1), :]
        for t in range(TM):
            a = xo[_row(pos_ref[j, 0, t]), :]
            u = xo[_row(pos_ref[j, 1, t]), :]
            stg[t * ROW:(t + 1) * ROW, :] = w_ref[j, 0, t] * a + w_ref[j, 1, t] * u
        y = jnp.concatenate([stg[pl.ds(c, TM, stride=ROW), :] for c in range(ROW)], axis=1)
        res = x_ref[...] + g2 * y
        if final:
            ms = jnp.mean(res * res, axis=-1, keepdims=True)
            res = res * lax.rsqrt(ms + EPS) * fg_ref[...]

            @pl.when(tile < NPT)
            def _():
                op_ref[...] = res

            @pl.when(tile >= NPT)
            def _():
                os_ref[...] = res
        else:
            o_ref[...] = res


def _experts(cnt, off, pos, wts, h, wg, wu, wd, x, mod_l, fg, *, layer, final):
    def tile_of(b, s):
        return b * TPB + jnp.clip(s - (NES - 1), 0, TPB - 1)

    def expert_of(b, s, *_):
        return (layer, jnp.minimum(s, NES - 1), 0, 0)

    smem_blk = pl.BlockSpec((None, TPB, 2, TM), lambda b, s, *_: (b, 0, 0, 0), memory_space=pltpu.SMEM)
    if final:
        out_specs = [pl.BlockSpec((TM, D), lambda b, s, *_: (jnp.minimum(tile_of(b, s), NPT - 1), 0)),
                     pl.BlockSpec((TM, D), lambda b, s, *_: (jnp.maximum(tile_of(b, s) - NPT, 0), 0))]
        out_shape = [jax.ShapeDtypeStruct((N_P, D), F32), jax.ShapeDtypeStruct((N_S, D), F32)]
    else:
        out_specs = pl.BlockSpec((TM, D), lambda b, s, *_: (tile_of(b, s), 0))
        out_shape = jax.ShapeDtypeStruct((N_TOK, D), F32)
    grid_spec = pltpu.PrefetchScalarGridSpec(
        num_scalar_prefetch=2,
        grid=(NB, NES + TPB - 1),
        in_specs=[
            smem_blk,
            smem_blk,
            pl.BlockSpec((BT, D), lambda b, s, *_: (b, 0)),
            pl.BlockSpec((None, XPS, D, D_EXP), expert_of),
            pl.BlockSpec((None, XPS, D, D_EXP), expert_of),
            pl.BlockSpec((None, XPS, D_EXP, D), expert_of),
            pl.BlockSpec((TM, D), lambda b, s, *_: (tile_of(b, s), 0)),
            pl.BlockSpec((6, MOD_ROWS, D), lambda b, s, *_: (0, 0, 0)),
            pl.BlockSpec((1, D), lambda b, s, *_: (0, 0)),
        ],
        out_specs=out_specs,
        scratch_shapes=[pltpu.VMEM((XO_ROWS * ROW, 128), F32), pltpu.VMEM((TM * ROW, 128), F32)],
    )
    return pl.pallas_call(
        functools.partial(_experts_kernel, final=final),
        grid_spec=grid_spec,
        out_shape=out_shape,
        compiler_params=_cparams("arbitrary", "arbitrary"),
        name="experts_final" if final else "experts",
    )(cnt, off, pos, wts, h, wg, wu, wd, x, mod_l, fg)


def _moe_sparse(x, mod_l, g, rw2, rb_col, wg, wu, wd, fg, *, layer, final):
    h, pos, wts, cnt, off = _router(x, mod_l, g, rw2, rb_col)
    cnt = cnt[:, :, 0].astype(jnp.int32)
    off = off[:, :, 0].astype(jnp.int32)
    return _experts(cnt, off, pos, wts, h, wg, wu, wd, x, mod_l, fg, layer=layer, final=final)


def kernel(x_prompt, x_sample, cache_na_k, cache_na_v, cache_sw_k, cache_sw_v, c, c_ctx, mod_w, mod_b, norm_mix_g, norm_ffn_g, ev_w_in, ev_pool_w, ev_pool_scale, ev_conv_w, ev_w_out, od_w_in, od_rpb, od_sink, od_w_out, router_w, router_b, moe_w_gate, moe_w_up, moe_w_down, final_norm_g):
    xp = x_prompt.reshape(N_P, D)
    xs = x_sample.reshape(N_S, D)
    cvec = jnp.concatenate([c_ctx[None, :], c, jnp.zeros((MOD_ROWS - 1 - DEC_BATCH, D), F32)], axis=0)
    mod = _modulation(cvec, mod_w, mod_b)

    rw_pad = jnp.pad(router_w, ((0, 0), (0, 128 - N_EXP)))
    rw_hi = rw_pad.astype(BF16)
    rw2 = jnp.stack([rw_hi, (rw_pad - rw_hi.astype(F32)).astype(BF16)])
    rb_col = jnp.pad(router_b, (0, 128 - N_EXP)).reshape(128, 1)

    conv_w = jnp.pad(ev_conv_w[0], ((0, 8 - ev_conv_w.shape[1]), (0, 0)))
    x, wg, wu, wd = _even_layer(xp, xs, mod[0], norm_mix_g[0:1], ev_w_in[0].astype(BF16),
                                ev_pool_w[0].astype(BF16), ev_pool_scale[0:1], conv_w, ev_w_out[0].astype(BF16),
                                moe_w_gate, moe_w_up, moe_w_down)
    fg = final_norm_g.reshape(1, D)
    x = _moe_sparse(x, mod[0], norm_ffn_g[0:1], rw2, rb_col, wg, wu, wd, fg, layer=0, final=False)

    cos, sin = _rope_tables()
    w_in = od_w_in[0].astype(BF16)
    g1 = norm_mix_g[1:2]
    qna_p, kna_p, vna_p, qsw_p, ksw_p, vsw_p, nak, nav, swk, swv = _odd_in(x, mod[1], g1, w_in, cos, sin, prompt=True)
    qna_s, kna_s, vna_s, qsw_s, ksw_s, vsw_s = _odd_in(x, mod[1], g1, w_in, cos, sin, prompt=False)
    sink = od_sink[0]
    ona_p, osw_p = _ctx_attn(sink, qna_p, kna_p, vna_p, qsw_p, ksw_p, vsw_p)
    bias = _na_bias_blocks(_na_bias(od_rpb[0]))
    ona_s = _na_attn(qna_s, kna_s, vna_s,
                     cache_na_k[:, 0].reshape(DEC_BATCH, PAST, NA_W).astype(BF16),
                     cache_na_v[:, 0].reshape(DEC_BATCH, PAST, NA_W).astype(BF16), bias)
    osw_s = _sw_attn(sink, qsw_s, ksw_s, vsw_s,
                     cache_sw_k[:, 0].reshape(DEC_BATCH, PAST, SWKV_W).astype(BF16),
                     cache_sw_v[:, 0].reshape(DEC_BATCH, PAST, SWKV_W).astype(BF16))
    x = _odd_out(ona_p, ona_s, osw_p, osw_s, x, mod[1], od_w_out[0].astype(BF16))
    y_prompt, y_sample = _moe_sparse(x, mod[1], norm_ffn_g[1:2], rw2, rb_col, wg, wu, wd, fg, layer=1, final=True)
    y_prompt = y_prompt.reshape(BATCH, SEQ, D)
    y_sample = y_sample.reshape(DEC_BATCH, DEC_SEQ, D)
    new_na_k = nak.reshape(BATCH, 1, SEQ, NA_H, DH)
    new_na_v = nav.reshape(BATCH, 1, SEQ, NA_H, DH)
    new_sw_k = swk.reshape(BATCH, 1, SEQ, SW_KV, DH)
    new_sw_v = swv.reshape(BATCH, 1, SEQ, SW_KV, DH)
    return (y_prompt, y_sample, new_na_k, new_na_v, new_sw_k, new_sw_v)
```

```python
import functools

import jax
import jax.numpy as jnp
import numpy as np
from jax import lax
from jax.experimental import pallas as pl
from jax.experimental.pallas import tpu as pltpu

D = 1024
BATCH = 16
SEQ = 256
DEC_BATCH = 4
DEC_SEQ = 4096
PAST = 512
GRID_W = 64
ROWS = DEC_SEQ // GRID_W
DH = 64
POOL_W = 512
POOL_WINDOWS = (2, 4, 8, 16)
POOL_GW = 128
CONV_W = 512
EVEN_IN = POOL_W + 3 * CONV_W
NA_H = 8
NA_ROWS = 8
NA_COLS = 16
SW_H = 8
SW_KV = 2
SW_G = SW_H // SW_KV
SW_WIN = 128
ABLK = 128
NA_W = NA_H * DH
SWQ_W = SW_H * DH
SWKV_W = SW_KV * DH
ODD_IN = 3 * NA_W + SWQ_W + 2 * SWKV_W
N_EXP = 16
N_GRP = 4
EPG = 4
D_EXP = 512
EPS = 1e-6
NEG = -1e30
ROPE_BASE = 10000.0
QK_SCALE = DH ** -0.5
assert QK_SCALE == 0.125

N_P = BATCH * SEQ
N_S = DEC_BATCH * DEC_SEQ
N_TOK = N_P + N_S
MOD_ROWS = 8

TM = 256
NPT = N_P // TM
TPS = DEC_SEQ // TM
NT = N_TOK // TM
HALO = 8
CAST_STEPS = 64

F32 = jnp.float32
BF16 = jnp.bfloat16
VMEM_LIMIT = 56 * 1024 * 1024


def _cparams(*sem):
    return pltpu.CompilerParams(dimension_semantics=sem, vmem_limit_bytes=VMEM_LIMIT)


def _mod_row(i):
    return jnp.where(i < NPT, 0, 1 + (i - NPT) // TPS)


def _rms_mod(x, g, shift, scale):
    ms = jnp.mean(x * x, axis=-1, keepdims=True)
    y = x * lax.rsqrt(ms + EPS) * g
    return y * (1.0 + scale) + shift


def _dot(a, b):
    return jnp.dot(a, b, preferred_element_type=F32)


def _dot_nt(a, b):
    return lax.dot_general(a, b, (((1,), (1,)), ((), ())), preferred_element_type=F32)


def _mod_kernel(cv_ref, w_ref, b_ref, o_ref):
    cv = cv_ref[...]
    a = cv / (1.0 + jnp.exp(-cv))
    o_ref[...] = jnp.dot(a, w_ref[...], preferred_element_type=F32,
                         precision=lax.Precision.HIGHEST) + b_ref[...]


def _modulation(cvec, mod_w, mod_b):
    depth = mod_w.shape[0]
    return pl.pallas_call(
        _mod_kernel,
        grid=(depth, 6),
        in_specs=[
            pl.BlockSpec((MOD_ROWS, D), lambda l, j: (0, 0)),
            pl.BlockSpec((None, D, D), lambda l, j: (l, 0, j)),
            pl.BlockSpec((None, None, 1, D), lambda l, j: (l, j, 0, 0)),
        ],
        out_specs=pl.BlockSpec((None, None, MOD_ROWS, D), lambda l, j: (l, j, 0, 0)),
        out_shape=jax.ShapeDtypeStruct((depth, 6, MOD_ROWS, D), F32),
        compiler_params=_cparams("arbitrary", "arbitrary"),
        name="modulation",
    )(cvec, mod_w, mod_b.reshape(depth, 6, 1, D))


def _dual_specs(tm, width, npt):
    return [
        pl.BlockSpec((tm, width), lambda i: (jnp.minimum(i, npt - 1), 0)),
        pl.BlockSpec((tm, width), lambda i: (jnp.maximum(i - npt, 0), 0)),
    ]


def _even_kernel(xp_ref, xs_ref, xprev_ref, xnext_ref, mod_ref, g_ref, wi_ref, pw_ref, ps_ref, cw_ref,
                 wo_ref, wgf_ref, wuf_ref, wdf_ref, o_ref, wgb_ref, wub_ref, wdb_ref, pext, uext):
    i = pl.program_id(0)

    @pl.when(i < CAST_STEPS)
    def _():
        wgb_ref[...] = wgf_ref[...].astype(BF16)
        wub_ref[...] = wuf_ref[...].astype(BF16)
        wdb_ref[...] = wdf_ref[...].astype(BF16)

    r = _mod_row(i)
    is_p = i < NPT
    t0 = jnp.where(is_p, 0, ((i - NPT) % TPS) * TM)
    seq_len = jnp.where(is_p, SEQ, DEC_SEQ)
    first = t0 == 0
    last = t0 + TM == seq_len

    x = jnp.where(is_p, xp_ref[...], xs_ref[...])
    xe = jnp.concatenate([xprev_ref[...], x, xnext_ref[...]], axis=0)
    h = _rms_mod(xe, g_ref[...], mod_ref[0, pl.ds(r, 1), :], mod_ref[1, pl.ds(r, 1), :])
    ze = _dot(h.astype(BF16), wi_ref[...])
    row = lax.broadcasted_iota(jnp.int32, (TM + 2 * HALO, 1), 0)
    outside = jnp.logical_or(jnp.logical_and(first, row < HALO), jnp.logical_and(last, row >= HALO + TM))
    ze = jnp.where(outside, 0.0, ze)
    pext[...] = ze[:, :POOL_W]
    uext[...] = ze[:, POOL_W + 2 * CONV_W:] * ze[:, POOL_W:POOL_W + CONV_W]
    p = ze[HALO:HALO + TM, :POOL_W]
    gb = ze[HALO:HALO + TM, POOL_W + CONV_W:POOL_W + 2 * CONV_W]

    t = t0 + lax.broadcasted_iota(jnp.int32, (TM, 1), 0)
    y = jnp.zeros((TM, D), F32)
    for g, w in enumerate(POOL_WINDOWS):
        cols = slice(g * POOL_GW, (g + 1) * POOL_GW)
        acc = jnp.zeros((TM, POOL_GW), F32)
        for k in range(-(w // 2), w - (w // 2)):
            acc = acc + pext[pl.ds(HALO + k, TM), cols]
        lo = jnp.maximum(t - w // 2, 0)
        hi = jnp.minimum(t + (w - 1 - w // 2), seq_len - 1)
        cnt = (hi - lo + 1).astype(F32)
        d = acc / cnt - p[:, cols]
        a = _dot(d.astype(BF16), pw_ref[g]) * ps_ref[:, cols]
        y = y + _dot(a.astype(BF16), wo_ref[cols, :])
    conv = (uext[pl.ds(HALO - 1, TM), :] * cw_ref[0:1, :] + uext[pl.ds(HALO, TM), :] * cw_ref[1:2, :]
            + uext[pl.ds(HALO + 1, TM), :] * cw_ref[2:3, :])
    b = gb * conv
    y = y + _dot(b.astype(BF16), wo_ref[POOL_W:, :])
    o_ref[...] = x + mod_ref[2, pl.ds(r, 1), :] * y


def _even_layer(xp, xs, mod_l, g, w_in, pool_w, pool_scale, conv_w, w_out, wg, wu, wd):
    hb = TM // HALO
    nhb = N_S // HALO
    wg2, wu2, wd2 = wg.reshape(-1, D_EXP), wu.reshape(-1, D_EXP), wd.reshape(-1, D)
    rows_in, rows_out = wg2.shape[0] // CAST_STEPS, wd2.shape[0] // CAST_STEPS
    assert rows_in * CAST_STEPS == wg2.shape[0] and rows_out * CAST_STEPS == wd2.shape[0] and CAST_STEPS <= NT
    cast_blk = lambda i: (jnp.minimum(i, CAST_STEPS - 1), 0)
    cast_specs = [pl.BlockSpec((rows_in, D_EXP), cast_blk), pl.BlockSpec((rows_in, D_EXP), cast_blk),
                  pl.BlockSpec((rows_out, D), cast_blk)]
    outs = pl.pallas_call(
        _even_kernel,
        grid=(NT,),
        in_specs=_dual_specs(TM, D, NPT) + [
            pl.BlockSpec((HALO, D), lambda i: (jnp.maximum((i - NPT) * hb - 1, 0), 0)),
            pl.BlockSpec((HALO, D), lambda i: (jnp.clip((i - NPT + 1) * hb, 0, nhb - 1), 0)),
            pl.BlockSpec((6, MOD_ROWS, D), lambda i: (0, 0, 0)),
            pl.BlockSpec((1, D), lambda i: (0, 0)),
            pl.BlockSpec((D, EVEN_IN), lambda i: (0, 0)),
            pl.BlockSpec((4, POOL_GW, POOL_GW), lambda i: (0, 0, 0)),
            pl.BlockSpec((1, POOL_W), lambda i: (0, 0)),
            pl.BlockSpec((8, CONV_W), lambda i: (0, 0)),
            pl.BlockSpec((D, D), lambda i: (0, 0)),
        ] + cast_specs,
        out_specs=[pl.BlockSpec((TM, D), lambda i: (i, 0))] + cast_specs,
        out_shape=[jax.ShapeDtypeStruct((N_TOK, D), F32), jax.ShapeDtypeStruct(wg2.shape, BF16),
                   jax.ShapeDtypeStruct(wu2.shape, BF16), jax.ShapeDtypeStruct(wd2.shape, BF16)],
        scratch_shapes=[pltpu.VMEM((TM + 2 * HALO, POOL_W), F32),
                        pltpu.VMEM((TM + 2 * HALO, CONV_W), F32)],
        compiler_params=_cparams("arbitrary"),
        name="even_layer",
    )(xp, xs, xs, xs, mod_l, g, w_in, pool_w, pool_scale, conv_w, w_out, wg2, wu2, wd2)
    return outs[0], outs[1].reshape(wg.shape), outs[2].reshape(wu.shape), outs[3].reshape(wd.shape)


def _rope(x, cos, sin_signed):
    n = x.shape[1] // 128
    cosf = jnp.concatenate([cos] * n, axis=1) if n > 1 else cos
    sinf = jnp.concatenate([sin_signed] * n, axis=1) if n > 1 else sin_signed
    w = x.shape[1]
    lane = lax.broadcasted_iota(jnp.int32, x.shape, 1)
    up = pltpu.roll(x, w - DH // 4, 1)
    dn = pltpu.roll(x, DH // 4, 1)
    rot = jnp.where((lane % (DH // 2)) < DH // 4, up, dn)
    return x * cosf + rot * sinf


def _odd_in_kernel(x_ref, mod_ref, g_ref, w_ref, cos_ref, sin_ref, *out_refs, tile0, rope, kv_f32):
    i = pl.program_id(0) + tile0
    r = _mod_row(i)
    h = _rms_mod(x_ref[...], g_ref[...], mod_ref[0, pl.ds(r, 1), :], mod_ref[1, pl.ds(r, 1), :])
    z = _dot(h.astype(BF16), w_ref[...])
    qna_ref, kna_ref, vna_ref, qsw_ref, ksw_ref, vsw_ref = out_refs[:6]
    c0 = 0
    qna = z[:, 0:NA_W]
    kna = z[:, NA_W:2 * NA_W]
    vna = z[:, 2 * NA_W:3 * NA_W]
    c0 = 3 * NA_W
    qsw = z[:, c0:c0 + SWQ_W]
    ksw = z[:, c0 + SWQ_W:c0 + SWQ_W + SWKV_W]
    vsw = z[:, c0 + SWQ_W + SWKV_W:]
    if kv_f32:
        for ref, val in zip(out_refs[6:10], (kna, vna, ksw, vsw)):
            heads = val.shape[1] // DH
            for hh in range(heads):
                ref[pl.ds(hh, SEQ, stride=heads), :] = val[:, hh * DH:(hh + 1) * DH]
    if rope:
        cos = cos_ref[...]
        sin = sin_ref[...]
        qsw = _rope(qsw, cos, sin)
        ksw = _rope(ksw, cos, sin)
    qna_ref[...] = (qna * QK_SCALE).astype(BF16)
    kna_ref[...] = kna.astype(BF16)
    vna_ref[...] = vna.astype(BF16)
    qsw_ref[...] = (qsw * QK_SCALE).astype(BF16)
    ksw_ref[...] = ksw.astype(BF16)
    vsw_ref[...] = vsw.astype(BF16)


def _odd_in(x, mod_l, g, w, cos, sin, *, prompt):
    tile0 = 0 if prompt else NPT
    nt = NPT if prompt else NT - NPT
    n = nt * TM
    widths = [NA_W, NA_W, NA_W, SWQ_W, SWKV_W, SWKV_W]
    out_shape = [jax.ShapeDtypeStruct((n, w_), BF16) for w_ in widths]
    out_specs = [pl.BlockSpec((TM, w_), lambda i: (i, 0)) for w_ in widths]
    if prompt:
        assert TM == SEQ
        for heads in (NA_H, NA_H, SW_KV, SW_KV):
            out_shape.append(jax.ShapeDtypeStruct((BATCH * SEQ * heads, DH), F32))
            out_specs.append(pl.BlockSpec((SEQ * heads, DH), lambda i: (i, 0)))
    return pl.pallas_call(
        functools.partial(_odd_in_kernel, tile0=tile0, rope=not prompt, kv_f32=prompt),
        grid=(nt,),
        in_specs=[
            pl.BlockSpec((TM, D), lambda i: (i + tile0, 0)),
            pl.BlockSpec((6, MOD_ROWS, D), lambda i: (0, 0, 0)),
            pl.BlockSpec((1, D), lambda i: (0, 0)),
            pl.BlockSpec((D, ODD_IN), lambda i: (0, 0)),
            pl.BlockSpec((TM, 128), lambda i: (i % TPS, 0)),
            pl.BlockSpec((TM, 128), lambda i: (i % TPS, 0)),
        ],
        out_specs=out_specs,
        out_shape=out_shape,
        compiler_params=_cparams("parallel"),
        name="odd_in_prompt" if prompt else "odd_in_latent",
    )(x, mod_l, g, w, cos, sin)


def _rope_tables():
    t = np.arange(DEC_SEQ)
    quarter = DH // 4
    inv = 1.0 / (ROPE_BASE ** (np.arange(quarter, dtype=np.float64) / quarter))

    def cos_sin(pos):
        ang = pos.astype(np.float64)[:, None] * inv[None, :]
        ang = np.concatenate([ang, ang], axis=-1)
        return np.cos(ang), np.sin(ang)

    cr, sr = cos_sin(t // GRID_W)
    cc, sc = cos_sin(t % GRID_W)
    cos = np.concatenate([cr, cc], axis=-1)
    sin = np.concatenate([sr, sc], axis=-1)
    sign = np.where((np.arange(DH) % (DH // 2)) < DH // 4, -1.0, 1.0)
    sin = sin * sign[None, :]
    cos = np.concatenate([cos, cos], axis=-1).astype(np.float32)
    sin = np.concatenate([sin, sin], axis=-1).astype(np.float32)
    return jnp.asarray(cos), jnp.asarray(sin)


def _softmax_pv(segs, sink=None):
    m = None
    for s, _ in segs:
        sm = jnp.max(s, axis=-1, keepdims=True)
        m = sm if m is None else jnp.maximum(m, sm)
    if sink is not None:
        m = jnp.maximum(m, sink)
    den = None
    acc = None
    for s, v in segs:
        p = jnp.exp(s - m)
        ps = jnp.sum(p, axis=-1, keepdims=True)
        den = ps if den is None else den + ps
        pv = _dot(p.astype(BF16), v)
        acc = pv if acc is None else acc + pv
    if sink is not None:
        den = den + jnp.exp(sink - m)
    return acc / den


def _sink_col(sink_ref, g, rows_per_head):
    row = lax.broadcasted_iota(jnp.int32, (SW_G * rows_per_head, 1), 0)
    col = jnp.zeros((SW_G * rows_per_head, 1), F32)
    for r in range(SW_G):
        col = jnp.where(row // rows_per_head == r, sink_ref[g * SW_G + r], col)
    return col


def _ctx_attn_kernel(sink_ref, qna_ref, kna_ref, vna_ref, qsw_ref, ksw_ref, vsw_ref, ona_ref, osw_ref):
    lane = lax.broadcasted_iota(jnp.int32, (SEQ, 2 * DH), 1)
    for p in range(NA_H // 2):
        cols = slice(p * 2 * DH, (p + 1) * 2 * DH)
        q = qna_ref[:, cols]
        k = kna_ref[:, cols]
        v = vna_ref[:, cols]
        outs = []
        for half in range(2):
            mine = (lane < DH) if half == 0 else (lane >= DH)
            qm = jnp.where(mine, q, jnp.zeros_like(q))
            outs.append(_softmax_pv([(_dot_nt(qm, k), v)]))
        ona_ref[:, cols] = jnp.where(lane < DH, outs[0], outs[1]).astype(BF16)
    outs = []
    for g in range(SW_KV):
        kc = slice(g * DH, (g + 1) * DH)
        q = jnp.concatenate([qsw_ref[:, (g * SW_G + r) * DH:(g * SW_G + r + 1) * DH] for r in range(SW_G)], axis=0)
        s = _dot_nt(q, ksw_ref[:, kc])
        o = _softmax_pv([(s, vsw_ref[:, kc])], sink=_sink_col(sink_ref, g, SEQ))
        outs.extend(o[r * SEQ:(r + 1) * SEQ, :] for r in range(SW_G))
    osw_ref[...] = jnp.concatenate(outs, axis=1).astype(BF16)


def _ctx_attn(sink, qna, kna, vna, qsw, ksw, vsw):
    def spec(w):
        return pl.BlockSpec((SEQ, w), lambda b: (b, 0))

    return pl.pallas_call(
        _ctx_attn_kernel,
        grid=(BATCH,),
        in_specs=[pl.BlockSpec(memory_space=pltpu.SMEM), spec(NA_W), spec(NA_W), spec(NA_W),
                  spec(SWQ_W), spec(SWKV_W), spec(SWKV_W)],
        out_specs=[spec(NA_W), spec(SWQ_W)],
        out_shape=[jax.ShapeDtypeStruct((N_P, NA_W), BF16), jax.ShapeDtypeStruct((N_P, SWQ_W), BF16)],
        compiler_params=_cparams("parallel"),
        name="ctx_attn",
    )(sink, qna, kna, vna, qsw, ksw, vsw)


def _na_bias_kernel(rpb_ref, o_ref):
    h = pl.program_id(0)
    cq = lax.broadcasted_iota(jnp.int32, (GRID_W, GRID_W), 0)
    ck = lax.broadcasted_iota(jnp.int32, (GRID_W, GRID_W), 1)
    cstart = jnp.clip(cq - NA_COLS // 2, 0, GRID_W - NA_COLS)
    ok = (ck >= cstart) & (ck < cstart + NA_COLS)
    dc = jnp.clip(ck - cq + NA_COLS - 1, 0, 2 * NA_COLS - 2)
    ndc = 2 * NA_COLS - 1
    for d in range(2 * NA_ROWS - 1):
        b = jnp.zeros((GRID_W, GRID_W), F32)
        for e in range(ndc):
            b = jnp.where(dc == e, rpb_ref[h, d * ndc + e], b)
        o_ref[d] = jnp.where(ok, b, NEG)


def _na_bias(rpb):
    nd = 2 * NA_ROWS - 1
    return pl.pallas_call(
        _na_bias_kernel,
        grid=(NA_H,),
        in_specs=[pl.BlockSpec(memory_space=pltpu.SMEM)],
        out_specs=pl.BlockSpec((None, nd, GRID_W, GRID_W), lambda h: (h, 0, 0, 0)),
        out_shape=jax.ShapeDtypeStruct((NA_H, nd, GRID_W, GRID_W), F32),
        compiler_params=_cparams("parallel"),
        name="na_bias",
    )(rpb.reshape(NA_H, nd * (2 * NA_COLS - 1)))


NA_QB = 4
NA_KR = 12
NA_NQB = ROWS // NA_QB


def _na_bias_blocks(bias):
    neg = jnp.full((NA_H, GRID_W, GRID_W), NEG, F32)
    cases = []
    for case in range(3):
        rows = []
        for i in range(NA_QB):
            blocks = []
            for j in range(NA_KR):
                if case == 0:
                    valid, dr = j < NA_ROWS, j - i + NA_ROWS - 1
                elif case == 1:
                    valid, dr = i <= j < i + NA_ROWS, j - i + NA_ROWS // 2 - 1
                else:
                    valid, dr = j >= NA_KR - NA_ROWS, j - i + NA_ROWS - 1 - (NA_KR - NA_QB)
                blocks.append(bias[:, dr] if valid else neg)
            rows.append(jnp.concatenate(blocks, axis=2))
        cases.append(jnp.concatenate(rows, axis=1))
    return jnp.stack(cases)


def _na_kernel(q_ref, k_ref, v_ref, kc_ref, vc_ref, bias_ref, o_ref):
    r0 = pl.program_id(1) * NA_QB
    start = pl.multiple_of(jnp.clip(r0 - NA_ROWS // 2, 0, ROWS - NA_KR) * GRID_W, GRID_W)
    nq = NA_QB * GRID_W
    lane = lax.broadcasted_iota(jnp.int32, (nq, 2 * DH), 1)
    for p in range(NA_H // 2):
        cols = slice(p * 2 * DH, (p + 1) * 2 * DH)
        q = q_ref[:, cols]
        kl = k_ref[pl.ds(start, NA_KR * GRID_W), cols]
        vl = v_ref[pl.ds(start, NA_KR * GRID_W), cols]
        kc = kc_ref[:, cols]
        vc = vc_ref[:, cols]
        outs = []
        for half in range(2):
            mine = (lane < DH) if half == 0 else (lane >= DH)
            qm = jnp.where(mine, q, jnp.zeros_like(q))
            s_loc = _dot_nt(qm, kl) + bias_ref[2 * p + half]
            s_ctx = _dot_nt(qm, kc)
            outs.append(_softmax_pv([(s_loc, vl), (s_ctx, vc)]))
        o_ref[:, cols] = jnp.where(lane < DH, outs[0], outs[1]).astype(BF16)


def _na_attn(q, k, v, kc, vc, bias):
    nq = NA_QB * GRID_W

    def bias_case(b, rb):
        return (jnp.where(rb == 0, 0, jnp.where(rb == NA_NQB - 1, 2, 1)), 0, 0, 0)

    return pl.pallas_call(
        _na_kernel,
        grid=(DEC_BATCH, NA_NQB),
        in_specs=[
            pl.BlockSpec((nq, NA_W), lambda b, rb: (b * NA_NQB + rb, 0)),
            pl.BlockSpec((DEC_SEQ, NA_W), lambda b, rb: (b, 0)),
            pl.BlockSpec((DEC_SEQ, NA_W), lambda b, rb: (b, 0)),
            pl.BlockSpec((None, PAST, NA_W), lambda b, rb: (b, 0, 0)),
            pl.BlockSpec((None, PAST, NA_W), lambda b, rb: (b, 0, 0)),
            pl.BlockSpec((None, NA_H, nq, NA_KR * GRID_W), bias_case),
        ],
        out_specs=pl.BlockSpec((nq, NA_W), lambda b, rb: (b * NA_NQB + rb, 0)),
        out_shape=jax.ShapeDtypeStruct((N_S, NA_W), BF16),
        compiler_params=_cparams("parallel", "arbitrary"),
        name="na_attn",
    )(q, k, v, kc, vc, bias)


def _sw_window_bias():
    q = np.arange(SW_G * ABLK)[:, None] % ABLK
    k = np.arange(3 * ABLK)[None, :]
    tables = [np.where(np.abs(q + lead - k) <= SW_WIN, 0.0, NEG) for lead in (0, ABLK, 2 * ABLK)]
    return jnp.asarray(np.stack(tables).astype(np.float32))


def _sw_kernel(sink_ref, q_ref, k_ref, v_ref, kc_ref, vc_ref, wb_ref, o_ref):
    j = pl.program_id(1)
    nk = 3 * ABLK
    start = pl.multiple_of(jnp.clip((j - 1) * ABLK, 0, DEC_SEQ - nk), ABLK)
    outs = []
    for g in range(SW_KV):
        kcols = slice(g * DH, (g + 1) * DH)
        q = jnp.concatenate([q_ref[:, (g * SW_G + r) * DH:(g * SW_G + r + 1) * DH] for r in range(SW_G)], axis=0)
        kw = k_ref[pl.ds(start, nk), kcols]
        vw = v_ref[pl.ds(start, nk), kcols]
        s_w = _dot_nt(q, kw) + wb_ref[...]
        s_c = _dot_nt(q, kc_ref[:, kcols])
        o = _softmax_pv([(s_w, vw), (s_c, vc_ref[:, kcols])], sink=_sink_col(sink_ref, g, ABLK))
        outs.extend(o[r * ABLK:(r + 1) * ABLK, :] for r in range(SW_G))
    o_ref[...] = jnp.concatenate(outs, axis=1).astype(BF16)


def _sw_attn(sink, q, k, v, kc, vc):
    nb = DEC_SEQ // ABLK
    return pl.pallas_call(
        _sw_kernel,
        grid=(DEC_BATCH, nb),
        in_specs=[
            pl.BlockSpec(memory_space=pltpu.SMEM),
            pl.BlockSpec((ABLK, SWQ_W), lambda b, j: (b * nb + j, 0)),
            pl.BlockSpec((DEC_SEQ, SWKV_W), lambda b, j: (b, 0)),
            pl.BlockSpec((DEC_SEQ, SWKV_W), lambda b, j: (b, 0)),
            pl.BlockSpec((None, PAST, SWKV_W), lambda b, j: (b, 0, 0)),
            pl.BlockSpec((None, PAST, SWKV_W), lambda b, j: (b, 0, 0)),
            pl.BlockSpec((None, SW_G * ABLK, 3 * ABLK),
                         lambda b, j: (jnp.where(j == 0, 0, jnp.where(j == nb - 1, 2, 1)), 0, 0)),
        ],
        out_specs=pl.BlockSpec((ABLK, SWQ_W), lambda b, j: (b * nb + j, 0)),
        out_shape=jax.ShapeDtypeStruct((N_S, SWQ_W), BF16),
        compiler_params=_cparams("parallel", "arbitrary"),
        name="sw_attn",
    )(sink, q, k, v, kc, vc, _sw_window_bias())


BT = 2560
NB = N_TOK // BT
TPB = BT // TM
CH = 256
CH_TAIL = 128
XPS = 2
NES = N_EXP // XPS
XO_ROWS = 2 * BT + CH
ROW = 8
assert D == ROW * 128


def _route_sparse(probs, sel):
    rank = []
    score = []
    for g in range(N_GRP):
        ids = range(g * EPG, (g + 1) * EPG)
        tot = None
        for e in ids:
            rk = jnp.zeros_like(sel[e])
            for j in ids:
                if j < e:
                    rk = rk + (sel[j] >= sel[e]).astype(F32)
                elif j > e:
                    rk = rk + (sel[j] > sel[e]).astype(F32)
            rank.append(rk)
            contrib = jnp.where(rk < 2.0, sel[e], 0.0)
            tot = contrib if tot is None else tot + contrib
        score.append(tot)
    top1, top2 = [], []
    for g in range(N_GRP):
        best = None
        for j in range(N_GRP):
            if j == g:
                continue
            c = (score[g] > score[j]) if j < g else (score[g] >= score[j])
            best = c if best is None else jnp.logical_and(best, c)
        for e in range(g * EPG, (g + 1) * EPG):
            top1.append(jnp.logical_and(best, rank[e] == 0.0))
            top2.append(jnp.logical_and(best, rank[e] == 1.0))
    return top1, top2


def _pick(masks, rows):
    acc = None
    for m, r in zip(masks, rows):
        v = jnp.where(m, r, 0.0)
        acc = v if acc is None else acc + v
    return acc


def _route_tile(xv, mod_ref, g_ref, rw_ref, rb_ref, h_ref, pos_ref, w_ref, cnt_ref, off_ref, meta, carry):
    i = pl.program_id(0)
    j = i % TPB
    r = _mod_row(i)
    h = _rms_mod(xv, g_ref[...], mod_ref[3, pl.ds(r, 1), :], mod_ref[4, pl.ds(r, 1), :])
    h_ref[...] = h.astype(BF16)
    h_hi = h.astype(BF16)
    h_lo = (h - h_hi.astype(F32)).astype(BF16)
    logits = _dot(h_lo, rw_ref[0]) + _dot(h_hi, rw_ref[1]) + _dot(h_hi, rw_ref[0])
    lt = logits.T[:N_EXP, :]
    m = jnp.max(lt, axis=0, keepdims=True)
    ex = jnp.exp(lt - m)
    pr = ex / jnp.sum(ex, axis=0, keepdims=True)
    se = pr + rb_ref[:N_EXP, :]
    probs = [pr[e:e + 1, :] for e in range(N_EXP)]
    sel = [se[e:e + 1, :] for e in range(N_EXP)]
    top1, top2 = _route_sparse(probs, sel)

    @pl.when(j == 0)
    def _():
        carry[...] = jnp.zeros_like(carry)

    member = jnp.concatenate([jnp.logical_or(a, b).astype(F32) for a, b in zip(top1, top2)], axis=0)
    s_idx = lax.broadcasted_iota(jnp.int32, (TM, TM), 0)
    t_idx = lax.broadcasted_iota(jnp.int32, (TM, TM), 1)
    before = jnp.where(s_idx < t_idx, 1.0, 0.0).astype(BF16)
    seen = _dot(member.astype(BF16), before) + carry[:, 0:1]
    seen_rows = [seen[e:e + 1, :] for e in range(N_EXP)]
    ids = [jnp.full((1, TM), float(e), F32) for e in range(N_EXP)]
    p1 = _pick(top1, probs)
    p2 = _pick(top2, probs)
    den = p1 + p2
    w_ref[j, 0:1, :] = p1 / den
    w_ref[j, 1:2, :] = p2 / den
    meta[j, 0:1, :] = _pick(top1, ids)
    meta[j, 1:2, :] = _pick(top2, ids)
    meta[j, 2:3, :] = _pick(top1, seen_rows)
    meta[j, 3:4, :] = _pick(top2, seen_rows)
    carry[...] = carry[...] + jnp.sum(member, axis=1, keepdims=True)

    @pl.when(j == TPB - 1)
    def _():
        cnt = carry[...]
        offs = [jnp.zeros((1, 128), F32)]
        for e in range(1, N_EXP):
            offs.append(offs[-1] + cnt[e - 1:e, :])
        cnt_ref[...] = cnt
        off_ref[...] = jnp.concatenate(offs, axis=0)
        for jj in range(TPB):
            for k in range(2):
                eid = meta[jj, k:k + 1, :]
                pos = meta[jj, 2 + k:3 + k, :]
                for e in range(1, N_EXP):
                    pos = pos + jnp.where(eid == float(e), offs[e][:, 0:1], 0.0)
                pos_ref[jj, k:k + 1, :] = pos.astype(jnp.int32)


def _router_kernel(x_ref, *refs):
    _route_tile(x_ref[...], *refs)


def _odd_out_router_kernel(nap_ref, nas_ref, swp_ref, sws_ref, x_ref, wo_ref, mod_ref, g_ref, rw_ref, rb_ref,
                           x1_ref, *refs):
    i = pl.program_id(0)
    is_p = i < NPT
    ona = jnp.where(is_p, nap_ref[...], nas_ref[...])
    osw = jnp.where(is_p, swp_ref[...], sws_ref[...])
    y = _dot(ona, wo_ref[:NA_W, :]) + _dot(osw, wo_ref[NA_W:, :])
    x1 = x_ref[...] + mod_ref[2, pl.ds(_mod_row(i), 1), :] * y
    x1_ref[...] = x1
    _route_tile(x1, mod_ref, g_ref, rw_ref, rb_ref, *refs)


def _router(x, mod_l, g, rw2, rb_col, attn=None):
    blk = lambda i: (i // TPB, 0, 0, 0)
    in_specs = [
        pl.BlockSpec((TM, D), lambda i: (i, 0)),
        pl.BlockSpec((6, MOD_ROWS, D), lambda i: (0, 0, 0)),
        pl.BlockSpec((1, D), lambda i: (0, 0)),
        pl.BlockSpec((2, D, 128), lambda i: (0, 0, 0)),
        pl.BlockSpec((128, 1), lambda i: (0, 0)),
    ]
    out_specs = [
        pl.BlockSpec((TM, D), lambda i: (i, 0)),
        pl.BlockSpec((None, TPB, 2, TM), blk),
        pl.BlockSpec((None, TPB, 2, TM), blk),
        pl.BlockSpec((None, N_EXP, 128), lambda i: (i // TPB, 0, 0)),
        pl.BlockSpec((None, N_EXP, 128), lambda i: (i // TPB, 0, 0)),
    ]
    out_shape = [
        jax.ShapeDtypeStruct((N_TOK, D), BF16),
        jax.ShapeDtypeStruct((NB, TPB, 2, TM), jnp.int32),
        jax.ShapeDtypeStruct((NB, TPB, 2, TM), F32),
        jax.ShapeDtypeStruct((NB, N_EXP, 128), F32),
        jax.ShapeDtypeStruct((NB, N_EXP, 128), F32),
    ]
    args = (x, mod_l, g, rw2, rb_col)
    body = _router_kernel
    if attn is not None:
        nap, nas, swp, sws, w_out = attn
        in_specs = (_dual_specs(TM, NA_W, NPT) + _dual_specs(TM, SWQ_W, NPT) + in_specs[:1]
                    + [pl.BlockSpec((D, D), lambda i: (0, 0))] + in_specs[1:])
        out_specs = [pl.BlockSpec((TM, D), lambda i: (i, 0))] + out_specs
        out_shape = [jax.ShapeDtypeStruct((N_TOK, D), F32)] + out_shape
        args = (nap, nas, swp, sws, x, w_out, mod_l, g, rw2, rb_col)
        body = _odd_out_router_kernel
    return pl.pallas_call(
        body,
        grid=(NT,),
        in_specs=in_specs,
        out_specs=out_specs,
        out_shape=out_shape,
        scratch_shapes=[pltpu.VMEM((TPB, 4, TM), F32), pltpu.VMEM((N_EXP, 128), F32)],
        compiler_params=_cparams("arbitrary"),
        name="router" if attn is None else "odd_out_router",
    )(*args)


def _row(p):
    return pl.ds(pl.multiple_of(p * ROW, ROW), ROW)


def _expert_segment(n, base, xo, wg_ref, wu_ref, wd_ref):
    def ffn_rows(row0, rows):
        x = jnp.concatenate([xo[pl.ds(row0 * ROW + c, rows, stride=ROW), :] for c in range(ROW)], axis=1)
        xb = x.astype(BF16)
        a = _dot(xb, wg_ref[...])
        u = _dot(xb, wu_ref[...])
        hid = (a / (1.0 + jnp.exp(-a))) * u
        out = _dot(hid.astype(BF16), wd_ref[...])
        valid = lax.broadcasted_iota(jnp.int32, (rows, 1), 0) < base + n - row0
        res = jnp.where(valid, out, x)
        for c in range(ROW):
            xo[pl.ds(row0 * ROW + c, rows, stride=ROW), :] = res[:, c * 128:(c + 1) * 128]

    def chunk(jc, carry):
        ffn_rows(base + jc * CH, CH)
        return carry

    nfull = (n + CH - CH_TAIL - 1) // CH
    lax.fori_loop(0, nfull, chunk, 0)

    @pl.when(n > nfull * CH)
    def _():
        ffn_rows(base + nfull * CH, CH_TAIL)


def _experts_kernel(cnt_ref, off_ref, pos_ref, w_ref, h_ref, wg_ref, wu_ref, wd_ref, x_ref, mod_ref, fg_ref,
                    *rest, final):
    if final:
        op_ref, os_ref, xo, stg = rest
    else:
        o_ref, xo, stg = rest
    b = pl.program_id(0)
    s = pl.program_id(1)

    @pl.when(s == 0)
    def _():
        xo[pl.ds(2 * BT * ROW, CH * ROW), :] = jnp.zeros((CH * ROW, 128), F32)

        def sub(j, carry):
            hs = h_ref[pl.ds(pl.multiple_of(j * TM, TM), TM), :].astype(F32)
            for c in range(ROW):
                stg[pl.ds(c, TM, stride=ROW), :] = hs[:, c * 128:(c + 1) * 128]
            for t in range(TM):
                v = stg[t * ROW:(t + 1) * ROW, :]
                xo[_row(pos_ref[j, 0, t]), :] = v
                xo[_row(pos_ref[j, 1, t]), :] = v
            return carry

        lax.fori_loop(0, TPB, sub, 0)

    @pl.when(s < NES)
    def _():
        for k in range(XPS):
            _expert_segment(cnt_ref[b, s * XPS + k], off_ref[b, s * XPS + k], xo, wg_ref.at[k], wu_ref.at[k],
                            wd_ref.at[k])

    @pl.when(s >= NES - 1)
    def _():
        j = s - (NES - 1)
        tile = b * TPB + j
        g2 = mod_ref[5, pl.ds(_mod_row(tile), 1), :]
        for t in range(TM):
            a = xo[_row(pos_ref[j, 0, t]), :]
            u = xo[_row(pos_ref[j, 1, t]), :]
            stg[t * ROW:(t + 1) * ROW, :] = w_ref[j, 0, t] * a + w_ref[j, 1, t] * u
        y = jnp.concatenate([stg[pl.ds(c, TM, stride=ROW), :] for c in range(ROW)], axis=1)
        res = x_ref[...] + g2 * y
        if final:
            ms = jnp.mean(res * res, axis=-1, keepdims=True)
            res = res * lax.rsqrt(ms + EPS) * fg_ref[...]

            @pl.when(tile < NPT)
            def _():
                op_ref[...] = res

            @pl.when(tile >= NPT)
            def _():
                os_ref[...] = res
        else:
            o_ref[...] = res


def _experts(cnt, off, pos, wts, h, wg, wu, wd, x, mod_l, fg, *, layer, final):
    def tile_of(b, s):
        return b * TPB + jnp.clip(s - (NES - 1), 0, TPB - 1)

    def expert_of(b, s, *_):
        return (layer, jnp.minimum(s, NES - 1), 0, 0)

    smem_blk = pl.BlockSpec((None, TPB, 2, TM), lambda b, s, *_: (b, 0, 0, 0), memory_space=pltpu.SMEM)
    if final:
        out_specs = [pl.BlockSpec((TM, D), lambda b, s, *_: (jnp.minimum(tile_of(b, s), NPT - 1), 0)),
                     pl.BlockSpec((TM, D), lambda b, s, *_: (jnp.maximum(tile_of(b, s) - NPT, 0), 0))]
        out_shape = [jax.ShapeDtypeStruct((N_P, D), F32), jax.ShapeDtypeStruct((N_S, D), F32)]
    else:
        out_specs = pl.BlockSpec((TM, D), lambda b, s, *_: (tile_of(b, s), 0))
        out_shape = jax.ShapeDtypeStruct((N_TOK, D), F32)
    grid_spec = pltpu.PrefetchScalarGridSpec(
        num_scalar_prefetch=2,
        grid=(NB, NES + TPB - 1),
        in_specs=[
            smem_blk,
            smem_blk,
            pl.BlockSpec((BT, D), lambda b, s, *_: (b, 0)),
            pl.BlockSpec((None, XPS, D, D_EXP), expert_of),
            pl.BlockSpec((None, XPS, D, D_EXP), expert_of),
            pl.BlockSpec((None, XPS, D_EXP, D), expert_of),
            pl.BlockSpec((TM, D), lambda b, s, *_: (tile_of(b, s), 0)),
            pl.BlockSpec((6, MOD_ROWS, D), lambda b, s, *_: (0, 0, 0)),
            pl.BlockSpec((1, D), lambda b, s, *_: (0, 0)),
        ],
        out_specs=out_specs,
        scratch_shapes=[pltpu.VMEM((XO_ROWS * ROW, 128), F32), pltpu.VMEM((TM * ROW, 128), F32)],
    )
    return pl.pallas_call(
        functools.partial(_experts_kernel, final=final),
        grid_spec=grid_spec,
        out_shape=out_shape,
        compiler_params=_cparams("arbitrary", "arbitrary"),
        name="experts_final" if final else "experts",
    )(cnt, off, pos, wts, h, wg, wu, wd, x, mod_l, fg)


def _moe_sparse(x, routed, mod_l, wg, wu, wd, fg, *, layer, final):
    h, pos, wts, cnt, off = routed
    cnt = cnt[:, :, 0].astype(jnp.int32)
    off = off[:, :, 0].astype(jnp.int32)
    return _experts(cnt, off, pos, wts, h, wg, wu, wd, x, mod_l, fg, layer=layer, final=final)


def kernel(x_prompt, x_sample, cache_na_k, cache_na_v, cache_sw_k, cache_sw_v, c, c_ctx, mod_w, mod_b, norm_mix_g, norm_ffn_g, ev_w_in, ev_pool_w, ev_pool_scale, ev_conv_w, ev_w_out, od_w_in, od_rpb, od_sink, od_w_out, router_w, router_b, moe_w_gate, moe_w_up, moe_w_down, final_norm_g):
    xp = x_prompt.reshape(N_P, D)
    xs = x_sample.reshape(N_S, D)
    cvec = jnp.concatenate([c_ctx[None, :], c, jnp.zeros((MOD_ROWS - 1 - DEC_BATCH, D), F32)], axis=0)
    mod = _modulation(cvec, mod_w, mod_b)

    rw_pad = jnp.pad(router_w, ((0, 0), (0, 128 - N_EXP)))
    rw_hi = rw_pad.astype(BF16)
    rw2 = jnp.stack([rw_hi, (rw_pad - rw_hi.astype(F32)).astype(BF16)])
    rb_col = jnp.pad(router_b, (0, 128 - N_EXP)).reshape(128, 1)

    conv_w = jnp.pad(ev_conv_w[0], ((0, 8 - ev_conv_w.shape[1]), (0, 0)))
    x, wg, wu, wd = _even_layer(xp, xs, mod[0], norm_mix_g[0:1], ev_w_in[0].astype(BF16),
                                ev_pool_w[0].astype(BF16), ev_pool_scale[0:1], conv_w, ev_w_out[0].astype(BF16),
                                moe_w_gate, moe_w_up, moe_w_down)
    fg = final_norm_g.reshape(1, D)
    routed = _router(x, mod[0], norm_ffn_g[0:1], rw2, rb_col)
    x = _moe_sparse(x, routed, mod[0], wg, wu, wd, fg, layer=0, final=False)

    cos, sin = _rope_tables()
    w_in = od_w_in[0].astype(BF16)
    g1 = norm_mix_g[1:2]
    qna_p, kna_p, vna_p, qsw_p, ksw_p, vsw_p, nak, nav, swk, swv = _odd_in(x, mod[1], g1, w_in, cos, sin, prompt=True)
    qna_s, kna_s, vna_s, qsw_s, ksw_s, vsw_s = _odd_in(x, mod[1], g1, w_in, cos, sin, prompt=False)
    sink = od_sink[0]
    ona_p, osw_p = _ctx_attn(sink, qna_p, kna_p, vna_p, qsw_p, ksw_p, vsw_p)
    bias = _na_bias_blocks(_na_bias(od_rpb[0]))
    ona_s = _na_attn(qna_s, kna_s, vna_s,
                     cache_na_k[:, 0].reshape(DEC_BATCH, PAST, NA_W).astype(BF16),
                     cache_na_v[:, 0].reshape(DEC_BATCH, PAST, NA_W).astype(BF16), bias)
    osw_s = _sw_attn(sink, qsw_s, ksw_s, vsw_s,
                     cache_sw_k[:, 0].reshape(DEC_BATCH, PAST, SWKV_W).astype(BF16),
                     cache_sw_v[:, 0].reshape(DEC_BATCH, PAST, SWKV_W).astype(BF16))
    x, *routed = _router(x, mod[1], norm_ffn_g[1:2], rw2, rb_col,
                         attn=(ona_p, ona_s, osw_p, osw_s, od_w_out[0].astype(BF16)))
    y_prompt, y_sample = _moe_sparse(x, routed, mod[1], wg, wu, wd, fg, layer=1, final=True)
    y_prompt = y_prompt.reshape(BATCH, SEQ, D)
    y_sample = y_sample.reshape(DEC_BATCH, DEC_SEQ, D)
    new_na_k = nak.reshape(BATCH, 1, SEQ, NA_H, DH)
    new_na_v = nav.reshape(BATCH, 1, SEQ, NA_H, DH)
    new_sw_k = swk.reshape(BATCH, 1, SEQ, SW_KV, DH)
    new_sw_v = swv.reshape(BATCH, 1, SEQ, SW_KV, DH)
    return (y_prompt, y_sample, new_na_k, new_na_v, new_sw_k, new_sw_v)
```

```python
import functools

import jax
import jax.numpy as jnp
import numpy as np
from jax import lax
from jax.experimental import pallas as pl
from jax.experimental.pallas import tpu as pltpu

D = 1024
BATCH = 16
SEQ = 256
DEC_BATCH = 4
DEC_SEQ = 4096
PAST = 512
GRID_W = 64
ROWS = DEC_SEQ // GRID_W
DH = 64
POOL_W = 512
POOL_WINDOWS = (2, 4, 8, 16)
POOL_GW = 128
CONV_W = 512
EVEN_IN = POOL_W + 3 * CONV_W
NA_H = 8
NA_ROWS = 8
NA_COLS = 16
SW_H = 8
SW_KV = 2
SW_G = SW_H // SW_KV
SW_WIN = 128
ABLK = 128
NA_W = NA_H * DH
SWQ_W = SW_H * DH
SWKV_W = SW_KV * DH
ODD_IN = 3 * NA_W + SWQ_W + 2 * SWKV_W
N_EXP = 16
N_GRP = 4
EPG = 4
D_EXP = 512
EPS = 1e-6
NEG = -1e30
ROPE_BASE = 10000.0
QK_SCALE = DH ** -0.5
assert QK_SCALE == 0.125

N_P = BATCH * SEQ
N_S = DEC_BATCH * DEC_SEQ
N_TOK = N_P + N_S
MOD_ROWS = 8

TM = 256
NPT = N_P // TM
TPS = DEC_SEQ // TM
NT = N_TOK // TM
HALO = 8
CAST_STEPS = 64

F32 = jnp.float32
BF16 = jnp.bfloat16
VMEM_LIMIT = 56 * 1024 * 1024


def _cparams(*sem):
    return pltpu.CompilerParams(dimension_semantics=sem, vmem_limit_bytes=VMEM_LIMIT)


def _mod_row(i):
    return jnp.where(i < NPT, 0, 1 + (i - NPT) // TPS)


def _rms_mod(x, g, shift, scale):
    ms = jnp.mean(x * x, axis=-1, keepdims=True)
    y = x * lax.rsqrt(ms + EPS) * g
    return y * (1.0 + scale) + shift


def _dot(a, b):
    return jnp.dot(a, b, preferred_element_type=F32)


def _dot_nt(a, b):
    return lax.dot_general(a, b, (((1,), (1,)), ((), ())), preferred_element_type=F32)


def _mod_kernel(cv_ref, w_ref, b_ref, o_ref):
    cv = cv_ref[...]
    a = cv / (1.0 + jnp.exp(-cv))
    o_ref[...] = jnp.dot(a, w_ref[...], preferred_element_type=F32,
                         precision=lax.Precision.HIGHEST) + b_ref[...]


def _modulation(cvec, mod_w, mod_b):
    depth = mod_w.shape[0]
    return pl.pallas_call(
        _mod_kernel,
        grid=(depth, 6),
        in_specs=[
            pl.BlockSpec((MOD_ROWS, D), lambda l, j: (0, 0)),
            pl.BlockSpec((None, D, D), lambda l, j: (l, 0, j)),
            pl.BlockSpec((None, None, 1, D), lambda l, j: (l, j, 0, 0)),
        ],
        out_specs=pl.BlockSpec((None, None, MOD_ROWS, D), lambda l, j: (l, j, 0, 0)),
        out_shape=jax.ShapeDtypeStruct((depth, 6, MOD_ROWS, D), F32),
        compiler_params=_cparams("arbitrary", "arbitrary"),
        name="modulation",
    )(cvec, mod_w, mod_b.reshape(depth, 6, 1, D))


def _dual_specs(tm, width, npt):
    return [
        pl.BlockSpec((tm, width), lambda i: (jnp.minimum(i, npt - 1), 0)),
        pl.BlockSpec((tm, width), lambda i: (jnp.maximum(i - npt, 0), 0)),
    ]


def _even_kernel(xp_ref, xs_ref, xprev_ref, xnext_ref, mod_ref, g_ref, wi_ref, pw_ref, ps_ref, cw_ref,
                 wo_ref, wgf_ref, wuf_ref, wdf_ref, o_ref, wgb_ref, wub_ref, wdb_ref, pext, uext):
    i = pl.program_id(0)

    @pl.when(i < CAST_STEPS)
    def _():
        wgb_ref[...] = wgf_ref[...].astype(BF16)
        wub_ref[...] = wuf_ref[...].astype(BF16)
        wdb_ref[...] = wdf_ref[...].astype(BF16)

    r = _mod_row(i)
    is_p = i < NPT
    t0 = jnp.where(is_p, 0, ((i - NPT) % TPS) * TM)
    seq_len = jnp.where(is_p, SEQ, DEC_SEQ)
    first = t0 == 0
    last = t0 + TM == seq_len

    x = jnp.where(is_p, xp_ref[...], xs_ref[...])
    xe = jnp.concatenate([xprev_ref[...], x, xnext_ref[...]], axis=0)
    h = _rms_mod(xe, g_ref[...], mod_ref[0, pl.ds(r, 1), :], mod_ref[1, pl.ds(r, 1), :])
    ze = _dot(h.astype(BF16), wi_ref[...])
    row = lax.broadcasted_iota(jnp.int32, (TM + 2 * HALO, 1), 0)
    outside = jnp.logical_or(jnp.logical_and(first, row < HALO), jnp.logical_and(last, row >= HALO + TM))
    ze = jnp.where(outside, 0.0, ze)
    pext[...] = ze[:, :POOL_W]
    uext[...] = ze[:, POOL_W + 2 * CONV_W:] * ze[:, POOL_W:POOL_W + CONV_W]
    p = ze[HALO:HALO + TM, :POOL_W]
    gb = ze[HALO:HALO + TM, POOL_W + CONV_W:POOL_W + 2 * CONV_W]

    t = t0 + lax.broadcasted_iota(jnp.int32, (TM, 1), 0)
    mixed = []
    for g, w in enumerate(POOL_WINDOWS):
        cols = slice(g * POOL_GW, (g + 1) * POOL_GW)
        acc = jnp.zeros((TM, POOL_GW), F32)
        for k in range(-(w // 2), w - (w // 2)):
            acc = acc + pext[pl.ds(HALO + k, TM), cols]
        lo = jnp.maximum(t - w // 2, 0)
        hi = jnp.minimum(t + (w - 1 - w // 2), seq_len - 1)
        cnt = (hi - lo + 1).astype(F32)
        d = acc / cnt - p[:, cols]
        mixed.append((_dot(d.astype(BF16), pw_ref[g]) * ps_ref[:, cols]).astype(BF16))
    conv = (uext[pl.ds(HALO - 1, TM), :] * cw_ref[0:1, :] + uext[pl.ds(HALO, TM), :] * cw_ref[1:2, :]
            + uext[pl.ds(HALO + 1, TM), :] * cw_ref[2:3, :])
    mixed.append((gb * conv).astype(BF16))
    y = _dot(jnp.concatenate(mixed, axis=1), wo_ref[...])
    o_ref[...] = x + mod_ref[2, pl.ds(r, 1), :] * y


def _even_layer(xp, xs, mod_l, g, w_in, pool_w, pool_scale, conv_w, w_out, wg, wu, wd):
    hb = TM // HALO
    nhb = N_S // HALO
    wg2, wu2, wd2 = wg.reshape(-1, D_EXP), wu.reshape(-1, D_EXP), wd.reshape(-1, D)
    rows_in, rows_out = wg2.shape[0] // CAST_STEPS, wd2.shape[0] // CAST_STEPS
    assert rows_in * CAST_STEPS == wg2.shape[0] and rows_out * CAST_STEPS == wd2.shape[0] and CAST_STEPS <= NT
    cast_blk = lambda i: (jnp.minimum(i, CAST_STEPS - 1), 0)
    cast_specs = [pl.BlockSpec((rows_in, D_EXP), cast_blk), pl.BlockSpec((rows_in, D_EXP), cast_blk),
                  pl.BlockSpec((rows_out, D), cast_blk)]
    outs = pl.pallas_call(
        _even_kernel,
        grid=(NT,),
        in_specs=_dual_specs(TM, D, NPT) + [
            pl.BlockSpec((HALO, D), lambda i: (jnp.maximum((i - NPT) * hb - 1, 0), 0)),
            pl.BlockSpec((HALO, D), lambda i: (jnp.clip((i - NPT + 1) * hb, 0, nhb - 1), 0)),
            pl.BlockSpec((6, MOD_ROWS, D), lambda i: (0, 0, 0)),
            pl.BlockSpec((1, D), lambda i: (0, 0)),
            pl.BlockSpec((D, EVEN_IN), lambda i: (0, 0)),
            pl.BlockSpec((4, POOL_GW, POOL_GW), lambda i: (0, 0, 0)),
            pl.BlockSpec((1, POOL_W), lambda i: (0, 0)),
            pl.BlockSpec((8, CONV_W), lambda i: (0, 0)),
            pl.BlockSpec((D, D), lambda i: (0, 0)),
        ] + cast_specs,
        out_specs=[pl.BlockSpec((TM, D), lambda i: (i, 0))] + cast_specs,
        out_shape=[jax.ShapeDtypeStruct((N_TOK, D), F32), jax.ShapeDtypeStruct(wg2.shape, BF16),
                   jax.ShapeDtypeStruct(wu2.shape, BF16), jax.ShapeDtypeStruct(wd2.shape, BF16)],
        scratch_shapes=[pltpu.VMEM((TM + 2 * HALO, POOL_W), F32),
                        pltpu.VMEM((TM + 2 * HALO, CONV_W), F32)],
        compiler_params=_cparams("arbitrary"),
        name="even_layer",
    )(xp, xs, xs, xs, mod_l, g, w_in, pool_w, pool_scale, conv_w, w_out, wg2, wu2, wd2)
    return outs[0], outs[1].reshape(wg.shape), outs[2].reshape(wu.shape), outs[3].reshape(wd.shape)


def _rope(x, cos, sin_signed):
    n = x.shape[1] // 128
    cosf = jnp.concatenate([cos] * n, axis=1) if n > 1 else cos
    sinf = jnp.concatenate([sin_signed] * n, axis=1) if n > 1 else sin_signed
    w = x.shape[1]
    lane = lax.broadcasted_iota(jnp.int32, x.shape, 1)
    up = pltpu.roll(x, w - DH // 4, 1)
    dn = pltpu.roll(x, DH // 4, 1)
    rot = jnp.where((lane % (DH // 2)) < DH // 4, up, dn)
    return x * cosf + rot * sinf


def _odd_in_kernel(x_ref, mod_ref, g_ref, w_ref, cos_ref, sin_ref, *out_refs, tm, tile0, rope, kv_f32):
    r = _mod_row((pl.program_id(0) + tile0) * (tm // TM))
    h = _rms_mod(x_ref[...], g_ref[...], mod_ref[0, pl.ds(r, 1), :], mod_ref[1, pl.ds(r, 1), :])
    z = _dot(h.astype(BF16), w_ref[...])
    qna_ref, kna_ref, vna_ref, qsw_ref, ksw_ref, vsw_ref = out_refs[:6]
    c0 = 0
    qna = z[:, 0:NA_W]
    kna = z[:, NA_W:2 * NA_W]
    vna = z[:, 2 * NA_W:3 * NA_W]
    c0 = 3 * NA_W
    qsw = z[:, c0:c0 + SWQ_W]
    ksw = z[:, c0 + SWQ_W:c0 + SWQ_W + SWKV_W]
    vsw = z[:, c0 + SWQ_W + SWKV_W:]
    if kv_f32:
        for ref, val in zip(out_refs[6:10], (kna, vna, ksw, vsw)):
            heads = val.shape[1] // DH
            for hh in range(heads):
                ref[pl.ds(hh, SEQ, stride=heads), :] = val[:, hh * DH:(hh + 1) * DH]
    if rope:
        cos = cos_ref[...]
        sin = sin_ref[...]
        qsw = _rope(qsw, cos, sin)
        ksw = _rope(ksw, cos, sin)
    qna_ref[...] = (qna * QK_SCALE).astype(BF16)
    kna_ref[...] = kna.astype(BF16)
    vna_ref[...] = vna.astype(BF16)
    qsw_ref[...] = (qsw * QK_SCALE).astype(BF16)
    ksw_ref[...] = ksw.astype(BF16)
    vsw_ref[...] = vsw.astype(BF16)


def _odd_in(x, mod_l, g, w, cos, sin, *, prompt):
    tm = TM if prompt else 2 * TM
    tile0 = 0 if prompt else N_P // tm
    nt = (N_P if prompt else N_S) // tm
    n = nt * tm
    widths = [NA_W, NA_W, NA_W, SWQ_W, SWKV_W, SWKV_W]
    out_shape = [jax.ShapeDtypeStruct((n, w_), BF16) for w_ in widths]
    out_specs = [pl.BlockSpec((tm, w_), lambda i: (i, 0)) for w_ in widths]
    if prompt:
        assert tm == SEQ
        for heads in (NA_H, NA_H, SW_KV, SW_KV):
            out_shape.append(jax.ShapeDtypeStruct((BATCH * SEQ * heads, DH), F32))
            out_specs.append(pl.BlockSpec((SEQ * heads, DH), lambda i: (i, 0)))
    return pl.pallas_call(
        functools.partial(_odd_in_kernel, tm=tm, tile0=tile0, rope=not prompt, kv_f32=prompt),
        grid=(nt,),
        in_specs=[
            pl.BlockSpec((tm, D), lambda i: (i + tile0, 0)),
            pl.BlockSpec((6, MOD_ROWS, D), lambda i: (0, 0, 0)),
            pl.BlockSpec((1, D), lambda i: (0, 0)),
            pl.BlockSpec((D, ODD_IN), lambda i: (0, 0)),
            pl.BlockSpec((tm, 128), lambda i: (i % (DEC_SEQ // tm), 0)),
            pl.BlockSpec((tm, 128), lambda i: (i % (DEC_SEQ // tm), 0)),
        ],
        out_specs=out_specs,
        out_shape=out_shape,
        compiler_params=_cparams("parallel"),
        name="odd_in_prompt" if prompt else "odd_in_latent",
    )(x, mod_l, g, w, cos, sin)


def _rope_tables():
    t = np.arange(DEC_SEQ)
    quarter = DH // 4
    inv = 1.0 / (ROPE_BASE ** (np.arange(quarter, dtype=np.float64) / quarter))

    def cos_sin(pos):
        ang = pos.astype(np.float64)[:, None] * inv[None, :]
        ang = np.concatenate([ang, ang], axis=-1)
        return np.cos(ang), np.sin(ang)

    cr, sr = cos_sin(t // GRID_W)
    cc, sc = cos_sin(t % GRID_W)
    cos = np.concatenate([cr, cc], axis=-1)
    sin = np.concatenate([sr, sc], axis=-1)
    sign = np.where((np.arange(DH) % (DH // 2)) < DH // 4, -1.0, 1.0)
    sin = sin * sign[None, :]
    cos = np.concatenate([cos, cos], axis=-1).astype(np.float32)
    sin = np.concatenate([sin, sin], axis=-1).astype(np.float32)
    return jnp.asarray(cos), jnp.asarray(sin)


def _softmax_pv(segs, sink=None):
    m = None
    for s, _ in segs:
        sm = jnp.max(s, axis=-1, keepdims=True)
        m = sm if m is None else jnp.maximum(m, sm)
    if sink is not None:
        m = jnp.maximum(m, sink)
    den = None
    acc = None
    for s, v in segs:
        p = jnp.exp(s - m)
        ps = jnp.sum(p, axis=-1, keepdims=True)
        den = ps if den is None else den + ps
        pv = _dot(p.astype(BF16), v)
        acc = pv if acc is None else acc + pv
    if sink is not None:
        den = den + jnp.exp(sink - m)
    return acc / den


def _sink_col(sink_ref, g, rows_per_head):
    row = lax.broadcasted_iota(jnp.int32, (SW_G * rows_per_head, 1), 0)
    col = jnp.zeros((SW_G * rows_per_head, 1), F32)
    for r in range(SW_G):
        col = jnp.where(row // rows_per_head == r, sink_ref[g * SW_G + r], col)
    return col


def _ctx_attn_kernel(sink_ref, qna_ref, kna_ref, vna_ref, qsw_ref, ksw_ref, vsw_ref, ona_ref, osw_ref):
    lane = lax.broadcasted_iota(jnp.int32, (SEQ, 2 * DH), 1)
    for p in range(NA_H // 2):
        cols = slice(p * 2 * DH, (p + 1) * 2 * DH)
        q = qna_ref[:, cols]
        k = kna_ref[:, cols]
        v = vna_ref[:, cols]
        outs = []
        for half in range(2):
            mine = (lane < DH) if half == 0 else (lane >= DH)
            qm = jnp.where(mine, q, jnp.zeros_like(q))
            outs.append(_softmax_pv([(_dot_nt(qm, k), v)]))
        ona_ref[:, cols] = jnp.where(lane < DH, outs[0], outs[1]).astype(BF16)
    outs = []
    for g in range(SW_KV):
        kc = slice(g * DH, (g + 1) * DH)
        q = jnp.concatenate([qsw_ref[:, (g * SW_G + r) * DH:(g * SW_G + r + 1) * DH] for r in range(SW_G)], axis=0)
        s = _dot_nt(q, ksw_ref[:, kc])
        o = _softmax_pv([(s, vsw_ref[:, kc])], sink=_sink_col(sink_ref, g, SEQ))
        outs.extend(o[r * SEQ:(r + 1) * SEQ, :] for r in range(SW_G))
    osw_ref[...] = jnp.concatenate(outs, axis=1).astype(BF16)


def _ctx_attn(sink, qna, kna, vna, qsw, ksw, vsw):
    def spec(w):
        return pl.BlockSpec((SEQ, w), lambda b: (b, 0))

    return pl.pallas_call(
        _ctx_attn_kernel,
        grid=(BATCH,),
        in_specs=[pl.BlockSpec(memory_space=pltpu.SMEM), spec(NA_W), spec(NA_W), spec(NA_W),
                  spec(SWQ_W), spec(SWKV_W), spec(SWKV_W)],
        out_specs=[spec(NA_W), spec(SWQ_W)],
        out_shape=[jax.ShapeDtypeStruct((N_P, NA_W), BF16), jax.ShapeDtypeStruct((N_P, SWQ_W), BF16)],
        compiler_params=_cparams("parallel"),
        name="ctx_attn",
    )(sink, qna, kna, vna, qsw, ksw, vsw)


def _na_bias_kernel(rpb_ref, o_ref):
    h = pl.program_id(0)
    cq = lax.broadcasted_iota(jnp.int32, (GRID_W, GRID_W), 0)
    ck = lax.broadcasted_iota(jnp.int32, (GRID_W, GRID_W), 1)
    cstart = jnp.clip(cq - NA_COLS // 2, 0, GRID_W - NA_COLS)
    ok = (ck >= cstart) & (ck < cstart + NA_COLS)
    dc = jnp.clip(ck - cq + NA_COLS - 1, 0, 2 * NA_COLS - 2)
    ndc = 2 * NA_COLS - 1
    for d in range(2 * NA_ROWS - 1):
        b = jnp.zeros((GRID_W, GRID_W), F32)
        for e in range(ndc):
            b = jnp.where(dc == e, rpb_ref[h, d * ndc + e], b)
        o_ref[d] = jnp.where(ok, b, NEG)


def _na_bias(rpb):
    nd = 2 * NA_ROWS - 1
    return pl.pallas_call(
        _na_bias_kernel,
        grid=(NA_H,),
        in_specs=[pl.BlockSpec(memory_space=pltpu.SMEM)],
        out_specs=pl.BlockSpec((None, nd, GRID_W, GRID_W), lambda h: (h, 0, 0, 0)),
        out_shape=jax.ShapeDtypeStruct((NA_H, nd, GRID_W, GRID_W), F32),
        compiler_params=_cparams("parallel"),
        name="na_bias",
    )(rpb.reshape(NA_H, nd * (2 * NA_COLS - 1)))


NA_QB = 4
NA_KR = 12
NA_NQB = ROWS // NA_QB


def _na_bias_blocks(bias):
    neg = jnp.full((NA_H, GRID_W, GRID_W), NEG, F32)
    cases = []
    for case in range(3):
        rows = []
        for i in range(NA_QB):
            blocks = []
            for j in range(NA_KR):
                if case == 0:
                    valid, dr = j < NA_ROWS, j - i + NA_ROWS - 1
                elif case == 1:
                    valid, dr = i <= j < i + NA_ROWS, j - i + NA_ROWS // 2 - 1
                else:
                    valid, dr = j >= NA_KR - NA_ROWS, j - i + NA_ROWS - 1 - (NA_KR - NA_QB)
                blocks.append(bias[:, dr] if valid else neg)
            rows.append(jnp.concatenate(blocks, axis=2))
        cases.append(jnp.concatenate(rows, axis=1))
    return jnp.stack(cases)


def _na_kernel(q_ref, k_ref, v_ref, kc_ref, vc_ref, bias_ref, o_ref):
    r0 = pl.program_id(1) * NA_QB
    start = pl.multiple_of(jnp.clip(r0 - NA_ROWS // 2, 0, ROWS - NA_KR) * GRID_W, GRID_W)
    nq = NA_QB * GRID_W
    lane = lax.broadcasted_iota(jnp.int32, (nq, 2 * DH), 1)
    for p in range(NA_H // 2):
        cols = slice(p * 2 * DH, (p + 1) * 2 * DH)
        q = q_ref[:, cols]
        kl = k_ref[pl.ds(start, NA_KR * GRID_W), cols]
        vl = v_ref[pl.ds(start, NA_KR * GRID_W), cols]
        kc = kc_ref[:, cols]
        vc = vc_ref[:, cols]
        zero = jnp.zeros_like(q)
        qm = jnp.concatenate([jnp.where(lane < DH, q, zero), jnp.where(lane < DH, zero, q)], axis=0)
        s_loc = _dot_nt(qm, kl)
        s_ctx = _dot_nt(qm, kc)
        outs = []
        for half in range(2):
            rows = slice(half * nq, (half + 1) * nq)
            outs.append(_softmax_pv([(s_loc[rows] + bias_ref[2 * p + half], vl), (s_ctx[rows], vc)]))
        o_ref[:, cols] = jnp.where(lane < DH, outs[0], outs[1]).astype(BF16)


def _na_attn(q, k, v, kc, vc, bias):
    nq = NA_QB * GRID_W

    def bias_case(b, rb):
        return (jnp.where(rb == 0, 0, jnp.where(rb == NA_NQB - 1, 2, 1)), 0, 0, 0)

    return pl.pallas_call(
        _na_kernel,
        grid=(DEC_BATCH, NA_NQB),
        in_specs=[
            pl.BlockSpec((nq, NA_W), lambda b, rb: (b * NA_NQB + rb, 0)),
            pl.BlockSpec((DEC_SEQ, NA_W), lambda b, rb: (b, 0)),
            pl.BlockSpec((DEC_SEQ, NA_W), lambda b, rb: (b, 0)),
            pl.BlockSpec((None, PAST, NA_W), lambda b, rb: (b, 0, 0)),
            pl.BlockSpec((None, PAST, NA_W), lambda b, rb: (b, 0, 0)),
            pl.BlockSpec((None, NA_H, nq, NA_KR * GRID_W), bias_case),
        ],
        out_specs=pl.BlockSpec((nq, NA_W), lambda b, rb: (b * NA_NQB + rb, 0)),
        out_shape=jax.ShapeDtypeStruct((N_S, NA_W), BF16),
        compiler_params=_cparams("parallel", "arbitrary"),
        name="na_attn",
    )(q, k, v, kc, vc, bias)


def _sw_window_bias():
    q = np.arange(SW_G * ABLK)[:, None] % ABLK
    k = np.arange(3 * ABLK)[None, :]
    tables = [np.where(np.abs(q + lead - k) <= SW_WIN, 0.0, NEG) for lead in (0, ABLK, 2 * ABLK)]
    return jnp.asarray(np.stack(tables).astype(np.float32))


def _sw_kernel(sink_ref, q_ref, k_ref, v_ref, kc_ref, vc_ref, wb_ref, o_ref):
    j = pl.program_id(1)
    nk = 3 * ABLK
    start = pl.multiple_of(jnp.clip((j - 1) * ABLK, 0, DEC_SEQ - nk), ABLK)
    outs = []
    for g in range(SW_KV):
        kcols = slice(g * DH, (g + 1) * DH)
        q = jnp.concatenate([q_ref[:, (g * SW_G + r) * DH:(g * SW_G + r + 1) * DH] for r in range(SW_G)], axis=0)
        kw = k_ref[pl.ds(start, nk), kcols]
        vw = v_ref[pl.ds(start, nk), kcols]
        s_w = _dot_nt(q, kw) + wb_ref[...]
        s_c = _dot_nt(q, kc_ref[:, kcols])
        o = _softmax_pv([(s_w, vw), (s_c, vc_ref[:, kcols])], sink=_sink_col(sink_ref, g, ABLK))
        outs.extend(o[r * ABLK:(r + 1) * ABLK, :] for r in range(SW_G))
    o_ref[...] = jnp.concatenate(outs, axis=1).astype(BF16)


def _sw_attn(sink, q, k, v, kc, vc):
    nb = DEC_SEQ // ABLK
    return pl.pallas_call(
        _sw_kernel,
        grid=(DEC_BATCH, nb),
        in_specs=[
            pl.BlockSpec(memory_space=pltpu.SMEM),
            pl.BlockSpec((ABLK, SWQ_W), lambda b, j: (b * nb + j, 0)),
            pl.BlockSpec((DEC_SEQ, SWKV_W), lambda b, j: (b, 0)),
            pl.BlockSpec((DEC_SEQ, SWKV_W), lambda b, j: (b, 0)),
            pl.BlockSpec((None, PAST, SWKV_W), lambda b, j: (b, 0, 0)),
            pl.BlockSpec((None, PAST, SWKV_W), lambda b, j: (b, 0, 0)),
            pl.BlockSpec((None, SW_G * ABLK, 3 * ABLK),
                         lambda b, j: (jnp.where(j == 0, 0, jnp.where(j == nb - 1, 2, 1)), 0, 0)),
        ],
        out_specs=pl.BlockSpec((ABLK, SWQ_W), lambda b, j: (b * nb + j, 0)),
        out_shape=jax.ShapeDtypeStruct((N_S, SWQ_W), BF16),
        compiler_params=_cparams("parallel", "arbitrary"),
        name="sw_attn",
    )(sink, q, k, v, kc, vc, _sw_window_bias())


BT = 2560
NB = N_TOK // BT
TPB = BT // TM
CH = 256
CH_TAIL = 128
XPS = 2
NES = N_EXP // XPS
XO_ROWS = 2 * BT + CH
ROW = 8
assert D == ROW * 128


def _route_sparse(probs, sel):
    rank = []
    score = []
    for g in range(N_GRP):
        ids = range(g * EPG, (g + 1) * EPG)
        tot = None
        for e in ids:
            rk = jnp.zeros_like(sel[e])
            for j in ids:
                if j < e:
                    rk = rk + (sel[j] >= sel[e]).astype(F32)
                elif j > e:
                    rk = rk + (sel[j] > sel[e]).astype(F32)
            rank.append(rk)
            contrib = jnp.where(rk < 2.0, sel[e], 0.0)
            tot = contrib if tot is None else tot + contrib
        score.append(tot)
    top1, top2 = [], []
    for g in range(N_GRP):
        best = None
        for j in range(N_GRP):
            if j == g:
                continue
            c = (score[g] > score[j]) if j < g else (score[g] >= score[j])
            best = c if best is None else jnp.logical_and(best, c)
        for e in range(g * EPG, (g + 1) * EPG):
            top1.append(jnp.logical_and(best, rank[e] == 0.0))
            top2.append(jnp.logical_and(best, rank[e] == 1.0))
    return top1, top2


def _pick(masks, rows):
    acc = None
    for m, r in zip(masks, rows):
        v = jnp.where(m, r, 0.0)
        acc = v if acc is None else acc + v
    return acc


def _route_tile(xv, mod_ref, g_ref, rw_ref, rb_ref, h_ref, pos_ref, w_ref, cnt_ref, off_ref, meta, carry):
    i = pl.program_id(0)
    j = i % TPB
    r = _mod_row(i)
    h = _rms_mod(xv, g_ref[...], mod_ref[3, pl.ds(r, 1), :], mod_ref[4, pl.ds(r, 1), :])
    h_ref[...] = h.astype(BF16)
    h_hi = h.astype(BF16)
    h_lo = (h - h_hi.astype(F32)).astype(BF16)
    logits = _dot(h_lo, rw_ref[0]) + _dot(h_hi, rw_ref[1]) + _dot(h_hi, rw_ref[0])
    lt = logits.T[:N_EXP, :]
    m = jnp.max(lt, axis=0, keepdims=True)
    ex = jnp.exp(lt - m)
    pr = ex / jnp.sum(ex, axis=0, keepdims=True)
    se = pr + rb_ref[:N_EXP, :]
    probs = [pr[e:e + 1, :] for e in range(N_EXP)]
    sel = [se[e:e + 1, :] for e in range(N_EXP)]
    top1, top2 = _route_sparse(probs, sel)

    @pl.when(j == 0)
    def _():
        carry[...] = jnp.zeros_like(carry)

    member = jnp.concatenate([jnp.logical_or(a, b).astype(F32) for a, b in zip(top1, top2)], axis=0)
    s_idx = lax.broadcasted_iota(jnp.int32, (TM, TM), 0)
    t_idx = lax.broadcasted_iota(jnp.int32, (TM, TM), 1)
    before = jnp.where(s_idx < t_idx, 1.0, 0.0).astype(BF16)
    seen = _dot(member.astype(BF16), before) + carry[:, 0:1]
    seen_rows = [seen[e:e + 1, :] for e in range(N_EXP)]
    ids = [jnp.full((1, TM), float(e), F32) for e in range(N_EXP)]
    p1 = _pick(top1, probs)
    p2 = _pick(top2, probs)
    den = p1 + p2
    w_ref[j, 0:1, :] = p1 / den
    w_ref[j, 1:2, :] = p2 / den
    meta[j, 0:1, :] = _pick(top1, ids)
    meta[j, 1:2, :] = _pick(top2, ids)
    meta[j, 2:3, :] = _pick(top1, seen_rows)
    meta[j, 3:4, :] = _pick(top2, seen_rows)
    carry[...] = carry[...] + jnp.sum(member, axis=1, keepdims=True)

    @pl.when(j == TPB - 1)
    def _():
        cnt = carry[...]
        offs = [jnp.zeros((1, 128), F32)]
        for e in range(1, N_EXP):
            offs.append(offs[-1] + cnt[e - 1:e, :])
        cnt_ref[...] = cnt
        off_ref[...] = jnp.concatenate(offs, axis=0)
        for jj in range(TPB):
            for k in range(2):
                eid = meta[jj, k:k + 1, :]
                pos = meta[jj, 2 + k:3 + k, :]
                for e in range(1, N_EXP):
                    pos = pos + jnp.where(eid == float(e), offs[e][:, 0:1], 0.0)
                pos_ref[jj, k:k + 1, :] = pos.astype(jnp.int32)


def _router_kernel(x_ref, *refs):
    _route_tile(x_ref[...], *refs)


def _odd_out_router_kernel(nap_ref, nas_ref, swp_ref, sws_ref, x_ref, wo_ref, mod_ref, g_ref, rw_ref, rb_ref,
                           x1_ref, *refs):
    i = pl.program_id(0)
    is_p = i < NPT
    ona = jnp.where(is_p, nap_ref[...], nas_ref[...])
    osw = jnp.where(is_p, swp_ref[...], sws_ref[...])
    y = _dot(ona, wo_ref[:NA_W, :]) + _dot(osw, wo_ref[NA_W:, :])
    x1 = x_ref[...] + mod_ref[2, pl.ds(_mod_row(i), 1), :] * y
    x1_ref[...] = x1
    _route_tile(x1, mod_ref, g_ref, rw_ref, rb_ref, *refs)


def _router(x, mod_l, g, rw2, rb_col, attn=None):
    blk = lambda i: (i // TPB, 0, 0, 0)
    in_specs = [
        pl.BlockSpec((TM, D), lambda i: (i, 0)),
        pl.BlockSpec((6, MOD_ROWS, D), lambda i: (0, 0, 0)),
        pl.BlockSpec((1, D), lambda i: (0, 0)),
        pl.BlockSpec((2, D, 128), lambda i: (0, 0, 0)),
        pl.BlockSpec((128, 1), lambda i: (0, 0)),
    ]
    out_specs = [
        pl.BlockSpec((TM, D), lambda i: (i, 0)),
        pl.BlockSpec((None, TPB, 2, TM), blk),
        pl.BlockSpec((None, TPB, 2, TM), blk),
        pl.BlockSpec((None, N_EXP, 128), lambda i: (i // TPB, 0, 0)),
        pl.BlockSpec((None, N_EXP, 128), lambda i: (i // TPB, 0, 0)),
    ]
    out_shape = [
        jax.ShapeDtypeStruct((N_TOK, D), BF16),
        jax.ShapeDtypeStruct((NB, TPB, 2, TM), jnp.int32),
        jax.ShapeDtypeStruct((NB, TPB, 2, TM), F32),
        jax.ShapeDtypeStruct((NB, N_EXP, 128), F32),
        jax.ShapeDtypeStruct((NB, N_EXP, 128), F32),
    ]
    args = (x, mod_l, g, rw2, rb_col)
    body = _router_kernel
    if attn is not None:
        nap, nas, swp, sws, w_out = attn
        in_specs = (_dual_specs(TM, NA_W, NPT) + _dual_specs(TM, SWQ_W, NPT) + in_specs[:1]
                    + [pl.BlockSpec((D, D), lambda i: (0, 0))] + in_specs[1:])
        out_specs = [pl.BlockSpec((TM, D), lambda i: (i, 0))] + out_specs
        out_shape = [jax.ShapeDtypeStruct((N_TOK, D), F32)] + out_shape
        args = (nap, nas, swp, sws, x, w_out, mod_l, g, rw2, rb_col)
        body = _odd_out_router_kernel
    return pl.pallas_call(
        body,
        grid=(NT,),
        in_specs=in_specs,
        out_specs=out_specs,
        out_shape=out_shape,
        scratch_shapes=[pltpu.VMEM((TPB, 4, TM), F32), pltpu.VMEM((N_EXP, 128), F32)],
        compiler_params=_cparams("arbitrary"),
        name="router" if attn is None else "odd_out_router",
    )(*args)


def _row(p):
    return pl.ds(pl.multiple_of(p * ROW, ROW), ROW)


def _expert_segment(n, base, xo, wg_ref, wu_ref, wd_ref):
    def ffn_rows(row0, rows):
        x = jnp.concatenate([xo[pl.ds(row0 * ROW + c, rows, stride=ROW), :] for c in range(ROW)], axis=1)
        xb = x.astype(BF16)
        a = _dot(xb, wg_ref[...])
        u = _dot(xb, wu_ref[...])
        hid = (a / (1.0 + jnp.exp(-a))) * u
        out = _dot(hid.astype(BF16), wd_ref[...])
        valid = lax.broadcasted_iota(jnp.int32, (rows, 1), 0) < base + n - row0
        res = jnp.where(valid, out, x)
        for c in range(ROW):
            xo[pl.ds(row0 * ROW + c, rows, stride=ROW), :] = res[:, c * 128:(c + 1) * 128]

    def chunk(jc, carry):
        ffn_rows(base + jc * CH, CH)
        return carry

    nfull = (n + CH - CH_TAIL - 1) // CH
    lax.fori_loop(0, nfull, chunk, 0)

    @pl.when(n > nfull * CH)
    def _():
        ffn_rows(base + nfull * CH, CH_TAIL)


def _experts_kernel(cnt_ref, off_ref, pos_ref, w_ref, h_ref, wg_ref, wu_ref, wd_ref, x_ref, mod_ref, fg_ref,
                    *rest, final):
    if final:
        op_ref, os_ref, xo, stg = rest
    else:
        o_ref, xo, stg = rest
    b = pl.program_id(0)
    s = pl.program_id(1)

    @pl.when(s == 0)
    def _():
        xo[pl.ds(2 * BT * ROW, CH * ROW), :] = jnp.zeros((CH * ROW, 128), F32)

        def sub(j, carry):
            hs = h_ref[pl.ds(pl.multiple_of(j * TM, TM), TM), :].astype(F32)
            for c in range(ROW):
                stg[pl.ds(c, TM, stride=ROW), :] = hs[:, c * 128:(c + 1) * 128]
            for t in range(TM):
                v = stg[t * ROW:(t + 1) * ROW, :]
                xo[_row(pos_ref[j, 0, t]), :] = v
                xo[_row(pos_ref[j, 1, t]), :] = v
            return carry

        lax.fori_loop(0, TPB, sub, 0)

    @pl.when(s < NES)
    def _():
        for k in range(XPS):
            _expert_segment(cnt_ref[b, s * XPS + k], off_ref[b, s * XPS + k], xo, wg_ref.at[k], wu_ref.at[k],
                            wd_ref.at[k])

    @pl.when(s >= NES - 1)
    def _():
        j = s - (NES - 1)
        tile = b * TPB + j
        g2 = mod_ref[5, pl.ds(_mod_row(tile), 1), :]
        for t in range(TM):
            a = xo[_row(pos_ref[j, 0, t]), :]
            u = xo[_row(pos_ref[j, 1, t]), :]
            stg[t * ROW:(t + 1) * ROW, :] = w_ref[j, 0, t] * a + w_ref[j, 1, t] * u
        y = jnp.concatenate([stg[pl.ds(c, TM, stride=ROW), :] for c in range(ROW)], axis=1)
        res = x_ref[...] + g2 * y
        if final:
            ms = jnp.mean(res * res, axis=-1, keepdims=True)
            res = res * lax.rsqrt(ms + EPS) * fg_ref[...]

            @pl.when(tile < NPT)
            def _():
                op_ref[...] = res

            @pl.when(tile >= NPT)
            def _():
                os_ref[...] = res
        else:
            o_ref[...] = res


def _experts(cnt, off, pos, wts, h, wg, wu, wd, x, mod_l, fg, *, layer, final):
    def tile_of(b, s):
        return b * TPB + jnp.clip(s - (NES - 1), 0, TPB - 1)

    def expert_of(b, s, *_):
        return (layer, jnp.minimum(s, NES - 1), 0, 0)

    smem_blk = pl.BlockSpec((None, TPB, 2, TM), lambda b, s, *_: (b, 0, 0, 0), memory_space=pltpu.SMEM)
    if final:
        out_specs = [pl.BlockSpec((TM, D), lambda b, s, *_: (jnp.minimum(tile_of(b, s), NPT - 1), 0)),
                     pl.BlockSpec((TM, D), lambda b, s, *_: (jnp.maximum(tile_of(b, s) - NPT, 0), 0))]
        out_shape = [jax.ShapeDtypeStruct((N_P, D), F32), jax.ShapeDtypeStruct((N_S, D), F32)]
    else:
        out_specs = pl.BlockSpec((TM, D), lambda b, s, *_: (tile_of(b, s), 0))
        out_shape = jax.ShapeDtypeStruct((N_TOK, D), F32)
    grid_spec = pltpu.PrefetchScalarGridSpec(
        num_scalar_prefetch=2,
        grid=(NB, NES + TPB - 1),
        in_specs=[
            smem_blk,
            smem_blk,
            pl.BlockSpec((BT, D), lambda b, s, *_: (b, 0)),
            pl.BlockSpec((None, XPS, D, D_EXP), expert_of),
            pl.BlockSpec((None, XPS, D, D_EXP), expert_of),
            pl.BlockSpec((None, XPS, D_EXP, D), expert_of),
            pl.BlockSpec((TM, D), lambda b, s, *_: (tile_of(b, s), 0)),
            pl.BlockSpec((6, MOD_ROWS, D), lambda b, s, *_: (0, 0, 0)),
            pl.BlockSpec((1, D), lambda b, s, *_: (0, 0)),
        ],
        out_specs=out_specs,
        scratch_shapes=[pltpu.VMEM((XO_ROWS * ROW, 128), F32), pltpu.VMEM((TM * ROW, 128), F32)],
    )
    return pl.pallas_call(
        functools.partial(_experts_kernel, final=final),
        grid_spec=grid_spec,
        out_shape=out_shape,
        compiler_params=_cparams("arbitrary", "arbitrary"),
        name="experts_final" if final else "experts",
    )(cnt, off, pos, wts, h, wg, wu, wd, x, mod_l, fg)


def _moe_sparse(x, routed, mod_l, wg, wu, wd, fg, *, layer, final):
    h, pos, wts, cnt, off = routed
    cnt = cnt[:, :, 0].astype(jnp.int32)
    off = off[:, :, 0].astype(jnp.int32)
    return _experts(cnt, off, pos, wts, h, wg, wu, wd, x, mod_l, fg, layer=layer, final=final)


def kernel(x_prompt, x_sample, cache_na_k, cache_na_v, cache_sw_k, cache_sw_v, c, c_ctx, mod_w, mod_b, norm_mix_g, norm_ffn_g, ev_w_in, ev_pool_w, ev_pool_scale, ev_conv_w, ev_w_out, od_w_in, od_rpb, od_sink, od_w_out, router_w, router_b, moe_w_gate, moe_w_up, moe_w_down, final_norm_g):
    xp = x_prompt.reshape(N_P, D)
    xs = x_sample.reshape(N_S, D)
    cvec = jnp.concatenate([c_ctx[None, :], c, jnp.zeros((MOD_ROWS - 1 - DEC_BATCH, D), F32)], axis=0)
    mod = _modulation(cvec, mod_w, mod_b)

    rw_pad = jnp.pad(router_w, ((0, 0), (0, 128 - N_EXP)))
    rw_hi = rw_pad.astype(BF16)
    rw2 = jnp.stack([rw_hi, (rw_pad - rw_hi.astype(F32)).astype(BF16)])
    rb_col = jnp.pad(router_b, (0, 128 - N_EXP)).reshape(128, 1)

    conv_w = jnp.pad(ev_conv_w[0], ((0, 8 - ev_conv_w.shape[1]), (0, 0)))
    x, wg, wu, wd = _even_layer(xp, xs, mod[0], norm_mix_g[0:1], ev_w_in[0].astype(BF16),
                                ev_pool_w[0].astype(BF16), ev_pool_scale[0:1], conv_w, ev_w_out[0].astype(BF16),
                                moe_w_gate, moe_w_up, moe_w_down)
    fg = final_norm_g.reshape(1, D)
    routed = _router(x, mod[0], norm_ffn_g[0:1], rw2, rb_col)
    x = _moe_sparse(x, routed, mod[0], wg, wu, wd, fg, layer=0, final=False)

    cos, sin = _rope_tables()
    w_in = od_w_in[0].astype(BF16)
    g1 = norm_mix_g[1:2]
    qna_p, kna_p, vna_p, qsw_p, ksw_p, vsw_p, nak, nav, swk, swv = _odd_in(x, mod[1], g1, w_in, cos, sin, prompt=True)
    qna_s, kna_s, vna_s, qsw_s, ksw_s, vsw_s = _odd_in(x, mod[1], g1, w_in, cos, sin, prompt=False)
    sink = od_sink[0]
    ona_p, osw_p = _ctx_attn(sink, qna_p, kna_p, vna_p, qsw_p, ksw_p, vsw_p)
    bias = _na_bias_blocks(_na_bias(od_rpb[0]))
    ona_s = _na_attn(qna_s, kna_s, vna_s,
                     cache_na_k[:, 0].reshape(DEC_BATCH, PAST, NA_W).astype(BF16),
                     cache_na_v[:, 0].reshape(DEC_BATCH, PAST, NA_W).astype(BF16), bias)
    osw_s = _sw_attn(sink, qsw_s, ksw_s, vsw_s,
                     cache_sw_k[:, 0].reshape(DEC_BATCH, PAST, SWKV_W).astype(BF16),
                     cache_sw_v[:, 0].reshape(DEC_BATCH, PAST, SWKV_W).astype(BF16))
    x, *routed = _router(x, mod[1], norm_ffn_g[1:2], rw2, rb_col,
                         attn=(ona_p, ona_s, osw_p, osw_s, od_w_out[0].astype(BF16)))
    y_prompt, y_sample = _moe_sparse(x, routed, mod[1], wg, wu, wd, fg, layer=1, final=True)
    y_prompt = y_prompt.reshape(BATCH, SEQ, D)
    y_sample = y_sample.reshape(DEC_BATCH, DEC_SEQ, D)
    new_na_k = nak.reshape(BATCH, 1, SEQ, NA_H, DH)
    new_na_v = nav.reshape(BATCH, 1, SEQ, NA_H, DH)
    new_sw_k = swk.reshape(BATCH, 1, SEQ, SW_KV, DH)
    new_sw_v = swv.reshape(BATCH, 1, SEQ, SW_KV, DH)
    return (y_prompt, y_sample, new_na_k, new_na_v, new_sw_k, new_sw_v)
```

```python
import functools

import jax
import jax.numpy as jnp
import numpy as np
from jax import lax
from jax.experimental import pallas as pl
from jax.experimental.pallas import tpu as pltpu

D = 1024
BATCH = 16
SEQ = 256
DEC_BATCH = 4
DEC_SEQ = 4096
PAST = 512
GRID_W = 64
ROWS = DEC_SEQ // GRID_W
DH = 64
POOL_W = 512
POOL_WINDOWS = (2, 4, 8, 16)
POOL_GW = 128
CONV_W = 512
EVEN_IN = POOL_W + 3 * CONV_W
NA_H = 8
NA_ROWS = 8
NA_COLS = 16
SW_H = 8
SW_KV = 2
SW_G = SW_H // SW_KV
SW_WIN = 128
ABLK = 128
NA_W = NA_H * DH
SWQ_W = SW_H * DH
SWKV_W = SW_KV * DH
ODD_IN = 3 * NA_W + SWQ_W + 2 * SWKV_W
N_EXP = 16
N_GRP = 4
EPG = 4
D_EXP = 512
EPS = 1e-6
NEG = -1e30
ROPE_BASE = 10000.0
QK_SCALE = DH ** -0.5
assert QK_SCALE == 0.125

N_P = BATCH * SEQ
N_S = DEC_BATCH * DEC_SEQ
N_TOK = N_P + N_S
MOD_ROWS = 8

TM = 256
NPT = N_P // TM
TPS = DEC_SEQ // TM
NT = N_TOK // TM
HALO = 8
CAST_STEPS = 64

F32 = jnp.float32
BF16 = jnp.bfloat16
VMEM_LIMIT = 56 * 1024 * 1024


def _cparams(*sem):
    return pltpu.CompilerParams(dimension_semantics=sem, vmem_limit_bytes=VMEM_LIMIT)


def _mod_row(i):
    return jnp.where(i < NPT, 0, 1 + (i - NPT) // TPS)


def _rms_mod(x, g, shift, scale):
    ms = jnp.mean(x * x, axis=-1, keepdims=True)
    y = x * lax.rsqrt(ms + EPS) * g
    return y * (1.0 + scale) + shift


def _dot(a, b):
    return jnp.dot(a, b, preferred_element_type=F32)


def _dot_nt(a, b):
    return lax.dot_general(a, b, (((1,), (1,)), ((), ())), preferred_element_type=F32)


def _mod_kernel(cv_ref, w_ref, b_ref, o_ref):
    cv = cv_ref[...]
    a = cv / (1.0 + jnp.exp(-cv))
    o_ref[...] = jnp.dot(a, w_ref[...], preferred_element_type=F32,
                         precision=lax.Precision.HIGHEST) + b_ref[...]


def _modulation(cvec, mod_w, mod_b):
    depth = mod_w.shape[0]
    return pl.pallas_call(
        _mod_kernel,
        grid=(depth, 6),
        in_specs=[
            pl.BlockSpec((MOD_ROWS, D), lambda l, j: (0, 0)),
            pl.BlockSpec((None, D, D), lambda l, j: (l, 0, j)),
            pl.BlockSpec((None, None, 1, D), lambda l, j: (l, j, 0, 0)),
        ],
        out_specs=pl.BlockSpec((None, None, MOD_ROWS, D), lambda l, j: (l, j, 0, 0)),
        out_shape=jax.ShapeDtypeStruct((depth, 6, MOD_ROWS, D), F32),
        compiler_params=_cparams("arbitrary", "arbitrary"),
        name="modulation",
    )(cvec, mod_w, mod_b.reshape(depth, 6, 1, D))


def _dual_specs(tm, width, npt):
    return [
        pl.BlockSpec((tm, width), lambda i: (jnp.minimum(i, npt - 1), 0)),
        pl.BlockSpec((tm, width), lambda i: (jnp.maximum(i - npt, 0), 0)),
    ]


def _even_kernel(xp_ref, xs_ref, xprev_ref, xnext_ref, mod_ref, g_ref, wi_ref, pw_ref, ps_ref, cw_ref,
                 wo_ref, wgf_ref, wuf_ref, wdf_ref, o_ref, wgb_ref, wub_ref, wdb_ref, pext, uext):
    i = pl.program_id(0)

    @pl.when(i < CAST_STEPS)
    def _():
        wgb_ref[...] = wgf_ref[...].astype(BF16)
        wub_ref[...] = wuf_ref[...].astype(BF16)
        wdb_ref[...] = wdf_ref[...].astype(BF16)

    r = _mod_row(i)
    is_p = i < NPT
    t0 = jnp.where(is_p, 0, ((i - NPT) % TPS) * TM)
    seq_len = jnp.where(is_p, SEQ, DEC_SEQ)
    first = t0 == 0
    last = t0 + TM == seq_len

    x = jnp.where(is_p, xp_ref[...], xs_ref[...])
    xe = jnp.concatenate([xprev_ref[...], x, xnext_ref[...]], axis=0)
    h = _rms_mod(xe, g_ref[...], mod_ref[0, pl.ds(r, 1), :], mod_ref[1, pl.ds(r, 1), :])
    ze = _dot(h.astype(BF16), wi_ref[...])
    row = lax.broadcasted_iota(jnp.int32, (TM + 2 * HALO, 1), 0)
    outside = jnp.logical_or(jnp.logical_and(first, row < HALO), jnp.logical_and(last, row >= HALO + TM))
    ze = jnp.where(outside, 0.0, ze)
    pext[...] = ze[:, :POOL_W]
    uext[...] = ze[:, POOL_W + 2 * CONV_W:] * ze[:, POOL_W:POOL_W + CONV_W]
    p = ze[HALO:HALO + TM, :POOL_W]
    gb = ze[HALO:HALO + TM, POOL_W + CONV_W:POOL_W + 2 * CONV_W]

    t = t0 + lax.broadcasted_iota(jnp.int32, (TM, 1), 0)
    mixed = []
    for g, w in enumerate(POOL_WINDOWS):
        cols = slice(g * POOL_GW, (g + 1) * POOL_GW)
        acc = jnp.zeros((TM, POOL_GW), F32)
        for k in range(-(w // 2), w - (w // 2)):
            acc = acc + pext[pl.ds(HALO + k, TM), cols]
        lo = jnp.maximum(t - w // 2, 0)
        hi = jnp.minimum(t + (w - 1 - w // 2), seq_len - 1)
        cnt = (hi - lo + 1).astype(F32)
        d = acc / cnt - p[:, cols]
        mixed.append((_dot(d.astype(BF16), pw_ref[g]) * ps_ref[:, cols]).astype(BF16))
    conv = (uext[pl.ds(HALO - 1, TM), :] * cw_ref[0:1, :] + uext[pl.ds(HALO, TM), :] * cw_ref[1:2, :]
            + uext[pl.ds(HALO + 1, TM), :] * cw_ref[2:3, :])
    mixed.append((gb * conv).astype(BF16))
    y = _dot(jnp.concatenate(mixed, axis=1), wo_ref[...])
    o_ref[...] = x + mod_ref[2, pl.ds(r, 1), :] * y


def _even_layer(xp, xs, mod_l, g, w_in, pool_w, pool_scale, conv_w, w_out, wg, wu, wd):
    hb = TM // HALO
    nhb = N_S // HALO
    wg2, wu2, wd2 = wg.reshape(-1, D_EXP), wu.reshape(-1, D_EXP), wd.reshape(-1, D)
    rows_in, rows_out = wg2.shape[0] // CAST_STEPS, wd2.shape[0] // CAST_STEPS
    assert rows_in * CAST_STEPS == wg2.shape[0] and rows_out * CAST_STEPS == wd2.shape[0] and CAST_STEPS <= NT
    cast_blk = lambda i: (jnp.minimum(i, CAST_STEPS - 1), 0)
    cast_specs = [pl.BlockSpec((rows_in, D_EXP), cast_blk), pl.BlockSpec((rows_in, D_EXP), cast_blk),
                  pl.BlockSpec((rows_out, D), cast_blk)]
    outs = pl.pallas_call(
        _even_kernel,
        grid=(NT,),
        in_specs=_dual_specs(TM, D, NPT) + [
            pl.BlockSpec((HALO, D), lambda i: (jnp.maximum((i - NPT) * hb - 1, 0), 0)),
            pl.BlockSpec((HALO, D), lambda i: (jnp.clip((i - NPT + 1) * hb, 0, nhb - 1), 0)),
            pl.BlockSpec((6, MOD_ROWS, D), lambda i: (0, 0, 0)),
            pl.BlockSpec((1, D), lambda i: (0, 0)),
            pl.BlockSpec((D, EVEN_IN), lambda i: (0, 0)),
            pl.BlockSpec((4, POOL_GW, POOL_GW), lambda i: (0, 0, 0)),
            pl.BlockSpec((1, POOL_W), lambda i: (0, 0)),
            pl.BlockSpec((8, CONV_W), lambda i: (0, 0)),
            pl.BlockSpec((D, D), lambda i: (0, 0)),
        ] + cast_specs,
        out_specs=[pl.BlockSpec((TM, D), lambda i: (i, 0))] + cast_specs,
        out_shape=[jax.ShapeDtypeStruct((N_TOK, D), F32), jax.ShapeDtypeStruct(wg2.shape, BF16),
                   jax.ShapeDtypeStruct(wu2.shape, BF16), jax.ShapeDtypeStruct(wd2.shape, BF16)],
        scratch_shapes=[pltpu.VMEM((TM + 2 * HALO, POOL_W), F32),
                        pltpu.VMEM((TM + 2 * HALO, CONV_W), F32)],
        compiler_params=_cparams("arbitrary"),
        name="even_layer",
    )(xp, xs, xs, xs, mod_l, g, w_in, pool_w, pool_scale, conv_w, w_out, wg2, wu2, wd2)
    return outs[0], outs[1].reshape(wg.shape), outs[2].reshape(wu.shape), outs[3].reshape(wd.shape)


def _rope(x, cos, sin_signed):
    n = x.shape[1] // 128
    cosf = jnp.concatenate([cos] * n, axis=1) if n > 1 else cos
    sinf = jnp.concatenate([sin_signed] * n, axis=1) if n > 1 else sin_signed
    w = x.shape[1]
    lane = lax.broadcasted_iota(jnp.int32, x.shape, 1)
    up = pltpu.roll(x, w - DH // 4, 1)
    dn = pltpu.roll(x, DH // 4, 1)
    rot = jnp.where((lane % (DH // 2)) < DH // 4, up, dn)
    return x * cosf + rot * sinf


def _odd_in_kernel(x_ref, mod_ref, g_ref, w_ref, cos_ref, sin_ref, *out_refs, tm, tile0, rope, kv_f32):
    r = _mod_row((pl.program_id(0) + tile0) * (tm // TM))
    h = _rms_mod(x_ref[...], g_ref[...], mod_ref[0, pl.ds(r, 1), :], mod_ref[1, pl.ds(r, 1), :])
    z = _dot(h.astype(BF16), w_ref[...])
    qna_ref, kna_ref, vna_ref, qsw_ref, ksw_ref, vsw_ref = out_refs[:6]
    c0 = 0
    qna = z[:, 0:NA_W]
    kna = z[:, NA_W:2 * NA_W]
    vna = z[:, 2 * NA_W:3 * NA_W]
    c0 = 3 * NA_W
    qsw = z[:, c0:c0 + SWQ_W]
    ksw = z[:, c0 + SWQ_W:c0 + SWQ_W + SWKV_W]
    vsw = z[:, c0 + SWQ_W + SWKV_W:]
    if kv_f32:
        for ref, val in zip(out_refs[6:10], (kna, vna, ksw, vsw)):
            heads = val.shape[1] // DH
            for hh in range(heads):
                ref[pl.ds(hh, SEQ, stride=heads), :] = val[:, hh * DH:(hh + 1) * DH]
    if rope:
        cos = cos_ref[...]
        sin = sin_ref[...]
        qsw = _rope(qsw, cos, sin)
        ksw = _rope(ksw, cos, sin)
    qna_ref[...] = (qna * QK_SCALE).astype(BF16)
    kna_ref[...] = kna.astype(BF16)
    vna_ref[...] = vna.astype(BF16)
    qsw_ref[...] = (qsw * QK_SCALE).astype(BF16)
    ksw_ref[...] = ksw.astype(BF16)
    vsw_ref[...] = vsw.astype(BF16)


def _odd_in(x, mod_l, g, w, cos, sin, *, prompt):
    tm = TM if prompt else 2 * TM
    tile0 = 0 if prompt else N_P // tm
    nt = (N_P if prompt else N_S) // tm
    n = nt * tm
    widths = [NA_W, NA_W, NA_W, SWQ_W, SWKV_W, SWKV_W]
    out_shape = [jax.ShapeDtypeStruct((n, w_), BF16) for w_ in widths]
    out_specs = [pl.BlockSpec((tm, w_), lambda i: (i, 0)) for w_ in widths]
    if prompt:
        assert tm == SEQ
        for heads in (NA_H, NA_H, SW_KV, SW_KV):
            out_shape.append(jax.ShapeDtypeStruct((BATCH * SEQ * heads, DH), F32))
            out_specs.append(pl.BlockSpec((SEQ * heads, DH), lambda i: (i, 0)))
    return pl.pallas_call(
        functools.partial(_odd_in_kernel, tm=tm, tile0=tile0, rope=not prompt, kv_f32=prompt),
        grid=(nt,),
        in_specs=[
            pl.BlockSpec((tm, D), lambda i: (i + tile0, 0)),
            pl.BlockSpec((6, MOD_ROWS, D), lambda i: (0, 0, 0)),
            pl.BlockSpec((1, D), lambda i: (0, 0)),
            pl.BlockSpec((D, ODD_IN), lambda i: (0, 0)),
            pl.BlockSpec((tm, 128), lambda i: (i % (DEC_SEQ // tm), 0)),
            pl.BlockSpec((tm, 128), lambda i: (i % (DEC_SEQ // tm), 0)),
        ],
        out_specs=out_specs,
        out_shape=out_shape,
        compiler_params=_cparams("parallel"),
        name="odd_in_prompt" if prompt else "odd_in_latent",
    )(x, mod_l, g, w, cos, sin)


def _rope_tables():
    t = np.arange(DEC_SEQ)
    quarter = DH // 4
    inv = 1.0 / (ROPE_BASE ** (np.arange(quarter, dtype=np.float64) / quarter))

    def cos_sin(pos):
        ang = pos.astype(np.float64)[:, None] * inv[None, :]
        ang = np.concatenate([ang, ang], axis=-1)
        return np.cos(ang), np.sin(ang)

    cr, sr = cos_sin(t // GRID_W)
    cc, sc = cos_sin(t % GRID_W)
    cos = np.concatenate([cr, cc], axis=-1)
    sin = np.concatenate([sr, sc], axis=-1)
    sign = np.where((np.arange(DH) % (DH // 2)) < DH // 4, -1.0, 1.0)
    sin = sin * sign[None, :]
    cos = np.concatenate([cos, cos], axis=-1).astype(np.float32)
    sin = np.concatenate([sin, sin], axis=-1).astype(np.float32)
    return jnp.asarray(cos), jnp.asarray(sin)


def _softmax_pv(segs, sink=None):
    m = None
    for s, _ in segs:
        sm = jnp.max(s, axis=-1, keepdims=True)
        m = sm if m is None else jnp.maximum(m, sm)
    if sink is not None:
        m = jnp.maximum(m, sink)
    den = None
    acc = None
    for s, v in segs:
        p = jnp.exp(s - m)
        ps = jnp.sum(p, axis=-1, keepdims=True)
        den = ps if den is None else den + ps
        pv = _dot(p.astype(BF16), v)
        acc = pv if acc is None else acc + pv
    if sink is not None:
        den = den + jnp.exp(sink - m)
    return acc / den


def _sink_col(sink_ref, g, rows_per_head):
    row = lax.broadcasted_iota(jnp.int32, (SW_G * rows_per_head, 1), 0)
    col = jnp.zeros((SW_G * rows_per_head, 1), F32)
    for r in range(SW_G):
        col = jnp.where(row // rows_per_head == r, sink_ref[g * SW_G + r], col)
    return col


def _ctx_attn_kernel(sink_ref, qna_ref, kna_ref, vna_ref, qsw_ref, ksw_ref, vsw_ref, ona_ref, osw_ref):
    lane = lax.broadcasted_iota(jnp.int32, (SEQ, 2 * DH), 1)
    for p in range(NA_H // 2):
        cols = slice(p * 2 * DH, (p + 1) * 2 * DH)
        q = qna_ref[:, cols]
        k = kna_ref[:, cols]
        v = vna_ref[:, cols]
        outs = []
        for half in range(2):
            mine = (lane < DH) if half == 0 else (lane >= DH)
            qm = jnp.where(mine, q, jnp.zeros_like(q))
            outs.append(_softmax_pv([(_dot_nt(qm, k), v)]))
        ona_ref[:, cols] = jnp.where(lane < DH, outs[0], outs[1]).astype(BF16)
    outs = []
    for g in range(SW_KV):
        kc = slice(g * DH, (g + 1) * DH)
        q = jnp.concatenate([qsw_ref[:, (g * SW_G + r) * DH:(g * SW_G + r + 1) * DH] for r in range(SW_G)], axis=0)
        s = _dot_nt(q, ksw_ref[:, kc])
        o = _softmax_pv([(s, vsw_ref[:, kc])], sink=_sink_col(sink_ref, g, SEQ))
        outs.extend(o[r * SEQ:(r + 1) * SEQ, :] for r in range(SW_G))
    osw_ref[...] = jnp.concatenate(outs, axis=1).astype(BF16)


def _ctx_attn(sink, qna, kna, vna, qsw, ksw, vsw):
    def spec(w):
        return pl.BlockSpec((SEQ, w), lambda b: (b, 0))

    return pl.pallas_call(
        _ctx_attn_kernel,
        grid=(BATCH,),
        in_specs=[pl.BlockSpec(memory_space=pltpu.SMEM), spec(NA_W), spec(NA_W), spec(NA_W),
                  spec(SWQ_W), spec(SWKV_W), spec(SWKV_W)],
        out_specs=[spec(NA_W), spec(SWQ_W)],
        out_shape=[jax.ShapeDtypeStruct((N_P, NA_W), BF16), jax.ShapeDtypeStruct((N_P, SWQ_W), BF16)],
        compiler_params=_cparams("parallel"),
        name="ctx_attn",
    )(sink, qna, kna, vna, qsw, ksw, vsw)


def _na_bias_kernel(rpb_ref, o_ref):
    h = pl.program_id(0)
    cq = lax.broadcasted_iota(jnp.int32, (GRID_W, GRID_W), 0)
    ck = lax.broadcasted_iota(jnp.int32, (GRID_W, GRID_W), 1)
    cstart = jnp.clip(cq - NA_COLS // 2, 0, GRID_W - NA_COLS)
    ok = (ck >= cstart) & (ck < cstart + NA_COLS)
    dc = jnp.clip(ck - cq + NA_COLS - 1, 0, 2 * NA_COLS - 2)
    ndc = 2 * NA_COLS - 1
    for d in range(2 * NA_ROWS - 1):
        b = jnp.zeros((GRID_W, GRID_W), F32)
        for e in range(ndc):
            b = jnp.where(dc == e, rpb_ref[h, d * ndc + e], b)
        o_ref[d] = jnp.where(ok, b, NEG)


def _na_bias(rpb):
    nd = 2 * NA_ROWS - 1
    return pl.pallas_call(
        _na_bias_kernel,
        grid=(NA_H,),
        in_specs=[pl.BlockSpec(memory_space=pltpu.SMEM)],
        out_specs=pl.BlockSpec((None, nd, GRID_W, GRID_W), lambda h: (h, 0, 0, 0)),
        out_shape=jax.ShapeDtypeStruct((NA_H, nd, GRID_W, GRID_W), F32),
        compiler_params=_cparams("parallel"),
        name="na_bias",
    )(rpb.reshape(NA_H, nd * (2 * NA_COLS - 1)))


NA_QB = 4
NA_KR = 12
NA_NQB = ROWS // NA_QB


def _na_bias_blocks(bias):
    neg = jnp.full((NA_H, GRID_W, GRID_W), NEG, F32)
    cases = []
    for case in range(3):
        rows = []
        for i in range(NA_QB):
            blocks = []
            for j in range(NA_KR):
                if case == 0:
                    valid, dr = j < NA_ROWS, j - i + NA_ROWS - 1
                elif case == 1:
                    valid, dr = i <= j < i + NA_ROWS, j - i + NA_ROWS // 2 - 1
                else:
                    valid, dr = j >= NA_KR - NA_ROWS, j - i + NA_ROWS - 1 - (NA_KR - NA_QB)
                blocks.append(bias[:, dr] if valid else neg)
            rows.append(jnp.concatenate(blocks, axis=2))
        cases.append(jnp.concatenate(rows, axis=1))
    return jnp.stack(cases)


def _na_kernel(q_ref, k_ref, v_ref, kc_ref, vc_ref, bias_ref, o_ref):
    r0 = pl.program_id(1) * NA_QB
    start = pl.multiple_of(jnp.clip(r0 - NA_ROWS // 2, 0, ROWS - NA_KR) * GRID_W, GRID_W)
    nq = NA_QB * GRID_W
    lane = lax.broadcasted_iota(jnp.int32, (nq, 2 * DH), 1)
    for p in range(NA_H // 2):
        cols = slice(p * 2 * DH, (p + 1) * 2 * DH)
        q = q_ref[:, cols]
        kl = k_ref[pl.ds(start, NA_KR * GRID_W), cols]
        vl = v_ref[pl.ds(start, NA_KR * GRID_W), cols]
        kc = kc_ref[:, cols]
        vc = vc_ref[:, cols]
        zero = jnp.zeros_like(q)
        qm = jnp.concatenate([jnp.where(lane < DH, q, zero), jnp.where(lane < DH, zero, q)], axis=0)
        s_loc = _dot_nt(qm, kl)
        s_ctx = _dot_nt(qm, kc)
        outs = []
        for half in range(2):
            rows = slice(half * nq, (half + 1) * nq)
            outs.append(_softmax_pv([(s_loc[rows] + bias_ref[2 * p + half], vl), (s_ctx[rows], vc)]))
        o_ref[:, cols] = jnp.where(lane < DH, outs[0], outs[1]).astype(BF16)


def _na_attn(q, k, v, kc, vc, bias):
    nq = NA_QB * GRID_W

    def bias_case(b, rb):
        return (jnp.where(rb == 0, 0, jnp.where(rb == NA_NQB - 1, 2, 1)), 0, 0, 0)

    return pl.pallas_call(
        _na_kernel,
        grid=(DEC_BATCH, NA_NQB),
        in_specs=[
            pl.BlockSpec((nq, NA_W), lambda b, rb: (b * NA_NQB + rb, 0)),
            pl.BlockSpec((DEC_SEQ, NA_W), lambda b, rb: (b, 0)),
            pl.BlockSpec((DEC_SEQ, NA_W), lambda b, rb: (b, 0)),
            pl.BlockSpec((None, PAST, NA_W), lambda b, rb: (b, 0, 0)),
            pl.BlockSpec((None, PAST, NA_W), lambda b, rb: (b, 0, 0)),
            pl.BlockSpec((None, NA_H, nq, NA_KR * GRID_W), bias_case),
        ],
        out_specs=pl.BlockSpec((nq, NA_W), lambda b, rb: (b * NA_NQB + rb, 0)),
        out_shape=jax.ShapeDtypeStruct((N_S, NA_W), BF16),
        compiler_params=_cparams("parallel", "arbitrary"),
        name="na_attn",
    )(q, k, v, kc, vc, bias)


def _sw_window_bias():
    q = np.arange(SW_G * ABLK)[:, None] % ABLK
    k = np.arange(3 * ABLK)[None, :]
    tables = [np.where(np.abs(q + lead - k) <= SW_WIN, 0.0, NEG) for lead in (0, ABLK, 2 * ABLK)]
    return jnp.asarray(np.stack(tables).astype(np.float32))


def _sw_kernel(sink_ref, q_ref, k_ref, v_ref, kc_ref, vc_ref, wb_ref, o_ref):
    j = pl.program_id(1)
    nk = 3 * ABLK
    start = pl.multiple_of(jnp.clip((j - 1) * ABLK, 0, DEC_SEQ - nk), ABLK)
    outs = []
    for g in range(SW_KV):
        kcols = slice(g * DH, (g + 1) * DH)
        q = jnp.concatenate([q_ref[:, (g * SW_G + r) * DH:(g * SW_G + r + 1) * DH] for r in range(SW_G)], axis=0)
        kw = k_ref[pl.ds(start, nk), kcols]
        vw = v_ref[pl.ds(start, nk), kcols]
        s_w = _dot_nt(q, kw) + wb_ref[...]
        s_c = _dot_nt(q, kc_ref[:, kcols])
        o = _softmax_pv([(s_w, vw), (s_c, vc_ref[:, kcols])], sink=_sink_col(sink_ref, g, ABLK))
        outs.extend(o[r * ABLK:(r + 1) * ABLK, :] for r in range(SW_G))
    o_ref[...] = jnp.concatenate(outs, axis=1).astype(BF16)


def _sw_attn(sink, q, k, v, kc, vc):
    nb = DEC_SEQ // ABLK
    return pl.pallas_call(
        _sw_kernel,
        grid=(DEC_BATCH, nb),
        in_specs=[
            pl.BlockSpec(memory_space=pltpu.SMEM),
            pl.BlockSpec((ABLK, SWQ_W), lambda b, j: (b * nb + j, 0)),
            pl.BlockSpec((DEC_SEQ, SWKV_W), lambda b, j: (b, 0)),
            pl.BlockSpec((DEC_SEQ, SWKV_W), lambda b, j: (b, 0)),
            pl.BlockSpec((None, PAST, SWKV_W), lambda b, j: (b, 0, 0)),
            pl.BlockSpec((None, PAST, SWKV_W), lambda b, j: (b, 0, 0)),
            pl.BlockSpec((None, SW_G * ABLK, 3 * ABLK),
                         lambda b, j: (jnp.where(j == 0, 0, jnp.where(j == nb - 1, 2, 1)), 0, 0)),
        ],
        out_specs=pl.BlockSpec((ABLK, SWQ_W), lambda b, j: (b * nb + j, 0)),
        out_shape=jax.ShapeDtypeStruct((N_S, SWQ_W), BF16),
        compiler_params=_cparams("parallel", "arbitrary"),
        name="sw_attn",
    )(sink, q, k, v, kc, vc, _sw_window_bias())


BT = 4096
NB = N_TOK // BT
TPB = BT // TM
CH = 256
CH_TAIL = 128
XPS = 2
NES = N_EXP // XPS
XO_ROWS = 2 * BT + CH
ROW = 8
assert D == ROW * 128


def _route_sparse(probs, sel):
    rank = []
    score = []
    for g in range(N_GRP):
        ids = range(g * EPG, (g + 1) * EPG)
        tot = None
        for e in ids:
            rk = jnp.zeros_like(sel[e])
            for j in ids:
                if j < e:
                    rk = rk + (sel[j] >= sel[e]).astype(F32)
                elif j > e:
                    rk = rk + (sel[j] > sel[e]).astype(F32)
            rank.append(rk)
            contrib = jnp.where(rk < 2.0, sel[e], 0.0)
            tot = contrib if tot is None else tot + contrib
        score.append(tot)
    top1, top2 = [], []
    for g in range(N_GRP):
        best = None
        for j in range(N_GRP):
            if j == g:
                continue
            c = (score[g] > score[j]) if j < g else (score[g] >= score[j])
            best = c if best is None else jnp.logical_and(best, c)
        for e in range(g * EPG, (g + 1) * EPG):
            top1.append(jnp.logical_and(best, rank[e] == 0.0))
            top2.append(jnp.logical_and(best, rank[e] == 1.0))
    return top1, top2


def _pick(masks, rows):
    acc = None
    for m, r in zip(masks, rows):
        v = jnp.where(m, r, 0.0)
        acc = v if acc is None else acc + v
    return acc


def _route_tile(xv, mod_ref, g_ref, rw_ref, rb_ref, h_ref, pos_ref, w_ref, cnt_ref, off_ref, meta, carry):
    i = pl.program_id(0)
    j = i % TPB
    r = _mod_row(i)
    h = _rms_mod(xv, g_ref[...], mod_ref[3, pl.ds(r, 1), :], mod_ref[4, pl.ds(r, 1), :])
    h_ref[...] = h.astype(BF16)
    h_hi = h.astype(BF16)
    h_lo = (h - h_hi.astype(F32)).astype(BF16)
    logits = _dot(h_lo, rw_ref[0]) + _dot(h_hi, rw_ref[1]) + _dot(h_hi, rw_ref[0])
    lt = logits.T[:N_EXP, :]
    m = jnp.max(lt, axis=0, keepdims=True)
    ex = jnp.exp(lt - m)
    pr = ex / jnp.sum(ex, axis=0, keepdims=True)
    se = pr + rb_ref[:N_EXP, :]
    probs = [pr[e:e + 1, :] for e in range(N_EXP)]
    sel = [se[e:e + 1, :] for e in range(N_EXP)]
    top1, top2 = _route_sparse(probs, sel)

    @pl.when(j == 0)
    def _():
        carry[...] = jnp.zeros_like(carry)

    member = jnp.concatenate([jnp.logical_or(a, b).astype(F32) for a, b in zip(top1, top2)], axis=0)
    s_idx = lax.broadcasted_iota(jnp.int32, (TM, TM), 0)
    t_idx = lax.broadcasted_iota(jnp.int32, (TM, TM), 1)
    before = jnp.where(s_idx < t_idx, 1.0, 0.0).astype(BF16)
    seen = _dot(member.astype(BF16), before) + carry[:, 0:1]
    seen_rows = [seen[e:e + 1, :] for e in range(N_EXP)]
    ids = [jnp.full((1, TM), float(e), F32) for e in range(N_EXP)]
    p1 = _pick(top1, probs)
    p2 = _pick(top2, probs)
    den = p1 + p2
    w_ref[j, 0:1, :] = p1 / den
    w_ref[j, 1:2, :] = p2 / den
    meta[j, 0:1, :] = _pick(top1, ids)
    meta[j, 1:2, :] = _pick(top2, ids)
    meta[j, 2:3, :] = _pick(top1, seen_rows)
    meta[j, 3:4, :] = _pick(top2, seen_rows)
    carry[...] = carry[...] + jnp.sum(member, axis=1, keepdims=True)

    @pl.when(j == TPB - 1)
    def _():
        cnt = carry[...]
        offs = [jnp.zeros((1, 128), F32)]
        for e in range(1, N_EXP):
            offs.append(offs[-1] + cnt[e - 1:e, :])
        cnt_ref[...] = cnt
        off_ref[...] = jnp.concatenate(offs, axis=0)
        for jj in range(TPB):
            for k in range(2):
                eid = meta[jj, k:k + 1, :]
                pos = meta[jj, 2 + k:3 + k, :]
                for e in range(1, N_EXP):
                    pos = pos + jnp.where(eid == float(e), offs[e][:, 0:1], 0.0)
                pos_ref[jj, k:k + 1, :] = pos.astype(jnp.int32)


def _router_kernel(x_ref, *refs):
    _route_tile(x_ref[...], *refs)


def _odd_out_router_kernel(nap_ref, nas_ref, swp_ref, sws_ref, x_ref, wo_ref, mod_ref, g_ref, rw_ref, rb_ref,
                           x1_ref, *refs):
    i = pl.program_id(0)
    is_p = i < NPT
    ona = jnp.where(is_p, nap_ref[...], nas_ref[...])
    osw = jnp.where(is_p, swp_ref[...], sws_ref[...])
    y = _dot(ona, wo_ref[:NA_W, :]) + _dot(osw, wo_ref[NA_W:, :])
    x1 = x_ref[...] + mod_ref[2, pl.ds(_mod_row(i), 1), :] * y
    x1_ref[...] = x1
    _route_tile(x1, mod_ref, g_ref, rw_ref, rb_ref, *refs)


def _router(x, mod_l, g, rw2, rb_col, attn=None):
    blk = lambda i: (i // TPB, 0, 0, 0)
    in_specs = [
        pl.BlockSpec((TM, D), lambda i: (i, 0)),
        pl.BlockSpec((6, MOD_ROWS, D), lambda i: (0, 0, 0)),
        pl.BlockSpec((1, D), lambda i: (0, 0)),
        pl.BlockSpec((2, D, 128), lambda i: (0, 0, 0)),
        pl.BlockSpec((128, 1), lambda i: (0, 0)),
    ]
    out_specs = [
        pl.BlockSpec((TM, D), lambda i: (i, 0)),
        pl.BlockSpec((None, TPB, 2, TM), blk),
        pl.BlockSpec((None, TPB, 2, TM), blk),
        pl.BlockSpec((None, N_EXP, 128), lambda i: (i // TPB, 0, 0)),
        pl.BlockSpec((None, N_EXP, 128), lambda i: (i // TPB, 0, 0)),
    ]
    out_shape = [
        jax.ShapeDtypeStruct((N_TOK, D), BF16),
        jax.ShapeDtypeStruct((NB, TPB, 2, TM), jnp.int32),
        jax.ShapeDtypeStruct((NB, TPB, 2, TM), F32),
        jax.ShapeDtypeStruct((NB, N_EXP, 128), F32),
        jax.ShapeDtypeStruct((NB, N_EXP, 128), F32),
    ]
    args = (x, mod_l, g, rw2, rb_col)
    body = _router_kernel
    if attn is not None:
        nap, nas, swp, sws, w_out = attn
        in_specs = (_dual_specs(TM, NA_W, NPT) + _dual_specs(TM, SWQ_W, NPT) + in_specs[:1]
                    + [pl.BlockSpec((D, D), lambda i: (0, 0))] + in_specs[1:])
        out_specs = [pl.BlockSpec((TM, D), lambda i: (i, 0))] + out_specs
        out_shape = [jax.ShapeDtypeStruct((N_TOK, D), F32)] + out_shape
        args = (nap, nas, swp, sws, x, w_out, mod_l, g, rw2, rb_col)
        body = _odd_out_router_kernel
    return pl.pallas_call(
        body,
        grid=(NT,),
        in_specs=in_specs,
        out_specs=out_specs,
        out_shape=out_shape,
        scratch_shapes=[pltpu.VMEM((TPB, 4, TM), F32), pltpu.VMEM((N_EXP, 128), F32)],
        compiler_params=_cparams("arbitrary"),
        name="router" if attn is None else "odd_out_router",
    )(*args)


def _row(p):
    return pl.ds(pl.multiple_of(p * ROW, ROW), ROW)


def _expert_segment(n, base, xo, wg_ref, wu_ref, wd_ref):
    def ffn_rows(row0, rows):
        x = jnp.concatenate([xo[pl.ds(row0 * ROW + c, rows, stride=ROW), :] for c in range(ROW)], axis=1)
        xb = x.astype(BF16)
        a = _dot(xb, wg_ref[...])
        u = _dot(xb, wu_ref[...])
        hid = (a / (1.0 + jnp.exp(-a))) * u
        out = _dot(hid.astype(BF16), wd_ref[...])
        valid = lax.broadcasted_iota(jnp.int32, (rows, 1), 0) < base + n - row0
        res = jnp.where(valid, out, x)
        for c in range(ROW):
            xo[pl.ds(row0 * ROW + c, rows, stride=ROW), :] = res[:, c * 128:(c + 1) * 128]

    def chunk(jc, carry):
        ffn_rows(base + jc * CH, CH)
        return carry

    nfull = (n + CH - CH_TAIL - 1) // CH
    lax.fori_loop(0, nfull, chunk, 0)

    @pl.when(n > nfull * CH)
    def _():
        ffn_rows(base + nfull * CH, CH_TAIL)


def _experts_kernel(cnt_ref, off_ref, pos_ref, w_ref, h_ref, wg_ref, wu_ref, wd_ref, x_ref, mod_ref, fg_ref,
                    *rest, final):
    if final:
        op_ref, os_ref, xo, stg = rest
    else:
        o_ref, xo, stg = rest
    b = pl.program_id(0)
    s = pl.program_id(1)

    @pl.when(s == 0)
    def _():
        xo[pl.ds(2 * BT * ROW, CH * ROW), :] = jnp.zeros((CH * ROW, 128), F32)

    @pl.when(s < TPB)
    def _():
        hs = h_ref[...].astype(F32)
        for c in range(ROW):
            stg[pl.ds(c, TM, stride=ROW), :] = hs[:, c * 128:(c + 1) * 128]
        for t in range(TM):
            v = stg[t * ROW:(t + 1) * ROW, :]
            xo[_row(pos_ref[s, 0, t]), :] = v
            xo[_row(pos_ref[s, 1, t]), :] = v

    @pl.when(jnp.logical_and(s >= TPB, s < TPB + NES))
    def _():
        for k in range(XPS):
            e = (s - TPB) * XPS + k
            _expert_segment(cnt_ref[b, e], off_ref[b, e], xo, wg_ref.at[k], wu_ref.at[k], wd_ref.at[k])

    @pl.when(s >= TPB + NES - 1)
    def _():
        j = s - (TPB + NES - 1)
        tile = b * TPB + j
        g2 = mod_ref[5, pl.ds(_mod_row(tile), 1), :]
        for t in range(TM):
            a = xo[_row(pos_ref[j, 0, t]), :]
            u = xo[_row(pos_ref[j, 1, t]), :]
            stg[t * ROW:(t + 1) * ROW, :] = w_ref[j, 0, t] * a + w_ref[j, 1, t] * u
        y = jnp.concatenate([stg[pl.ds(c, TM, stride=ROW), :] for c in range(ROW)], axis=1)
        res = x_ref[...] + g2 * y
        if final:
            ms = jnp.mean(res * res, axis=-1, keepdims=True)
            res = res * lax.rsqrt(ms + EPS) * fg_ref[...]

            @pl.when(tile < NPT)
            def _():
                op_ref[...] = res

            @pl.when(tile >= NPT)
            def _():
                os_ref[...] = res
        else:
            o_ref[...] = res


def _experts(cnt, off, pos, wts, h, wg, wu, wd, x, mod_l, fg, *, layer, final):
    def tile_of(b, s):
        return b * TPB + jnp.clip(s - (TPB + NES - 1), 0, TPB - 1)

    def expert_of(b, s, *_):
        return (layer, jnp.clip(s - TPB, 0, NES - 1), 0, 0)

    smem_blk = pl.BlockSpec((None, TPB, 2, TM), lambda b, s, *_: (b, 0, 0, 0), memory_space=pltpu.SMEM)
    if final:
        out_specs = [pl.BlockSpec((TM, D), lambda b, s, *_: (jnp.minimum(tile_of(b, s), NPT - 1), 0)),
                     pl.BlockSpec((TM, D), lambda b, s, *_: (jnp.maximum(tile_of(b, s) - NPT, 0), 0))]
        out_shape = [jax.ShapeDtypeStruct((N_P, D), F32), jax.ShapeDtypeStruct((N_S, D), F32)]
    else:
        out_specs = pl.BlockSpec((TM, D), lambda b, s, *_: (tile_of(b, s), 0))
        out_shape = jax.ShapeDtypeStruct((N_TOK, D), F32)
    grid_spec = pltpu.PrefetchScalarGridSpec(
        num_scalar_prefetch=2,
        grid=(NB, 2 * TPB + NES - 1),
        in_specs=[
            smem_blk,
            smem_blk,
            pl.BlockSpec((TM, D), lambda b, s, *_: (b * TPB + jnp.minimum(s, TPB - 1), 0)),
            pl.BlockSpec((None, XPS, D, D_EXP), expert_of),
            pl.BlockSpec((None, XPS, D, D_EXP), expert_of),
            pl.BlockSpec((None, XPS, D_EXP, D), expert_of),
            pl.BlockSpec((TM, D), lambda b, s, *_: (tile_of(b, s), 0)),
            pl.BlockSpec((6, MOD_ROWS, D), lambda b, s, *_: (0, 0, 0)),
            pl.BlockSpec((1, D), lambda b, s, *_: (0, 0)),
        ],
        out_specs=out_specs,
        scratch_shapes=[pltpu.VMEM((XO_ROWS * ROW, 128), F32), pltpu.VMEM((TM * ROW, 128), F32)],
    )
    return pl.pallas_call(
        functools.partial(_experts_kernel, final=final),
        grid_spec=grid_spec,
        out_shape=out_shape,
        compiler_params=_cparams("arbitrary", "arbitrary"),
        name="experts_final" if final else "experts",
    )(cnt, off, pos, wts, h, wg, wu, wd, x, mod_l, fg)


def _moe_sparse(x, routed, mod_l, wg, wu, wd, fg, *, layer, final):
    h, pos, wts, cnt, off = routed
    cnt = cnt[:, :, 0].astype(jnp.int32)
    off = off[:, :, 0].astype(jnp.int32)
    return _experts(cnt, off, pos, wts, h, wg, wu, wd, x, mod_l, fg, layer=layer, final=final)


def kernel(x_prompt, x_sample, cache_na_k, cache_na_v, cache_sw_k, cache_sw_v, c, c_ctx, mod_w, mod_b, norm_mix_g, norm_ffn_g, ev_w_in, ev_pool_w, ev_pool_scale, ev_conv_w, ev_w_out, od_w_in, od_rpb, od_sink, od_w_out, router_w, router_b, moe_w_gate, moe_w_up, moe_w_down, final_norm_g):
    xp = x_prompt.reshape(N_P, D)
    xs = x_sample.reshape(N_S, D)
    cvec = jnp.concatenate([c_ctx[None, :], c, jnp.zeros((MOD_ROWS - 1 - DEC_BATCH, D), F32)], axis=0)
    mod = _modulation(cvec, mod_w, mod_b)

    rw_pad = jnp.pad(router_w, ((0, 0), (0, 128 - N_EXP)))
    rw_hi = rw_pad.astype(BF16)
    rw2 = jnp.stack([rw_hi, (rw_pad - rw_hi.astype(F32)).astype(BF16)])
    rb_col = jnp.pad(router_b, (0, 128 - N_EXP)).reshape(128, 1)

    conv_w = jnp.pad(ev_conv_w[0], ((0, 8 - ev_conv_w.shape[1]), (0, 0)))
    x, wg, wu, wd = _even_layer(xp, xs, mod[0], norm_mix_g[0:1], ev_w_in[0].astype(BF16),
                                ev_pool_w[0].astype(BF16), ev_pool_scale[0:1], conv_w, ev_w_out[0].astype(BF16),
                                moe_w_gate, moe_w_up, moe_w_down)
    fg = final_norm_g.reshape(1, D)
    routed = _router(x, mod[0], norm_ffn_g[0:1], rw2, rb_col)
    x = _moe_sparse(x, routed, mod[0], wg, wu, wd, fg, layer=0, final=False)

    cos, sin = _rope_tables()
    w_in = od_w_in[0].astype(BF16)
    g1 = norm_mix_g[1:2]
    qna_p, kna_p, vna_p, qsw_p, ksw_p, vsw_p, nak, nav, swk, swv = _odd_in(x, mod[1], g1, w_in, cos, sin, prompt=True)
    qna_s, kna_s, vna_s, qsw_s, ksw_s, vsw_s = _odd_in(x, mod[1], g1, w_in, cos, sin, prompt=False)
    sink = od_sink[0]
    ona_p, osw_p = _ctx_attn(sink, qna_p, kna_p, vna_p, qsw_p, ksw_p, vsw_p)
    bias = _na_bias_blocks(_na_bias(od_rpb[0]))
    ona_s = _na_attn(qna_s, kna_s, vna_s,
                     cache_na_k[:, 0].reshape(DEC_BATCH, PAST, NA_W).astype(BF16),
                     cache_na_v[:, 0].reshape(DEC_BATCH, PAST, NA_W).astype(BF16), bias)
    osw_s = _sw_attn(sink, qsw_s, ksw_s, vsw_s,
                     cache_sw_k[:, 0].reshape(DEC_BATCH, PAST, SWKV_W).astype(BF16),
                     cache_sw_v[:, 0].reshape(DEC_BATCH, PAST, SWKV_W).astype(BF16))
    x, *routed = _router(x, mod[1], norm_ffn_g[1:2], rw2, rb_col,
                         attn=(ona_p, ona_s, osw_p, osw_s, od_w_out[0].astype(BF16)))
    y_prompt, y_sample = _moe_sparse(x, routed, mod[1], wg, wu, wd, fg, layer=1, final=True)
    y_prompt = y_prompt.reshape(BATCH, SEQ, D)
    y_sample = y_sample.reshape(DEC_BATCH, DEC_SEQ, D)
    new_na_k = nak.reshape(BATCH, 1, SEQ, NA_H, DH)
    new_na_v = nav.reshape(BATCH, 1, SEQ, NA_H, DH)
    new_sw_k = swk.reshape(BATCH, 1, SEQ, SW_KV, DH)
    new_sw_v = swv.reshape(BATCH, 1, SEQ, SW_KV, DH)
    return (y_prompt, y_sample, new_na_k, new_na_v, new_sw_k, new_sw_v)
```

```python
import functools

import jax
import jax.numpy as jnp
import numpy as np
from jax import lax
from jax.experimental import pallas as pl
from jax.experimental.pallas import tpu as pltpu

D = 1024
BATCH = 16
SEQ = 256
DEC_BATCH = 4
DEC_SEQ = 4096
PAST = 512
GRID_W = 64
ROWS = DEC_SEQ // GRID_W
DH = 64
POOL_W = 512
POOL_WINDOWS = (2, 4, 8, 16)
POOL_GW = 128
CONV_W = 512
EVEN_IN = POOL_W + 3 * CONV_W
NA_H = 8
NA_ROWS = 8
NA_COLS = 16
SW_H = 8
SW_KV = 2
SW_G = SW_H // SW_KV
SW_WIN = 128
ABLK = 128
NA_W = NA_H * DH
SWQ_W = SW_H * DH
SWKV_W = SW_KV * DH
ODD_IN = 3 * NA_W + SWQ_W + 2 * SWKV_W
N_EXP = 16
N_GRP = 4
EPG = 4
D_EXP = 512
EPS = 1e-6
NEG = -1e30
ROPE_BASE = 10000.0
QK_SCALE = DH ** -0.5
assert QK_SCALE == 0.125

N_P = BATCH * SEQ
N_S = DEC_BATCH * DEC_SEQ
N_TOK = N_P + N_S
MOD_ROWS = 8

TM = 256
NPT = N_P // TM
TPS = DEC_SEQ // TM
NT = N_TOK // TM
HALO = 8
CAST_STEPS = 64

F32 = jnp.float32
BF16 = jnp.bfloat16
VMEM_LIMIT = 56 * 1024 * 1024


def _cparams(*sem):
    return pltpu.CompilerParams(dimension_semantics=sem, vmem_limit_bytes=VMEM_LIMIT)


def _mod_row(i):
    return jnp.where(i < NPT, 0, 1 + (i - NPT) // TPS)


def _rms_mod(x, g, shift, scale):
    ms = jnp.mean(x * x, axis=-1, keepdims=True)
    y = x * lax.rsqrt(ms + EPS) * g
    return y * (1.0 + scale) + shift


def _dot(a, b):
    return jnp.dot(a, b, preferred_element_type=F32)


def _dot_nt(a, b):
    return lax.dot_general(a, b, (((1,), (1,)), ((), ())), preferred_element_type=F32)


def _mod_kernel(cv_ref, w_ref, b_ref, o_ref):
    cv = cv_ref[...]
    a = cv / (1.0 + jnp.exp(-cv))
    o_ref[...] = jnp.dot(a, w_ref[...], preferred_element_type=F32,
                         precision=lax.Precision.HIGHEST) + b_ref[...]


def _modulation(cvec, mod_w, mod_b):
    depth = mod_w.shape[0]
    return pl.pallas_call(
        _mod_kernel,
        grid=(depth, 6),
        in_specs=[
            pl.BlockSpec((MOD_ROWS, D), lambda l, j: (0, 0)),
            pl.BlockSpec((None, D, D), lambda l, j: (l, 0, j)),
            pl.BlockSpec((None, None, 1, D), lambda l, j: (l, j, 0, 0)),
        ],
        out_specs=pl.BlockSpec((None, None, MOD_ROWS, D), lambda l, j: (l, j, 0, 0)),
        out_shape=jax.ShapeDtypeStruct((depth, 6, MOD_ROWS, D), F32),
        compiler_params=_cparams("arbitrary", "arbitrary"),
        name="modulation",
    )(cvec, mod_w, mod_b.reshape(depth, 6, 1, D))


def _dual_specs(tm, width, npt):
    return [
        pl.BlockSpec((tm, width), lambda i: (jnp.minimum(i, npt - 1), 0)),
        pl.BlockSpec((tm, width), lambda i: (jnp.maximum(i - npt, 0), 0)),
    ]


def _even_kernel(xp_ref, xs_ref, xprev_ref, xnext_ref, mod_ref, g_ref, wi_ref, pw_ref, ps_ref, cw_ref,
                 wo_ref, wgf_ref, wuf_ref, wdf_ref, o_ref, wgb_ref, wub_ref, wdb_ref, pext, uext):
    i = pl.program_id(0)

    @pl.when(i < CAST_STEPS)
    def _():
        wgb_ref[...] = wgf_ref[...].astype(BF16)
        wub_ref[...] = wuf_ref[...].astype(BF16)
        wdb_ref[...] = wdf_ref[...].astype(BF16)

    r = _mod_row(i)
    is_p = i < NPT
    t0 = jnp.where(is_p, 0, ((i - NPT) % TPS) * TM)
    seq_len = jnp.where(is_p, SEQ, DEC_SEQ)
    first = t0 == 0
    last = t0 + TM == seq_len

    x = jnp.where(is_p, xp_ref[...], xs_ref[...])
    xe = jnp.concatenate([xprev_ref[...], x, xnext_ref[...]], axis=0)
    h = _rms_mod(xe, g_ref[...], mod_ref[0, pl.ds(r, 1), :], mod_ref[1, pl.ds(r, 1), :])
    ze = _dot(h.astype(BF16), wi_ref[...])
    row = lax.broadcasted_iota(jnp.int32, (TM + 2 * HALO, 1), 0)
    outside = jnp.logical_or(jnp.logical_and(first, row < HALO), jnp.logical_and(last, row >= HALO + TM))
    ze = jnp.where(outside, 0.0, ze)
    pext[...] = ze[:, :POOL_W]
    uext[...] = ze[:, POOL_W + 2 * CONV_W:] * ze[:, POOL_W:POOL_W + CONV_W]
    p = ze[HALO:HALO + TM, :POOL_W]
    gb = ze[HALO:HALO + TM, POOL_W + CONV_W:POOL_W + 2 * CONV_W]

    t = t0 + lax.broadcasted_iota(jnp.int32, (TM, 1), 0)
    mixed = []
    for g, w in enumerate(POOL_WINDOWS):
        cols = slice(g * POOL_GW, (g + 1) * POOL_GW)
        acc = jnp.zeros((TM, POOL_GW), F32)
        for k in range(-(w // 2), w - (w // 2)):
            acc = acc + pext[pl.ds(HALO + k, TM), cols]
        lo = jnp.maximum(t - w // 2, 0)
        hi = jnp.minimum(t + (w - 1 - w // 2), seq_len - 1)
        cnt = (hi - lo + 1).astype(F32)
        d = acc / cnt - p[:, cols]
        mixed.append((_dot(d.astype(BF16), pw_ref[g]) * ps_ref[:, cols]).astype(BF16))
    conv = (uext[pl.ds(HALO - 1, TM), :] * cw_ref[0:1, :] + uext[pl.ds(HALO, TM), :] * cw_ref[1:2, :]
            + uext[pl.ds(HALO + 1, TM), :] * cw_ref[2:3, :])
    mixed.append((gb * conv).astype(BF16))
    y = _dot(jnp.concatenate(mixed, axis=1), wo_ref[...])
    o_ref[...] = x + mod_ref[2, pl.ds(r, 1), :] * y


def _even_layer(xp, xs, mod_l, g, w_in, pool_w, pool_scale, conv_w, w_out, wg, wu, wd):
    hb = TM // HALO
    nhb = N_S // HALO
    wg2, wu2, wd2 = wg.reshape(-1, D_EXP), wu.reshape(-1, D_EXP), wd.reshape(-1, D)
    rows_in, rows_out = wg2.shape[0] // CAST_STEPS, wd2.shape[0] // CAST_STEPS
    assert rows_in * CAST_STEPS == wg2.shape[0] and rows_out * CAST_STEPS == wd2.shape[0] and CAST_STEPS <= NT
    cast_blk = lambda i: (jnp.minimum(i, CAST_STEPS - 1), 0)
    cast_specs = [pl.BlockSpec((rows_in, D_EXP), cast_blk), pl.BlockSpec((rows_in, D_EXP), cast_blk),
                  pl.BlockSpec((rows_out, D), cast_blk)]
    outs = pl.pallas_call(
        _even_kernel,
        grid=(NT,),
        in_specs=_dual_specs(TM, D, NPT) + [
            pl.BlockSpec((HALO, D), lambda i: (jnp.maximum((i - NPT) * hb - 1, 0), 0)),
            pl.BlockSpec((HALO, D), lambda i: (jnp.clip((i - NPT + 1) * hb, 0, nhb - 1), 0)),
            pl.BlockSpec((6, MOD_ROWS, D), lambda i: (0, 0, 0)),
            pl.BlockSpec((1, D), lambda i: (0, 0)),
            pl.BlockSpec((D, EVEN_IN), lambda i: (0, 0)),
            pl.BlockSpec((4, POOL_GW, POOL_GW), lambda i: (0, 0, 0)),
            pl.BlockSpec((1, POOL_W), lambda i: (0, 0)),
            pl.BlockSpec((8, CONV_W), lambda i: (0, 0)),
            pl.BlockSpec((D, D), lambda i: (0, 0)),
        ] + cast_specs,
        out_specs=[pl.BlockSpec((TM, D), lambda i: (i, 0))] + cast_specs,
        out_shape=[jax.ShapeDtypeStruct((N_TOK, D), F32), jax.ShapeDtypeStruct(wg2.shape, BF16),
                   jax.ShapeDtypeStruct(wu2.shape, BF16), jax.ShapeDtypeStruct(wd2.shape, BF16)],
        scratch_shapes=[pltpu.VMEM((TM + 2 * HALO, POOL_W), F32),
                        pltpu.VMEM((TM + 2 * HALO, CONV_W), F32)],
        compiler_params=_cparams("arbitrary"),
        name="even_layer",
    )(xp, xs, xs, xs, mod_l, g, w_in, pool_w, pool_scale, conv_w, w_out, wg2, wu2, wd2)
    return outs[0], outs[1].reshape(wg.shape), outs[2].reshape(wu.shape), outs[3].reshape(wd.shape)


def _rope(x, cos, sin_signed):
    n = x.shape[1] // 128
    cosf = jnp.concatenate([cos] * n, axis=1) if n > 1 else cos
    sinf = jnp.concatenate([sin_signed] * n, axis=1) if n > 1 else sin_signed
    w = x.shape[1]
    lane = lax.broadcasted_iota(jnp.int32, x.shape, 1)
    up = pltpu.roll(x, w - DH // 4, 1)
    dn = pltpu.roll(x, DH // 4, 1)
    rot = jnp.where((lane % (DH // 2)) < DH // 4, up, dn)
    return x * cosf + rot * sinf


def _odd_in_kernel(x_ref, mod_ref, g_ref, w_ref, cos_ref, sin_ref, *out_refs, tm, tile0, rope, kv_f32):
    r = _mod_row((pl.program_id(0) + tile0) * (tm // TM))
    h = _rms_mod(x_ref[...], g_ref[...], mod_ref[0, pl.ds(r, 1), :], mod_ref[1, pl.ds(r, 1), :])
    z = _dot(h.astype(BF16), w_ref[...])
    qna_ref, kna_ref, vna_ref, qsw_ref, ksw_ref, vsw_ref = out_refs[:6]
    c0 = 0
    qna = z[:, 0:NA_W]
    kna = z[:, NA_W:2 * NA_W]
    vna = z[:, 2 * NA_W:3 * NA_W]
    c0 = 3 * NA_W
    qsw = z[:, c0:c0 + SWQ_W]
    ksw = z[:, c0 + SWQ_W:c0 + SWQ_W + SWKV_W]
    vsw = z[:, c0 + SWQ_W + SWKV_W:]
    if kv_f32:
        for ref, val in zip(out_refs[6:10], (kna, vna, ksw, vsw)):
            heads = val.shape[1] // DH
            for hh in range(heads):
                ref[pl.ds(hh, SEQ, stride=heads), :] = val[:, hh * DH:(hh + 1) * DH]
    if rope:
        cos = cos_ref[...]
        sin = sin_ref[...]
        qsw = _rope(qsw, cos, sin)
        ksw = _rope(ksw, cos, sin)
    qna_ref[...] = (qna * QK_SCALE).astype(BF16)
    kna_ref[...] = kna.astype(BF16)
    vna_ref[...] = vna.astype(BF16)
    qsw_ref[...] = (qsw * QK_SCALE).astype(BF16)
    ksw_ref[...] = ksw.astype(BF16)
    vsw_ref[...] = vsw.astype(BF16)


def _odd_in(x, mod_l, g, w, cos, sin, *, prompt):
    tm = TM if prompt else 2 * TM
    tile0 = 0 if prompt else N_P // tm
    nt = (N_P if prompt else N_S) // tm
    n = nt * tm
    widths = [NA_W, NA_W, NA_W, SWQ_W, SWKV_W, SWKV_W]
    out_shape = [jax.ShapeDtypeStruct((n, w_), BF16) for w_ in widths]
    out_specs = [pl.BlockSpec((tm, w_), lambda i: (i, 0)) for w_ in widths]
    if prompt:
        assert tm == SEQ
        for heads in (NA_H, NA_H, SW_KV, SW_KV):
            out_shape.append(jax.ShapeDtypeStruct((BATCH * SEQ * heads, DH), F32))
            out_specs.append(pl.BlockSpec((SEQ * heads, DH), lambda i: (i, 0)))
    return pl.pallas_call(
        functools.partial(_odd_in_kernel, tm=tm, tile0=tile0, rope=not prompt, kv_f32=prompt),
        grid=(nt,),
        in_specs=[
            pl.BlockSpec((tm, D), lambda i: (i + tile0, 0)),
            pl.BlockSpec((6, MOD_ROWS, D), lambda i: (0, 0, 0)),
            pl.BlockSpec((1, D), lambda i: (0, 0)),
            pl.BlockSpec((D, ODD_IN), lambda i: (0, 0)),
            pl.BlockSpec((tm, 128), lambda i: (i % (DEC_SEQ // tm), 0)),
            pl.BlockSpec((tm, 128), lambda i: (i % (DEC_SEQ // tm), 0)),
        ],
        out_specs=out_specs,
        out_shape=out_shape,
        compiler_params=_cparams("parallel"),
        name="odd_in_prompt" if prompt else "odd_in_latent",
    )(x, mod_l, g, w, cos, sin)


def _rope_tables():
    t = np.arange(DEC_SEQ)
    quarter = DH // 4
    inv = 1.0 / (ROPE_BASE ** (np.arange(quarter, dtype=np.float64) / quarter))

    def cos_sin(pos):
        ang = pos.astype(np.float64)[:, None] * inv[None, :]
        ang = np.concatenate([ang, ang], axis=-1)
        return np.cos(ang), np.sin(ang)

    cr, sr = cos_sin(t // GRID_W)
    cc, sc = cos_sin(t % GRID_W)
    cos = np.concatenate([cr, cc], axis=-1)
    sin = np.concatenate([sr, sc], axis=-1)
    sign = np.where((np.arange(DH) % (DH // 2)) < DH // 4, -1.0, 1.0)
    sin = sin * sign[None, :]
    cos = np.concatenate([cos, cos], axis=-1).astype(np.float32)
    sin = np.concatenate([sin, sin], axis=-1).astype(np.float32)
    return jnp.asarray(cos), jnp.asarray(sin)


def _softmax_pv(segs, sink=None):
    m = None
    for s, _ in segs:
        sm = jnp.max(s, axis=-1, keepdims=True)
        m = sm if m is None else jnp.maximum(m, sm)
    if sink is not None:
        m = jnp.maximum(m, sink)
    den = None
    acc = None
    for s, v in segs:
        p = jnp.exp(s - m)
        ps = jnp.sum(p, axis=-1, keepdims=True)
        den = ps if den is None else den + ps
        pv = _dot(p.astype(BF16), v)
        acc = pv if acc is None else acc + pv
    if sink is not None:
        den = den + jnp.exp(sink - m)
    return acc / den


def _sink_col(sink_ref, g, rows_per_head):
    row = lax.broadcasted_iota(jnp.int32, (SW_G * rows_per_head, 1), 0)
    col = jnp.zeros((SW_G * rows_per_head, 1), F32)
    for r in range(SW_G):
        col = jnp.where(row // rows_per_head == r, sink_ref[g * SW_G + r], col)
    return col


def _ctx_attn_kernel(sink_ref, qna_ref, kna_ref, vna_ref, qsw_ref, ksw_ref, vsw_ref, ona_ref, osw_ref):
    lane = lax.broadcasted_iota(jnp.int32, (SEQ, 2 * DH), 1)
    for p in range(NA_H // 2):
        cols = slice(p * 2 * DH, (p + 1) * 2 * DH)
        q = qna_ref[:, cols]
        k = kna_ref[:, cols]
        v = vna_ref[:, cols]
        outs = []
        for half in range(2):
            mine = (lane < DH) if half == 0 else (lane >= DH)
            qm = jnp.where(mine, q, jnp.zeros_like(q))
            outs.append(_softmax_pv([(_dot_nt(qm, k), v)]))
        ona_ref[:, cols] = jnp.where(lane < DH, outs[0], outs[1]).astype(BF16)
    outs = []
    for g in range(SW_KV):
        kc = slice(g * DH, (g + 1) * DH)
        q = jnp.concatenate([qsw_ref[:, (g * SW_G + r) * DH:(g * SW_G + r + 1) * DH] for r in range(SW_G)], axis=0)
        s = _dot_nt(q, ksw_ref[:, kc])
        o = _softmax_pv([(s, vsw_ref[:, kc])], sink=_sink_col(sink_ref, g, SEQ))
        outs.extend(o[r * SEQ:(r + 1) * SEQ, :] for r in range(SW_G))
    osw_ref[...] = jnp.concatenate(outs, axis=1).astype(BF16)


def _ctx_attn(sink, qna, kna, vna, qsw, ksw, vsw):
    def spec(w):
        return pl.BlockSpec((SEQ, w), lambda b: (b, 0))

    return pl.pallas_call(
        _ctx_attn_kernel,
        grid=(BATCH,),
        in_specs=[pl.BlockSpec(memory_space=pltpu.SMEM), spec(NA_W), spec(NA_W), spec(NA_W),
                  spec(SWQ_W), spec(SWKV_W), spec(SWKV_W)],
        out_specs=[spec(NA_W), spec(SWQ_W)],
        out_shape=[jax.ShapeDtypeStruct((N_P, NA_W), BF16), jax.ShapeDtypeStruct((N_P, SWQ_W), BF16)],
        compiler_params=_cparams("parallel"),
        name="ctx_attn",
    )(sink, qna, kna, vna, qsw, ksw, vsw)


def _na_bias_kernel(rpb_ref, o_ref):
    h = pl.program_id(0)
    cq = lax.broadcasted_iota(jnp.int32, (GRID_W, GRID_W), 0)
    ck = lax.broadcasted_iota(jnp.int32, (GRID_W, GRID_W), 1)
    cstart = jnp.clip(cq - NA_COLS // 2, 0, GRID_W - NA_COLS)
    ok = (ck >= cstart) & (ck < cstart + NA_COLS)
    dc = jnp.clip(ck - cq + NA_COLS - 1, 0, 2 * NA_COLS - 2)
    ndc = 2 * NA_COLS - 1
    for d in range(2 * NA_ROWS - 1):
        b = jnp.zeros((GRID_W, GRID_W), F32)
        for e in range(ndc):
            b = jnp.where(dc == e, rpb_ref[h, d * ndc + e], b)
        o_ref[d] = jnp.where(ok, b, NEG)


def _na_bias(rpb):
    nd = 2 * NA_ROWS - 1
    return pl.pallas_call(
        _na_bias_kernel,
        grid=(NA_H,),
        in_specs=[pl.BlockSpec(memory_space=pltpu.SMEM)],
        out_specs=pl.BlockSpec((None, nd, GRID_W, GRID_W), lambda h: (h, 0, 0, 0)),
        out_shape=jax.ShapeDtypeStruct((NA_H, nd, GRID_W, GRID_W), F32),
        compiler_params=_cparams("parallel"),
        name="na_bias",
    )(rpb.reshape(NA_H, nd * (2 * NA_COLS - 1)))


NA_QB = 4
NA_KR = 12
NA_NQB = ROWS // NA_QB


def _na_bias_blocks(bias):
    neg = jnp.full((NA_H, GRID_W, GRID_W), NEG, F32)
    cases = []
    for case in range(3):
        rows = []
        for i in range(NA_QB):
            blocks = []
            for j in range(NA_KR):
                if case == 0:
                    valid, dr = j < NA_ROWS, j - i + NA_ROWS - 1
                elif case == 1:
                    valid, dr = i <= j < i + NA_ROWS, j - i + NA_ROWS // 2 - 1
                else:
                    valid, dr = j >= NA_KR - NA_ROWS, j - i + NA_ROWS - 1 - (NA_KR - NA_QB)
                blocks.append(bias[:, dr] if valid else neg)
            rows.append(jnp.concatenate(blocks, axis=2))
        cases.append(jnp.concatenate(rows, axis=1))
    return jnp.stack(cases)


def _na_kernel(q_ref, k_ref, v_ref, kc_ref, vc_ref, bias_ref, o_ref):
    r0 = pl.program_id(1) * NA_QB
    start = pl.multiple_of(jnp.clip(r0 - NA_ROWS // 2, 0, ROWS - NA_KR) * GRID_W, GRID_W)
    nq = NA_QB * GRID_W
    lane = lax.broadcasted_iota(jnp.int32, (nq, 2 * DH), 1)
    for p in range(NA_H // 2):
        cols = slice(p * 2 * DH, (p + 1) * 2 * DH)
        q = q_ref[:, cols]
        kl = k_ref[pl.ds(start, NA_KR * GRID_W), cols]
        vl = v_ref[pl.ds(start, NA_KR * GRID_W), cols]
        kc = kc_ref[:, cols]
        vc = vc_ref[:, cols]
        zero = jnp.zeros_like(q)
        qm = jnp.concatenate([jnp.where(lane < DH, q, zero), jnp.where(lane < DH, zero, q)], axis=0)
        s_loc = _dot_nt(qm, kl)
        s_ctx = _dot_nt(qm, kc)
        outs = []
        for half in range(2):
            rows = slice(half * nq, (half + 1) * nq)
            outs.append(_softmax_pv([(s_loc[rows] + bias_ref[2 * p + half], vl), (s_ctx[rows], vc)]))
        o_ref[:, cols] = jnp.where(lane < DH, outs[0], outs[1]).astype(BF16)


def _na_attn(q, k, v, kc, vc, bias):
    nq = NA_QB * GRID_W

    def bias_case(b, rb):
        return (jnp.where(rb == 0, 0, jnp.where(rb == NA_NQB - 1, 2, 1)), 0, 0, 0)

    return pl.pallas_call(
        _na_kernel,
        grid=(DEC_BATCH, NA_NQB),
        in_specs=[
            pl.BlockSpec((nq, NA_W), lambda b, rb: (b * NA_NQB + rb, 0)),
            pl.BlockSpec((DEC_SEQ, NA_W), lambda b, rb: (b, 0)),
            pl.BlockSpec((DEC_SEQ, NA_W), lambda b, rb: (b, 0)),
            pl.BlockSpec((None, PAST, NA_W), lambda b, rb: (b, 0, 0)),
            pl.BlockSpec((None, PAST, NA_W), lambda b, rb: (b, 0, 0)),
            pl.BlockSpec((None, NA_H, nq, NA_KR * GRID_W), bias_case),
        ],
        out_specs=pl.BlockSpec((nq, NA_W), lambda b, rb: (b * NA_NQB + rb, 0)),
        out_shape=jax.ShapeDtypeStruct((N_S, NA_W), BF16),
        compiler_params=_cparams("parallel", "arbitrary"),
        name="na_attn",
    )(q, k, v, kc, vc, bias)


def _sw_window_bias():
    q = np.arange(SW_G * ABLK)[:, None] % ABLK
    k = np.arange(3 * ABLK)[None, :]
    tables = [np.where(np.abs(q + lead - k) <= SW_WIN, 0.0, NEG) for lead in (0, ABLK, 2 * ABLK)]
    return jnp.asarray(np.stack(tables).astype(np.float32))


def _sw_kernel(sink_ref, q_ref, k_ref, v_ref, kc_ref, vc_ref, wb_ref, o_ref):
    j = pl.program_id(1)
    nk = 3 * ABLK
    start = pl.multiple_of(jnp.clip((j - 1) * ABLK, 0, DEC_SEQ - nk), ABLK)
    outs = []
    for g in range(SW_KV):
        kcols = slice(g * DH, (g + 1) * DH)
        q = jnp.concatenate([q_ref[:, (g * SW_G + r) * DH:(g * SW_G + r + 1) * DH] for r in range(SW_G)], axis=0)
        kw = k_ref[pl.ds(start, nk), kcols]
        vw = v_ref[pl.ds(start, nk), kcols]
        s_w = _dot_nt(q, kw) + wb_ref[...]
        s_c = _dot_nt(q, kc_ref[:, kcols])
        o = _softmax_pv([(s_w, vw), (s_c, vc_ref[:, kcols])], sink=_sink_col(sink_ref, g, ABLK))
        outs.extend(o[r * ABLK:(r + 1) * ABLK, :] for r in range(SW_G))
    o_ref[...] = jnp.concatenate(outs, axis=1).astype(BF16)


def _sw_attn(sink, q, k, v, kc, vc):
    nb = DEC_SEQ // ABLK
    return pl.pallas_call(
        _sw_kernel,
        grid=(DEC_BATCH, nb),
        in_specs=[
            pl.BlockSpec(memory_space=pltpu.SMEM),
            pl.BlockSpec((ABLK, SWQ_W), lambda b, j: (b * nb + j, 0)),
            pl.BlockSpec((DEC_SEQ, SWKV_W), lambda b, j: (b, 0)),
            pl.BlockSpec((DEC_SEQ, SWKV_W), lambda b, j: (b, 0)),
            pl.BlockSpec((None, PAST, SWKV_W), lambda b, j: (b, 0, 0)),
            pl.BlockSpec((None, PAST, SWKV_W), lambda b, j: (b, 0, 0)),
            pl.BlockSpec((None, SW_G * ABLK, 3 * ABLK),
                         lambda b, j: (jnp.where(j == 0, 0, jnp.where(j == nb - 1, 2, 1)), 0, 0)),
        ],
        out_specs=pl.BlockSpec((ABLK, SWQ_W), lambda b, j: (b * nb + j, 0)),
        out_shape=jax.ShapeDtypeStruct((N_S, SWQ_W), BF16),
        compiler_params=_cparams("parallel", "arbitrary"),
        name="sw_attn",
    )(sink, q, k, v, kc, vc, _sw_window_bias())


BT = 4096
NB = N_TOK // BT
TPB = BT // TM
CH = 256
CH_TAIL = 128
XPS = 2
NES = N_EXP // XPS
XO_ROWS = 2 * BT + CH
ROW = 8
assert D == ROW * 128


def _route_sparse(probs, sel):
    rank = []
    score = []
    for g in range(N_GRP):
        ids = range(g * EPG, (g + 1) * EPG)
        tot = None
        for e in ids:
            rk = jnp.zeros_like(sel[e])
            for j in ids:
                if j < e:
                    rk = rk + (sel[j] >= sel[e]).astype(F32)
                elif j > e:
                    rk = rk + (sel[j] > sel[e]).astype(F32)
            rank.append(rk)
            contrib = jnp.where(rk < 2.0, sel[e], 0.0)
            tot = contrib if tot is None else tot + contrib
        score.append(tot)
    top1, top2 = [], []
    for g in range(N_GRP):
        best = None
        for j in range(N_GRP):
            if j == g:
                continue
            c = (score[g] > score[j]) if j < g else (score[g] >= score[j])
            best = c if best is None else jnp.logical_and(best, c)
        for e in range(g * EPG, (g + 1) * EPG):
            top1.append(jnp.logical_and(best, rank[e] == 0.0))
            top2.append(jnp.logical_and(best, rank[e] == 1.0))
    return top1, top2


def _pick(masks, rows):
    acc = None
    for m, r in zip(masks, rows):
        v = jnp.where(m, r, 0.0)
        acc = v if acc is None else acc + v
    return acc


def _route_tile(xv, mod_ref, g_ref, rw_ref, rb_ref, h_ref, pos_ref, w_ref, cnt_ref, off_ref, meta, carry):
    i = pl.program_id(0)
    j = i % TPB
    r = _mod_row(i)
    h = _rms_mod(xv, g_ref[...], mod_ref[3, pl.ds(r, 1), :], mod_ref[4, pl.ds(r, 1), :])
    h_ref[...] = h.astype(BF16)
    h_hi = h.astype(BF16)
    h_lo = (h - h_hi.astype(F32)).astype(BF16)
    logits = _dot(h_lo, rw_ref[0]) + _dot(h_hi, rw_ref[1]) + _dot(h_hi, rw_ref[0])
    lt = logits.T[:N_EXP, :]
    m = jnp.max(lt, axis=0, keepdims=True)
    ex = jnp.exp(lt - m)
    pr = ex / jnp.sum(ex, axis=0, keepdims=True)
    se = pr + rb_ref[:N_EXP, :]
    probs = [pr[e:e + 1, :] for e in range(N_EXP)]
    sel = [se[e:e + 1, :] for e in range(N_EXP)]
    top1, top2 = _route_sparse(probs, sel)

    @pl.when(j == 0)
    def _():
        carry[...] = jnp.zeros_like(carry)

    member = jnp.concatenate([jnp.logical_or(a, b).astype(F32) for a, b in zip(top1, top2)], axis=0)
    s_idx = lax.broadcasted_iota(jnp.int32, (TM, TM), 0)
    t_idx = lax.broadcasted_iota(jnp.int32, (TM, TM), 1)
    before = jnp.where(s_idx < t_idx, 1.0, 0.0).astype(BF16)
    seen = _dot(member.astype(BF16), before) + carry[:, 0:1]
    seen_rows = [seen[e:e + 1, :] for e in range(N_EXP)]
    ids = [jnp.full((1, TM), float(e), F32) for e in range(N_EXP)]
    p1 = _pick(top1, probs)
    p2 = _pick(top2, probs)
    den = p1 + p2
    w_ref[j, 0:1, :] = p1 / den
    w_ref[j, 1:2, :] = p2 / den
    meta[j, 0:1, :] = _pick(top1, ids)
    meta[j, 1:2, :] = _pick(top2, ids)
    meta[j, 2:3, :] = _pick(top1, seen_rows)
    meta[j, 3:4, :] = _pick(top2, seen_rows)
    carry[...] = carry[...] + jnp.sum(member, axis=1, keepdims=True)

    @pl.when(j == TPB - 1)
    def _():
        cnt = carry[...]
        offs = [jnp.zeros((1, 128), F32)]
        for e in range(1, N_EXP):
            offs.append(offs[-1] + cnt[e - 1:e, :])
        cnt_ref[...] = cnt
        off_ref[...] = jnp.concatenate(offs, axis=0)
        for jj in range(TPB):
            for k in range(2):
                eid = meta[jj, k:k + 1, :]
                pos = meta[jj, 2 + k:3 + k, :]
                for e in range(1, N_EXP):
                    pos = pos + jnp.where(eid == float(e), offs[e][:, 0:1], 0.0)
                pos_ref[jj, k:k + 1, :] = (pos * float(ROW)).astype(jnp.int32)


def _router_kernel(x_ref, *refs):
    _route_tile(x_ref[...], *refs)


def _odd_out_router_kernel(nap_ref, nas_ref, swp_ref, sws_ref, x_ref, wo_ref, mod_ref, g_ref, rw_ref, rb_ref,
                           x1_ref, *refs):
    i = pl.program_id(0)
    is_p = i < NPT
    ona = jnp.where(is_p, nap_ref[...], nas_ref[...])
    osw = jnp.where(is_p, swp_ref[...], sws_ref[...])
    y = _dot(ona, wo_ref[:NA_W, :]) + _dot(osw, wo_ref[NA_W:, :])
    x1 = x_ref[...] + mod_ref[2, pl.ds(_mod_row(i), 1), :] * y
    x1_ref[...] = x1
    _route_tile(x1, mod_ref, g_ref, rw_ref, rb_ref, *refs)


def _router(x, mod_l, g, rw2, rb_col, attn=None):
    blk = lambda i: (i // TPB, 0, 0, 0)
    in_specs = [
        pl.BlockSpec((TM, D), lambda i: (i, 0)),
        pl.BlockSpec((6, MOD_ROWS, D), lambda i: (0, 0, 0)),
        pl.BlockSpec((1, D), lambda i: (0, 0)),
        pl.BlockSpec((2, D, 128), lambda i: (0, 0, 0)),
        pl.BlockSpec((128, 1), lambda i: (0, 0)),
    ]
    out_specs = [
        pl.BlockSpec((TM, D), lambda i: (i, 0)),
        pl.BlockSpec((None, TPB, 2, TM), blk),
        pl.BlockSpec((None, TPB, 2, TM), blk),
        pl.BlockSpec((None, N_EXP, 128), lambda i: (i // TPB, 0, 0)),
        pl.BlockSpec((None, N_EXP, 128), lambda i: (i // TPB, 0, 0)),
    ]
    out_shape = [
        jax.ShapeDtypeStruct((N_TOK, D), BF16),
        jax.ShapeDtypeStruct((NB, TPB, 2, TM), jnp.int32),
        jax.ShapeDtypeStruct((NB, TPB, 2, TM), F32),
        jax.ShapeDtypeStruct((NB, N_EXP, 128), F32),
        jax.ShapeDtypeStruct((NB, N_EXP, 128), F32),
    ]
    args = (x, mod_l, g, rw2, rb_col)
    body = _router_kernel
    if attn is not None:
        nap, nas, swp, sws, w_out = attn
        in_specs = (_dual_specs(TM, NA_W, NPT) + _dual_specs(TM, SWQ_W, NPT) + in_specs[:1]
                    + [pl.BlockSpec((D, D), lambda i: (0, 0))] + in_specs[1:])
        out_specs = [pl.BlockSpec((TM, D), lambda i: (i, 0))] + out_specs
        out_shape = [jax.ShapeDtypeStruct((N_TOK, D), F32)] + out_shape
        args = (nap, nas, swp, sws, x, w_out, mod_l, g, rw2, rb_col)
        body = _odd_out_router_kernel
    return pl.pallas_call(
        body,
        grid=(NT,),
        in_specs=in_specs,
        out_specs=out_specs,
        out_shape=out_shape,
        scratch_shapes=[pltpu.VMEM((TPB, 4, TM), F32), pltpu.VMEM((N_EXP, 128), F32)],
        compiler_params=_cparams("arbitrary"),
        name="router" if attn is None else "odd_out_router",
    )(*args)


def _row(p):
    return pl.ds(pl.multiple_of(p, ROW), ROW)


def _expert_segment(n, base, xo, wg_ref, wu_ref, wd_ref):
    def ffn_rows(row0, rows):
        x = jnp.concatenate([xo[pl.ds(row0 * ROW + c, rows, stride=ROW), :] for c in range(ROW)], axis=1)
        xb = x.astype(BF16)
        a = _dot(xb, wg_ref[...])
        u = _dot(xb, wu_ref[...])
        hid = (a / (1.0 + jnp.exp(-a))) * u
        out = _dot(hid.astype(BF16), wd_ref[...])
        valid = lax.broadcasted_iota(jnp.int32, (rows, 1), 0) < base + n - row0
        res = jnp.where(valid, out, x)
        for c in range(ROW):
            xo[pl.ds(row0 * ROW + c, rows, stride=ROW), :] = res[:, c * 128:(c + 1) * 128]

    def chunk(jc, carry):
        ffn_rows(base + jc * CH, CH)
        return carry

    nfull = (n + CH - CH_TAIL - 1) // CH
    lax.fori_loop(0, nfull, chunk, 0)

    @pl.when(n > nfull * CH)
    def _():
        ffn_rows(base + nfull * CH, CH_TAIL)


def _experts_kernel(cnt_ref, off_ref, pos_ref, w_ref, h_ref, wg_ref, wu_ref, wd_ref, x_ref, mod_ref, fg_ref,
                    *rest, final):
    if final:
        op_ref, os_ref, xo, stg = rest
    else:
        o_ref, xo, stg = rest
    b = pl.program_id(0)
    s = pl.program_id(1)

    @pl.when(s == 0)
    def _():
        xo[pl.ds(2 * BT * ROW, CH * ROW), :] = jnp.zeros((CH * ROW, 128), F32)

    @pl.when(s < TPB)
    def _():
        hs = h_ref[...].astype(F32)
        for c in range(ROW):
            stg[pl.ds(c, TM, stride=ROW), :] = hs[:, c * 128:(c + 1) * 128]
        for t in range(TM):
            v = stg[t * ROW:(t + 1) * ROW, :]
            xo[_row(pos_ref[0, t]), :] = v
            xo[_row(pos_ref[1, t]), :] = v

    @pl.when(jnp.logical_and(s >= TPB, s < TPB + NES))
    def _():
        for k in range(XPS):
            e = (s - TPB) * XPS + k
            _expert_segment(cnt_ref[b, e], off_ref[b, e], xo, wg_ref.at[k], wu_ref.at[k], wd_ref.at[k])

    @pl.when(s >= TPB + NES - 1)
    def _():
        j = s - (TPB + NES - 1)
        tile = b * TPB + j
        g2 = mod_ref[5, pl.ds(_mod_row(tile), 1), :]
        for t in range(TM):
            a = xo[_row(pos_ref[0, t]), :]
            u = xo[_row(pos_ref[1, t]), :]
            stg[t * ROW:(t + 1) * ROW, :] = w_ref[0, t] * a + w_ref[1, t] * u
        y = jnp.concatenate([stg[pl.ds(c, TM, stride=ROW), :] for c in range(ROW)], axis=1)
        res = x_ref[...] + g2 * y
        if final:
            ms = jnp.mean(res * res, axis=-1, keepdims=True)
            res = res * lax.rsqrt(ms + EPS) * fg_ref[...]

            @pl.when(tile < NPT)
            def _():
                op_ref[...] = res

            @pl.when(tile >= NPT)
            def _():
                os_ref[...] = res
        else:
            o_ref[...] = res


def _experts(cnt, off, pos, wts, h, wg, wu, wd, x, mod_l, fg, *, layer, final):
    def tile_of(b, s):
        return b * TPB + jnp.clip(s - (TPB + NES - 1), 0, TPB - 1)

    def expert_of(b, s, *_):
        return (layer, jnp.clip(s - TPB, 0, NES - 1), 0, 0)

    def smem_tile(b, s, *_):
        return (b, jnp.where(s < TPB, s, jnp.clip(s - (TPB + NES - 1), 0, TPB - 1)), 0, 0)

    smem_blk = pl.BlockSpec((None, None, 2, TM), smem_tile, memory_space=pltpu.SMEM)
    if final:
        out_specs = [pl.BlockSpec((TM, D), lambda b, s, *_: (jnp.minimum(tile_of(b, s), NPT - 1), 0)),
                     pl.BlockSpec((TM, D), lambda b, s, *_: (jnp.maximum(tile_of(b, s) - NPT, 0), 0))]
        out_shape = [jax.ShapeDtypeStruct((N_P, D), F32), jax.ShapeDtypeStruct((N_S, D), F32)]
    else:
        out_specs = pl.BlockSpec((TM, D), lambda b, s, *_: (tile_of(b, s), 0))
        out_shape = jax.ShapeDtypeStruct((N_TOK, D), F32)
    grid_spec = pltpu.PrefetchScalarGridSpec(
        num_scalar_prefetch=2,
        grid=(NB, 2 * TPB + NES - 1),
        in_specs=[
            smem_blk,
            smem_blk,
            pl.BlockSpec((TM, D), lambda b, s, *_: (b * TPB + jnp.minimum(s, TPB - 1), 0)),
            pl.BlockSpec((None, XPS, D, D_EXP), expert_of),
            pl.BlockSpec((None, XPS, D, D_EXP), expert_of),
            pl.BlockSpec((None, XPS, D_EXP, D), expert_of),
            pl.BlockSpec((TM, D), lambda b, s, *_: (tile_of(b, s), 0)),
            pl.BlockSpec((6, MOD_ROWS, D), lambda b, s, *_: (0, 0, 0)),
            pl.BlockSpec((1, D), lambda b, s, *_: (0, 0)),
        ],
        out_specs=out_specs,
        scratch_shapes=[pltpu.VMEM((XO_ROWS * ROW, 128), F32), pltpu.VMEM((TM * ROW, 128), F32)],
    )
    return pl.pallas_call(
        functools.partial(_experts_kernel, final=final),
        grid_spec=grid_spec,
        out_shape=out_shape,
        compiler_params=_cparams("arbitrary", "arbitrary"),
        name="experts_final" if final else "experts",
    )(cnt, off, pos, wts, h, wg, wu, wd, x, mod_l, fg)


def _moe_sparse(x, routed, mod_l, wg, wu, wd, fg, *, layer, final):
    h, pos, wts, cnt, off = routed
    cnt = cnt[:, :, 0].astype(jnp.int32)
    off = off[:, :, 0].astype(jnp.int32)
    return _experts(cnt, off, pos, wts, h, wg, wu, wd, x, mod_l, fg, layer=layer, final=final)


def kernel(x_prompt, x_sample, cache_na_k, cache_na_v, cache_sw_k, cache_sw_v, c, c_ctx, mod_w, mod_b, norm_mix_g, norm_ffn_g, ev_w_in, ev_pool_w, ev_pool_scale, ev_conv_w, ev_w_out, od_w_in, od_rpb, od_sink, od_w_out, router_w, router_b, moe_w_gate, moe_w_up, moe_w_down, final_norm_g):
    xp = x_prompt.reshape(N_P, D)
    xs = x_sample.reshape(N_S, D)
    cvec = jnp.concatenate([c_ctx[None, :], c, jnp.zeros((MOD_ROWS - 1 - DEC_BATCH, D), F32)], axis=0)
    mod = _modulation(cvec, mod_w, mod_b)

    rw_pad = jnp.pad(router_w, ((0, 0), (0, 128 - N_EXP)))
    rw_hi = rw_pad.astype(BF16)
    rw2 = jnp.stack([rw_hi, (rw_pad - rw_hi.astype(F32)).astype(BF16)])
    rb_col = jnp.pad(router_b, (0, 128 - N_EXP)).reshape(128, 1)

    conv_w = jnp.pad(ev_conv_w[0], ((0, 8 - ev_conv_w.shape[1]), (0, 0)))
    x, wg, wu, wd = _even_layer(xp, xs, mod[0], norm_mix_g[0:1], ev_w_in[0].astype(BF16),
                                ev_pool_w[0].astype(BF16), ev_pool_scale[0:1], conv_w, ev_w_out[0].astype(BF16),
                                moe_w_gate, moe_w_up, moe_w_down)
    fg = final_norm_g.reshape(1, D)
    routed = _router(x, mod[0], norm_ffn_g[0:1], rw2, rb_col)
    x = _moe_sparse(x, routed, mod[0], wg, wu, wd, fg, layer=0, final=False)

    cos, sin = _rope_tables()
    w_in = od_w_in[0].astype(BF16)
    g1 = norm_mix_g[1:2]
    qna_p, kna_p, vna_p, qsw_p, ksw_p, vsw_p, nak, nav, swk, swv = _odd_in(x, mod[1], g1, w_in, cos, sin, prompt=True)
    qna_s, kna_s, vna_s, qsw_s, ksw_s, vsw_s = _odd_in(x, mod[1], g1, w_in, cos, sin, prompt=False)
    sink = od_sink[0]
    ona_p, osw_p = _ctx_attn(sink, qna_p, kna_p, vna_p, qsw_p, ksw_p, vsw_p)
    bias = _na_bias_blocks(_na_bias(od_rpb[0]))
    ona_s = _na_attn(qna_s, kna_s, vna_s,
                     cache_na_k[:, 0].reshape(DEC_BATCH, PAST, NA_W).astype(BF16),
                     cache_na_v[:, 0].reshape(DEC_BATCH, PAST, NA_W).astype(BF16), bias)
    osw_s = _sw_attn(sink, qsw_s, ksw_s, vsw_s,
                     cache_sw_k[:, 0].reshape(DEC_BATCH, PAST, SWKV_W).astype(BF16),
                     cache_sw_v[:, 0].reshape(DEC_BATCH, PAST, SWKV_W).astype(BF16))
    x, *routed = _router(x, mod[1], norm_ffn_g[1:2], rw2, rb_col,
                         attn=(ona_p, ona_s, osw_p, osw_s, od_w_out[0].astype(BF16)))
    y_prompt, y_sample = _moe_sparse(x, routed, mod[1], wg, wu, wd, fg, layer=1, final=True)
    y_prompt = y_prompt.reshape(BATCH, SEQ, D)
    y_sample = y_sample.reshape(DEC_BATCH, DEC_SEQ, D)
    new_na_k = nak.reshape(BATCH, 1, SEQ, NA_H, DH)
    new_na_v = nav.reshape(BATCH, 1, SEQ, NA_H, DH)
    new_sw_k = swk.reshape(BATCH, 1, SEQ, SW_KV, DH)
    new_sw_v = swv.reshape(BATCH, 1, SEQ, SW_KV, DH)
    return (y_prompt, y_sample, new_na_k, new_na_v, new_sw_k, new_sw_v)
```

```python
import functools

import jax
import jax.numpy as jnp
import numpy as np
from jax import lax
from jax.experimental import pallas as pl
from jax.experimental.pallas import tpu as pltpu

D = 1024
BATCH = 16
SEQ = 256
DEC_BATCH = 4
DEC_SEQ = 4096
PAST = 512
GRID_W = 64
ROWS = DEC_SEQ // GRID_W
DH = 64
POOL_W = 512
POOL_WINDOWS = (2, 4, 8, 16)
POOL_GW = 128
CONV_W = 512
EVEN_IN = POOL_W + 3 * CONV_W
NA_H = 8
NA_ROWS = 8
NA_COLS = 16
SW_H = 8
SW_KV = 2
SW_G = SW_H // SW_KV
SW_WIN = 128
ABLK = 128
NA_W = NA_H * DH
SWQ_W = SW_H * DH
SWKV_W = SW_KV * DH
ODD_IN = 3 * NA_W + SWQ_W + 2 * SWKV_W
N_EXP = 16
N_GRP = 4
EPG = 4
D_EXP = 512
EPS = 1e-6
NEG = -1e30
ROPE_BASE = 10000.0
QK_SCALE = DH ** -0.5
assert QK_SCALE == 0.125

N_P = BATCH * SEQ
N_S = DEC_BATCH * DEC_SEQ
N_TOK = N_P + N_S
MOD_ROWS = 8

TM = 256
NPT = N_P // TM
TPS = DEC_SEQ // TM
NT = N_TOK // TM
HALO = 8
CAST_STEPS = 64

F32 = jnp.float32
BF16 = jnp.bfloat16
VMEM_LIMIT = 56 * 1024 * 1024


def _cparams(*sem):
    return pltpu.CompilerParams(dimension_semantics=sem, vmem_limit_bytes=VMEM_LIMIT)


def _mod_row(i):
    return jnp.where(i < NPT, 0, 1 + (i - NPT) // TPS)


def _rms_mod(x, g, shift, scale):
    ms = jnp.mean(x * x, axis=-1, keepdims=True)
    y = x * lax.rsqrt(ms + EPS) * g
    return y * (1.0 + scale) + shift


def _dot(a, b):
    return jnp.dot(a, b, preferred_element_type=F32)


def _dot_nt(a, b):
    return lax.dot_general(a, b, (((1,), (1,)), ((), ())), preferred_element_type=F32)


def _mod_kernel(cv_ref, w_ref, b_ref, o_ref):
    cv = cv_ref[...]
    a = cv / (1.0 + jnp.exp(-cv))
    o_ref[...] = jnp.dot(a, w_ref[...], preferred_element_type=F32,
                         precision=lax.Precision.HIGHEST) + b_ref[...]


def _modulation(cvec, mod_w, mod_b):
    depth = mod_w.shape[0]
    return pl.pallas_call(
        _mod_kernel,
        grid=(depth, 6),
        in_specs=[
            pl.BlockSpec((MOD_ROWS, D), lambda l, j: (0, 0)),
            pl.BlockSpec((None, D, D), lambda l, j: (l, 0, j)),
            pl.BlockSpec((None, None, 1, D), lambda l, j: (l, j, 0, 0)),
        ],
        out_specs=pl.BlockSpec((None, None, MOD_ROWS, D), lambda l, j: (l, j, 0, 0)),
        out_shape=jax.ShapeDtypeStruct((depth, 6, MOD_ROWS, D), F32),
        compiler_params=_cparams("arbitrary", "arbitrary"),
        name="modulation",
    )(cvec, mod_w, mod_b.reshape(depth, 6, 1, D))


def _dual_specs(tm, width, npt):
    return [
        pl.BlockSpec((tm, width), lambda i: (jnp.minimum(i, npt - 1), 0)),
        pl.BlockSpec((tm, width), lambda i: (jnp.maximum(i - npt, 0), 0)),
    ]


def _even_kernel(xp_ref, xs_ref, xprev_ref, xnext_ref, mod_ref, g_ref, wi_ref, pw_ref, ps_ref, cw_ref,
                 wo_ref, wgf_ref, wuf_ref, wdf_ref, o_ref, wgb_ref, wub_ref, wdb_ref, pext, uext):
    i = pl.program_id(0)

    @pl.when(i < CAST_STEPS)
    def _():
        wgb_ref[...] = wgf_ref[...].astype(BF16)
        wub_ref[...] = wuf_ref[...].astype(BF16)
        wdb_ref[...] = wdf_ref[...].astype(BF16)

    r = _mod_row(i)
    is_p = i < NPT
    t0 = jnp.where(is_p, 0, ((i - NPT) % TPS) * TM)
    seq_len = jnp.where(is_p, SEQ, DEC_SEQ)
    first = t0 == 0
    last = t0 + TM == seq_len

    x = jnp.where(is_p, xp_ref[...], xs_ref[...])
    xe = jnp.concatenate([xprev_ref[...], x, xnext_ref[...]], axis=0)
    h = _rms_mod(xe, g_ref[...], mod_ref[0, pl.ds(r, 1), :], mod_ref[1, pl.ds(r, 1), :])
    ze = _dot(h.astype(BF16), wi_ref[...])
    row = lax.broadcasted_iota(jnp.int32, (TM + 2 * HALO, 1), 0)
    outside = jnp.logical_or(jnp.logical_and(first, row < HALO), jnp.logical_and(last, row >= HALO + TM))
    ze = jnp.where(outside, 0.0, ze)
    pext[...] = ze[:, :POOL_W]
    uext[...] = ze[:, POOL_W + 2 * CONV_W:] * ze[:, POOL_W:POOL_W + CONV_W]
    p = ze[HALO:HALO + TM, :POOL_W]
    gb = ze[HALO:HALO + TM, POOL_W + CONV_W:POOL_W + 2 * CONV_W]

    t = t0 + lax.broadcasted_iota(jnp.int32, (TM, 1), 0)
    mixed = []
    for g, w in enumerate(POOL_WINDOWS):
        cols = slice(g * POOL_GW, (g + 1) * POOL_GW)
        acc = jnp.zeros((TM, POOL_GW), F32)
        for k in range(-(w // 2), w - (w // 2)):
            acc = acc + pext[pl.ds(HALO + k, TM), cols]
        lo = jnp.maximum(t - w // 2, 0)
        hi = jnp.minimum(t + (w - 1 - w // 2), seq_len - 1)
        cnt = (hi - lo + 1).astype(F32)
        d = acc / cnt - p[:, cols]
        mixed.append((_dot(d.astype(BF16), pw_ref[g]) * ps_ref[:, cols]).astype(BF16))
    conv = (uext[pl.ds(HALO - 1, TM), :] * cw_ref[0:1, :] + uext[pl.ds(HALO, TM), :] * cw_ref[1:2, :]
            + uext[pl.ds(HALO + 1, TM), :] * cw_ref[2:3, :])
    mixed.append((gb * conv).astype(BF16))
    y = _dot(jnp.concatenate(mixed, axis=1), wo_ref[...])
    o_ref[...] = x + mod_ref[2, pl.ds(r, 1), :] * y


def _even_layer(xp, xs, mod_l, g, w_in, pool_w, pool_scale, conv_w, w_out, wg, wu, wd):
    hb = TM // HALO
    nhb = N_S // HALO
    wg2, wu2, wd2 = wg.reshape(-1, D_EXP), wu.reshape(-1, D_EXP), wd.reshape(-1, D)
    rows_in, rows_out = wg2.shape[0] // CAST_STEPS, wd2.shape[0] // CAST_STEPS
    assert rows_in * CAST_STEPS == wg2.shape[0] and rows_out * CAST_STEPS == wd2.shape[0] and CAST_STEPS <= NT
    cast_blk = lambda i: (jnp.minimum(i, CAST_STEPS - 1), 0)
    cast_specs = [pl.BlockSpec((rows_in, D_EXP), cast_blk), pl.BlockSpec((rows_in, D_EXP), cast_blk),
                  pl.BlockSpec((rows_out, D), cast_blk)]
    outs = pl.pallas_call(
        _even_kernel,
        grid=(NT,),
        in_specs=_dual_specs(TM, D, NPT) + [
            pl.BlockSpec((HALO, D), lambda i: (jnp.maximum((i - NPT) * hb - 1, 0), 0)),
            pl.BlockSpec((HALO, D), lambda i: (jnp.clip((i - NPT + 1) * hb, 0, nhb - 1), 0)),
            pl.BlockSpec((6, MOD_ROWS, D), lambda i: (0, 0, 0)),
            pl.BlockSpec((1, D), lambda i: (0, 0)),
            pl.BlockSpec((D, EVEN_IN), lambda i: (0, 0)),
            pl.BlockSpec((4, POOL_GW, POOL_GW), lambda i: (0, 0, 0)),
            pl.BlockSpec((1, POOL_W), lambda i: (0, 0)),
            pl.BlockSpec((8, CONV_W), lambda i: (0, 0)),
            pl.BlockSpec((D, D), lambda i: (0, 0)),
        ] + cast_specs,
        out_specs=[pl.BlockSpec((TM, D), lambda i: (i, 0))] + cast_specs,
        out_shape=[jax.ShapeDtypeStruct((N_TOK, D), F32), jax.ShapeDtypeStruct(wg2.shape, BF16),
                   jax.ShapeDtypeStruct(wu2.shape, BF16), jax.ShapeDtypeStruct(wd2.shape, BF16)],
        scratch_shapes=[pltpu.VMEM((TM + 2 * HALO, POOL_W), F32),
                        pltpu.VMEM((TM + 2 * HALO, CONV_W), F32)],
        compiler_params=_cparams("arbitrary"),
        name="even_layer",
    )(xp, xs, xs, xs, mod_l, g, w_in, pool_w, pool_scale, conv_w, w_out, wg2, wu2, wd2)
    return outs[0], outs[1].reshape(wg.shape), outs[2].reshape(wu.shape), outs[3].reshape(wd.shape)


def _rope(x, cos, sin_signed):
    n = x.shape[1] // 128
    cosf = jnp.concatenate([cos] * n, axis=1) if n > 1 else cos
    sinf = jnp.concatenate([sin_signed] * n, axis=1) if n > 1 else sin_signed
    w = x.shape[1]
    lane = lax.broadcasted_iota(jnp.int32, x.shape, 1)
    up = pltpu.roll(x, w - DH // 4, 1)
    dn = pltpu.roll(x, DH // 4, 1)
    rot = jnp.where((lane % (DH // 2)) < DH // 4, up, dn)
    return x * cosf + rot * sinf


def _odd_in_kernel(x_ref, mod_ref, g_ref, w_ref, cos_ref, sin_ref, *out_refs, tm, tile0, rope, kv_f32):
    r = _mod_row((pl.program_id(0) + tile0) * (tm // TM))
    h = _rms_mod(x_ref[...], g_ref[...], mod_ref[0, pl.ds(r, 1), :], mod_ref[1, pl.ds(r, 1), :])
    z = _dot(h.astype(BF16), w_ref[...])
    qna_ref, kna_ref, vna_ref, qsw_ref, ksw_ref, vsw_ref = out_refs[:6]
    c0 = 0
    qna = z[:, 0:NA_W]
    kna = z[:, NA_W:2 * NA_W]
    vna = z[:, 2 * NA_W:3 * NA_W]
    c0 = 3 * NA_W
    qsw = z[:, c0:c0 + SWQ_W]
    ksw = z[:, c0 + SWQ_W:c0 + SWQ_W + SWKV_W]
    vsw = z[:, c0 + SWQ_W + SWKV_W:]
    if kv_f32:
        for ref, val in zip(out_refs[6:10], (kna, vna, ksw, vsw)):
            heads = val.shape[1] // DH
            for hh in range(heads):
                ref[pl.ds(hh, SEQ, stride=heads), :] = val[:, hh * DH:(hh + 1) * DH]
    if rope:
        cos = cos_ref[...]
        sin = sin_ref[...]
        qsw = _rope(qsw, cos, sin)
        ksw = _rope(ksw, cos, sin)
    qna_ref[...] = (qna * QK_SCALE).astype(BF16)
    kna_ref[...] = kna.astype(BF16)
    vna_ref[...] = vna.astype(BF16)
    qsw_ref[...] = (qsw * QK_SCALE).astype(BF16)
    ksw_ref[...] = ksw.astype(BF16)
    vsw_ref[...] = vsw.astype(BF16)


def _odd_in(x, mod_l, g, w, cos, sin, *, prompt):
    tm = TM if prompt else 2 * TM
    tile0 = 0 if prompt else N_P // tm
    nt = (N_P if prompt else N_S) // tm
    n = nt * tm
    widths = [NA_W, NA_W, NA_W, SWQ_W, SWKV_W, SWKV_W]
    out_shape = [jax.ShapeDtypeStruct((n, w_), BF16) for w_ in widths]
    out_specs = [pl.BlockSpec((tm, w_), lambda i: (i, 0)) for w_ in widths]
    if prompt:
        assert tm == SEQ
        for heads in (NA_H, NA_H, SW_KV, SW_KV):
            out_shape.append(jax.ShapeDtypeStruct((BATCH * SEQ * heads, DH), F32))
            out_specs.append(pl.BlockSpec((SEQ * heads, DH), lambda i: (i, 0)))
    return pl.pallas_call(
        functools.partial(_odd_in_kernel, tm=tm, tile0=tile0, rope=not prompt, kv_f32=prompt),
        grid=(nt,),
        in_specs=[
            pl.BlockSpec((tm, D), lambda i: (i + tile0, 0)),
            pl.BlockSpec((6, MOD_ROWS, D), lambda i: (0, 0, 0)),
            pl.BlockSpec((1, D), lambda i: (0, 0)),
            pl.BlockSpec((D, ODD_IN), lambda i: (0, 0)),
            pl.BlockSpec((tm, 128), lambda i: (i % (DEC_SEQ // tm), 0)),
            pl.BlockSpec((tm, 128), lambda i: (i % (DEC_SEQ // tm), 0)),
        ],
        out_specs=out_specs,
        out_shape=out_shape,
        compiler_params=_cparams("parallel"),
        name="odd_in_prompt" if prompt else "odd_in_latent",
    )(x, mod_l, g, w, cos, sin)


def _rope_tables():
    t = np.arange(DEC_SEQ)
    quarter = DH // 4
    inv = 1.0 / (ROPE_BASE ** (np.arange(quarter, dtype=np.float64) / quarter))

    def cos_sin(pos):
        ang = pos.astype(np.float64)[:, None] * inv[None, :]
        ang = np.concatenate([ang, ang], axis=-1)
        return np.cos(ang), np.sin(ang)

    cr, sr = cos_sin(t // GRID_W)
    cc, sc = cos_sin(t % GRID_W)
    cos = np.concatenate([cr, cc], axis=-1)
    sin = np.concatenate([sr, sc], axis=-1)
    sign = np.where((np.arange(DH) % (DH // 2)) < DH // 4, -1.0, 1.0)
    sin = sin * sign[None, :]
    cos = np.concatenate([cos, cos], axis=-1).astype(np.float32)
    sin = np.concatenate([sin, sin], axis=-1).astype(np.float32)
    return jnp.asarray(cos), jnp.asarray(sin)


def _softmax_pv(segs, sink=None):
    m = None
    for s, _ in segs:
        sm = jnp.max(s, axis=-1, keepdims=True)
        m = sm if m is None else jnp.maximum(m, sm)
    if sink is not None:
        m = jnp.maximum(m, sink)
    den = None
    acc = None
    for s, v in segs:
        p = jnp.exp(s - m)
        ps = jnp.sum(p, axis=-1, keepdims=True)
        den = ps if den is None else den + ps
        pv = _dot(p.astype(BF16), v)
        acc = pv if acc is None else acc + pv
    if sink is not None:
        den = den + jnp.exp(sink - m)
    return acc / den


def _sink_col(sink_ref, g, rows_per_head):
    row = lax.broadcasted_iota(jnp.int32, (SW_G * rows_per_head, 1), 0)
    col = jnp.zeros((SW_G * rows_per_head, 1), F32)
    for r in range(SW_G):
        col = jnp.where(row // rows_per_head == r, sink_ref[g * SW_G + r], col)
    return col


def _ctx_attn_kernel(sink_ref, qna_ref, kna_ref, vna_ref, qsw_ref, ksw_ref, vsw_ref, ona_ref, osw_ref):
    lane = lax.broadcasted_iota(jnp.int32, (SEQ, 2 * DH), 1)
    for p in range(NA_H // 2):
        cols = slice(p * 2 * DH, (p + 1) * 2 * DH)
        q = qna_ref[:, cols]
        k = kna_ref[:, cols]
        v = vna_ref[:, cols]
        outs = []
        for half in range(2):
            mine = (lane < DH) if half == 0 else (lane >= DH)
            qm = jnp.where(mine, q, jnp.zeros_like(q))
            outs.append(_softmax_pv([(_dot_nt(qm, k), v)]))
        ona_ref[:, cols] = jnp.where(lane < DH, outs[0], outs[1]).astype(BF16)
    outs = []
    for g in range(SW_KV):
        kc = slice(g * DH, (g + 1) * DH)
        q = jnp.concatenate([qsw_ref[:, (g * SW_G + r) * DH:(g * SW_G + r + 1) * DH] for r in range(SW_G)], axis=0)
        s = _dot_nt(q, ksw_ref[:, kc])
        o = _softmax_pv([(s, vsw_ref[:, kc])], sink=_sink_col(sink_ref, g, SEQ))
        outs.extend(o[r * SEQ:(r + 1) * SEQ, :] for r in range(SW_G))
    osw_ref[...] = jnp.concatenate(outs, axis=1).astype(BF16)


def _ctx_attn(sink, qna, kna, vna, qsw, ksw, vsw):
    def spec(w):
        return pl.BlockSpec((SEQ, w), lambda b: (b, 0))

    return pl.pallas_call(
        _ctx_attn_kernel,
        grid=(BATCH,),
        in_specs=[pl.BlockSpec(memory_space=pltpu.SMEM), spec(NA_W), spec(NA_W), spec(NA_W),
                  spec(SWQ_W), spec(SWKV_W), spec(SWKV_W)],
        out_specs=[spec(NA_W), spec(SWQ_W)],
        out_shape=[jax.ShapeDtypeStruct((N_P, NA_W), BF16), jax.ShapeDtypeStruct((N_P, SWQ_W), BF16)],
        compiler_params=_cparams("parallel"),
        name="ctx_attn",
    )(sink, qna, kna, vna, qsw, ksw, vsw)


NA_QB = 4
NA_KR = 12
NA_NQB = ROWS // NA_QB


def _na_block_offset(case, i, j):
    if case == 0:
        valid, dr = j < NA_ROWS, j - i + NA_ROWS - 1
    elif case == 1:
        valid, dr = i <= j < i + NA_ROWS, j - i + NA_ROWS // 2 - 1
    else:
        valid, dr = j >= NA_KR - NA_ROWS, j - i + NA_ROWS - 1 - (NA_KR - NA_QB)
    return dr if valid else None


def _na_bias_kernel(rpb_ref, o_ref):
    h = pl.program_id(0)
    cq = lax.broadcasted_iota(jnp.int32, (GRID_W, GRID_W), 0)
    ck = lax.broadcasted_iota(jnp.int32, (GRID_W, GRID_W), 1)
    cstart = jnp.clip(cq - NA_COLS // 2, 0, GRID_W - NA_COLS)
    ok = (ck >= cstart) & (ck < cstart + NA_COLS)
    dc = jnp.clip(ck - cq + NA_COLS - 1, 0, 2 * NA_COLS - 2)
    ndc = 2 * NA_COLS - 1
    neg = jnp.full((GRID_W, GRID_W), NEG, F32)
    by_offset = []
    for dr in range(2 * NA_ROWS - 1):
        b = jnp.zeros((GRID_W, GRID_W), F32)
        for e in range(ndc):
            b = jnp.where(dc == e, rpb_ref[h, dr * ndc + e], b)
        by_offset.append(jnp.where(ok, b, NEG))
    for case in range(3):
        for i in range(NA_QB):
            blocks = []
            for j in range(NA_KR):
                dr = _na_block_offset(case, i, j)
                blocks.append(neg if dr is None else by_offset[dr])
            o_ref[case, i * GRID_W:(i + 1) * GRID_W, :] = jnp.concatenate(blocks, axis=1)


def _na_bias(rpb):
    nd = 2 * NA_ROWS - 1
    return pl.pallas_call(
        _na_bias_kernel,
        grid=(NA_H,),
        in_specs=[pl.BlockSpec(memory_space=pltpu.SMEM)],
        out_specs=pl.BlockSpec((3, None, NA_QB * GRID_W, NA_KR * GRID_W), lambda h: (0, h, 0, 0)),
        out_shape=jax.ShapeDtypeStruct((3, NA_H, NA_QB * GRID_W, NA_KR * GRID_W), F32),
        compiler_params=_cparams("parallel"),
        name="na_bias",
    )(rpb.reshape(NA_H, nd * (2 * NA_COLS - 1)))


def _na_kernel(q_ref, k_ref, v_ref, kc_ref, vc_ref, bias_ref, o_ref):
    r0 = pl.program_id(1) * NA_QB
    start = pl.multiple_of(jnp.clip(r0 - NA_ROWS // 2, 0, ROWS - NA_KR) * GRID_W, GRID_W)
    nq = NA_QB * GRID_W
    lane = lax.broadcasted_iota(jnp.int32, (nq, 2 * DH), 1)
    for p in range(NA_H // 2):
        cols = slice(p * 2 * DH, (p + 1) * 2 * DH)
        q = q_ref[:, cols]
        kl = k_ref[pl.ds(start, NA_KR * GRID_W), cols]
        vl = v_ref[pl.ds(start, NA_KR * GRID_W), cols]
        kc = kc_ref[:, cols]
        vc = vc_ref[:, cols]
        zero = jnp.zeros_like(q)
        qm = jnp.concatenate([jnp.where(lane < DH, q, zero), jnp.where(lane < DH, zero, q)], axis=0)
        s_loc = _dot_nt(qm, kl)
        s_ctx = _dot_nt(qm, kc)
        outs = []
        for half in range(2):
            rows = slice(half * nq, (half + 1) * nq)
            outs.append(_softmax_pv([(s_loc[rows] + bias_ref[2 * p + half], vl), (s_ctx[rows], vc)]))
        o_ref[:, cols] = jnp.where(lane < DH, outs[0], outs[1]).astype(BF16)


def _na_attn(q, k, v, kc, vc, bias):
    nq = NA_QB * GRID_W

    def bias_case(b, rb):
        return (jnp.where(rb == 0, 0, jnp.where(rb == NA_NQB - 1, 2, 1)), 0, 0, 0)

    return pl.pallas_call(
        _na_kernel,
        grid=(DEC_BATCH, NA_NQB),
        in_specs=[
            pl.BlockSpec((nq, NA_W), lambda b, rb: (b * NA_NQB + rb, 0)),
            pl.BlockSpec((DEC_SEQ, NA_W), lambda b, rb: (b, 0)),
            pl.BlockSpec((DEC_SEQ, NA_W), lambda b, rb: (b, 0)),
            pl.BlockSpec((None, PAST, NA_W), lambda b, rb: (b, 0, 0)),
            pl.BlockSpec((None, PAST, NA_W), lambda b, rb: (b, 0, 0)),
            pl.BlockSpec((None, NA_H, nq, NA_KR * GRID_W), bias_case),
        ],
        out_specs=pl.BlockSpec((nq, NA_W), lambda b, rb: (b * NA_NQB + rb, 0)),
        out_shape=jax.ShapeDtypeStruct((N_S, NA_W), BF16),
        compiler_params=_cparams("parallel", "arbitrary"),
        name="na_attn",
    )(q, k, v, kc, vc, bias)


def _sw_window_bias():
    q = np.arange(SW_G * ABLK)[:, None] % ABLK
    k = np.arange(3 * ABLK)[None, :]
    tables = [np.where(np.abs(q + lead - k) <= SW_WIN, 0.0, NEG) for lead in (0, ABLK, 2 * ABLK)]
    return jnp.asarray(np.stack(tables).astype(np.float32))


def _sw_kernel(sink_ref, q_ref, k_ref, v_ref, kc_ref, vc_ref, wb_ref, o_ref):
    j = pl.program_id(1)
    nk = 3 * ABLK
    start = pl.multiple_of(jnp.clip((j - 1) * ABLK, 0, DEC_SEQ - nk), ABLK)
    outs = []
    for g in range(SW_KV):
        kcols = slice(g * DH, (g + 1) * DH)
        q = jnp.concatenate([q_ref[:, (g * SW_G + r) * DH:(g * SW_G + r + 1) * DH] for r in range(SW_G)], axis=0)
        kw = k_ref[pl.ds(start, nk), kcols]
        vw = v_ref[pl.ds(start, nk), kcols]
        s_w = _dot_nt(q, kw) + wb_ref[...]
        s_c = _dot_nt(q, kc_ref[:, kcols])
        o = _softmax_pv([(s_w, vw), (s_c, vc_ref[:, kcols])], sink=_sink_col(sink_ref, g, ABLK))
        outs.extend(o[r * ABLK:(r + 1) * ABLK, :] for r in range(SW_G))
    o_ref[...] = jnp.concatenate(outs, axis=1).astype(BF16)


def _sw_attn(sink, q, k, v, kc, vc):
    nb = DEC_SEQ // ABLK
    return pl.pallas_call(
        _sw_kernel,
        grid=(DEC_BATCH, nb),
        in_specs=[
            pl.BlockSpec(memory_space=pltpu.SMEM),
            pl.BlockSpec((ABLK, SWQ_W), lambda b, j: (b * nb + j, 0)),
            pl.BlockSpec((DEC_SEQ, SWKV_W), lambda b, j: (b, 0)),
            pl.BlockSpec((DEC_SEQ, SWKV_W), lambda b, j: (b, 0)),
            pl.BlockSpec((None, PAST, SWKV_W), lambda b, j: (b, 0, 0)),
            pl.BlockSpec((None, PAST, SWKV_W), lambda b, j: (b, 0, 0)),
            pl.BlockSpec((None, SW_G * ABLK, 3 * ABLK),
                         lambda b, j: (jnp.where(j == 0, 0, jnp.where(j == nb - 1, 2, 1)), 0, 0)),
        ],
        out_specs=pl.BlockSpec((ABLK, SWQ_W), lambda b, j: (b * nb + j, 0)),
        out_shape=jax.ShapeDtypeStruct((N_S, SWQ_W), BF16),
        compiler_params=_cparams("parallel", "arbitrary"),
        name="sw_attn",
    )(sink, q, k, v, kc, vc, _sw_window_bias())


BT = 4096
NB = N_TOK // BT
TPB = BT // TM
CH = 256
CH_TAIL = 128
XPS = 2
NES = N_EXP // XPS
XO_ROWS = 2 * BT + CH
ROW = 8
assert D == ROW * 128


def _route_sparse(probs, sel):
    rank = []
    score = []
    for g in range(N_GRP):
        ids = range(g * EPG, (g + 1) * EPG)
        tot = None
        for e in ids:
            rk = jnp.zeros_like(sel[e])
            for j in ids:
                if j < e:
                    rk = rk + (sel[j] >= sel[e]).astype(F32)
                elif j > e:
                    rk = rk + (sel[j] > sel[e]).astype(F32)
            rank.append(rk)
            contrib = jnp.where(rk < 2.0, sel[e], 0.0)
            tot = contrib if tot is None else tot + contrib
        score.append(tot)
    top1, top2 = [], []
    for g in range(N_GRP):
        best = None
        for j in range(N_GRP):
            if j == g:
                continue
            c = (score[g] > score[j]) if j < g else (score[g] >= score[j])
            best = c if best is None else jnp.logical_and(best, c)
        for e in range(g * EPG, (g + 1) * EPG):
            top1.append(jnp.logical_and(best, rank[e] == 0.0))
            top2.append(jnp.logical_and(best, rank[e] == 1.0))
    return top1, top2


def _pick(masks, rows):
    acc = None
    for m, r in zip(masks, rows):
        v = jnp.where(m, r, 0.0)
        acc = v if acc is None else acc + v
    return acc


def _route_tile(xv, mod_ref, g_ref, rw_ref, rb_ref, h_ref, pos_ref, w_ref, cnt_ref, off_ref, meta, carry):
    i = pl.program_id(0)
    j = i % TPB
    r = _mod_row(i)
    h = _rms_mod(xv, g_ref[...], mod_ref[3, pl.ds(r, 1), :], mod_ref[4, pl.ds(r, 1), :])
    h_ref[...] = h.astype(BF16)
    h_hi = h.astype(BF16)
    h_lo = (h - h_hi.astype(F32)).astype(BF16)
    logits = _dot(h_lo, rw_ref[0]) + _dot(h_hi, rw_ref[1]) + _dot(h_hi, rw_ref[0])
    lt = logits.T[:N_EXP, :]
    m = jnp.max(lt, axis=0, keepdims=True)
    ex = jnp.exp(lt - m)
    pr = ex / jnp.sum(ex, axis=0, keepdims=True)
    se = pr + rb_ref[:N_EXP, :]
    probs = [pr[e:e + 1, :] for e in range(N_EXP)]
    sel = [se[e:e + 1, :] for e in range(N_EXP)]
    top1, top2 = _route_sparse(probs, sel)

    @pl.when(j == 0)
    def _():
        carry[...] = jnp.zeros_like(carry)

    member = jnp.concatenate([jnp.logical_or(a, b).astype(F32) for a, b in zip(top1, top2)], axis=0)
    s_idx = lax.broadcasted_iota(jnp.int32, (TM, TM), 0)
    t_idx = lax.broadcasted_iota(jnp.int32, (TM, TM), 1)
    before = jnp.where(s_idx < t_idx, 1.0, 0.0).astype(BF16)
    seen = _dot(member.astype(BF16), before) + carry[:, 0:1]
    seen_rows = [seen[e:e + 1, :] for e in range(N_EXP)]
    ids = [jnp.full((1, TM), float(e), F32) for e in range(N_EXP)]
    p1 = _pick(top1, probs)
    p2 = _pick(top2, probs)
    den = p1 + p2
    w_ref[j, 0:1, :] = p1 / den
    w_ref[j, 1:2, :] = p2 / den
    meta[j, 0:1, :] = _pick(top1, ids)
    meta[j, 1:2, :] = _pick(top2, ids)
    meta[j, 2:3, :] = _pick(top1, seen_rows)
    meta[j, 3:4, :] = _pick(top2, seen_rows)
    carry[...] = carry[...] + jnp.sum(member, axis=1, keepdims=True)

    @pl.when(j == TPB - 1)
    def _():
        cnt = carry[...]
        offs = [jnp.zeros((1, 128), F32)]
        for e in range(1, N_EXP):
            offs.append(offs[-1] + cnt[e - 1:e, :])
        cnt_ref[...] = cnt
        off_ref[...] = jnp.concatenate(offs, axis=0)
        for jj in range(TPB):
            for k in range(2):
                eid = meta[jj, k:k + 1, :]
                pos = meta[jj, 2 + k:3 + k, :]
                for e in range(1, N_EXP):
                    pos = pos + jnp.where(eid == float(e), offs[e][:, 0:1], 0.0)
                pos_ref[jj, k:k + 1, :] = (pos * float(ROW)).astype(jnp.int32)


def _router_kernel(x_ref, *refs):
    _route_tile(x_ref[...], *refs)


def _odd_out_router_kernel(nap_ref, nas_ref, swp_ref, sws_ref, x_ref, wo_ref, mod_ref, g_ref, rw_ref, rb_ref,
                           x1_ref, *refs):
    i = pl.program_id(0)
    is_p = i < NPT
    ona = jnp.where(is_p, nap_ref[...], nas_ref[...])
    osw = jnp.where(is_p, swp_ref[...], sws_ref[...])
    y = _dot(ona, wo_ref[:NA_W, :]) + _dot(osw, wo_ref[NA_W:, :])
    x1 = x_ref[...] + mod_ref[2, pl.ds(_mod_row(i), 1), :] * y
    x1_ref[...] = x1
    _route_tile(x1, mod_ref, g_ref, rw_ref, rb_ref, *refs)


def _router(x, mod_l, g, rw2, rb_col, attn=None):
    blk = lambda i: (i // TPB, 0, 0, 0)
    in_specs = [
        pl.BlockSpec((TM, D), lambda i: (i, 0)),
        pl.BlockSpec((6, MOD_ROWS, D), lambda i: (0, 0, 0)),
        pl.BlockSpec((1, D), lambda i: (0, 0)),
        pl.BlockSpec((2, D, 128), lambda i: (0, 0, 0)),
        pl.BlockSpec((128, 1), lambda i: (0, 0)),
    ]
    out_specs = [
        pl.BlockSpec((TM, D), lambda i: (i, 0)),
        pl.BlockSpec((None, TPB, 2, TM), blk),
        pl.BlockSpec((None, TPB, 2, TM), blk),
        pl.BlockSpec((None, N_EXP, 128), lambda i: (i // TPB, 0, 0)),
        pl.BlockSpec((None, N_EXP, 128), lambda i: (i // TPB, 0, 0)),
    ]
    out_shape = [
        jax.ShapeDtypeStruct((N_TOK, D), BF16),
        jax.ShapeDtypeStruct((NB, TPB, 2, TM), jnp.int32),
        jax.ShapeDtypeStruct((NB, TPB, 2, TM), F32),
        jax.ShapeDtypeStruct((NB, N_EXP, 128), F32),
        jax.ShapeDtypeStruct((NB, N_EXP, 128), F32),
    ]
    args = (x, mod_l, g, rw2, rb_col)
    body = _router_kernel
    if attn is not None:
        nap, nas, swp, sws, w_out = attn
        in_specs = (_dual_specs(TM, NA_W, NPT) + _dual_specs(TM, SWQ_W, NPT) + in_specs[:1]
                    + [pl.BlockSpec((D, D), lambda i: (0, 0))] + in_specs[1:])
        out_specs = [pl.BlockSpec((TM, D), lambda i: (i, 0))] + out_specs
        out_shape = [jax.ShapeDtypeStruct((N_TOK, D), F32)] + out_shape
        args = (nap, nas, swp, sws, x, w_out, mod_l, g, rw2, rb_col)
        body = _odd_out_router_kernel
    return pl.pallas_call(
        body,
        grid=(NT,),
        in_specs=in_specs,
        out_specs=out_specs,
        out_shape=out_shape,
        scratch_shapes=[pltpu.VMEM((TPB, 4, TM), F32), pltpu.VMEM((N_EXP, 128), F32)],
        compiler_params=_cparams("arbitrary"),
        name="router" if attn is None else "odd_out_router",
    )(*args)


def _row(p):
    return pl.ds(pl.multiple_of(p, ROW), ROW)


def _expert_segment(n, base, xo, wg_ref, wu_ref, wd_ref):
    def ffn_rows(row0, rows):
        x = jnp.concatenate([xo[pl.ds(row0 * ROW + c, rows, stride=ROW), :] for c in range(ROW)], axis=1)
        xb = x.astype(BF16)
        a = _dot(xb, wg_ref[...])
        u = _dot(xb, wu_ref[...])
        hid = (a / (1.0 + jnp.exp(-a))) * u
        out = _dot(hid.astype(BF16), wd_ref[...])
        valid = lax.broadcasted_iota(jnp.int32, (rows, 1), 0) < base + n - row0
        res = jnp.where(valid, out, x)
        for c in range(ROW):
            xo[pl.ds(row0 * ROW + c, rows, stride=ROW), :] = res[:, c * 128:(c + 1) * 128]

    def chunk(jc, carry):
        ffn_rows(base + jc * CH, CH)
        return carry

    nfull = (n + CH - CH_TAIL - 1) // CH
    lax.fori_loop(0, nfull, chunk, 0)

    @pl.when(n > nfull * CH)
    def _():
        ffn_rows(base + nfull * CH, CH_TAIL)


def _experts_kernel(cnt_ref, off_ref, pos_ref, w_ref, h_ref, wg_ref, wu_ref, wd_ref, x_ref, mod_ref, fg_ref,
                    *rest, final):
    if final:
        op_ref, os_ref, xo, stg = rest
    else:
        o_ref, xo, stg = rest
    b = pl.program_id(0)
    s = pl.program_id(1)

    @pl.when(s == 0)
    def _():
        xo[pl.ds(2 * BT * ROW, CH * ROW), :] = jnp.zeros((CH * ROW, 128), F32)

    @pl.when(s < TPB)
    def _():
        hs = h_ref[...].astype(F32)
        for c in range(ROW):
            stg[pl.ds(c, TM, stride=ROW), :] = hs[:, c * 128:(c + 1) * 128]
        for t in range(TM):
            v = stg[t * ROW:(t + 1) * ROW, :]
            xo[_row(pos_ref[0, t]), :] = v
            xo[_row(pos_ref[1, t]), :] = v

    @pl.when(jnp.logical_and(s >= TPB, s < TPB + NES))
    def _():
        for k in range(XPS):
            e = (s - TPB) * XPS + k
            _expert_segment(cnt_ref[b, e], off_ref[b, e], xo, wg_ref.at[k], wu_ref.at[k], wd_ref.at[k])

    @pl.when(s >= TPB + NES - 1)
    def _():
        j = s - (TPB + NES - 1)
        tile = b * TPB + j
        g2 = mod_ref[5, pl.ds(_mod_row(tile), 1), :]
        for t in range(TM):
            a = xo[_row(pos_ref[0, t]), :]
            u = xo[_row(pos_ref[1, t]), :]
            stg[t * ROW:(t + 1) * ROW, :] = w_ref[0, t] * a + w_ref[1, t] * u
        y = jnp.concatenate([stg[pl.ds(c, TM, stride=ROW), :] for c in range(ROW)], axis=1)
        res = x_ref[...] + g2 * y
        if final:
            ms = jnp.mean(res * res, axis=-1, keepdims=True)
            res = res * lax.rsqrt(ms + EPS) * fg_ref[...]

            @pl.when(tile < NPT)
            def _():
                op_ref[...] = res

            @pl.when(tile >= NPT)
            def _():
                os_ref[...] = res
        else:
            o_ref[...] = res


def _experts(cnt, off, pos, wts, h, wg, wu, wd, x, mod_l, fg, *, layer, final):
    def tile_of(b, s):
        return b * TPB + jnp.clip(s - (TPB + NES - 1), 0, TPB - 1)

    def expert_of(b, s, *_):
        return (layer, jnp.clip(s - TPB, 0, NES - 1), 0, 0)

    def smem_tile(b, s, *_):
        return (b, jnp.where(s < TPB, s, jnp.clip(s - (TPB + NES - 1), 0, TPB - 1)), 0, 0)

    smem_blk = pl.BlockSpec((None, None, 2, TM), smem_tile, memory_space=pltpu.SMEM)
    if final:
        out_specs = [pl.BlockSpec((TM, D), lambda b, s, *_: (jnp.minimum(tile_of(b, s), NPT - 1), 0)),
                     pl.BlockSpec((TM, D), lambda b, s, *_: (jnp.maximum(tile_of(b, s) - NPT, 0), 0))]
        out_shape = [jax.ShapeDtypeStruct((N_P, D), F32), jax.ShapeDtypeStruct((N_S, D), F32)]
    else:
        out_specs = pl.BlockSpec((TM, D), lambda b, s, *_: (tile_of(b, s), 0))
        out_shape = jax.ShapeDtypeStruct((N_TOK, D), F32)
    grid_spec = pltpu.PrefetchScalarGridSpec(
        num_scalar_prefetch=2,
        grid=(NB, 2 * TPB + NES - 1),
        in_specs=[
            smem_blk,
            smem_blk,
            pl.BlockSpec((TM, D), lambda b, s, *_: (b * TPB + jnp.minimum(s, TPB - 1), 0)),
            pl.BlockSpec((None, XPS, D, D_EXP), expert_of),
            pl.BlockSpec((None, XPS, D, D_EXP), expert_of),
            pl.BlockSpec((None, XPS, D_EXP, D), expert_of),
            pl.BlockSpec((TM, D), lambda b, s, *_: (tile_of(b, s), 0)),
            pl.BlockSpec((6, MOD_ROWS, D), lambda b, s, *_: (0, 0, 0)),
            pl.BlockSpec((1, D), lambda b, s, *_: (0, 0)),
        ],
        out_specs=out_specs,
        scratch_shapes=[pltpu.VMEM((XO_ROWS * ROW, 128), F32), pltpu.VMEM((TM * ROW, 128), F32)],
    )
    return pl.pallas_call(
        functools.partial(_experts_kernel, final=final),
        grid_spec=grid_spec,
        out_shape=out_shape,
        compiler_params=_cparams("arbitrary", "arbitrary"),
        name="experts_final" if final else "experts",
    )(cnt, off, pos, wts, h, wg, wu, wd, x, mod_l, fg)


def _moe_sparse(x, routed, mod_l, wg, wu, wd, fg, *, layer, final):
    h, pos, wts, cnt, off = routed
    cnt = cnt[:, :, 0].astype(jnp.int32)
    off = off[:, :, 0].astype(jnp.int32)
    return _experts(cnt, off, pos, wts, h, wg, wu, wd, x, mod_l, fg, layer=layer, final=final)


def kernel(x_prompt, x_sample, cache_na_k, cache_na_v, cache_sw_k, cache_sw_v, c, c_ctx, mod_w, mod_b, norm_mix_g, norm_ffn_g, ev_w_in, ev_pool_w, ev_pool_scale, ev_conv_w, ev_w_out, od_w_in, od_rpb, od_sink, od_w_out, router_w, router_b, moe_w_gate, moe_w_up, moe_w_down, final_norm_g):
    xp = x_prompt.reshape(N_P, D)
    xs = x_sample.reshape(N_S, D)
    cvec = jnp.concatenate([c_ctx[None, :], c, jnp.zeros((MOD_ROWS - 1 - DEC_BATCH, D), F32)], axis=0)
    mod = _modulation(cvec, mod_w, mod_b)

    rw_pad = jnp.pad(router_w, ((0, 0), (0, 128 - N_EXP)))
    rw_hi = rw_pad.astype(BF16)
    rw2 = jnp.stack([rw_hi, (rw_pad - rw_hi.astype(F32)).astype(BF16)])
    rb_col = jnp.pad(router_b, (0, 128 - N_EXP)).reshape(128, 1)

    conv_w = jnp.pad(ev_conv_w[0], ((0, 8 - ev_conv_w.shape[1]), (0, 0)))
    x, wg, wu, wd = _even_layer(xp, xs, mod[0], norm_mix_g[0:1], ev_w_in[0].astype(BF16),
                                ev_pool_w[0].astype(BF16), ev_pool_scale[0:1], conv_w, ev_w_out[0].astype(BF16),
                                moe_w_gate, moe_w_up, moe_w_down)
    fg = final_norm_g.reshape(1, D)
    routed = _router(x, mod[0], norm_ffn_g[0:1], rw2, rb_col)
    x = _moe_sparse(x, routed, mod[0], wg, wu, wd, fg, layer=0, final=False)

    cos, sin = _rope_tables()
    w_in = od_w_in[0].astype(BF16)
    g1 = norm_mix_g[1:2]
    qna_p, kna_p, vna_p, qsw_p, ksw_p, vsw_p, nak, nav, swk, swv = _odd_in(x, mod[1], g1, w_in, cos, sin, prompt=True)
    qna_s, kna_s, vna_s, qsw_s, ksw_s, vsw_s = _odd_in(x, mod[1], g1, w_in, cos, sin, prompt=False)
    sink = od_sink[0]
    ona_p, osw_p = _ctx_attn(sink, qna_p, kna_p, vna_p, qsw_p, ksw_p, vsw_p)
    bias = _na_bias(od_rpb[0])
    ona_s = _na_attn(qna_s, kna_s, vna_s,
                     cache_na_k[:, 0].reshape(DEC_BATCH, PAST, NA_W).astype(BF16),
                     cache_na_v[:, 0].reshape(DEC_BATCH, PAST, NA_W).astype(BF16), bias)
    osw_s = _sw_attn(sink, qsw_s, ksw_s, vsw_s,
                     cache_sw_k[:, 0].reshape(DEC_BATCH, PAST, SWKV_W).astype(BF16),
                     cache_sw_v[:, 0].reshape(DEC_BATCH, PAST, SWKV_W).astype(BF16))
    x, *routed = _router(x, mod[1], norm_ffn_g[1:2], rw2, rb_col,
                         attn=(ona_p, ona_s, osw_p, osw_s, od_w_out[0].astype(BF16)))
    y_prompt, y_sample = _moe_sparse(x, routed, mod[1], wg, wu, wd, fg, layer=1, final=True)
    y_prompt = y_prompt.reshape(BATCH, SEQ, D)
    y_sample = y_sample.reshape(DEC_BATCH, DEC_SEQ, D)
    new_na_k = nak.reshape(BATCH, 1, SEQ, NA_H, DH)
    new_na_v = nav.reshape(BATCH, 1, SEQ, NA_H, DH)
    new_sw_k = swk.reshape(BATCH, 1, SEQ, SW_KV, DH)
    new_sw_v = swv.reshape(BATCH, 1, SEQ, SW_KV, DH)
    return (y_prompt, y_sample, new_na_k, new_na_v, new_sw_k, new_sw_v)
```

```python
import functools

import jax
import jax.numpy as jnp
import numpy as np
from jax import lax
from jax.experimental import pallas as pl
from jax.experimental.pallas import tpu as pltpu

D = 1024
BATCH = 16
SEQ = 256
DEC_BATCH = 4
DEC_SEQ = 4096
PAST = 512
GRID_W = 64
ROWS = DEC_SEQ // GRID_W
DH = 64
POOL_W = 512
POOL_WINDOWS = (2, 4, 8, 16)
POOL_GW = 128
CONV_W = 512
EVEN_IN = POOL_W + 3 * CONV_W
NA_H = 8
NA_ROWS = 8
NA_COLS = 16
SW_H = 8
SW_KV = 2
SW_G = SW_H // SW_KV
SW_WIN = 128
ABLK = 128
NA_W = NA_H * DH
SWQ_W = SW_H * DH
SWKV_W = SW_KV * DH
ODD_IN = 3 * NA_W + SWQ_W + 2 * SWKV_W
N_EXP = 16
N_GRP = 4
EPG = 4
D_EXP = 512
EPS = 1e-6
NEG = -1e30
ROPE_BASE = 10000.0
QK_SCALE = DH ** -0.5
assert QK_SCALE == 0.125

N_P = BATCH * SEQ
N_S = DEC_BATCH * DEC_SEQ
N_TOK = N_P + N_S
MOD_ROWS = 8

TM = 256
NPT = N_P // TM
TPS = DEC_SEQ // TM
NT = N_TOK // TM
HALO = 8
EV_HALVES = 2
EV_TM = EV_HALVES * TM
CAST_STEPS = 32

F32 = jnp.float32
BF16 = jnp.bfloat16
VMEM_LIMIT = 56 * 1024 * 1024


def _cparams(*sem):
    return pltpu.CompilerParams(dimension_semantics=sem, vmem_limit_bytes=VMEM_LIMIT)


def _mod_row(i):
    return jnp.where(i < NPT, 0, 1 + (i - NPT) // TPS)


def _rms_mod(x, g, shift, scale):
    ms = jnp.mean(x * x, axis=-1, keepdims=True)
    y = x * lax.rsqrt(ms + EPS) * g
    return y * (1.0 + scale) + shift


def _dot(a, b):
    return jnp.dot(a, b, preferred_element_type=F32)


def _dot_nt(a, b):
    return lax.dot_general(a, b, (((1,), (1,)), ((), ())), preferred_element_type=F32)


def _mod_kernel(cv_ref, w_ref, b_ref, o_ref):
    cv = cv_ref[...]
    a = cv / (1.0 + jnp.exp(-cv))
    o_ref[...] = jnp.dot(a, w_ref[...], preferred_element_type=F32,
                         precision=lax.Precision.HIGHEST) + b_ref[...]


def _modulation(cvec, mod_w, mod_b):
    depth = mod_w.shape[0]
    return pl.pallas_call(
        _mod_kernel,
        grid=(depth, 6),
        in_specs=[
            pl.BlockSpec((MOD_ROWS, D), lambda l, j: (0, 0)),
            pl.BlockSpec((None, D, D), lambda l, j: (l, 0, j)),
            pl.BlockSpec((None, None, 1, D), lambda l, j: (l, j, 0, 0)),
        ],
        out_specs=pl.BlockSpec((None, None, MOD_ROWS, D), lambda l, j: (l, j, 0, 0)),
        out_shape=jax.ShapeDtypeStruct((depth, 6, MOD_ROWS, D), F32),
        compiler_params=_cparams("arbitrary", "arbitrary"),
        name="modulation",
    )(cvec, mod_w, mod_b.reshape(depth, 6, 1, D))


def _dual_specs(tm, width, npt):
    return [
        pl.BlockSpec((tm, width), lambda i: (jnp.minimum(i, npt - 1), 0)),
        pl.BlockSpec((tm, width), lambda i: (jnp.maximum(i - npt, 0), 0)),
    ]


def _even_kernel(xp_ref, xs_ref, xprev_ref, xnext_ref, mod_ref, g_ref, wi_ref, pw_ref, ps_ref, cw_ref,
                 wo_ref, wgf_ref, wuf_ref, wdf_ref, o_ref, wgb_ref, wub_ref, wdb_ref, pext, uext):
    i = pl.program_id(0)

    @pl.when(i < CAST_STEPS)
    def _():
        wgb_ref[...] = wgf_ref[...].astype(BF16)
        wub_ref[...] = wuf_ref[...].astype(BF16)
        wdb_ref[...] = wdf_ref[...].astype(BF16)

    npt = NPT // EV_HALVES
    r = _mod_row(i * EV_HALVES)
    is_p = i < npt
    seq_len = jnp.where(is_p, SEQ, DEC_SEQ)

    x = jnp.where(is_p, xp_ref[...], xs_ref[...])
    xe = jnp.concatenate([xprev_ref[...], x, xnext_ref[...]], axis=0)
    h = _rms_mod(xe, g_ref[...], mod_ref[0, pl.ds(r, 1), :], mod_ref[1, pl.ds(r, 1), :])
    ze_all = _dot(h.astype(BF16), wi_ref[...])
    row = lax.broadcasted_iota(jnp.int32, (TM + 2 * HALO, 1), 0)
    for hf in range(EV_HALVES):
        t0 = jnp.where(is_p, 0, ((i - npt) % (DEC_SEQ // EV_TM)) * EV_TM + hf * TM)
        first = t0 == 0
        last = t0 + TM == seq_len
        ze = ze_all[hf * TM:hf * TM + TM + 2 * HALO, :]
        outside = jnp.logical_or(jnp.logical_and(first, row < HALO), jnp.logical_and(last, row >= HALO + TM))
        ze = jnp.where(outside, 0.0, ze)
        pext[...] = ze[:, :POOL_W]
        uext[...] = ze[:, POOL_W + 2 * CONV_W:] * ze[:, POOL_W:POOL_W + CONV_W]
        p = ze[HALO:HALO + TM, :POOL_W]
        gb = ze[HALO:HALO + TM, POOL_W + CONV_W:POOL_W + 2 * CONV_W]

        t = t0 + lax.broadcasted_iota(jnp.int32, (TM, 1), 0)
        mixed = []
        for g, w in enumerate(POOL_WINDOWS):
            cols = slice(g * POOL_GW, (g + 1) * POOL_GW)
            acc = jnp.zeros((TM, POOL_GW), F32)
            for k in range(-(w // 2), w - (w // 2)):
                acc = acc + pext[pl.ds(HALO + k, TM), cols]
            lo = jnp.maximum(t - w // 2, 0)
            hi = jnp.minimum(t + (w - 1 - w // 2), seq_len - 1)
            cnt = (hi - lo + 1).astype(F32)
            d = acc / cnt - p[:, cols]
            mixed.append((_dot(d.astype(BF16), pw_ref[g]) * ps_ref[:, cols]).astype(BF16))
        conv = (uext[pl.ds(HALO - 1, TM), :] * cw_ref[0:1, :] + uext[pl.ds(HALO, TM), :] * cw_ref[1:2, :]
                + uext[pl.ds(HALO + 1, TM), :] * cw_ref[2:3, :])
        mixed.append((gb * conv).astype(BF16))
        y = _dot(jnp.concatenate(mixed, axis=1), wo_ref[...])
        rows = slice(hf * TM, (hf + 1) * TM)
        o_ref[rows, :] = x[rows, :] + mod_ref[2, pl.ds(r, 1), :] * y


def _even_layer(xp, xs, mod_l, g, w_in, pool_w, pool_scale, conv_w, w_out, wg, wu, wd):
    hb = EV_TM // HALO
    nhb = N_S // HALO
    npt = N_P // EV_TM
    nt = N_TOK // EV_TM
    wg2, wu2, wd2 = wg.reshape(-1, D_EXP), wu.reshape(-1, D_EXP), wd.reshape(-1, D)
    rows_in, rows_out = wg2.shape[0] // CAST_STEPS, wd2.shape[0] // CAST_STEPS
    assert rows_in * CAST_STEPS == wg2.shape[0] and rows_out * CAST_STEPS == wd2.shape[0] and CAST_STEPS <= nt
    cast_blk = lambda i: (jnp.minimum(i, CAST_STEPS - 1), 0)
    cast_specs = [pl.BlockSpec((rows_in, D_EXP), cast_blk), pl.BlockSpec((rows_in, D_EXP), cast_blk),
                  pl.BlockSpec((rows_out, D), cast_blk)]
    outs = pl.pallas_call(
        _even_kernel,
        grid=(nt,),
        in_specs=_dual_specs(EV_TM, D, npt) + [
            pl.BlockSpec((HALO, D), lambda i: (jnp.maximum((i - npt) * hb - 1, 0), 0)),
            pl.BlockSpec((HALO, D), lambda i: (jnp.clip((i - npt + 1) * hb, 0, nhb - 1), 0)),
            pl.BlockSpec((6, MOD_ROWS, D), lambda i: (0, 0, 0)),
            pl.BlockSpec((1, D), lambda i: (0, 0)),
            pl.BlockSpec((D, EVEN_IN), lambda i: (0, 0)),
            pl.BlockSpec((4, POOL_GW, POOL_GW), lambda i: (0, 0, 0)),
            pl.BlockSpec((1, POOL_W), lambda i: (0, 0)),
            pl.BlockSpec((8, CONV_W), lambda i: (0, 0)),
            pl.BlockSpec((D, D), lambda i: (0, 0)),
        ] + cast_specs,
        out_specs=[pl.BlockSpec((EV_TM, D), lambda i: (i, 0))] + cast_specs,
        out_shape=[jax.ShapeDtypeStruct((N_TOK, D), F32), jax.ShapeDtypeStruct(wg2.shape, BF16),
                   jax.ShapeDtypeStruct(wu2.shape, BF16), jax.ShapeDtypeStruct(wd2.shape, BF16)],
        scratch_shapes=[pltpu.VMEM((TM + 2 * HALO, POOL_W), F32),
                        pltpu.VMEM((TM + 2 * HALO, CONV_W), F32)],
        compiler_params=_cparams("arbitrary"),
        name="even_layer",
    )(xp, xs, xs, xs, mod_l, g, w_in, pool_w, pool_scale, conv_w, w_out, wg2, wu2, wd2)
    return outs[0], outs[1].reshape(wg.shape), outs[2].reshape(wu.shape), outs[3].reshape(wd.shape)


def _rope(x, cos, sin_signed):
    n = x.shape[1] // 128
    cosf = jnp.concatenate([cos] * n, axis=1) if n > 1 else cos
    sinf = jnp.concatenate([sin_signed] * n, axis=1) if n > 1 else sin_signed
    w = x.shape[1]
    lane = lax.broadcasted_iota(jnp.int32, x.shape, 1)
    up = pltpu.roll(x, w - DH // 4, 1)
    dn = pltpu.roll(x, DH // 4, 1)
    rot = jnp.where((lane % (DH // 2)) < DH // 4, up, dn)
    return x * cosf + rot * sinf


def _odd_in_kernel(x_ref, mod_ref, g_ref, w_ref, cos_ref, sin_ref, *out_refs, tm, tile0, rope, kv_f32):
    r = _mod_row((pl.program_id(0) + tile0) * (tm // TM))
    h = _rms_mod(x_ref[...], g_ref[...], mod_ref[0, pl.ds(r, 1), :], mod_ref[1, pl.ds(r, 1), :])
    z = _dot(h.astype(BF16), w_ref[...])
    qna_ref, kna_ref, vna_ref, qsw_ref, ksw_ref, vsw_ref = out_refs[:6]
    c0 = 0
    qna = z[:, 0:NA_W]
    kna = z[:, NA_W:2 * NA_W]
    vna = z[:, 2 * NA_W:3 * NA_W]
    c0 = 3 * NA_W
    qsw = z[:, c0:c0 + SWQ_W]
    ksw = z[:, c0 + SWQ_W:c0 + SWQ_W + SWKV_W]
    vsw = z[:, c0 + SWQ_W + SWKV_W:]
    if kv_f32:
        for ref, val in zip(out_refs[6:10], (kna, vna, ksw, vsw)):
            heads = val.shape[1] // DH
            for hh in range(heads):
                ref[pl.ds(hh, SEQ, stride=heads), :] = val[:, hh * DH:(hh + 1) * DH]
    if rope:
        cos = cos_ref[...]
        sin = sin_ref[...]
        qsw = _rope(qsw, cos, sin)
        ksw = _rope(ksw, cos, sin)
    qna_ref[...] = (qna * QK_SCALE).astype(BF16)
    kna_ref[...] = kna.astype(BF16)
    vna_ref[...] = vna.astype(BF16)
    qsw_ref[...] = (qsw * QK_SCALE).astype(BF16)
    ksw_ref[...] = ksw.astype(BF16)
    vsw_ref[...] = vsw.astype(BF16)


def _odd_in(x, mod_l, g, w, cos, sin, *, prompt):
    tm = TM if prompt else 2 * TM
    tile0 = 0 if prompt else N_P // tm
    nt = (N_P if prompt else N_S) // tm
    n = nt * tm
    widths = [NA_W, NA_W, NA_W, SWQ_W, SWKV_W, SWKV_W]
    out_shape = [jax.ShapeDtypeStruct((n, w_), BF16) for w_ in widths]
    out_specs = [pl.BlockSpec((tm, w_), lambda i: (i, 0)) for w_ in widths]
    if prompt:
        assert tm == SEQ
        for heads in (NA_H, NA_H, SW_KV, SW_KV):
            out_shape.append(jax.ShapeDtypeStruct((BATCH * SEQ * heads, DH), F32))
            out_specs.append(pl.BlockSpec((SEQ * heads, DH), lambda i: (i, 0)))
    return pl.pallas_call(
        functools.partial(_odd_in_kernel, tm=tm, tile0=tile0, rope=not prompt, kv_f32=prompt),
        grid=(nt,),
        in_specs=[
            pl.BlockSpec((tm, D), lambda i: (i + tile0, 0)),
            pl.BlockSpec((6, MOD_ROWS, D), lambda i: (0, 0, 0)),
            pl.BlockSpec((1, D), lambda i: (0, 0)),
            pl.BlockSpec((D, ODD_IN), lambda i: (0, 0)),
            pl.BlockSpec((tm, 128), lambda i: (i % (DEC_SEQ // tm), 0)),
            pl.BlockSpec((tm, 128), lambda i: (i % (DEC_SEQ // tm), 0)),
        ],
        out_specs=out_specs,
        out_shape=out_shape,
        compiler_params=_cparams("parallel"),
        name="odd_in_prompt" if prompt else "odd_in_latent",
    )(x, mod_l, g, w, cos, sin)


def _rope_tables():
    t = np.arange(DEC_SEQ)
    quarter = DH // 4
    inv = 1.0 / (ROPE_BASE ** (np.arange(quarter, dtype=np.float64) / quarter))

    def cos_sin(pos):
        ang = pos.astype(np.float64)[:, None] * inv[None, :]
        ang = np.concatenate([ang, ang], axis=-1)
        return np.cos(ang), np.sin(ang)

    cr, sr = cos_sin(t // GRID_W)
    cc, sc = cos_sin(t % GRID_W)
    cos = np.concatenate([cr, cc], axis=-1)
    sin = np.concatenate([sr, sc], axis=-1)
    sign = np.where((np.arange(DH) % (DH // 2)) < DH // 4, -1.0, 1.0)
    sin = sin * sign[None, :]
    cos = np.concatenate([cos, cos], axis=-1).astype(np.float32)
    sin = np.concatenate([sin, sin], axis=-1).astype(np.float32)
    return jnp.asarray(cos), jnp.asarray(sin)


def _softmax_pv(segs, sink=None):
    m = None
    for s, _ in segs:
        sm = jnp.max(s, axis=-1, keepdims=True)
        m = sm if m is None else jnp.maximum(m, sm)
    if sink is not None:
        m = jnp.maximum(m, sink)
    den = None
    acc = None
    for s, v in segs:
        p = jnp.exp(s - m)
        ps = jnp.sum(p, axis=-1, keepdims=True)
        den = ps if den is None else den + ps
        pv = _dot(p.astype(BF16), v)
        acc = pv if acc is None else acc + pv
    if sink is not None:
        den = den + jnp.exp(sink - m)
    return acc / den


def _sink_col(sink_ref, g, rows_per_head):
    row = lax.broadcasted_iota(jnp.int32, (SW_G * rows_per_head, 1), 0)
    col = jnp.zeros((SW_G * rows_per_head, 1), F32)
    for r in range(SW_G):
        col = jnp.where(row // rows_per_head == r, sink_ref[g * SW_G + r], col)
    return col


def _ctx_attn_kernel(sink_ref, qna_ref, kna_ref, vna_ref, qsw_ref, ksw_ref, vsw_ref, ona_ref, osw_ref):
    lane = lax.broadcasted_iota(jnp.int32, (SEQ, 2 * DH), 1)
    for p in range(NA_H // 2):
        cols = slice(p * 2 * DH, (p + 1) * 2 * DH)
        q = qna_ref[:, cols]
        k = kna_ref[:, cols]
        v = vna_ref[:, cols]
        outs = []
        for half in range(2):
            mine = (lane < DH) if half == 0 else (lane >= DH)
            qm = jnp.where(mine, q, jnp.zeros_like(q))
            outs.append(_softmax_pv([(_dot_nt(qm, k), v)]))
        ona_ref[:, cols] = jnp.where(lane < DH, outs[0], outs[1]).astype(BF16)
    outs = []
    for g in range(SW_KV):
        kc = slice(g * DH, (g + 1) * DH)
        q = jnp.concatenate([qsw_ref[:, (g * SW_G + r) * DH:(g * SW_G + r + 1) * DH] for r in range(SW_G)], axis=0)
        s = _dot_nt(q, ksw_ref[:, kc])
        o = _softmax_pv([(s, vsw_ref[:, kc])], sink=_sink_col(sink_ref, g, SEQ))
        outs.extend(o[r * SEQ:(r + 1) * SEQ, :] for r in range(SW_G))
    osw_ref[...] = jnp.concatenate(outs, axis=1).astype(BF16)


def _ctx_attn(sink, qna, kna, vna, qsw, ksw, vsw):
    def spec(w):
        return pl.BlockSpec((SEQ, w), lambda b: (b, 0))

    return pl.pallas_call(
        _ctx_attn_kernel,
        grid=(BATCH,),
        in_specs=[pl.BlockSpec(memory_space=pltpu.SMEM), spec(NA_W), spec(NA_W), spec(NA_W),
                  spec(SWQ_W), spec(SWKV_W), spec(SWKV_W)],
        out_specs=[spec(NA_W), spec(SWQ_W)],
        out_shape=[jax.ShapeDtypeStruct((N_P, NA_W), BF16), jax.ShapeDtypeStruct((N_P, SWQ_W), BF16)],
        compiler_params=_cparams("parallel"),
        name="ctx_attn",
    )(sink, qna, kna, vna, qsw, ksw, vsw)


NA_QB = 4
NA_KR = 12
NA_NQB = ROWS // NA_QB


def _na_block_offset(case, i, j):
    if case == 0:
        valid, dr = j < NA_ROWS, j - i + NA_ROWS - 1
    elif case == 1:
        valid, dr = i <= j < i + NA_ROWS, j - i + NA_ROWS // 2 - 1
    else:
        valid, dr = j >= NA_KR - NA_ROWS, j - i + NA_ROWS - 1 - (NA_KR - NA_QB)
    return dr if valid else None


def _na_bias_kernel(rpb_ref, o_ref):
    h = pl.program_id(0)
    cq = lax.broadcasted_iota(jnp.int32, (GRID_W, GRID_W), 0)
    ck = lax.broadcasted_iota(jnp.int32, (GRID_W, GRID_W), 1)
    cstart = jnp.clip(cq - NA_COLS // 2, 0, GRID_W - NA_COLS)
    ok = (ck >= cstart) & (ck < cstart + NA_COLS)
    dc = jnp.clip(ck - cq + NA_COLS - 1, 0, 2 * NA_COLS - 2)
    ndc = 2 * NA_COLS - 1
    neg = jnp.full((GRID_W, GRID_W), NEG, F32)
    by_offset = []
    for dr in range(2 * NA_ROWS - 1):
        b = jnp.zeros((GRID_W, GRID_W), F32)
        for e in range(ndc):
            b = jnp.where(dc == e, rpb_ref[h, dr * ndc + e], b)
        by_offset.append(jnp.where(ok, b, NEG))
    for case in range(3):
        for i in range(NA_QB):
            blocks = []
            for j in range(NA_KR):
                dr = _na_block_offset(case, i, j)
                blocks.append(neg if dr is None else by_offset[dr])
            o_ref[case, i * GRID_W:(i + 1) * GRID_W, :] = jnp.concatenate(blocks, axis=1)


def _na_bias(rpb):
    nd = 2 * NA_ROWS - 1
    return pl.pallas_call(
        _na_bias_kernel,
        grid=(NA_H,),
        in_specs=[pl.BlockSpec(memory_space=pltpu.SMEM)],
        out_specs=pl.BlockSpec((3, None, NA_QB * GRID_W, NA_KR * GRID_W), lambda h: (0, h, 0, 0)),
        out_shape=jax.ShapeDtypeStruct((3, NA_H, NA_QB * GRID_W, NA_KR * GRID_W), F32),
        compiler_params=_cparams("parallel"),
        name="na_bias",
    )(rpb.reshape(NA_H, nd * (2 * NA_COLS - 1)))


def _na_kernel(q_ref, k_ref, v_ref, kc_ref, vc_ref, bias_ref, o_ref):
    r0 = pl.program_id(1) * NA_QB
    start = pl.multiple_of(jnp.clip(r0 - NA_ROWS // 2, 0, ROWS - NA_KR) * GRID_W, GRID_W)
    nq = NA_QB * GRID_W
    lane = lax.broadcasted_iota(jnp.int32, (nq, 2 * DH), 1)
    for p in range(NA_H // 2):
        cols = slice(p * 2 * DH, (p + 1) * 2 * DH)
        q = q_ref[:, cols]
        kl = k_ref[pl.ds(start, NA_KR * GRID_W), cols]
        vl = v_ref[pl.ds(start, NA_KR * GRID_W), cols]
        kc = kc_ref[:, cols]
        vc = vc_ref[:, cols]
        zero = jnp.zeros_like(q)
        qm = jnp.concatenate([jnp.where(lane < DH, q, zero), jnp.where(lane < DH, zero, q)], axis=0)
        s_loc = _dot_nt(qm, kl)
        s_ctx = _dot_nt(qm, kc)
        outs = []
        for half in range(2):
            rows = slice(half * nq, (half + 1) * nq)
            outs.append(_softmax_pv([(s_loc[rows] + bias_ref[2 * p + half], vl), (s_ctx[rows], vc)]))
        o_ref[:, cols] = jnp.where(lane < DH, outs[0], outs[1]).astype(BF16)


def _na_attn(q, k, v, kc, vc, bias):
    nq = NA_QB * GRID_W

    def bias_case(b, rb):
        return (jnp.where(rb == 0, 0, jnp.where(rb == NA_NQB - 1, 2, 1)), 0, 0, 0)

    return pl.pallas_call(
        _na_kernel,
        grid=(DEC_BATCH, NA_NQB),
        in_specs=[
            pl.BlockSpec((nq, NA_W), lambda b, rb: (b * NA_NQB + rb, 0)),
            pl.BlockSpec((DEC_SEQ, NA_W), lambda b, rb: (b, 0)),
            pl.BlockSpec((DEC_SEQ, NA_W), lambda b, rb: (b, 0)),
            pl.BlockSpec((None, PAST, NA_W), lambda b, rb: (b, 0, 0)),
            pl.BlockSpec((None, PAST, NA_W), lambda b, rb: (b, 0, 0)),
            pl.BlockSpec((None, NA_H, nq, NA_KR * GRID_W), bias_case),
        ],
        out_specs=pl.BlockSpec((nq, NA_W), lambda b, rb: (b * NA_NQB + rb, 0)),
        out_shape=jax.ShapeDtypeStruct((N_S, NA_W), BF16),
        compiler_params=_cparams("parallel", "arbitrary"),
        name="na_attn",
    )(q, k, v, kc, vc, bias)


def _sw_window_bias():
    q = np.arange(SW_G * ABLK)[:, None] % ABLK
    k = np.arange(3 * ABLK)[None, :]
    tables = [np.where(np.abs(q + lead - k) <= SW_WIN, 0.0, NEG) for lead in (0, ABLK, 2 * ABLK)]
    return jnp.asarray(np.stack(tables).astype(np.float32))


def _sw_kernel(sink_ref, q_ref, k_ref, v_ref, kc_ref, vc_ref, wb_ref, o_ref):
    j = pl.program_id(1)
    nk = 3 * ABLK
    start = pl.multiple_of(jnp.clip((j - 1) * ABLK, 0, DEC_SEQ - nk), ABLK)
    outs = []
    for g in range(SW_KV):
        kcols = slice(g * DH, (g + 1) * DH)
        q = jnp.concatenate([q_ref[:, (g * SW_G + r) * DH:(g * SW_G + r + 1) * DH] for r in range(SW_G)], axis=0)
        kw = k_ref[pl.ds(start, nk), kcols]
        vw = v_ref[pl.ds(start, nk), kcols]
        s_w = _dot_nt(q, kw) + wb_ref[...]
        s_c = _dot_nt(q, kc_ref[:, kcols])
        o = _softmax_pv([(s_w, vw), (s_c, vc_ref[:, kcols])], sink=_sink_col(sink_ref, g, ABLK))
        outs.extend(o[r * ABLK:(r + 1) * ABLK, :] for r in range(SW_G))
    o_ref[...] = jnp.concatenate(outs, axis=1).astype(BF16)


def _sw_attn(sink, q, k, v, kc, vc):
    nb = DEC_SEQ // ABLK
    return pl.pallas_call(
        _sw_kernel,
        grid=(DEC_BATCH, nb),
        in_specs=[
            pl.BlockSpec(memory_space=pltpu.SMEM),
            pl.BlockSpec((ABLK, SWQ_W), lambda b, j: (b * nb + j, 0)),
            pl.BlockSpec((DEC_SEQ, SWKV_W), lambda b, j: (b, 0)),
            pl.BlockSpec((DEC_SEQ, SWKV_W), lambda b, j: (b, 0)),
            pl.BlockSpec((None, PAST, SWKV_W), lambda b, j: (b, 0, 0)),
            pl.BlockSpec((None, PAST, SWKV_W), lambda b, j: (b, 0, 0)),
            pl.BlockSpec((None, SW_G * ABLK, 3 * ABLK),
                         lambda b, j: (jnp.where(j == 0, 0, jnp.where(j == nb - 1, 2, 1)), 0, 0)),
        ],
        out_specs=pl.BlockSpec((ABLK, SWQ_W), lambda b, j: (b * nb + j, 0)),
        out_shape=jax.ShapeDtypeStruct((N_S, SWQ_W), BF16),
        compiler_params=_cparams("parallel", "arbitrary"),
        name="sw_attn",
    )(sink, q, k, v, kc, vc, _sw_window_bias())


BT = 4096
NB = N_TOK // BT
TPB = BT // TM
CH = 256
CH_TAIL = 128
XPS = 2
NES = N_EXP // XPS
XO_ROWS = 2 * BT + CH
ROW = 8
assert D == ROW * 128


def _route_sparse(probs, sel):
    rank = []
    score = []
    for g in range(N_GRP):
        ids = range(g * EPG, (g + 1) * EPG)
        tot = None
        for e in ids:
            rk = jnp.zeros_like(sel[e])
            for j in ids:
                if j < e:
                    rk = rk + (sel[j] >= sel[e]).astype(F32)
                elif j > e:
                    rk = rk + (sel[j] > sel[e]).astype(F32)
            rank.append(rk)
            contrib = jnp.where(rk < 2.0, sel[e], 0.0)
            tot = contrib if tot is None else tot + contrib
        score.append(tot)
    top1, top2 = [], []
    for g in range(N_GRP):
        best = None
        for j in range(N_GRP):
            if j == g:
                continue
            c = (score[g] > score[j]) if j < g else (score[g] >= score[j])
            best = c if best is None else jnp.logical_and(best, c)
        for e in range(g * EPG, (g + 1) * EPG):
            top1.append(jnp.logical_and(best, rank[e] == 0.0))
            top2.append(jnp.logical_and(best, rank[e] == 1.0))
    return top1, top2


def _pick(masks, rows):
    acc = None
    for m, r in zip(masks, rows):
        v = jnp.where(m, r, 0.0)
        acc = v if acc is None else acc + v
    return acc


def _route_tile(xv, mod_ref, g_ref, rw_ref, rb_ref, h_ref, pos_ref, w_ref, cnt_ref, off_ref, meta, carry):
    i = pl.program_id(0)
    j = i % TPB
    r = _mod_row(i)
    h = _rms_mod(xv, g_ref[...], mod_ref[3, pl.ds(r, 1), :], mod_ref[4, pl.ds(r, 1), :])
    h_ref[...] = h.astype(BF16)
    h_hi = h.astype(BF16)
    h_lo = (h - h_hi.astype(F32)).astype(BF16)
    logits = _dot(h_lo, rw_ref[0]) + _dot(h_hi, rw_ref[1]) + _dot(h_hi, rw_ref[0])
    lt = logits.T[:N_EXP, :]
    m = jnp.max(lt, axis=0, keepdims=True)
    ex = jnp.exp(lt - m)
    pr = ex / jnp.sum(ex, axis=0, keepdims=True)
    se = pr + rb_ref[:N_EXP, :]
    probs = [pr[e:e + 1, :] for e in range(N_EXP)]
    sel = [se[e:e + 1, :] for e in range(N_EXP)]
    top1, top2 = _route_sparse(probs, sel)

    @pl.when(j == 0)
    def _():
        carry[...] = jnp.zeros_like(carry)

    member = jnp.concatenate([jnp.logical_or(a, b).astype(F32) for a, b in zip(top1, top2)], axis=0)
    s_idx = lax.broadcasted_iota(jnp.int32, (TM, TM), 0)
    t_idx = lax.broadcasted_iota(jnp.int32, (TM, TM), 1)
    before = jnp.where(s_idx < t_idx, 1.0, 0.0).astype(BF16)
    seen = _dot(member.astype(BF16), before) + carry[:, 0:1]
    seen_rows = [seen[e:e + 1, :] for e in range(N_EXP)]
    ids = [jnp.full((1, TM), float(e), F32) for e in range(N_EXP)]
    p1 = _pick(top1, probs)
    p2 = _pick(top2, probs)
    den = p1 + p2
    w_ref[j, 0:1, :] = p1 / den
    w_ref[j, 1:2, :] = p2 / den
    meta[j, 0:1, :] = _pick(top1, ids)
    meta[j, 1:2, :] = _pick(top2, ids)
    meta[j, 2:3, :] = _pick(top1, seen_rows)
    meta[j, 3:4, :] = _pick(top2, seen_rows)
    carry[...] = carry[...] + jnp.sum(member, axis=1, keepdims=True)

    @pl.when(j == TPB - 1)
    def _():
        cnt = carry[...]
        offs = [jnp.zeros((1, 128), F32)]
        for e in range(1, N_EXP):
            offs.append(offs[-1] + cnt[e - 1:e, :])
        cnt_ref[...] = cnt
        off_ref[...] = jnp.concatenate(offs, axis=0)
        for jj in range(TPB):
            for k in range(2):
                eid = meta[jj, k:k + 1, :]
                pos = meta[jj, 2 + k:3 + k, :]
                for e in range(1, N_EXP):
                    pos = pos + jnp.where(eid == float(e), offs[e][:, 0:1], 0.0)
                pos_ref[jj, k:k + 1, :] = (pos * float(ROW)).astype(jnp.int32)


def _router_kernel(x_ref, *refs):
    _route_tile(x_ref[...], *refs)


def _odd_out_router_kernel(nap_ref, nas_ref, swp_ref, sws_ref, x_ref, wo_ref, mod_ref, g_ref, rw_ref, rb_ref,
                           x1_ref, *refs):
    i = pl.program_id(0)
    is_p = i < NPT
    ona = jnp.where(is_p, nap_ref[...], nas_ref[...])
    osw = jnp.where(is_p, swp_ref[...], sws_ref[...])
    y = _dot(ona, wo_ref[:NA_W, :]) + _dot(osw, wo_ref[NA_W:, :])
    x1 = x_ref[...] + mod_ref[2, pl.ds(_mod_row(i), 1), :] * y
    x1_ref[...] = x1
    _route_tile(x1, mod_ref, g_ref, rw_ref, rb_ref, *refs)


def _router(x, mod_l, g, rw2, rb_col, attn=None):
    blk = lambda i: (i // TPB, 0, 0, 0)
    in_specs = [
        pl.BlockSpec((TM, D), lambda i: (i, 0)),
        pl.BlockSpec((6, MOD_ROWS, D), lambda i: (0, 0, 0)),
        pl.BlockSpec((1, D), lambda i: (0, 0)),
        pl.BlockSpec((2, D, 128), lambda i: (0, 0, 0)),
        pl.BlockSpec((128, 1), lambda i: (0, 0)),
    ]
    out_specs = [
        pl.BlockSpec((TM, D), lambda i: (i, 0)),
        pl.BlockSpec((None, TPB, 2, TM), blk),
        pl.BlockSpec((None, TPB, 2, TM), blk),
        pl.BlockSpec((None, N_EXP, 128), lambda i: (i // TPB, 0, 0)),
        pl.BlockSpec((None, N_EXP, 128), lambda i: (i // TPB, 0, 0)),
    ]
    out_shape = [
        jax.ShapeDtypeStruct((N_TOK, D), BF16),
        jax.ShapeDtypeStruct((NB, TPB, 2, TM), jnp.int32),
        jax.ShapeDtypeStruct((NB, TPB, 2, TM), F32),
        jax.ShapeDtypeStruct((NB, N_EXP, 128), F32),
        jax.ShapeDtypeStruct((NB, N_EXP, 128), F32),
    ]
    args = (x, mod_l, g, rw2, rb_col)
    body = _router_kernel
    if attn is not None:
        nap, nas, swp, sws, w_out = attn
        in_specs = (_dual_specs(TM, NA_W, NPT) + _dual_specs(TM, SWQ_W, NPT) + in_specs[:1]
                    + [pl.BlockSpec((D, D), lambda i: (0, 0))] + in_specs[1:])
        out_specs = [pl.BlockSpec((TM, D), lambda i: (i, 0))] + out_specs
        out_shape = [jax.ShapeDtypeStruct((N_TOK, D), F32)] + out_shape
        args = (nap, nas, swp, sws, x, w_out, mod_l, g, rw2, rb_col)
        body = _odd_out_router_kernel
    return pl.pallas_call(
        body,
        grid=(NT,),
        in_specs=in_specs,
        out_specs=out_specs,
        out_shape=out_shape,
        scratch_shapes=[pltpu.VMEM((TPB, 4, TM), F32), pltpu.VMEM((N_EXP, 128), F32)],
        compiler_params=_cparams("arbitrary"),
        name="router" if attn is None else "odd_out_router",
    )(*args)


def _row(p):
    return pl.ds(pl.multiple_of(p, ROW), ROW)


def _expert_segment(n, base, xo, wg_ref, wu_ref, wd_ref):
    def ffn_rows(row0, rows):
        x = jnp.concatenate([xo[pl.ds(row0 * ROW + c, rows, stride=ROW), :] for c in range(ROW)], axis=1)
        xb = x.astype(BF16)
        a = _dot(xb, wg_ref[...])
        u = _dot(xb, wu_ref[...])
        hid = (a / (1.0 + jnp.exp(-a))) * u
        out = _dot(hid.astype(BF16), wd_ref[...])
        valid = lax.broadcasted_iota(jnp.int32, (rows, 1), 0) < base + n - row0
        res = jnp.where(valid, out, x)
        for c in range(ROW):
            xo[pl.ds(row0 * ROW + c, rows, stride=ROW), :] = res[:, c * 128:(c + 1) * 128]

    def chunk(jc, carry):
        ffn_rows(base + jc * CH, CH)
        return carry

    nfull = (n + CH - CH_TAIL - 1) // CH
    lax.fori_loop(0, nfull, chunk, 0)

    @pl.when(n > nfull * CH)
    def _():
        ffn_rows(base + nfull * CH, CH_TAIL)


def _experts_kernel(cnt_ref, off_ref, pos_ref, w_ref, h_ref, wg_ref, wu_ref, wd_ref, x_ref, mod_ref, fg_ref,
                    *rest, final):
    if final:
        op_ref, os_ref, xo, stg = rest
    else:
        o_ref, xo, stg = rest
    b = pl.program_id(0)
    s = pl.program_id(1)

    @pl.when(s == 0)
    def _():
        xo[pl.ds(2 * BT * ROW, CH * ROW), :] = jnp.zeros((CH * ROW, 128), F32)

    @pl.when(s < TPB)
    def _():
        hs = h_ref[...].astype(F32)
        for c in range(ROW):
            stg[pl.ds(c, TM, stride=ROW), :] = hs[:, c * 128:(c + 1) * 128]
        for t in range(TM):
            v = stg[t * ROW:(t + 1) * ROW, :]
            xo[_row(pos_ref[0, t]), :] = v
            xo[_row(pos_ref[1, t]), :] = v

    @pl.when(jnp.logical_and(s >= TPB, s < TPB + NES))
    def _():
        for k in range(XPS):
            e = (s - TPB) * XPS + k
            _expert_segment(cnt_ref[b, e], off_ref[b, e], xo, wg_ref.at[k], wu_ref.at[k], wd_ref.at[k])

    @pl.when(s >= TPB + NES - 1)
    def _():
        j = s - (TPB + NES - 1)
        tile = b * TPB + j
        g2 = mod_ref[5, pl.ds(_mod_row(tile), 1), :]
        for t in range(TM):
            a = xo[_row(pos_ref[0, t]), :]
            u = xo[_row(pos_ref[1, t]), :]
            stg[t * ROW:(t + 1) * ROW, :] = w_ref[0, t] * a + w_ref[1, t] * u
        y = jnp.concatenate([stg[pl.ds(c, TM, stride=ROW), :] for c in range(ROW)], axis=1)
        res = x_ref[...] + g2 * y
        if final:
            ms = jnp.mean(res * res, axis=-1, keepdims=True)
            res = res * lax.rsqrt(ms + EPS) * fg_ref[...]

            @pl.when(tile < NPT)
            def _():
                op_ref[...] = res

            @pl.when(tile >= NPT)
            def _():
                os_ref[...] = res
        else:
            o_ref[...] = res


def _experts(cnt, off, pos, wts, h, wg, wu, wd, x, mod_l, fg, *, layer, final):
    def tile_of(b, s):
        return b * TPB + jnp.clip(s - (TPB + NES - 1), 0, TPB - 1)

    def expert_of(b, s, *_):
        return (layer, jnp.clip(s - TPB, 0, NES - 1), 0, 0)

    def smem_tile(b, s, *_):
        return (b, jnp.where(s < TPB, s, jnp.clip(s - (TPB + NES - 1), 0, TPB - 1)), 0, 0)

    smem_blk = pl.BlockSpec((None, None, 2, TM), smem_tile, memory_space=pltpu.SMEM)
    if final:
        out_specs = [pl.BlockSpec((TM, D), lambda b, s, *_: (jnp.minimum(tile_of(b, s), NPT - 1), 0)),
                     pl.BlockSpec((TM, D), lambda b, s, *_: (jnp.maximum(tile_of(b, s) - NPT, 0), 0))]
        out_shape = [jax.ShapeDtypeStruct((N_P, D), F32), jax.ShapeDtypeStruct((N_S, D), F32)]
    else:
        out_specs = pl.BlockSpec((TM, D), lambda b, s, *_: (tile_of(b, s), 0))
        out_shape = jax.ShapeDtypeStruct((N_TOK, D), F32)
    grid_spec = pltpu.PrefetchScalarGridSpec(
        num_scalar_prefetch=2,
        grid=(NB, 2 * TPB + NES - 1),
        in_specs=[
            smem_blk,
            smem_blk,
            pl.BlockSpec((TM, D), lambda b, s, *_: (b * TPB + jnp.minimum(s, TPB - 1), 0)),
            pl.BlockSpec((None, XPS, D, D_EXP), expert_of),
            pl.BlockSpec((None, XPS, D, D_EXP), expert_of),
            pl.BlockSpec((None, XPS, D_EXP, D), expert_of),
            pl.BlockSpec((TM, D), lambda b, s, *_: (tile_of(b, s), 0)),
            pl.BlockSpec((6, MOD_ROWS, D), lambda b, s, *_: (0, 0, 0)),
            pl.BlockSpec((1, D), lambda b, s, *_: (0, 0)),
        ],
        out_specs=out_specs,
        scratch_shapes=[pltpu.VMEM((XO_ROWS * ROW, 128), F32), pltpu.VMEM((TM * ROW, 128), F32)],
    )
    return pl.pallas_call(
        functools.partial(_experts_kernel, final=final),
        grid_spec=grid_spec,
        out_shape=out_shape,
        compiler_params=_cparams("arbitrary", "arbitrary"),
        name="experts_final" if final else "experts",
    )(cnt, off, pos, wts, h, wg, wu, wd, x, mod_l, fg)


def _moe_sparse(x, routed, mod_l, wg, wu, wd, fg, *, layer, final):
    h, pos, wts, cnt, off = routed
    cnt = cnt[:, :, 0].astype(jnp.int32)
    off = off[:, :, 0].astype(jnp.int32)
    return _experts(cnt, off, pos, wts, h, wg, wu, wd, x, mod_l, fg, layer=layer, final=final)


def kernel(x_prompt, x_sample, cache_na_k, cache_na_v, cache_sw_k, cache_sw_v, c, c_ctx, mod_w, mod_b, norm_mix_g, norm_ffn_g, ev_w_in, ev_pool_w, ev_pool_scale, ev_conv_w, ev_w_out, od_w_in, od_rpb, od_sink, od_w_out, router_w, router_b, moe_w_gate, moe_w_up, moe_w_down, final_norm_g):
    xp = x_prompt.reshape(N_P, D)
    xs = x_sample.reshape(N_S, D)
    cvec = jnp.concatenate([c_ctx[None, :], c, jnp.zeros((MOD_ROWS - 1 - DEC_BATCH, D), F32)], axis=0)
    mod = _modulation(cvec, mod_w, mod_b)

    rw_pad = jnp.pad(router_w, ((0, 0), (0, 128 - N_EXP)))
    rw_hi = rw_pad.astype(BF16)
    rw2 = jnp.stack([rw_hi, (rw_pad - rw_hi.astype(F32)).astype(BF16)])
    rb_col = jnp.pad(router_b, (0, 128 - N_EXP)).reshape(128, 1)

    conv_w = jnp.pad(ev_conv_w[0], ((0, 8 - ev_conv_w.shape[1]), (0, 0)))
    x, wg, wu, wd = _even_layer(xp, xs, mod[0], norm_mix_g[0:1], ev_w_in[0].astype(BF16),
                                ev_pool_w[0].astype(BF16), ev_pool_scale[0:1], conv_w, ev_w_out[0].astype(BF16),
                                moe_w_gate, moe_w_up, moe_w_down)
    fg = final_norm_g.reshape(1, D)
    routed = _router(x, mod[0], norm_ffn_g[0:1], rw2, rb_col)
    x = _moe_sparse(x, routed, mod[0], wg, wu, wd, fg, layer=0, final=False)

    cos, sin = _rope_tables()
    w_in = od_w_in[0].astype(BF16)
    g1 = norm_mix_g[1:2]
    qna_p, kna_p, vna_p, qsw_p, ksw_p, vsw_p, nak, nav, swk, swv = _odd_in(x, mod[1], g1, w_in, cos, sin, prompt=True)
    qna_s, kna_s, vna_s, qsw_s, ksw_s, vsw_s = _odd_in(x, mod[1], g1, w_in, cos, sin, prompt=False)
    sink = od_sink[0]
    ona_p, osw_p = _ctx_attn(sink, qna_p, kna_p, vna_p, qsw_p, ksw_p, vsw_p)
    bias = _na_bias(od_rpb[0])
    ona_s = _na_attn(qna_s, kna_s, vna_s,
                     cache_na_k[:, 0].reshape(DEC_BATCH, PAST, NA_W).astype(BF16),
                     cache_na_v[:, 0].reshape(DEC_BATCH, PAST, NA_W).astype(BF16), bias)
    osw_s = _sw_attn(sink, qsw_s, ksw_s, vsw_s,
                     cache_sw_k[:, 0].reshape(DEC_BATCH, PAST, SWKV_W).astype(BF16),
                     cache_sw_v[:, 0].reshape(DEC_BATCH, PAST, SWKV_W).astype(BF16))
    x, *routed = _router(x, mod[1], norm_ffn_g[1:2], rw2, rb_col,
                         attn=(ona_p, ona_s, osw_p, osw_s, od_w_out[0].astype(BF16)))
    y_prompt, y_sample = _moe_sparse(x, routed, mod[1], wg, wu, wd, fg, layer=1, final=True)
    y_prompt = y_prompt.reshape(BATCH, SEQ, D)
    y_sample = y_sample.reshape(DEC_BATCH, DEC_SEQ, D)
    new_na_k = nak.reshape(BATCH, 1, SEQ, NA_H, DH)
    new_na_v = nav.reshape(BATCH, 1, SEQ, NA_H, DH)
    new_sw_k = swk.reshape(BATCH, 1, SEQ, SW_KV, DH)
    new_sw_v = swv.reshape(BATCH, 1, SEQ, SW_KV, DH)
    return (y_prompt, y_sample, new_na_k, new_na_v, new_sw_k, new_sw_v)
```

```python
import functools

import jax
import jax.numpy as jnp
import numpy as np
from jax import lax
from jax.experimental import pallas as pl
from jax.experimental.pallas import tpu as pltpu

D = 1024
BATCH = 16
SEQ = 256
DEC_BATCH = 4
DEC_SEQ = 4096
PAST = 512
GRID_W = 64
ROWS = DEC_SEQ // GRID_W
DH = 64
POOL_W = 512
POOL_WINDOWS = (2, 4, 8, 16)
POOL_GW = 128
CONV_W = 512
EVEN_IN = POOL_W + 3 * CONV_W
NA_H = 8
NA_ROWS = 8
NA_COLS = 16
SW_H = 8
SW_KV = 2
SW_G = SW_H // SW_KV
SW_WIN = 128
ABLK = 128
NA_W = NA_H * DH
SWQ_W = SW_H * DH
SWKV_W = SW_KV * DH
ODD_IN = 3 * NA_W + SWQ_W + 2 * SWKV_W
N_EXP = 16
N_GRP = 4
EPG = 4
D_EXP = 512
EPS = 1e-6
NEG = -1e30
ROPE_BASE = 10000.0
QK_SCALE = DH ** -0.5
assert QK_SCALE == 0.125

N_P = BATCH * SEQ
N_S = DEC_BATCH * DEC_SEQ
N_TOK = N_P + N_S
MOD_ROWS = 8

TM = 256
NPT = N_P // TM
TPS = DEC_SEQ // TM
NT = N_TOK // TM
HALO = 8
EV_HALVES = 2
EV_TM = EV_HALVES * TM
CAST_STEPS = 32

F32 = jnp.float32
BF16 = jnp.bfloat16
VMEM_LIMIT = 56 * 1024 * 1024


def _cparams(*sem):
    return pltpu.CompilerParams(dimension_semantics=sem, vmem_limit_bytes=VMEM_LIMIT)


def _mod_row(i):
    return jnp.where(i < NPT, 0, 1 + (i - NPT) // TPS)


def _rms_mod(x, g, shift, scale):
    ms = jnp.mean(x * x, axis=-1, keepdims=True)
    y = x * lax.rsqrt(ms + EPS) * g
    return y * (1.0 + scale) + shift


def _dot(a, b):
    return jnp.dot(a, b, preferred_element_type=F32)


def _dot_nt(a, b):
    return lax.dot_general(a, b, (((1,), (1,)), ((), ())), preferred_element_type=F32)


def _mod_kernel(cv_ref, w_ref, b_ref, o_ref):
    cv = cv_ref[...]
    a = cv / (1.0 + jnp.exp(-cv))
    o_ref[...] = jnp.dot(a, w_ref[...], preferred_element_type=F32,
                         precision=lax.Precision.HIGHEST) + b_ref[...]


def _modulation(cvec, mod_w, mod_b):
    depth = mod_w.shape[0]
    return pl.pallas_call(
        _mod_kernel,
        grid=(depth, 6),
        in_specs=[
            pl.BlockSpec((MOD_ROWS, D), lambda l, j: (0, 0)),
            pl.BlockSpec((None, D, D), lambda l, j: (l, 0, j)),
            pl.BlockSpec((None, None, 1, D), lambda l, j: (l, j, 0, 0)),
        ],
        out_specs=pl.BlockSpec((None, None, MOD_ROWS, D), lambda l, j: (l, j, 0, 0)),
        out_shape=jax.ShapeDtypeStruct((depth, 6, MOD_ROWS, D), F32),
        compiler_params=_cparams("arbitrary", "arbitrary"),
        name="modulation",
    )(cvec, mod_w, mod_b.reshape(depth, 6, 1, D))


def _dual_specs(tm, width, npt):
    return [
        pl.BlockSpec((tm, width), lambda i: (jnp.minimum(i, npt - 1), 0)),
        pl.BlockSpec((tm, width), lambda i: (jnp.maximum(i - npt, 0), 0)),
    ]


def _even_kernel(xp_ref, xs_ref, xprev_ref, xnext_ref, mod_ref, g_ref, wi_ref, pw_ref, ps_ref, cw_ref,
                 wo_ref, wgf_ref, wuf_ref, wdf_ref, o_ref, wgb_ref, wub_ref, wdb_ref, pext, uext):
    i = pl.program_id(0)

    @pl.when(i < CAST_STEPS)
    def _():
        wgb_ref[...] = wgf_ref[...].astype(BF16)
        wub_ref[...] = wuf_ref[...].astype(BF16)
        wdb_ref[...] = wdf_ref[...].astype(BF16)

    npt = NPT // EV_HALVES
    r = _mod_row(i * EV_HALVES)
    is_p = i < npt
    seq_len = jnp.where(is_p, SEQ, DEC_SEQ)

    x = jnp.where(is_p, xp_ref[...], xs_ref[...])
    xe = jnp.concatenate([xprev_ref[...], x, xnext_ref[...]], axis=0)
    h = _rms_mod(xe, g_ref[...], mod_ref[0, pl.ds(r, 1), :], mod_ref[1, pl.ds(r, 1), :])
    ze_all = _dot(h.astype(BF16), wi_ref[...])
    row = lax.broadcasted_iota(jnp.int32, (TM + 2 * HALO, 1), 0)
    for hf in range(EV_HALVES):
        t0 = jnp.where(is_p, 0, ((i - npt) % (DEC_SEQ // EV_TM)) * EV_TM + hf * TM)
        first = t0 == 0
        last = t0 + TM == seq_len
        ze = ze_all[hf * TM:hf * TM + TM + 2 * HALO, :]
        outside = jnp.logical_or(jnp.logical_and(first, row < HALO), jnp.logical_and(last, row >= HALO + TM))
        ze = jnp.where(outside, 0.0, ze)
        pext[...] = ze[:, :POOL_W]
        uext[...] = ze[:, POOL_W + 2 * CONV_W:] * ze[:, POOL_W:POOL_W + CONV_W]
        p = ze[HALO:HALO + TM, :POOL_W]
        gb = ze[HALO:HALO + TM, POOL_W + CONV_W:POOL_W + 2 * CONV_W]

        t = t0 + lax.broadcasted_iota(jnp.int32, (TM, 1), 0)
        mixed = []
        for g, w in enumerate(POOL_WINDOWS):
            cols = slice(g * POOL_GW, (g + 1) * POOL_GW)
            acc = jnp.zeros((TM, POOL_GW), F32)
            for k in range(-(w // 2), w - (w // 2)):
                acc = acc + pext[pl.ds(HALO + k, TM), cols]
            lo = jnp.maximum(t - w // 2, 0)
            hi = jnp.minimum(t + (w - 1 - w // 2), seq_len - 1)
            cnt = (hi - lo + 1).astype(F32)
            d = acc / cnt - p[:, cols]
            mixed.append((_dot(d.astype(BF16), pw_ref[g]) * ps_ref[:, cols]).astype(BF16))
        conv = (uext[pl.ds(HALO - 1, TM), :] * cw_ref[0:1, :] + uext[pl.ds(HALO, TM), :] * cw_ref[1:2, :]
                + uext[pl.ds(HALO + 1, TM), :] * cw_ref[2:3, :])
        mixed.append((gb * conv).astype(BF16))
        y = _dot(jnp.concatenate(mixed, axis=1), wo_ref[...])
        rows = slice(hf * TM, (hf + 1) * TM)
        o_ref[rows, :] = x[rows, :] + mod_ref[2, pl.ds(r, 1), :] * y


def _even_layer(xp, xs, mod_l, g, w_in, pool_w, pool_scale, conv_w, w_out, wg, wu, wd):
    hb = EV_TM // HALO
    nhb = N_S // HALO
    npt = N_P // EV_TM
    nt = N_TOK // EV_TM
    wg2, wu2, wd2 = wg.reshape(-1, D_EXP), wu.reshape(-1, D_EXP), wd.reshape(-1, D)
    rows_in, rows_out = wg2.shape[0] // CAST_STEPS, wd2.shape[0] // CAST_STEPS
    assert rows_in * CAST_STEPS == wg2.shape[0] and rows_out * CAST_STEPS == wd2.shape[0] and CAST_STEPS <= nt
    cast_blk = lambda i: (jnp.minimum(i, CAST_STEPS - 1), 0)
    cast_specs = [pl.BlockSpec((rows_in, D_EXP), cast_blk), pl.BlockSpec((rows_in, D_EXP), cast_blk),
                  pl.BlockSpec((rows_out, D), cast_blk)]
    outs = pl.pallas_call(
        _even_kernel,
        grid=(nt,),
        in_specs=_dual_specs(EV_TM, D, npt) + [
            pl.BlockSpec((HALO, D), lambda i: (jnp.maximum((i - npt) * hb - 1, 0), 0)),
            pl.BlockSpec((HALO, D), lambda i: (jnp.clip((i - npt + 1) * hb, 0, nhb - 1), 0)),
            pl.BlockSpec((6, MOD_ROWS, D), lambda i: (0, 0, 0)),
            pl.BlockSpec((1, D), lambda i: (0, 0)),
            pl.BlockSpec((D, EVEN_IN), lambda i: (0, 0)),
            pl.BlockSpec((4, POOL_GW, POOL_GW), lambda i: (0, 0, 0)),
            pl.BlockSpec((1, POOL_W), lambda i: (0, 0)),
            pl.BlockSpec((8, CONV_W), lambda i: (0, 0)),
            pl.BlockSpec((D, D), lambda i: (0, 0)),
        ] + cast_specs,
        out_specs=[pl.BlockSpec((EV_TM, D), lambda i: (i, 0))] + cast_specs,
        out_shape=[jax.ShapeDtypeStruct((N_TOK, D), F32), jax.ShapeDtypeStruct(wg2.shape, BF16),
                   jax.ShapeDtypeStruct(wu2.shape, BF16), jax.ShapeDtypeStruct(wd2.shape, BF16)],
        scratch_shapes=[pltpu.VMEM((TM + 2 * HALO, POOL_W), F32),
                        pltpu.VMEM((TM + 2 * HALO, CONV_W), F32)],
        compiler_params=_cparams("arbitrary"),
        name="even_layer",
    )(xp, xs, xs, xs, mod_l, g, w_in, pool_w, pool_scale, conv_w, w_out, wg2, wu2, wd2)
    return outs[0], outs[1].reshape(wg.shape), outs[2].reshape(wu.shape), outs[3].reshape(wd.shape)


def _rope(x, cos, sin_signed):
    n = x.shape[1] // 128
    cosf = jnp.concatenate([cos] * n, axis=1) if n > 1 else cos
    sinf = jnp.concatenate([sin_signed] * n, axis=1) if n > 1 else sin_signed
    w = x.shape[1]
    lane = lax.broadcasted_iota(jnp.int32, x.shape, 1)
    up = pltpu.roll(x, w - DH // 4, 1)
    dn = pltpu.roll(x, DH // 4, 1)
    rot = jnp.where((lane % (DH // 2)) < DH // 4, up, dn)
    return x * cosf + rot * sinf


def _odd_in_kernel(x_ref, mod_ref, g_ref, w_ref, cos_ref, sin_ref, *out_refs, tm, tile0, rope, kv_f32):
    r = _mod_row((pl.program_id(0) + tile0) * (tm // TM))
    h = _rms_mod(x_ref[...], g_ref[...], mod_ref[0, pl.ds(r, 1), :], mod_ref[1, pl.ds(r, 1), :])
    z = _dot(h.astype(BF16), w_ref[...])
    qna_ref, kna_ref, vna_ref, qsw_ref, ksw_ref, vsw_ref = out_refs[:6]
    c0 = 0
    qna = z[:, 0:NA_W]
    kna = z[:, NA_W:2 * NA_W]
    vna = z[:, 2 * NA_W:3 * NA_W]
    c0 = 3 * NA_W
    qsw = z[:, c0:c0 + SWQ_W]
    ksw = z[:, c0 + SWQ_W:c0 + SWQ_W + SWKV_W]
    vsw = z[:, c0 + SWQ_W + SWKV_W:]
    if kv_f32:
        for ref, val in zip(out_refs[6:10], (kna, vna, ksw, vsw)):
            heads = val.shape[1] // DH
            for hh in range(heads):
                ref[pl.ds(hh, SEQ, stride=heads), :] = val[:, hh * DH:(hh + 1) * DH]
    if rope:
        cos = cos_ref[...]
        sin = sin_ref[...]
        qsw = _rope(qsw, cos, sin)
        ksw = _rope(ksw, cos, sin)
    qna_ref[...] = (qna * QK_SCALE).astype(BF16)
    kna_ref[...] = kna.astype(BF16)
    vna_ref[...] = vna.astype(BF16)
    qsw_ref[...] = (qsw * QK_SCALE).astype(BF16)
    ksw_ref[...] = ksw.astype(BF16)
    vsw_ref[...] = vsw.astype(BF16)


def _odd_in(x, mod_l, g, w, cos, sin, *, prompt):
    tm = TM if prompt else 2 * TM
    tile0 = 0 if prompt else N_P // tm
    nt = (N_P if prompt else N_S) // tm
    n = nt * tm
    widths = [NA_W, NA_W, NA_W, SWQ_W, SWKV_W, SWKV_W]
    out_shape = [jax.ShapeDtypeStruct((n, w_), BF16) for w_ in widths]
    out_specs = [pl.BlockSpec((tm, w_), lambda i: (i, 0)) for w_ in widths]
    if prompt:
        assert tm == SEQ
        for heads in (NA_H, NA_H, SW_KV, SW_KV):
            out_shape.append(jax.ShapeDtypeStruct((BATCH * SEQ * heads, DH), F32))
            out_specs.append(pl.BlockSpec((SEQ * heads, DH), lambda i: (i, 0)))
    return pl.pallas_call(
        functools.partial(_odd_in_kernel, tm=tm, tile0=tile0, rope=not prompt, kv_f32=prompt),
        grid=(nt,),
        in_specs=[
            pl.BlockSpec((tm, D), lambda i: (i + tile0, 0)),
            pl.BlockSpec((6, MOD_ROWS, D), lambda i: (0, 0, 0)),
            pl.BlockSpec((1, D), lambda i: (0, 0)),
            pl.BlockSpec((D, ODD_IN), lambda i: (0, 0)),
            pl.BlockSpec((tm, 128), lambda i: (i % (DEC_SEQ // tm), 0)),
            pl.BlockSpec((tm, 128), lambda i: (i % (DEC_SEQ // tm), 0)),
        ],
        out_specs=out_specs,
        out_shape=out_shape,
        compiler_params=_cparams("parallel"),
        name="odd_in_prompt" if prompt else "odd_in_latent",
    )(x, mod_l, g, w, cos, sin)


def _rope_tables():
    t = np.arange(DEC_SEQ)
    quarter = DH // 4
    inv = 1.0 / (ROPE_BASE ** (np.arange(quarter, dtype=np.float64) / quarter))

    def cos_sin(pos):
        ang = pos.astype(np.float64)[:, None] * inv[None, :]
        ang = np.concatenate([ang, ang], axis=-1)
        return np.cos(ang), np.sin(ang)

    cr, sr = cos_sin(t // GRID_W)
    cc, sc = cos_sin(t % GRID_W)
    cos = np.concatenate([cr, cc], axis=-1)
    sin = np.concatenate([sr, sc], axis=-1)
    sign = np.where((np.arange(DH) % (DH // 2)) < DH // 4, -1.0, 1.0)
    sin = sin * sign[None, :]
    cos = np.concatenate([cos, cos], axis=-1).astype(np.float32)
    sin = np.concatenate([sin, sin], axis=-1).astype(np.float32)
    return jnp.asarray(cos), jnp.asarray(sin)


def _softmax_pv(segs, sink=None):
    m = None
    for s, _ in segs:
        sm = jnp.max(s, axis=-1, keepdims=True)
        m = sm if m is None else jnp.maximum(m, sm)
    if sink is not None:
        m = jnp.maximum(m, sink)
    den = None
    acc = None
    for s, v in segs:
        p = jnp.exp(s - m)
        ps = jnp.sum(p, axis=-1, keepdims=True)
        den = ps if den is None else den + ps
        pv = _dot(p.astype(BF16), v)
        acc = pv if acc is None else acc + pv
    if sink is not None:
        den = den + jnp.exp(sink - m)
    return acc / den


def _sink_col(sink_ref, g, rows_per_head):
    row = lax.broadcasted_iota(jnp.int32, (SW_G * rows_per_head, 1), 0)
    col = jnp.zeros((SW_G * rows_per_head, 1), F32)
    for r in range(SW_G):
        col = jnp.where(row // rows_per_head == r, sink_ref[g * SW_G + r], col)
    return col


def _ctx_attn_kernel(sink_ref, qna_ref, kna_ref, vna_ref, qsw_ref, ksw_ref, vsw_ref, ona_ref, osw_ref):
    lane = lax.broadcasted_iota(jnp.int32, (SEQ, 2 * DH), 1)
    for p in range(NA_H // 2):
        cols = slice(p * 2 * DH, (p + 1) * 2 * DH)
        q = qna_ref[:, cols]
        k = kna_ref[:, cols]
        v = vna_ref[:, cols]
        outs = []
        for half in range(2):
            mine = (lane < DH) if half == 0 else (lane >= DH)
            qm = jnp.where(mine, q, jnp.zeros_like(q))
            outs.append(_softmax_pv([(_dot_nt(qm, k), v)]))
        ona_ref[:, cols] = jnp.where(lane < DH, outs[0], outs[1]).astype(BF16)
    outs = []
    for g in range(SW_KV):
        kc = slice(g * DH, (g + 1) * DH)
        q = jnp.concatenate([qsw_ref[:, (g * SW_G + r) * DH:(g * SW_G + r + 1) * DH] for r in range(SW_G)], axis=0)
        s = _dot_nt(q, ksw_ref[:, kc])
        o = _softmax_pv([(s, vsw_ref[:, kc])], sink=_sink_col(sink_ref, g, SEQ))
        outs.extend(o[r * SEQ:(r + 1) * SEQ, :] for r in range(SW_G))
    osw_ref[...] = jnp.concatenate(outs, axis=1).astype(BF16)


def _ctx_attn(sink, qna, kna, vna, qsw, ksw, vsw):
    def spec(w):
        return pl.BlockSpec((SEQ, w), lambda b: (b, 0))

    return pl.pallas_call(
        _ctx_attn_kernel,
        grid=(BATCH,),
        in_specs=[pl.BlockSpec(memory_space=pltpu.SMEM), spec(NA_W), spec(NA_W), spec(NA_W),
                  spec(SWQ_W), spec(SWKV_W), spec(SWKV_W)],
        out_specs=[spec(NA_W), spec(SWQ_W)],
        out_shape=[jax.ShapeDtypeStruct((N_P, NA_W), BF16), jax.ShapeDtypeStruct((N_P, SWQ_W), BF16)],
        compiler_params=_cparams("parallel"),
        name="ctx_attn",
    )(sink, qna, kna, vna, qsw, ksw, vsw)


NA_QB = 4
NA_KR = 12
NA_NQB = ROWS // NA_QB


def _na_block_offset(case, i, j):
    if case == 0:
        valid, dr = j < NA_ROWS, j - i + NA_ROWS - 1
    elif case == 1:
        valid, dr = i <= j < i + NA_ROWS, j - i + NA_ROWS // 2 - 1
    else:
        valid, dr = j >= NA_KR - NA_ROWS, j - i + NA_ROWS - 1 - (NA_KR - NA_QB)
    return dr if valid else None


def _na_bias_kernel(rpb_ref, o_ref):
    h = pl.program_id(0)
    cq = lax.broadcasted_iota(jnp.int32, (GRID_W, GRID_W), 0)
    ck = lax.broadcasted_iota(jnp.int32, (GRID_W, GRID_W), 1)
    cstart = jnp.clip(cq - NA_COLS // 2, 0, GRID_W - NA_COLS)
    ok = (ck >= cstart) & (ck < cstart + NA_COLS)
    dc = jnp.clip(ck - cq + NA_COLS - 1, 0, 2 * NA_COLS - 2)
    ndc = 2 * NA_COLS - 1
    neg = jnp.full((GRID_W, GRID_W), NEG, F32)
    by_offset = []
    for dr in range(2 * NA_ROWS - 1):
        b = jnp.zeros((GRID_W, GRID_W), F32)
        for e in range(ndc):
            b = jnp.where(dc == e, rpb_ref[h, dr * ndc + e], b)
        by_offset.append(jnp.where(ok, b, NEG))
    for case in range(3):
        for i in range(NA_QB):
            blocks = []
            for j in range(NA_KR):
                dr = _na_block_offset(case, i, j)
                blocks.append(neg if dr is None else by_offset[dr])
            o_ref[case, i * GRID_W:(i + 1) * GRID_W, :] = jnp.concatenate(blocks, axis=1)


def _na_bias(rpb):
    nd = 2 * NA_ROWS - 1
    return pl.pallas_call(
        _na_bias_kernel,
        grid=(NA_H,),
        in_specs=[pl.BlockSpec(memory_space=pltpu.SMEM)],
        out_specs=pl.BlockSpec((3, None, NA_QB * GRID_W, NA_KR * GRID_W), lambda h: (0, h, 0, 0)),
        out_shape=jax.ShapeDtypeStruct((3, NA_H, NA_QB * GRID_W, NA_KR * GRID_W), F32),
        compiler_params=_cparams("parallel"),
        name="na_bias",
    )(rpb.reshape(NA_H, nd * (2 * NA_COLS - 1)))


def _na_kernel(q_ref, k_ref, v_ref, kc_ref, vc_ref, bias_ref, o_ref):
    r0 = pl.program_id(1) * NA_QB
    start = pl.multiple_of(jnp.clip(r0 - NA_ROWS // 2, 0, ROWS - NA_KR) * GRID_W, GRID_W)
    nq = NA_QB * GRID_W
    lane = lax.broadcasted_iota(jnp.int32, (nq, 2 * DH), 1)
    for p in range(NA_H // 2):
        cols = slice(p * 2 * DH, (p + 1) * 2 * DH)
        q = q_ref[:, cols]
        kl = k_ref[pl.ds(start, NA_KR * GRID_W), cols]
        vl = v_ref[pl.ds(start, NA_KR * GRID_W), cols]
        kc = kc_ref[:, cols]
        vc = vc_ref[:, cols]
        zero = jnp.zeros_like(q)
        qm = jnp.concatenate([jnp.where(lane < DH, q, zero), jnp.where(lane < DH, zero, q)], axis=0)
        s_loc = _dot_nt(qm, kl)
        s_ctx = _dot_nt(qm, kc)
        outs = []
        for half in range(2):
            rows = slice(half * nq, (half + 1) * nq)
            outs.append(_softmax_pv([(s_loc[rows] + bias_ref[2 * p + half], vl), (s_ctx[rows], vc)]))
        o_ref[:, cols] = jnp.where(lane < DH, outs[0], outs[1]).astype(BF16)


def _na_attn(q, k, v, kc, vc, bias):
    nq = NA_QB * GRID_W

    def bias_case(b, rb):
        return (jnp.where(rb == 0, 0, jnp.where(rb == NA_NQB - 1, 2, 1)), 0, 0, 0)

    return pl.pallas_call(
        _na_kernel,
        grid=(DEC_BATCH, NA_NQB),
        in_specs=[
            pl.BlockSpec((nq, NA_W), lambda b, rb: (b * NA_NQB + rb, 0)),
            pl.BlockSpec((DEC_SEQ, NA_W), lambda b, rb: (b, 0)),
            pl.BlockSpec((DEC_SEQ, NA_W), lambda b, rb: (b, 0)),
            pl.BlockSpec((None, PAST, NA_W), lambda b, rb: (b, 0, 0)),
            pl.BlockSpec((None, PAST, NA_W), lambda b, rb: (b, 0, 0)),
            pl.BlockSpec((None, NA_H, nq, NA_KR * GRID_W), bias_case),
        ],
        out_specs=pl.BlockSpec((nq, NA_W), lambda b, rb: (b * NA_NQB + rb, 0)),
        out_shape=jax.ShapeDtypeStruct((N_S, NA_W), BF16),
        compiler_params=_cparams("parallel", "arbitrary"),
        name="na_attn",
    )(q, k, v, kc, vc, bias)


def _sw_window_bias():
    q = np.arange(SW_G * ABLK)[:, None] % ABLK
    k = np.arange(3 * ABLK)[None, :]
    tables = [np.where(np.abs(q + lead - k) <= SW_WIN, 0.0, NEG) for lead in (0, ABLK, 2 * ABLK)]
    return jnp.asarray(np.stack(tables).astype(np.float32))


SW_QB = 4


def _sw_kernel(sink_ref, q_ref, k_ref, v_ref, kc_ref, vc_ref, wb_ref, o_ref):
    nk = 3 * ABLK
    nb = DEC_SEQ // ABLK
    for sub in range(SW_QB):
        j = pl.program_id(1) * SW_QB + sub
        rows = slice(sub * ABLK, (sub + 1) * ABLK)
        start = pl.multiple_of(jnp.clip((j - 1) * ABLK, 0, DEC_SEQ - nk), ABLK)
        wb = wb_ref[jnp.where(j == 0, 0, jnp.where(j == nb - 1, 2, 1))]
        outs = []
        for g in range(SW_KV):
            kcols = slice(g * DH, (g + 1) * DH)
            q = jnp.concatenate([q_ref[rows, (g * SW_G + r) * DH:(g * SW_G + r + 1) * DH] for r in range(SW_G)],
                                axis=0)
            kw = k_ref[pl.ds(start, nk), kcols]
            vw = v_ref[pl.ds(start, nk), kcols]
            s_w = _dot_nt(q, kw) + wb
            s_c = _dot_nt(q, kc_ref[:, kcols])
            o = _softmax_pv([(s_w, vw), (s_c, vc_ref[:, kcols])], sink=_sink_col(sink_ref, g, ABLK))
            outs.extend(o[r * ABLK:(r + 1) * ABLK, :] for r in range(SW_G))
        o_ref[rows, :] = jnp.concatenate(outs, axis=1).astype(BF16)


def _sw_attn(sink, q, k, v, kc, vc):
    nsteps = DEC_SEQ // (ABLK * SW_QB)
    return pl.pallas_call(
        _sw_kernel,
        grid=(DEC_BATCH, nsteps),
        in_specs=[
            pl.BlockSpec(memory_space=pltpu.SMEM),
            pl.BlockSpec((SW_QB * ABLK, SWQ_W), lambda b, j: (b * nsteps + j, 0)),
            pl.BlockSpec((DEC_SEQ, SWKV_W), lambda b, j: (b, 0)),
            pl.BlockSpec((DEC_SEQ, SWKV_W), lambda b, j: (b, 0)),
            pl.BlockSpec((None, PAST, SWKV_W), lambda b, j: (b, 0, 0)),
            pl.BlockSpec((None, PAST, SWKV_W), lambda b, j: (b, 0, 0)),
            pl.BlockSpec((3, SW_G * ABLK, 3 * ABLK), lambda b, j: (0, 0, 0)),
        ],
        out_specs=pl.BlockSpec((SW_QB * ABLK, SWQ_W), lambda b, j: (b * nsteps + j, 0)),
        out_shape=jax.ShapeDtypeStruct((N_S, SWQ_W), BF16),
        compiler_params=_cparams("parallel", "arbitrary"),
        name="sw_attn",
    )(sink, q, k, v, kc, vc, _sw_window_bias())


BT = 4096
NB = N_TOK // BT
TPB = BT // TM
CH = 256
CH_TAIL = 128
XPS = 2
NES = N_EXP // XPS
XO_ROWS = 2 * BT + CH
ROW = 8
assert D == ROW * 128


def _route_sparse(probs, sel):
    rank = []
    score = []
    for g in range(N_GRP):
        ids = range(g * EPG, (g + 1) * EPG)
        tot = None
        for e in ids:
            rk = jnp.zeros_like(sel[e])
            for j in ids:
                if j < e:
                    rk = rk + (sel[j] >= sel[e]).astype(F32)
                elif j > e:
                    rk = rk + (sel[j] > sel[e]).astype(F32)
            rank.append(rk)
            contrib = jnp.where(rk < 2.0, sel[e], 0.0)
            tot = contrib if tot is None else tot + contrib
        score.append(tot)
    top1, top2 = [], []
    for g in range(N_GRP):
        best = None
        for j in range(N_GRP):
            if j == g:
                continue
            c = (score[g] > score[j]) if j < g else (score[g] >= score[j])
            best = c if best is None else jnp.logical_and(best, c)
        for e in range(g * EPG, (g + 1) * EPG):
            top1.append(jnp.logical_and(best, rank[e] == 0.0))
            top2.append(jnp.logical_and(best, rank[e] == 1.0))
    return top1, top2


def _pick(masks, rows):
    acc = None
    for m, r in zip(masks, rows):
        v = jnp.where(m, r, 0.0)
        acc = v if acc is None else acc + v
    return acc


def _route_tile(xv, mod_ref, g_ref, rw_ref, rb_ref, h_ref, pos_ref, w_ref, cnt_ref, off_ref, meta, carry):
    i = pl.program_id(0)
    j = i % TPB
    r = _mod_row(i)
    h = _rms_mod(xv, g_ref[...], mod_ref[3, pl.ds(r, 1), :], mod_ref[4, pl.ds(r, 1), :])
    h_ref[...] = h.astype(BF16)
    h_hi = h.astype(BF16)
    h_lo = (h - h_hi.astype(F32)).astype(BF16)
    logits = _dot(h_lo, rw_ref[0]) + _dot(h_hi, rw_ref[1]) + _dot(h_hi, rw_ref[0])
    lt = logits.T[:N_EXP, :]
    m = jnp.max(lt, axis=0, keepdims=True)
    ex = jnp.exp(lt - m)
    pr = ex / jnp.sum(ex, axis=0, keepdims=True)
    se = pr + rb_ref[:N_EXP, :]
    probs = [pr[e:e + 1, :] for e in range(N_EXP)]
    sel = [se[e:e + 1, :] for e in range(N_EXP)]
    top1, top2 = _route_sparse(probs, sel)

    @pl.when(j == 0)
    def _():
        carry[...] = jnp.zeros_like(carry)

    member = jnp.concatenate([jnp.logical_or(a, b).astype(F32) for a, b in zip(top1, top2)], axis=0)
    s_idx = lax.broadcasted_iota(jnp.int32, (TM, TM), 0)
    t_idx = lax.broadcasted_iota(jnp.int32, (TM, TM), 1)
    before = jnp.where(s_idx < t_idx, 1.0, 0.0).astype(BF16)
    seen = _dot(member.astype(BF16), before) + carry[:, 0:1]
    seen_rows = [seen[e:e + 1, :] for e in range(N_EXP)]
    ids = [jnp.full((1, TM), float(e), F32) for e in range(N_EXP)]
    p1 = _pick(top1, probs)
    p2 = _pick(top2, probs)
    den = p1 + p2
    w_ref[j, 0:1, :] = p1 / den
    w_ref[j, 1:2, :] = p2 / den
    meta[j, 0:1, :] = _pick(top1, ids)
    meta[j, 1:2, :] = _pick(top2, ids)
    meta[j, 2:3, :] = _pick(top1, seen_rows)
    meta[j, 3:4, :] = _pick(top2, seen_rows)
    carry[...] = carry[...] + jnp.sum(member, axis=1, keepdims=True)

    @pl.when(j == TPB - 1)
    def _():
        cnt = carry[...]
        offs = [jnp.zeros((1, 128), F32)]
        for e in range(1, N_EXP):
            offs.append(offs[-1] + cnt[e - 1:e, :])
        cnt_ref[...] = cnt
        off_ref[...] = jnp.concatenate(offs, axis=0)
        for jj in range(TPB):
            for k in range(2):
                eid = meta[jj, k:k + 1, :]
                pos = meta[jj, 2 + k:3 + k, :]
                for e in range(1, N_EXP):
                    pos = pos + jnp.where(eid == float(e), offs[e][:, 0:1], 0.0)
                pos_ref[jj, k:k + 1, :] = (pos * float(ROW)).astype(jnp.int32)


def _router_kernel(x_ref, *refs):
    _route_tile(x_ref[...], *refs)


def _odd_out_router_kernel(nap_ref, nas_ref, swp_ref, sws_ref, x_ref, wo_ref, mod_ref, g_ref, rw_ref, rb_ref,
                           x1_ref, *refs):
    i = pl.program_id(0)
    is_p = i < NPT
    ona = jnp.where(is_p, nap_ref[...], nas_ref[...])
    osw = jnp.where(is_p, swp_ref[...], sws_ref[...])
    y = _dot(ona, wo_ref[:NA_W, :]) + _dot(osw, wo_ref[NA_W:, :])
    x1 = x_ref[...] + mod_ref[2, pl.ds(_mod_row(i), 1), :] * y
    x1_ref[...] = x1
    _route_tile(x1, mod_ref, g_ref, rw_ref, rb_ref, *refs)


def _router(x, mod_l, g, rw2, rb_col, attn=None):
    blk = lambda i: (i // TPB, 0, 0, 0)
    in_specs = [
        pl.BlockSpec((TM, D), lambda i: (i, 0)),
        pl.BlockSpec((6, MOD_ROWS, D), lambda i: (0, 0, 0)),
        pl.BlockSpec((1, D), lambda i: (0, 0)),
        pl.BlockSpec((2, D, 128), lambda i: (0, 0, 0)),
        pl.BlockSpec((128, 1), lambda i: (0, 0)),
    ]
    out_specs = [
        pl.BlockSpec((TM, D), lambda i: (i, 0)),
        pl.BlockSpec((None, TPB, 2, TM), blk),
        pl.BlockSpec((None, TPB, 2, TM), blk),
        pl.BlockSpec((None, N_EXP, 128), lambda i: (i // TPB, 0, 0)),
        pl.BlockSpec((None, N_EXP, 128), lambda i: (i // TPB, 0, 0)),
    ]
    out_shape = [
        jax.ShapeDtypeStruct((N_TOK, D), BF16),
        jax.ShapeDtypeStruct((NB, TPB, 2, TM), jnp.int32),
        jax.ShapeDtypeStruct((NB, TPB, 2, TM), F32),
        jax.ShapeDtypeStruct((NB, N_EXP, 128), F32),
        jax.ShapeDtypeStruct((NB, N_EXP, 128), F32),
    ]
    args = (x, mod_l, g, rw2, rb_col)
    body = _router_kernel
    if attn is not None:
        nap, nas, swp, sws, w_out = attn
        in_specs = (_dual_specs(TM, NA_W, NPT) + _dual_specs(TM, SWQ_W, NPT) + in_specs[:1]
                    + [pl.BlockSpec((D, D), lambda i: (0, 0))] + in_specs[1:])
        out_specs = [pl.BlockSpec((TM, D), lambda i: (i, 0))] + out_specs
        out_shape = [jax.ShapeDtypeStruct((N_TOK, D), F32)] + out_shape
        args = (nap, nas, swp, sws, x, w_out, mod_l, g, rw2, rb_col)
        body = _odd_out_router_kernel
    return pl.pallas_call(
        body,
        grid=(NT,),
        in_specs=in_specs,
        out_specs=out_specs,
        out_shape=out_shape,
        scratch_shapes=[pltpu.VMEM((TPB, 4, TM), F32), pltpu.VMEM((N_EXP, 128), F32)],
        compiler_params=_cparams("arbitrary"),
        name="router" if attn is None else "odd_out_router",
    )(*args)


def _row(p):
    return pl.ds(pl.multiple_of(p, ROW), ROW)


def _expert_segment(n, base, xo, wg_ref, wu_ref, wd_ref):
    def ffn_rows(row0, rows):
        x = jnp.concatenate([xo[pl.ds(row0 * ROW + c, rows, stride=ROW), :] for c in range(ROW)], axis=1)
        xb = x.astype(BF16)
        a = _dot(xb, wg_ref[...])
        u = _dot(xb, wu_ref[...])
        hid = (a / (1.0 + jnp.exp(-a))) * u
        out = _dot(hid.astype(BF16), wd_ref[...])
        valid = lax.broadcasted_iota(jnp.int32, (rows, 1), 0) < base + n - row0
        res = jnp.where(valid, out, x)
        for c in range(ROW):
            xo[pl.ds(row0 * ROW + c, rows, stride=ROW), :] = res[:, c * 128:(c + 1) * 128]

    def chunk(jc, carry):
        ffn_rows(base + jc * CH, CH)
        return carry

    nfull = (n + CH - CH_TAIL - 1) // CH
    lax.fori_loop(0, nfull, chunk, 0)

    @pl.when(n > nfull * CH)
    def _():
        ffn_rows(base + nfull * CH, CH_TAIL)


def _experts_kernel(cnt_ref, off_ref, pos_ref, w_ref, h_ref, wg_ref, wu_ref, wd_ref, x_ref, mod_ref, fg_ref,
                    *rest, final):
    if final:
        op_ref, os_ref, xo, stg = rest
    else:
        o_ref, xo, stg = rest
    b = pl.program_id(0)
    s = pl.program_id(1)

    @pl.when(s == 0)
    def _():
        xo[pl.ds(2 * BT * ROW, CH * ROW), :] = jnp.zeros((CH * ROW, 128), F32)

    @pl.when(s < TPB)
    def _():
        hs = h_ref[...].astype(F32)
        for c in range(ROW):
            stg[pl.ds(c, TM, stride=ROW), :] = hs[:, c * 128:(c + 1) * 128]
        for t in range(TM):
            v = stg[t * ROW:(t + 1) * ROW, :]
            xo[_row(pos_ref[0, t]), :] = v
            xo[_row(pos_ref[1, t]), :] = v

    @pl.when(jnp.logical_and(s >= TPB, s < TPB + NES))
    def _():
        for k in range(XPS):
            e = (s - TPB) * XPS + k
            _expert_segment(cnt_ref[b, e], off_ref[b, e], xo, wg_ref.at[k], wu_ref.at[k], wd_ref.at[k])

    @pl.when(s >= TPB + NES - 1)
    def _():
        j = s - (TPB + NES - 1)
        tile = b * TPB + j
        g2 = mod_ref[5, pl.ds(_mod_row(tile), 1), :]
        for t in range(TM):
            a = xo[_row(pos_ref[0, t]), :]
            u = xo[_row(pos_ref[1, t]), :]
            stg[t * ROW:(t + 1) * ROW, :] = w_ref[0, t] * a + w_ref[1, t] * u
        y = jnp.concatenate([stg[pl.ds(c, TM, stride=ROW), :] for c in range(ROW)], axis=1)
        res = x_ref[...] + g2 * y
        if final:
            ms = jnp.mean(res * res, axis=-1, keepdims=True)
            res = res * lax.rsqrt(ms + EPS) * fg_ref[...]

            @pl.when(tile < NPT)
            def _():
                op_ref[...] = res

            @pl.when(tile >= NPT)
            def _():
                os_ref[...] = res
        else:
            o_ref[...] = res


def _experts(cnt, off, pos, wts, h, wg, wu, wd, x, mod_l, fg, *, layer, final):
    def tile_of(b, s):
        return b * TPB + jnp.clip(s - (TPB + NES - 1), 0, TPB - 1)

    def expert_of(b, s, *_):
        return (layer, jnp.clip(s - TPB, 0, NES - 1), 0, 0)

    def smem_tile(b, s, *_):
        return (b, jnp.where(s < TPB, s, jnp.clip(s - (TPB + NES - 1), 0, TPB - 1)), 0, 0)

    smem_blk = pl.BlockSpec((None, None, 2, TM), smem_tile, memory_space=pltpu.SMEM)
    if final:
        out_specs = [pl.BlockSpec((TM, D), lambda b, s, *_: (jnp.minimum(tile_of(b, s), NPT - 1), 0)),
                     pl.BlockSpec((TM, D), lambda b, s, *_: (jnp.maximum(tile_of(b, s) - NPT, 0), 0))]
        out_shape = [jax.ShapeDtypeStruct((N_P, D), F32), jax.ShapeDtypeStruct((N_S, D), F32)]
    else:
        out_specs = pl.BlockSpec((TM, D), lambda b, s, *_: (tile_of(b, s), 0))
        out_shape = jax.ShapeDtypeStruct((N_TOK, D), F32)
    grid_spec = pltpu.PrefetchScalarGridSpec(
        num_scalar_prefetch=2,
        grid=(NB, 2 * TPB + NES - 1),
        in_specs=[
            smem_blk,
            smem_blk,
            pl.BlockSpec((TM, D), lambda b, s, *_: (b * TPB + jnp.minimum(s, TPB - 1), 0)),
            pl.BlockSpec((None, XPS, D, D_EXP), expert_of),
            pl.BlockSpec((None, XPS, D, D_EXP), expert_of),
            pl.BlockSpec((None, XPS, D_EXP, D), expert_of),
            pl.BlockSpec((TM, D), lambda b, s, *_: (tile_of(b, s), 0)),
            pl.BlockSpec((6, MOD_ROWS, D), lambda b, s, *_: (0, 0, 0)),
            pl.BlockSpec((1, D), lambda b, s, *_: (0, 0)),
        ],
        out_specs=out_specs,
        scratch_shapes=[pltpu.VMEM((XO_ROWS * ROW, 128), F32), pltpu.VMEM((TM * ROW, 128), F32)],
    )
    return pl.pallas_call(
        functools.partial(_experts_kernel, final=final),
        grid_spec=grid_spec,
        out_shape=out_shape,
        compiler_params=_cparams("arbitrary", "arbitrary"),
        name="experts_final" if final else "experts",
    )(cnt, off, pos, wts, h, wg, wu, wd, x, mod_l, fg)


def _moe_sparse(x, routed, mod_l, wg, wu, wd, fg, *, layer, final):
    h, pos, wts, cnt, off = routed
    cnt = cnt[:, :, 0].astype(jnp.int32)
    off = off[:, :, 0].astype(jnp.int32)
    return _experts(cnt, off, pos, wts, h, wg, wu, wd, x, mod_l, fg, layer=layer, final=final)


def kernel(x_prompt, x_sample, cache_na_k, cache_na_v, cache_sw_k, cache_sw_v, c, c_ctx, mod_w, mod_b, norm_mix_g, norm_ffn_g, ev_w_in, ev_pool_w, ev_pool_scale, ev_conv_w, ev_w_out, od_w_in, od_rpb, od_sink, od_w_out, router_w, router_b, moe_w_gate, moe_w_up, moe_w_down, final_norm_g):
    xp = x_prompt.reshape(N_P, D)
    xs = x_sample.reshape(N_S, D)
    cvec = jnp.concatenate([c_ctx[None, :], c, jnp.zeros((MOD_ROWS - 1 - DEC_BATCH, D), F32)], axis=0)
    mod = _modulation(cvec, mod_w, mod_b)

    rw_pad = jnp.pad(router_w, ((0, 0), (0, 128 - N_EXP)))
    rw_hi = rw_pad.astype(BF16)
    rw2 = jnp.stack([rw_hi, (rw_pad - rw_hi.astype(F32)).astype(BF16)])
    rb_col = jnp.pad(router_b, (0, 128 - N_EXP)).reshape(128, 1)

    conv_w = jnp.pad(ev_conv_w[0], ((0, 8 - ev_conv_w.shape[1]), (0, 0)))
    x, wg, wu, wd = _even_layer(xp, xs, mod[0], norm_mix_g[0:1], ev_w_in[0].astype(BF16),
                                ev_pool_w[0].astype(BF16), ev_pool_scale[0:1], conv_w, ev_w_out[0].astype(BF16),
                                moe_w_gate, moe_w_up, moe_w_down)
    fg = final_norm_g.reshape(1, D)
    routed = _router(x, mod[0], norm_ffn_g[0:1], rw2, rb_col)
    x = _moe_sparse(x, routed, mod[0], wg, wu, wd, fg, layer=0, final=False)

    cos, sin = _rope_tables()
    w_in = od_w_in[0].astype(BF16)
    g1 = norm_mix_g[1:2]
    qna_p, kna_p, vna_p, qsw_p, ksw_p, vsw_p, nak, nav, swk, swv = _odd_in(x, mod[1], g1, w_in, cos, sin, prompt=True)
    qna_s, kna_s, vna_s, qsw_s, ksw_s, vsw_s = _odd_in(x, mod[1], g1, w_in, cos, sin, prompt=False)
    sink = od_sink[0]
    ona_p, osw_p = _ctx_attn(sink, qna_p, kna_p, vna_p, qsw_p, ksw_p, vsw_p)
    bias = _na_bias(od_rpb[0])
    ona_s = _na_attn(qna_s, kna_s, vna_s,
                     cache_na_k[:, 0].reshape(DEC_BATCH, PAST, NA_W).astype(BF16),
                     cache_na_v[:, 0].reshape(DEC_BATCH, PAST, NA_W).astype(BF16), bias)
    osw_s = _sw_attn(sink, qsw_s, ksw_s, vsw_s,
                     cache_sw_k[:, 0].reshape(DEC_BATCH, PAST, SWKV_W).astype(BF16),
                     cache_sw_v[:, 0].reshape(DEC_BATCH, PAST, SWKV_W).astype(BF16))
    x, *routed = _router(x, mod[1], norm_ffn_g[1:2], rw2, rb_col,
                         attn=(ona_p, ona_s, osw_p, osw_s, od_w_out[0].astype(BF16)))
    y_prompt, y_sample = _moe_sparse(x, routed, mod[1], wg, wu, wd, fg, layer=1, final=True)
    y_prompt = y_prompt.reshape(BATCH, SEQ, D)
    y_sample = y_sample.reshape(DEC_BATCH, DEC_SEQ, D)
    new_na_k = nak.reshape(BATCH, 1, SEQ, NA_H, DH)
    new_na_v = nav.reshape(BATCH, 1, SEQ, NA_H, DH)
    new_sw_k = swk.reshape(BATCH, 1, SEQ, SW_KV, DH)
    new_sw_v = swv.reshape(BATCH, 1, SEQ, SW_KV, DH)
    return (y_prompt, y_sample, new_na_k, new_na_v, new_sw_k, new_sw_v)
```

```python
import functools

import jax
import jax.numpy as jnp
import numpy as np
from jax import lax
from jax.experimental import pallas as pl
from jax.experimental.pallas import tpu as pltpu

D = 1024
BATCH = 16
SEQ = 256
DEC_BATCH = 4
DEC_SEQ = 4096
PAST = 512
GRID_W = 64
ROWS = DEC_SEQ // GRID_W
DH = 64
POOL_W = 512
POOL_WINDOWS = (2, 4, 8, 16)
POOL_GW = 128
CONV_W = 512
EVEN_IN = POOL_W + 3 * CONV_W
NA_H = 8
NA_ROWS = 8
NA_COLS = 16
SW_H = 8
SW_KV = 2
SW_G = SW_H // SW_KV
SW_WIN = 128
ABLK = 128
NA_W = NA_H * DH
SWQ_W = SW_H * DH
SWKV_W = SW_KV * DH
ODD_IN = 3 * NA_W + SWQ_W + 2 * SWKV_W
N_EXP = 16
N_GRP = 4
EPG = 4
D_EXP = 512
EPS = 1e-6
NEG = -1e30
ROPE_BASE = 10000.0
QK_SCALE = DH ** -0.5
assert QK_SCALE == 0.125

N_P = BATCH * SEQ
N_S = DEC_BATCH * DEC_SEQ
N_TOK = N_P + N_S
MOD_ROWS = 8

TM = 256
NPT = N_P // TM
TPS = DEC_SEQ // TM
NT = N_TOK // TM
HALO = 8
EV_HALVES = 2
EV_TM = EV_HALVES * TM
CAST_STEPS = 32

F32 = jnp.float32
BF16 = jnp.bfloat16
VMEM_LIMIT = 56 * 1024 * 1024


def _cparams(*sem):
    return pltpu.CompilerParams(dimension_semantics=sem, vmem_limit_bytes=VMEM_LIMIT)


def _mod_row(i):
    return jnp.where(i < NPT, 0, 1 + (i - NPT) // TPS)


def _rms_mod(x, g, shift, scale):
    ms = jnp.mean(x * x, axis=-1, keepdims=True)
    y = x * lax.rsqrt(ms + EPS) * g
    return y * (1.0 + scale) + shift


def _dot(a, b):
    return jnp.dot(a, b, preferred_element_type=F32)


def _dot_nt(a, b):
    return lax.dot_general(a, b, (((1,), (1,)), ((), ())), preferred_element_type=F32)


def _mod_kernel(cv_ref, w_ref, b_ref, o_ref):
    cv = cv_ref[...]
    a = cv / (1.0 + jnp.exp(-cv))
    o_ref[...] = jnp.dot(a, w_ref[...], preferred_element_type=F32,
                         precision=lax.Precision.HIGHEST) + b_ref[...]


def _modulation(cvec, mod_w, mod_b):
    depth = mod_w.shape[0]
    return pl.pallas_call(
        _mod_kernel,
        grid=(depth, 6),
        in_specs=[
            pl.BlockSpec((MOD_ROWS, D), lambda l, j: (0, 0)),
            pl.BlockSpec((None, D, D), lambda l, j: (l, 0, j)),
            pl.BlockSpec((None, None, 1, D), lambda l, j: (l, j, 0, 0)),
        ],
        out_specs=pl.BlockSpec((None, None, MOD_ROWS, D), lambda l, j: (l, j, 0, 0)),
        out_shape=jax.ShapeDtypeStruct((depth, 6, MOD_ROWS, D), F32),
        compiler_params=_cparams("arbitrary", "arbitrary"),
        name="modulation",
    )(cvec, mod_w, mod_b.reshape(depth, 6, 1, D))


def _dual_specs(tm, width, npt):
    return [
        pl.BlockSpec((tm, width), lambda i: (jnp.minimum(i, npt - 1), 0)),
        pl.BlockSpec((tm, width), lambda i: (jnp.maximum(i - npt, 0), 0)),
    ]


def _even_kernel(xp_ref, xs_ref, xprev_ref, xnext_ref, mod_ref, g_ref, wi_ref, pw_ref, ps_ref, cw_ref,
                 wo_ref, wgf_ref, wuf_ref, wdf_ref, o_ref, wgb_ref, wub_ref, wdb_ref, pext, uext):
    i = pl.program_id(0)

    @pl.when(i < CAST_STEPS)
    def _():
        wgb_ref[...] = wgf_ref[...].astype(BF16)
        wub_ref[...] = wuf_ref[...].astype(BF16)
        wdb_ref[...] = wdf_ref[...].astype(BF16)

    npt = NPT // EV_HALVES
    r = _mod_row(i * EV_HALVES)
    is_p = i < npt
    seq_len = jnp.where(is_p, SEQ, DEC_SEQ)

    x = jnp.where(is_p, xp_ref[...], xs_ref[...])
    xe = jnp.concatenate([xprev_ref[...], x, xnext_ref[...]], axis=0)
    h = _rms_mod(xe, g_ref[...], mod_ref[0, pl.ds(r, 1), :], mod_ref[1, pl.ds(r, 1), :])
    ze_all = _dot(h.astype(BF16), wi_ref[...])
    row = lax.broadcasted_iota(jnp.int32, (TM + 2 * HALO, 1), 0)
    for hf in range(EV_HALVES):
        t0 = jnp.where(is_p, 0, ((i - npt) % (DEC_SEQ // EV_TM)) * EV_TM + hf * TM)
        first = t0 == 0
        last = t0 + TM == seq_len
        ze = ze_all[hf * TM:hf * TM + TM + 2 * HALO, :]
        outside = jnp.logical_or(jnp.logical_and(first, row < HALO), jnp.logical_and(last, row >= HALO + TM))
        ze = jnp.where(outside, 0.0, ze)
        pext[...] = ze[:, :POOL_W]
        uext[...] = ze[:, POOL_W + 2 * CONV_W:] * ze[:, POOL_W:POOL_W + CONV_W]
        p = ze[HALO:HALO + TM, :POOL_W]
        gb = ze[HALO:HALO + TM, POOL_W + CONV_W:POOL_W + 2 * CONV_W]

        t = t0 + lax.broadcasted_iota(jnp.int32, (TM, 1), 0)
        mixed = []
        for g, w in enumerate(POOL_WINDOWS):
            cols = slice(g * POOL_GW, (g + 1) * POOL_GW)
            acc = jnp.zeros((TM, POOL_GW), F32)
            for k in range(-(w // 2), w - (w // 2)):
                acc = acc + pext[pl.ds(HALO + k, TM), cols]
            lo = jnp.maximum(t - w // 2, 0)
            hi = jnp.minimum(t + (w - 1 - w // 2), seq_len - 1)
            cnt = (hi - lo + 1).astype(F32)
            d = acc / cnt - p[:, cols]
            mixed.append((_dot(d.astype(BF16), pw_ref[g]) * ps_ref[:, cols]).astype(BF16))
        conv = (uext[pl.ds(HALO - 1, TM), :] * cw_ref[0:1, :] + uext[pl.ds(HALO, TM), :] * cw_ref[1:2, :]
                + uext[pl.ds(HALO + 1, TM), :] * cw_ref[2:3, :])
        mixed.append((gb * conv).astype(BF16))
        y = _dot(jnp.concatenate(mixed, axis=1), wo_ref[...])
        rows = slice(hf * TM, (hf + 1) * TM)
        o_ref[rows, :] = x[rows, :] + mod_ref[2, pl.ds(r, 1), :] * y


def _even_layer(xp, xs, mod_l, g, w_in, pool_w, pool_scale, conv_w, w_out, wg, wu, wd):
    hb = EV_TM // HALO
    nhb = N_S // HALO
    npt = N_P // EV_TM
    nt = N_TOK // EV_TM
    wg2, wu2, wd2 = wg.reshape(-1, D_EXP), wu.reshape(-1, D_EXP), wd.reshape(-1, D)
    rows_in, rows_out = wg2.shape[0] // CAST_STEPS, wd2.shape[0] // CAST_STEPS
    assert rows_in * CAST_STEPS == wg2.shape[0] and rows_out * CAST_STEPS == wd2.shape[0] and CAST_STEPS <= nt
    cast_blk = lambda i: (jnp.minimum(i, CAST_STEPS - 1), 0)
    cast_specs = [pl.BlockSpec((rows_in, D_EXP), cast_blk), pl.BlockSpec((rows_in, D_EXP), cast_blk),
                  pl.BlockSpec((rows_out, D), cast_blk)]
    outs = pl.pallas_call(
        _even_kernel,
        grid=(nt,),
        in_specs=_dual_specs(EV_TM, D, npt) + [
            pl.BlockSpec((HALO, D), lambda i: (jnp.maximum((i - npt) * hb - 1, 0), 0)),
            pl.BlockSpec((HALO, D), lambda i: (jnp.clip((i - npt + 1) * hb, 0, nhb - 1), 0)),
            pl.BlockSpec((6, MOD_ROWS, D), lambda i: (0, 0, 0)),
            pl.BlockSpec((1, D), lambda i: (0, 0)),
            pl.BlockSpec((D, EVEN_IN), lambda i: (0, 0)),
            pl.BlockSpec((4, POOL_GW, POOL_GW), lambda i: (0, 0, 0)),
            pl.BlockSpec((1, POOL_W), lambda i: (0, 0)),
            pl.BlockSpec((8, CONV_W), lambda i: (0, 0)),
            pl.BlockSpec((D, D), lambda i: (0, 0)),
        ] + cast_specs,
        out_specs=[pl.BlockSpec((EV_TM, D), lambda i: (i, 0))] + cast_specs,
        out_shape=[jax.ShapeDtypeStruct((N_TOK, D), F32), jax.ShapeDtypeStruct(wg2.shape, BF16),
                   jax.ShapeDtypeStruct(wu2.shape, BF16), jax.ShapeDtypeStruct(wd2.shape, BF16)],
        scratch_shapes=[pltpu.VMEM((TM + 2 * HALO, POOL_W), F32),
                        pltpu.VMEM((TM + 2 * HALO, CONV_W), F32)],
        compiler_params=_cparams("arbitrary"),
        name="even_layer",
    )(xp, xs, xs, xs, mod_l, g, w_in, pool_w, pool_scale, conv_w, w_out, wg2, wu2, wd2)
    return outs[0], outs[1].reshape(wg.shape), outs[2].reshape(wu.shape), outs[3].reshape(wd.shape)


def _rope(x, cos, sin_signed):
    n = x.shape[1] // 128
    cosf = jnp.concatenate([cos] * n, axis=1) if n > 1 else cos
    sinf = jnp.concatenate([sin_signed] * n, axis=1) if n > 1 else sin_signed
    w = x.shape[1]
    lane = lax.broadcasted_iota(jnp.int32, x.shape, 1)
    up = pltpu.roll(x, w - DH // 4, 1)
    dn = pltpu.roll(x, DH // 4, 1)
    rot = jnp.where((lane % (DH // 2)) < DH // 4, up, dn)
    return x * cosf + rot * sinf


def _odd_in_kernel(x_ref, mod_ref, g_ref, w_ref, cos_ref, sin_ref, *out_refs, tm, tile0, rope, kv_f32):
    r = _mod_row((pl.program_id(0) + tile0) * (tm // TM))
    h = _rms_mod(x_ref[...], g_ref[...], mod_ref[0, pl.ds(r, 1), :], mod_ref[1, pl.ds(r, 1), :])
    z = _dot(h.astype(BF16), w_ref[...])
    qna_ref, kna_ref, vna_ref, qsw_ref, ksw_ref, vsw_ref = out_refs[:6]
    c0 = 0
    qna = z[:, 0:NA_W]
    kna = z[:, NA_W:2 * NA_W]
    vna = z[:, 2 * NA_W:3 * NA_W]
    c0 = 3 * NA_W
    qsw = z[:, c0:c0 + SWQ_W]
    ksw = z[:, c0 + SWQ_W:c0 + SWQ_W + SWKV_W]
    vsw = z[:, c0 + SWQ_W + SWKV_W:]
    if kv_f32:
        for ref, val in zip(out_refs[6:10], (kna, vna, ksw, vsw)):
            heads = val.shape[1] // DH
            for hh in range(heads):
                ref[pl.ds(hh, SEQ, stride=heads), :] = val[:, hh * DH:(hh + 1) * DH]
    if rope:
        cos = cos_ref[...]
        sin = sin_ref[...]
        qsw = _rope(qsw, cos, sin)
        ksw = _rope(ksw, cos, sin)
    qna_ref[...] = (qna * QK_SCALE).astype(BF16)
    kna_ref[...] = kna.astype(BF16)
    vna_ref[...] = vna.astype(BF16)
    qsw_ref[...] = (qsw * QK_SCALE).astype(BF16)
    ksw_ref[...] = ksw.astype(BF16)
    vsw_ref[...] = vsw.astype(BF16)


def _odd_in(x, mod_l, g, w, cos, sin, *, prompt):
    tm = TM if prompt else 2 * TM
    tile0 = 0 if prompt else N_P // tm
    nt = (N_P if prompt else N_S) // tm
    n = nt * tm
    widths = [NA_W, NA_W, NA_W, SWQ_W, SWKV_W, SWKV_W]
    out_shape = [jax.ShapeDtypeStruct((n, w_), BF16) for w_ in widths]
    out_specs = [pl.BlockSpec((tm, w_), lambda i: (i, 0)) for w_ in widths]
    if prompt:
        assert tm == SEQ
        for heads in (NA_H, NA_H, SW_KV, SW_KV):
            out_shape.append(jax.ShapeDtypeStruct((BATCH * SEQ * heads, DH), F32))
            out_specs.append(pl.BlockSpec((SEQ * heads, DH), lambda i: (i, 0)))
    return pl.pallas_call(
        functools.partial(_odd_in_kernel, tm=tm, tile0=tile0, rope=not prompt, kv_f32=prompt),
        grid=(nt,),
        in_specs=[
            pl.BlockSpec((tm, D), lambda i: (i + tile0, 0)),
            pl.BlockSpec((6, MOD_ROWS, D), lambda i: (0, 0, 0)),
            pl.BlockSpec((1, D), lambda i: (0, 0)),
            pl.BlockSpec((D, ODD_IN), lambda i: (0, 0)),
            pl.BlockSpec((tm, 128), lambda i: (i % (DEC_SEQ // tm), 0)),
            pl.BlockSpec((tm, 128), lambda i: (i % (DEC_SEQ // tm), 0)),
        ],
        out_specs=out_specs,
        out_shape=out_shape,
        compiler_params=_cparams("parallel"),
        name="odd_in_prompt" if prompt else "odd_in_latent",
    )(x, mod_l, g, w, cos, sin)


def _rope_tables():
    t = np.arange(DEC_SEQ)
    quarter = DH // 4
    inv = 1.0 / (ROPE_BASE ** (np.arange(quarter, dtype=np.float64) / quarter))

    def cos_sin(pos):
        ang = pos.astype(np.float64)[:, None] * inv[None, :]
        ang = np.concatenate([ang, ang], axis=-1)
        return np.cos(ang), np.sin(ang)

    cr, sr = cos_sin(t // GRID_W)
    cc, sc = cos_sin(t % GRID_W)
    cos = np.concatenate([cr, cc], axis=-1)
    sin = np.concatenate([sr, sc], axis=-1)
    sign = np.where((np.arange(DH) % (DH // 2)) < DH // 4, -1.0, 1.0)
    sin = sin * sign[None, :]
    cos = np.concatenate([cos, cos], axis=-1).astype(np.float32)
    sin = np.concatenate([sin, sin], axis=-1).astype(np.float32)
    return jnp.asarray(cos), jnp.asarray(sin)


def _softmax_pv(segs, sink=None):
    m = None
    for s, _ in segs:
        sm = jnp.max(s, axis=-1, keepdims=True)
        m = sm if m is None else jnp.maximum(m, sm)
    if sink is not None:
        m = jnp.maximum(m, sink)
    den = None
    acc = None
    for s, v in segs:
        p = jnp.exp(s - m)
        ps = jnp.sum(p, axis=-1, keepdims=True)
        den = ps if den is None else den + ps
        pv = _dot(p.astype(BF16), v)
        acc = pv if acc is None else acc + pv
    if sink is not None:
        den = den + jnp.exp(sink - m)
    return acc / den


def _sink_col(sink_ref, g, rows_per_head):
    row = lax.broadcasted_iota(jnp.int32, (SW_G * rows_per_head, 1), 0)
    col = jnp.zeros((SW_G * rows_per_head, 1), F32)
    for r in range(SW_G):
        col = jnp.where(row // rows_per_head == r, sink_ref[g * SW_G + r], col)
    return col


def _ctx_attn_kernel(sink_ref, qna_ref, kna_ref, vna_ref, qsw_ref, ksw_ref, vsw_ref, ona_ref, osw_ref):
    lane = lax.broadcasted_iota(jnp.int32, (SEQ, 2 * DH), 1)
    for p in range(NA_H // 2):
        cols = slice(p * 2 * DH, (p + 1) * 2 * DH)
        q = qna_ref[:, cols]
        k = kna_ref[:, cols]
        v = vna_ref[:, cols]
        outs = []
        for half in range(2):
            mine = (lane < DH) if half == 0 else (lane >= DH)
            qm = jnp.where(mine, q, jnp.zeros_like(q))
            outs.append(_softmax_pv([(_dot_nt(qm, k), v)]))
        ona_ref[:, cols] = jnp.where(lane < DH, outs[0], outs[1]).astype(BF16)
    outs = []
    for g in range(SW_KV):
        kc = slice(g * DH, (g + 1) * DH)
        q = jnp.concatenate([qsw_ref[:, (g * SW_G + r) * DH:(g * SW_G + r + 1) * DH] for r in range(SW_G)], axis=0)
        s = _dot_nt(q, ksw_ref[:, kc])
        o = _softmax_pv([(s, vsw_ref[:, kc])], sink=_sink_col(sink_ref, g, SEQ))
        outs.extend(o[r * SEQ:(r + 1) * SEQ, :] for r in range(SW_G))
    osw_ref[...] = jnp.concatenate(outs, axis=1).astype(BF16)


def _ctx_attn(sink, qna, kna, vna, qsw, ksw, vsw):
    def spec(w):
        return pl.BlockSpec((SEQ, w), lambda b: (b, 0))

    return pl.pallas_call(
        _ctx_attn_kernel,
        grid=(BATCH,),
        in_specs=[pl.BlockSpec(memory_space=pltpu.SMEM), spec(NA_W), spec(NA_W), spec(NA_W),
                  spec(SWQ_W), spec(SWKV_W), spec(SWKV_W)],
        out_specs=[spec(NA_W), spec(SWQ_W)],
        out_shape=[jax.ShapeDtypeStruct((N_P, NA_W), BF16), jax.ShapeDtypeStruct((N_P, SWQ_W), BF16)],
        compiler_params=_cparams("parallel"),
        name="ctx_attn",
    )(sink, qna, kna, vna, qsw, ksw, vsw)


NA_QB = 4
NA_KR = 12
NA_NQB = ROWS // NA_QB


def _na_block_offset(case, i, j):
    if case == 0:
        valid, dr = j < NA_ROWS, j - i + NA_ROWS - 1
    elif case == 1:
        valid, dr = i <= j < i + NA_ROWS, j - i + NA_ROWS // 2 - 1
    else:
        valid, dr = j >= NA_KR - NA_ROWS, j - i + NA_ROWS - 1 - (NA_KR - NA_QB)
    return dr if valid else None


def _na_bias_kernel(rpb_ref, o_ref):
    h = pl.program_id(0)
    cq = lax.broadcasted_iota(jnp.int32, (GRID_W, GRID_W), 0)
    ck = lax.broadcasted_iota(jnp.int32, (GRID_W, GRID_W), 1)
    cstart = jnp.clip(cq - NA_COLS // 2, 0, GRID_W - NA_COLS)
    ok = (ck >= cstart) & (ck < cstart + NA_COLS)
    dc = jnp.clip(ck - cq + NA_COLS - 1, 0, 2 * NA_COLS - 2)
    ndc = 2 * NA_COLS - 1
    neg = jnp.full((GRID_W, GRID_W), NEG, F32)
    by_offset = []
    for dr in range(2 * NA_ROWS - 1):
        b = jnp.zeros((GRID_W, GRID_W), F32)
        for e in range(ndc):
            b = jnp.where(dc == e, rpb_ref[h, dr * ndc + e], b)
        by_offset.append(jnp.where(ok, b, NEG))
    for case in range(3):
        for i in range(NA_QB):
            blocks = []
            for j in range(NA_KR):
                dr = _na_block_offset(case, i, j)
                blocks.append(neg if dr is None else by_offset[dr])
            o_ref[case, i * GRID_W:(i + 1) * GRID_W, :] = jnp.concatenate(blocks, axis=1)


def _na_bias(rpb):
    nd = 2 * NA_ROWS - 1
    return pl.pallas_call(
        _na_bias_kernel,
        grid=(NA_H,),
        in_specs=[pl.BlockSpec(memory_space=pltpu.SMEM)],
        out_specs=pl.BlockSpec((3, None, NA_QB * GRID_W, NA_KR * GRID_W), lambda h: (0, h, 0, 0)),
        out_shape=jax.ShapeDtypeStruct((3, NA_H, NA_QB * GRID_W, NA_KR * GRID_W), F32),
        compiler_params=_cparams("parallel"),
        name="na_bias",
    )(rpb.reshape(NA_H, nd * (2 * NA_COLS - 1)))


def _na_kernel(q_ref, k_ref, v_ref, kc_ref, vc_ref, bias_ref, o_ref):
    r0 = pl.program_id(1) * NA_QB
    start = pl.multiple_of(jnp.clip(r0 - NA_ROWS // 2, 0, ROWS - NA_KR) * GRID_W, GRID_W)
    nq = NA_QB * GRID_W
    lane = lax.broadcasted_iota(jnp.int32, (nq, 2 * DH), 1)
    for p in range(NA_H // 2):
        cols = slice(p * 2 * DH, (p + 1) * 2 * DH)
        q = q_ref[:, cols]
        kl = k_ref[pl.ds(start, NA_KR * GRID_W), cols]
        vl = v_ref[pl.ds(start, NA_KR * GRID_W), cols]
        kc = kc_ref[:, cols]
        vc = vc_ref[:, cols]
        zero = jnp.zeros_like(q)
        qm = jnp.concatenate([jnp.where(lane < DH, q, zero), jnp.where(lane < DH, zero, q)], axis=0)
        s_loc = _dot_nt(qm, kl)
        s_ctx = _dot_nt(qm, kc)
        outs = []
        for half in range(2):
            rows = slice(half * nq, (half + 1) * nq)
            outs.append(_softmax_pv([(s_loc[rows] + bias_ref[2 * p + half], vl), (s_ctx[rows], vc)]))
        o_ref[:, cols] = jnp.where(lane < DH, outs[0], outs[1]).astype(BF16)


def _na_attn(q, k, v, kc, vc, bias):
    nq = NA_QB * GRID_W

    def bias_case(b, rb):
        return (jnp.where(rb == 0, 0, jnp.where(rb == NA_NQB - 1, 2, 1)), 0, 0, 0)

    return pl.pallas_call(
        _na_kernel,
        grid=(DEC_BATCH, NA_NQB),
        in_specs=[
            pl.BlockSpec((nq, NA_W), lambda b, rb: (b * NA_NQB + rb, 0)),
            pl.BlockSpec((DEC_SEQ, NA_W), lambda b, rb: (b, 0)),
            pl.BlockSpec((DEC_SEQ, NA_W), lambda b, rb: (b, 0)),
            pl.BlockSpec((None, PAST, NA_W), lambda b, rb: (b, 0, 0)),
            pl.BlockSpec((None, PAST, NA_W), lambda b, rb: (b, 0, 0)),
            pl.BlockSpec((None, NA_H, nq, NA_KR * GRID_W), bias_case),
        ],
        out_specs=pl.BlockSpec((nq, NA_W), lambda b, rb: (b * NA_NQB + rb, 0)),
        out_shape=jax.ShapeDtypeStruct((N_S, NA_W), BF16),
        compiler_params=_cparams("parallel", "arbitrary"),
        name="na_attn",
    )(q, k, v, kc, vc, bias)


def _sw_window_bias():
    q = np.arange(SW_G * ABLK)[:, None] % ABLK
    k = np.arange(3 * ABLK)[None, :]
    tables = [np.where(np.abs(q + lead - k) <= SW_WIN, 0.0, NEG) for lead in (0, ABLK, 2 * ABLK)]
    return jnp.asarray(np.stack(tables).astype(np.float32))


SW_QB = 4


def _sw_kernel(sink_ref, q_ref, k_ref, v_ref, kc_ref, vc_ref, wb_ref, o_ref):
    nk = 3 * ABLK
    nb = DEC_SEQ // ABLK
    for sub in range(SW_QB):
        j = pl.program_id(1) * SW_QB + sub
        rows = slice(sub * ABLK, (sub + 1) * ABLK)
        start = pl.multiple_of(jnp.clip((j - 1) * ABLK, 0, DEC_SEQ - nk), ABLK)
        wb = wb_ref[jnp.where(j == 0, 0, jnp.where(j == nb - 1, 2, 1))]
        outs = []
        for g in range(SW_KV):
            kcols = slice(g * DH, (g + 1) * DH)
            q = jnp.concatenate([q_ref[rows, (g * SW_G + r) * DH:(g * SW_G + r + 1) * DH] for r in range(SW_G)],
                                axis=0)
            kw = k_ref[pl.ds(start, nk), kcols]
            vw = v_ref[pl.ds(start, nk), kcols]
            s_w = _dot_nt(q, kw) + wb
            s_c = _dot_nt(q, kc_ref[:, kcols])
            o = _softmax_pv([(s_w, vw), (s_c, vc_ref[:, kcols])], sink=_sink_col(sink_ref, g, ABLK))
            outs.extend(o[r * ABLK:(r + 1) * ABLK, :] for r in range(SW_G))
        o_ref[rows, :] = jnp.concatenate(outs, axis=1).astype(BF16)


def _sw_attn(sink, q, k, v, kc, vc):
    nsteps = DEC_SEQ // (ABLK * SW_QB)
    return pl.pallas_call(
        _sw_kernel,
        grid=(DEC_BATCH, nsteps),
        in_specs=[
            pl.BlockSpec(memory_space=pltpu.SMEM),
            pl.BlockSpec((SW_QB * ABLK, SWQ_W), lambda b, j: (b * nsteps + j, 0)),
            pl.BlockSpec((DEC_SEQ, SWKV_W), lambda b, j: (b, 0)),
            pl.BlockSpec((DEC_SEQ, SWKV_W), lambda b, j: (b, 0)),
            pl.BlockSpec((None, PAST, SWKV_W), lambda b, j: (b, 0, 0)),
            pl.BlockSpec((None, PAST, SWKV_W), lambda b, j: (b, 0, 0)),
            pl.BlockSpec((3, SW_G * ABLK, 3 * ABLK), lambda b, j: (0, 0, 0)),
        ],
        out_specs=pl.BlockSpec((SW_QB * ABLK, SWQ_W), lambda b, j: (b * nsteps + j, 0)),
        out_shape=jax.ShapeDtypeStruct((N_S, SWQ_W), BF16),
        compiler_params=_cparams("parallel", "arbitrary"),
        name="sw_attn",
    )(sink, q, k, v, kc, vc, _sw_window_bias())


BT = 4096
NB = N_TOK // BT
TPB = BT // TM
CH = 256
CH_TAIL = 128
XPS = 2
NES = N_EXP // XPS
DT = 2
NDS = TPB // DT
XO_ROWS = 2 * BT + CH
ROW = 8
assert D == ROW * 128


def _route_dense(sel):
    e = lax.broadcasted_iota(jnp.int32, sel.shape, 0)
    el = e % EPG
    g = e // EPG

    def group_rot(a, k):
        return jnp.where(el >= k, pltpu.roll(a, k, 0), pltpu.roll(a, (k - EPG) % N_EXP, 0))

    rank = jnp.zeros_like(sel)
    for k in range(1, EPG):
        partner = group_rot(sel, k)
        ahead = jnp.logical_or(partner > sel, jnp.logical_and(el >= k, partner == sel))
        rank = rank + ahead.astype(F32)
    contrib = jnp.where(rank < 2.0, sel, 0.0)
    score = contrib
    for k in range(1, EPG):
        score = score + group_rot(contrib, k)
    best = None
    for k in range(1, N_GRP):
        other = pltpu.roll(score, k * EPG, 0)
        c = jnp.logical_or(score > other, jnp.logical_and(g < k, score == other))
        best = c if best is None else jnp.logical_and(best, c)
    return jnp.logical_and(best, rank == 0.0), jnp.logical_and(best, rank == 1.0)


def _pick(mask, vals):
    return jnp.sum(jnp.where(mask, vals, 0.0), axis=0, keepdims=True)


def _route_tile(xv, mod_ref, g_ref, rw_ref, rb_ref, h_ref, pos_ref, w_ref, cnt_ref, off_ref, meta, carry):
    i = pl.program_id(0)
    j = i % TPB
    r = _mod_row(i)
    h = _rms_mod(xv, g_ref[...], mod_ref[3, pl.ds(r, 1), :], mod_ref[4, pl.ds(r, 1), :])
    h_ref[...] = h.astype(BF16)
    h_hi = h.astype(BF16)
    h_lo = (h - h_hi.astype(F32)).astype(BF16)
    logits = _dot(h_lo, rw_ref[0]) + _dot(h_hi, rw_ref[1]) + _dot(h_hi, rw_ref[0])
    lt = logits.T[:N_EXP, :]
    m = jnp.max(lt, axis=0, keepdims=True)
    ex = jnp.exp(lt - m)
    pr = ex / jnp.sum(ex, axis=0, keepdims=True)
    se = pr + rb_ref[:N_EXP, :]
    top1, top2 = _route_dense(se)

    @pl.when(j == 0)
    def _():
        carry[...] = jnp.zeros_like(carry)

    member = jnp.logical_or(top1, top2).astype(F32)
    s_idx = lax.broadcasted_iota(jnp.int32, (TM, TM), 0)
    t_idx = lax.broadcasted_iota(jnp.int32, (TM, TM), 1)
    before = jnp.where(s_idx < t_idx, 1.0, 0.0).astype(BF16)
    seen = _dot(member.astype(BF16), before) + carry[:, 0:1]
    ids = lax.broadcasted_iota(jnp.int32, (N_EXP, TM), 0).astype(F32)
    p1 = _pick(top1, pr)
    p2 = _pick(top2, pr)
    den = p1 + p2
    w_ref[j, 0:1, :] = p1 / den
    w_ref[j, 1:2, :] = p2 / den
    meta[j, 0:1, :] = _pick(top1, ids)
    meta[j, 1:2, :] = _pick(top2, ids)
    meta[j, 2:3, :] = _pick(top1, seen)
    meta[j, 3:4, :] = _pick(top2, seen)
    carry[...] = carry[...] + jnp.sum(member, axis=1, keepdims=True)

    @pl.when(j == TPB - 1)
    def _():
        cnt = carry[...]
        offs = [jnp.zeros((1, 128), F32)]
        for e in range(1, N_EXP):
            offs.append(offs[-1] + cnt[e - 1:e, :])
        cnt_ref[...] = cnt
        off_ref[...] = jnp.concatenate(offs, axis=0)
        for jj in range(TPB):
            for k in range(2):
                eid = meta[jj, k:k + 1, :]
                pos = meta[jj, 2 + k:3 + k, :]
                for e in range(1, N_EXP):
                    pos = pos + jnp.where(eid == float(e), offs[e][:, 0:1], 0.0)
                pos_ref[jj, k:k + 1, :] = (pos * float(ROW)).astype(jnp.int32)


def _router_kernel(x_ref, *refs):
    _route_tile(x_ref[...], *refs)


def _odd_out_router_kernel(nap_ref, nas_ref, swp_ref, sws_ref, x_ref, wo_ref, mod_ref, g_ref, rw_ref, rb_ref,
                           x1_ref, *refs):
    i = pl.program_id(0)
    is_p = i < NPT
    ona = jnp.where(is_p, nap_ref[...], nas_ref[...])
    osw = jnp.where(is_p, swp_ref[...], sws_ref[...])
    y = _dot(ona, wo_ref[:NA_W, :]) + _dot(osw, wo_ref[NA_W:, :])
    x1 = x_ref[...] + mod_ref[2, pl.ds(_mod_row(i), 1), :] * y
    x1_ref[...] = x1
    _route_tile(x1, mod_ref, g_ref, rw_ref, rb_ref, *refs)


def _router(x, mod_l, g, rw2, rb_col, attn=None):
    blk = lambda i: (i // TPB, 0, 0, 0)
    in_specs = [
        pl.BlockSpec((TM, D), lambda i: (i, 0)),
        pl.BlockSpec((6, MOD_ROWS, D), lambda i: (0, 0, 0)),
        pl.BlockSpec((1, D), lambda i: (0, 0)),
        pl.BlockSpec((2, D, 128), lambda i: (0, 0, 0)),
        pl.BlockSpec((128, 1), lambda i: (0, 0)),
    ]
    out_specs = [
        pl.BlockSpec((TM, D), lambda i: (i, 0)),
        pl.BlockSpec((None, TPB, 2, TM), blk),
        pl.BlockSpec((None, TPB, 2, TM), blk),
        pl.BlockSpec((None, N_EXP, 128), lambda i: (i // TPB, 0, 0)),
        pl.BlockSpec((None, N_EXP, 128), lambda i: (i // TPB, 0, 0)),
    ]
    out_shape = [
        jax.ShapeDtypeStruct((N_TOK, D), BF16),
        jax.ShapeDtypeStruct((NB, TPB, 2, TM), jnp.int32),
        jax.ShapeDtypeStruct((NB, TPB, 2, TM), F32),
        jax.ShapeDtypeStruct((NB, N_EXP, 128), F32),
        jax.ShapeDtypeStruct((NB, N_EXP, 128), F32),
    ]
    args = (x, mod_l, g, rw2, rb_col)
    body = _router_kernel
    if attn is not None:
        nap, nas, swp, sws, w_out = attn
        in_specs = (_dual_specs(TM, NA_W, NPT) + _dual_specs(TM, SWQ_W, NPT) + in_specs[:1]
                    + [pl.BlockSpec((D, D), lambda i: (0, 0))] + in_specs[1:])
        out_specs = [pl.BlockSpec((TM, D), lambda i: (i, 0))] + out_specs
        out_shape = [jax.ShapeDtypeStruct((N_TOK, D), F32)] + out_shape
        args = (nap, nas, swp, sws, x, w_out, mod_l, g, rw2, rb_col)
        body = _odd_out_router_kernel
    return pl.pallas_call(
        body,
        grid=(NT,),
        in_specs=in_specs,
        out_specs=out_specs,
        out_shape=out_shape,
        scratch_shapes=[pltpu.VMEM((TPB, 4, TM), F32), pltpu.VMEM((N_EXP, 128), F32)],
        compiler_params=_cparams("arbitrary"),
        name="router" if attn is None else "odd_out_router",
    )(*args)


def _row(p):
    return pl.ds(pl.multiple_of(p, ROW), ROW)


def _expert_segment(n, base, xo, wg_ref, wu_ref, wd_ref):
    def ffn_rows(row0, rows):
        x = jnp.concatenate([xo[pl.ds(row0 * ROW + c, rows, stride=ROW), :] for c in range(ROW)], axis=1)
        xb = x.astype(BF16)
        a = _dot(xb, wg_ref[...])
        u = _dot(xb, wu_ref[...])
        hid = (a / (1.0 + jnp.exp(-a))) * u
        out = _dot(hid.astype(BF16), wd_ref[...])
        valid = lax.broadcasted_iota(jnp.int32, (rows, 1), 0) < base + n - row0
        res = jnp.where(valid, out, x)
        for c in range(ROW):
            xo[pl.ds(row0 * ROW + c, rows, stride=ROW), :] = res[:, c * 128:(c + 1) * 128]

    def chunk(jc, carry):
        ffn_rows(base + jc * CH, CH)
        return carry

    nfull = (n + CH - CH_TAIL - 1) // CH
    lax.fori_loop(0, nfull, chunk, 0)

    @pl.when(n > nfull * CH)
    def _():
        ffn_rows(base + nfull * CH, CH_TAIL)


def _experts_kernel(cnt_ref, off_ref, dpos_ref, pos_ref, w_ref, h_ref, wg_ref, wu_ref, wd_ref, x_ref, mod_ref,
                    fg_ref, *rest, final):
    if final:
        op_ref, os_ref, xo, stg = rest
    else:
        o_ref, xo, stg = rest
    b = pl.program_id(0)
    s = pl.program_id(1)

    @pl.when(s == 0)
    def _():
        xo[pl.ds(2 * BT * ROW, CH * ROW), :] = jnp.zeros((CH * ROW, 128), F32)

    @pl.when(s < NDS)
    def _():
        for dt in range(DT):
            hs = h_ref[dt * TM:(dt + 1) * TM, :].astype(F32)
            for c in range(ROW):
                stg[pl.ds(c, TM, stride=ROW), :] = hs[:, c * 128:(c + 1) * 128]
            for t in range(TM):
                v = stg[t * ROW:(t + 1) * ROW, :]
                xo[_row(dpos_ref[dt, 0, t]), :] = v
                xo[_row(dpos_ref[dt, 1, t]), :] = v

    @pl.when(jnp.logical_and(s >= NDS, s < NDS + NES))
    def _():
        for k in range(XPS):
            e = (s - NDS) * XPS + k
            _expert_segment(cnt_ref[b, e], off_ref[b, e], xo, wg_ref.at[k], wu_ref.at[k], wd_ref.at[k])

    @pl.when(s >= NDS + NES - 1)
    def _():
        j = s - (NDS + NES - 1)
        tile = b * TPB + j
        g2 = mod_ref[5, pl.ds(_mod_row(tile), 1), :]
        for t in range(TM):
            a = xo[_row(pos_ref[0, t]), :]
            u = xo[_row(pos_ref[1, t]), :]
            stg[t * ROW:(t + 1) * ROW, :] = w_ref[0, t] * a + w_ref[1, t] * u
        y = jnp.concatenate([stg[pl.ds(c, TM, stride=ROW), :] for c in range(ROW)], axis=1)
        res = x_ref[...] + g2 * y
        if final:
            ms = jnp.mean(res * res, axis=-1, keepdims=True)
            res = res * lax.rsqrt(ms + EPS) * fg_ref[...]

            @pl.when(tile < NPT)
            def _():
                op_ref[...] = res

            @pl.when(tile >= NPT)
            def _():
                os_ref[...] = res
        else:
            o_ref[...] = res


def _experts(cnt, off, pos, wts, h, wg, wu, wd, x, mod_l, fg, *, layer, final):
    def tile_of(b, s):
        return b * TPB + jnp.clip(s - (NDS + NES - 1), 0, TPB - 1)

    def expert_of(b, s, *_):
        return (layer, jnp.clip(s - NDS, 0, NES - 1), 0, 0)

    def dispatch_of(b, s, *_):
        return (b, jnp.minimum(s, NDS - 1), 0, 0)

    def combine_of(b, s, *_):
        return (b, jnp.clip(s - (NDS + NES - 1), 0, TPB - 1), 0, 0)

    smem_disp = pl.BlockSpec((None, DT, 2, TM), dispatch_of, memory_space=pltpu.SMEM)
    smem_blk = pl.BlockSpec((None, None, 2, TM), combine_of, memory_space=pltpu.SMEM)
    if final:
        out_specs = [pl.BlockSpec((TM, D), lambda b, s, *_: (jnp.minimum(tile_of(b, s), NPT - 1), 0)),
                     pl.BlockSpec((TM, D), lambda b, s, *_: (jnp.maximum(tile_of(b, s) - NPT, 0), 0))]
        out_shape = [jax.ShapeDtypeStruct((N_P, D), F32), jax.ShapeDtypeStruct((N_S, D), F32)]
    else:
        out_specs = pl.BlockSpec((TM, D), lambda b, s, *_: (tile_of(b, s), 0))
        out_shape = jax.ShapeDtypeStruct((N_TOK, D), F32)
    grid_spec = pltpu.PrefetchScalarGridSpec(
        num_scalar_prefetch=2,
        grid=(NB, NDS + NES + TPB - 1),
        in_specs=[
            smem_disp,
            smem_blk,
            smem_blk,
            pl.BlockSpec((DT * TM, D), lambda b, s, *_: (b * NDS + jnp.minimum(s, NDS - 1), 0)),
            pl.BlockSpec((None, XPS, D, D_EXP), expert_of),
            pl.BlockSpec((None, XPS, D, D_EXP), expert_of),
            pl.BlockSpec((None, XPS, D_EXP, D), expert_of),
            pl.BlockSpec((TM, D), lambda b, s, *_: (tile_of(b, s), 0)),
            pl.BlockSpec((6, MOD_ROWS, D), lambda b, s, *_: (0, 0, 0)),
            pl.BlockSpec((1, D), lambda b, s, *_: (0, 0)),
        ],
        out_specs=out_specs,
        scratch_shapes=[pltpu.VMEM((XO_ROWS * ROW, 128), F32), pltpu.VMEM((TM * ROW, 128), F32)],
    )
    return pl.pallas_call(
        functools.partial(_experts_kernel, final=final),
        grid_spec=grid_spec,
        out_shape=out_shape,
        compiler_params=_cparams("arbitrary", "arbitrary"),
        name="experts_final" if final else "experts",
    )(cnt, off, pos, pos, wts, h, wg, wu, wd, x, mod_l, fg)


def _moe_sparse(x, routed, mod_l, wg, wu, wd, fg, *, layer, final):
    h, pos, wts, cnt, off = routed
    cnt = cnt[:, :, 0].astype(jnp.int32)
    off = off[:, :, 0].astype(jnp.int32)
    return _experts(cnt, off, pos, wts, h, wg, wu, wd, x, mod_l, fg, layer=layer, final=final)


def kernel(x_prompt, x_sample, cache_na_k, cache_na_v, cache_sw_k, cache_sw_v, c, c_ctx, mod_w, mod_b, norm_mix_g, norm_ffn_g, ev_w_in, ev_pool_w, ev_pool_scale, ev_conv_w, ev_w_out, od_w_in, od_rpb, od_sink, od_w_out, router_w, router_b, moe_w_gate, moe_w_up, moe_w_down, final_norm_g):
    xp = x_prompt.reshape(N_P, D)
    xs = x_sample.reshape(N_S, D)
    cvec = jnp.concatenate([c_ctx[None, :], c, jnp.zeros((MOD_ROWS - 1 - DEC_BATCH, D), F32)], axis=0)
    mod = _modulation(cvec, mod_w, mod_b)

    rw_pad = jnp.pad(router_w, ((0, 0), (0, 128 - N_EXP)))
    rw_hi = rw_pad.astype(BF16)
    rw2 = jnp.stack([rw_hi, (rw_pad - rw_hi.astype(F32)).astype(BF16)])
    rb_col = jnp.pad(router_b, (0, 128 - N_EXP)).reshape(128, 1)

    conv_w = jnp.pad(ev_conv_w[0], ((0, 8 - ev_conv_w.shape[1]), (0, 0)))
    x, wg, wu, wd = _even_layer(xp, xs, mod[0], norm_mix_g[0:1], ev_w_in[0].astype(BF16),
                                ev_pool_w[0].astype(BF16), ev_pool_scale[0:1], conv_w, ev_w_out[0].astype(BF16),
                                moe_w_gate, moe_w_up, moe_w_down)
    fg = final_norm_g.reshape(1, D)
    routed = _router(x, mod[0], norm_ffn_g[0:1], rw2, rb_col)
    x = _moe_sparse(x, routed, mod[0], wg, wu, wd, fg, layer=0, final=False)

    cos, sin = _rope_tables()
    w_in = od_w_in[0].astype(BF16)
    g1 = norm_mix_g[1:2]
    qna_p, kna_p, vna_p, qsw_p, ksw_p, vsw_p, nak, nav, swk, swv = _odd_in(x, mod[1], g1, w_in, cos, sin, prompt=True)
    qna_s, kna_s, vna_s, qsw_s, ksw_s, vsw_s = _odd_in(x, mod[1], g1, w_in, cos, sin, prompt=False)
    sink = od_sink[0]
    ona_p, osw_p = _ctx_attn(sink, qna_p, kna_p, vna_p, qsw_p, ksw_p, vsw_p)
    bias = _na_bias(od_rpb[0])
    ona_s = _na_attn(qna_s, kna_s, vna_s,
                     cache_na_k[:, 0].reshape(DEC_BATCH, PAST, NA_W).astype(BF16),
                     cache_na_v[:, 0].reshape(DEC_BATCH, PAST, NA_W).astype(BF16), bias)
    osw_s = _sw_attn(sink, qsw_s, ksw_s, vsw_s,
                     cache_sw_k[:, 0].reshape(DEC_BATCH, PAST, SWKV_W).astype(BF16),
                     cache_sw_v[:, 0].reshape(DEC_BATCH, PAST, SWKV_W).astype(BF16))
    x, *routed = _router(x, mod[1], norm_ffn_g[1:2], rw2, rb_col,
                         attn=(ona_p, ona_s, osw_p, osw_s, od_w_out[0].astype(BF16)))
    y_prompt, y_sample = _moe_sparse(x, routed, mod[1], wg, wu, wd, fg, layer=1, final=True)
    y_prompt = y_prompt.reshape(BATCH, SEQ, D)
    y_sample = y_sample.reshape(DEC_BATCH, DEC_SEQ, D)
    new_na_k = nak.reshape(BATCH, 1, SEQ, NA_H, DH)
    new_na_v = nav.reshape(BATCH, 1, SEQ, NA_H, DH)
    new_sw_k = swk.reshape(BATCH, 1, SEQ, SW_KV, DH)
    new_sw_v = swv.reshape(BATCH, 1, SEQ, SW_KV, DH)
    return (y_prompt, y_sample, new_na_k, new_na_v, new_sw_k, new_sw_v)
```

```python
import functools

import jax
import jax.numpy as jnp
import numpy as np
from jax import lax
from jax.experimental import pallas as pl
from jax.experimental.pallas import tpu as pltpu

D = 1024
BATCH = 16
SEQ = 256
DEC_BATCH = 4
DEC_SEQ = 4096
PAST = 512
GRID_W = 64
ROWS = DEC_SEQ // GRID_W
DH = 64
POOL_W = 512
POOL_WINDOWS = (2, 4, 8, 16)
POOL_GW = 128
CONV_W = 512
EVEN_IN = POOL_W + 3 * CONV_W
NA_H = 8
NA_ROWS = 8
NA_COLS = 16
SW_H = 8
SW_KV = 2
SW_G = SW_H // SW_KV
SW_WIN = 128
ABLK = 128
NA_W = NA_H * DH
SWQ_W = SW_H * DH
SWKV_W = SW_KV * DH
ODD_IN = 3 * NA_W + SWQ_W + 2 * SWKV_W
N_EXP = 16
N_GRP = 4
EPG = 4
D_EXP = 512
EPS = 1e-6
NEG = -1e30
ROPE_BASE = 10000.0
QK_SCALE = DH ** -0.5
assert QK_SCALE == 0.125

N_P = BATCH * SEQ
N_S = DEC_BATCH * DEC_SEQ
N_TOK = N_P + N_S
MOD_ROWS = 8

TM = 256
NPT = N_P // TM
TPS = DEC_SEQ // TM
NT = N_TOK // TM
HALO = 8
EV_HALVES = 2
EV_TM = EV_HALVES * TM
CAST_STEPS = 32

F32 = jnp.float32
BF16 = jnp.bfloat16
VMEM_LIMIT = 60 * 1024 * 1024


def _cparams(*sem):
    return pltpu.CompilerParams(dimension_semantics=sem, vmem_limit_bytes=VMEM_LIMIT)


def _mod_row(i):
    return jnp.where(i < NPT, 0, 1 + (i - NPT) // TPS)


def _rms_mod(x, g, shift, scale):
    ms = jnp.mean(x * x, axis=-1, keepdims=True)
    y = x * lax.rsqrt(ms + EPS) * g
    return y * (1.0 + scale) + shift


def _dot(a, b):
    return jnp.dot(a, b, preferred_element_type=F32)


def _dot_nt(a, b):
    return lax.dot_general(a, b, (((1,), (1,)), ((), ())), preferred_element_type=F32)


def _mod_kernel(cv_ref, w_ref, b_ref, o_ref):
    cv = cv_ref[...]
    a = cv / (1.0 + jnp.exp(-cv))
    o_ref[...] = jnp.dot(a, w_ref[...], preferred_element_type=F32,
                         precision=lax.Precision.HIGHEST) + b_ref[...]


def _modulation(cvec, mod_w, mod_b):
    depth = mod_w.shape[0]
    return pl.pallas_call(
        _mod_kernel,
        grid=(depth, 6),
        in_specs=[
            pl.BlockSpec((MOD_ROWS, D), lambda l, j: (0, 0)),
            pl.BlockSpec((None, D, D), lambda l, j: (l, 0, j)),
            pl.BlockSpec((None, None, 1, D), lambda l, j: (l, j, 0, 0)),
        ],
        out_specs=pl.BlockSpec((None, None, MOD_ROWS, D), lambda l, j: (l, j, 0, 0)),
        out_shape=jax.ShapeDtypeStruct((depth, 6, MOD_ROWS, D), F32),
        compiler_params=_cparams("arbitrary", "arbitrary"),
        name="modulation",
    )(cvec, mod_w, mod_b.reshape(depth, 6, 1, D))


def _dual_specs(tm, width, npt):
    return [
        pl.BlockSpec((tm, width), lambda i: (jnp.minimum(i, npt - 1), 0)),
        pl.BlockSpec((tm, width), lambda i: (jnp.maximum(i - npt, 0), 0)),
    ]


def _even_kernel(xp_ref, xs_ref, xprev_ref, xnext_ref, mod_ref, g_ref, wi_ref, pw_ref, ps_ref, cw_ref,
                 wo_ref, wgf_ref, wuf_ref, wdf_ref, o_ref, wgb_ref, wub_ref, wdb_ref, pext, uext):
    i = pl.program_id(0)

    @pl.when(i < CAST_STEPS)
    def _():
        wgb_ref[...] = wgf_ref[...].astype(BF16)
        wub_ref[...] = wuf_ref[...].astype(BF16)
        wdb_ref[...] = wdf_ref[...].astype(BF16)

    npt = NPT // EV_HALVES
    r = _mod_row(i * EV_HALVES)
    is_p = i < npt
    seq_len = jnp.where(is_p, SEQ, DEC_SEQ)

    x = jnp.where(is_p, xp_ref[...], xs_ref[...])
    xe = jnp.concatenate([xprev_ref[...], x, xnext_ref[...]], axis=0)
    h = _rms_mod(xe, g_ref[...], mod_ref[0, pl.ds(r, 1), :], mod_ref[1, pl.ds(r, 1), :])
    ze_all = _dot(h.astype(BF16), wi_ref[...])
    row = lax.broadcasted_iota(jnp.int32, (TM + 2 * HALO, 1), 0)
    for hf in range(EV_HALVES):
        t0 = jnp.where(is_p, 0, ((i - npt) % (DEC_SEQ // EV_TM)) * EV_TM + hf * TM)
        first = t0 == 0
        last = t0 + TM == seq_len
        ze = ze_all[hf * TM:hf * TM + TM + 2 * HALO, :]
        outside = jnp.logical_or(jnp.logical_and(first, row < HALO), jnp.logical_and(last, row >= HALO + TM))
        ze = jnp.where(outside, 0.0, ze)
        pext[...] = ze[:, :POOL_W]
        uext[...] = ze[:, POOL_W + 2 * CONV_W:] * ze[:, POOL_W:POOL_W + CONV_W]
        p = ze[HALO:HALO + TM, :POOL_W]
        gb = ze[HALO:HALO + TM, POOL_W + CONV_W:POOL_W + 2 * CONV_W]

        t = t0 + lax.broadcasted_iota(jnp.int32, (TM, 1), 0)
        mixed = []
        for g, w in enumerate(POOL_WINDOWS):
            cols = slice(g * POOL_GW, (g + 1) * POOL_GW)
            acc = jnp.zeros((TM, POOL_GW), F32)
            for k in range(-(w // 2), w - (w // 2)):
                acc = acc + pext[pl.ds(HALO + k, TM), cols]
            lo = jnp.maximum(t - w // 2, 0)
            hi = jnp.minimum(t + (w - 1 - w // 2), seq_len - 1)
            cnt = (hi - lo + 1).astype(F32)
            d = acc / cnt - p[:, cols]
            mixed.append((_dot(d.astype(BF16), pw_ref[g]) * ps_ref[:, cols]).astype(BF16))
        conv = (uext[pl.ds(HALO - 1, TM), :] * cw_ref[0:1, :] + uext[pl.ds(HALO, TM), :] * cw_ref[1:2, :]
                + uext[pl.ds(HALO + 1, TM), :] * cw_ref[2:3, :])
        mixed.append((gb * conv).astype(BF16))
        y = _dot(jnp.concatenate(mixed, axis=1), wo_ref[...])
        rows = slice(hf * TM, (hf + 1) * TM)
        o_ref[rows, :] = x[rows, :] + mod_ref[2, pl.ds(r, 1), :] * y


def _even_layer(xp, xs, mod_l, g, w_in, pool_w, pool_scale, conv_w, w_out, wg, wu, wd):
    hb = EV_TM // HALO
    nhb = N_S // HALO
    npt = N_P // EV_TM
    nt = N_TOK // EV_TM
    wg2, wu2, wd2 = wg.reshape(-1, D_EXP), wu.reshape(-1, D_EXP), wd.reshape(-1, D)
    rows_in, rows_out = wg2.shape[0] // CAST_STEPS, wd2.shape[0] // CAST_STEPS
    assert rows_in * CAST_STEPS == wg2.shape[0] and rows_out * CAST_STEPS == wd2.shape[0] and CAST_STEPS <= nt
    cast_blk = lambda i: (jnp.minimum(i, CAST_STEPS - 1), 0)
    cast_specs = [pl.BlockSpec((rows_in, D_EXP), cast_blk), pl.BlockSpec((rows_in, D_EXP), cast_blk),
                  pl.BlockSpec((rows_out, D), cast_blk)]
    outs = pl.pallas_call(
        _even_kernel,
        grid=(nt,),
        in_specs=_dual_specs(EV_TM, D, npt) + [
            pl.BlockSpec((HALO, D), lambda i: (jnp.maximum((i - npt) * hb - 1, 0), 0)),
            pl.BlockSpec((HALO, D), lambda i: (jnp.clip((i - npt + 1) * hb, 0, nhb - 1), 0)),
            pl.BlockSpec((6, MOD_ROWS, D), lambda i: (0, 0, 0)),
            pl.BlockSpec((1, D), lambda i: (0, 0)),
            pl.BlockSpec((D, EVEN_IN), lambda i: (0, 0)),
            pl.BlockSpec((4, POOL_GW, POOL_GW), lambda i: (0, 0, 0)),
            pl.BlockSpec((1, POOL_W), lambda i: (0, 0)),
            pl.BlockSpec((8, CONV_W), lambda i: (0, 0)),
            pl.BlockSpec((D, D), lambda i: (0, 0)),
        ] + cast_specs,
        out_specs=[pl.BlockSpec((EV_TM, D), lambda i: (i, 0))] + cast_specs,
        out_shape=[jax.ShapeDtypeStruct((N_TOK, D), F32), jax.ShapeDtypeStruct(wg2.shape, BF16),
                   jax.ShapeDtypeStruct(wu2.shape, BF16), jax.ShapeDtypeStruct(wd2.shape, BF16)],
        scratch_shapes=[pltpu.VMEM((TM + 2 * HALO, POOL_W), F32),
                        pltpu.VMEM((TM + 2 * HALO, CONV_W), F32)],
        compiler_params=_cparams("arbitrary"),
        name="even_layer",
    )(xp, xs, xs, xs, mod_l, g, w_in, pool_w, pool_scale, conv_w, w_out, wg2, wu2, wd2)
    return outs[0], outs[1].reshape(wg.shape), outs[2].reshape(wu.shape), outs[3].reshape(wd.shape)


def _rope(x, cos, sin_signed):
    n = x.shape[1] // 128
    cosf = jnp.concatenate([cos] * n, axis=1) if n > 1 else cos
    sinf = jnp.concatenate([sin_signed] * n, axis=1) if n > 1 else sin_signed
    w = x.shape[1]
    lane = lax.broadcasted_iota(jnp.int32, x.shape, 1)
    up = pltpu.roll(x, w - DH // 4, 1)
    dn = pltpu.roll(x, DH // 4, 1)
    rot = jnp.where((lane % (DH // 2)) < DH // 4, up, dn)
    return x * cosf + rot * sinf


def _odd_in_kernel(x_ref, mod_ref, g_ref, w_ref, cos_ref, sin_ref, *out_refs, tm, tile0, rope, kv_f32):
    r = _mod_row((pl.program_id(0) + tile0) * (tm // TM))
    h = _rms_mod(x_ref[...], g_ref[...], mod_ref[0, pl.ds(r, 1), :], mod_ref[1, pl.ds(r, 1), :])
    z = _dot(h.astype(BF16), w_ref[...])
    qna_ref, kna_ref, vna_ref, qsw_ref, ksw_ref, vsw_ref = out_refs[:6]
    c0 = 0
    qna = z[:, 0:NA_W]
    kna = z[:, NA_W:2 * NA_W]
    vna = z[:, 2 * NA_W:3 * NA_W]
    c0 = 3 * NA_W
    qsw = z[:, c0:c0 + SWQ_W]
    ksw = z[:, c0 + SWQ_W:c0 + SWQ_W + SWKV_W]
    vsw = z[:, c0 + SWQ_W + SWKV_W:]
    if kv_f32:
        for ref, val in zip(out_refs[6:10], (kna, vna, ksw, vsw)):
            heads = val.shape[1] // DH
            for hh in range(heads):
                ref[pl.ds(hh, SEQ, stride=heads), :] = val[:, hh * DH:(hh + 1) * DH]
    if rope:
        cos = cos_ref[...]
        sin = sin_ref[...]
        qsw = _rope(qsw, cos, sin)
        ksw = _rope(ksw, cos, sin)
    qna_ref[...] = (qna * QK_SCALE).astype(BF16)
    kna_ref[...] = kna.astype(BF16)
    vna_ref[...] = vna.astype(BF16)
    qsw_ref[...] = (qsw * QK_SCALE).astype(BF16)
    ksw_ref[...] = ksw.astype(BF16)
    vsw_ref[...] = vsw.astype(BF16)


def _odd_in(x, mod_l, g, w, cos, sin, *, prompt):
    tm = TM if prompt else 2 * TM
    tile0 = 0 if prompt else N_P // tm
    nt = (N_P if prompt else N_S) // tm
    n = nt * tm
    widths = [NA_W, NA_W, NA_W, SWQ_W, SWKV_W, SWKV_W]
    out_shape = [jax.ShapeDtypeStruct((n, w_), BF16) for w_ in widths]
    out_specs = [pl.BlockSpec((tm, w_), lambda i: (i, 0)) for w_ in widths]
    if prompt:
        assert tm == SEQ
        for heads in (NA_H, NA_H, SW_KV, SW_KV):
            out_shape.append(jax.ShapeDtypeStruct((BATCH * SEQ * heads, DH), F32))
            out_specs.append(pl.BlockSpec((SEQ * heads, DH), lambda i: (i, 0)))
    return pl.pallas_call(
        functools.partial(_odd_in_kernel, tm=tm, tile0=tile0, rope=not prompt, kv_f32=prompt),
        grid=(nt,),
        in_specs=[
            pl.BlockSpec((tm, D), lambda i: (i + tile0, 0)),
            pl.BlockSpec((6, MOD_ROWS, D), lambda i: (0, 0, 0)),
            pl.BlockSpec((1, D), lambda i: (0, 0)),
            pl.BlockSpec((D, ODD_IN), lambda i: (0, 0)),
            pl.BlockSpec((tm, 128), lambda i: (i % (DEC_SEQ // tm), 0)),
            pl.BlockSpec((tm, 128), lambda i: (i % (DEC_SEQ // tm), 0)),
        ],
        out_specs=out_specs,
        out_shape=out_shape,
        compiler_params=_cparams("parallel"),
        name="odd_in_prompt" if prompt else "odd_in_latent",
    )(x, mod_l, g, w, cos, sin)


def _rope_tables():
    t = np.arange(DEC_SEQ)
    quarter = DH // 4
    inv = 1.0 / (ROPE_BASE ** (np.arange(quarter, dtype=np.float64) / quarter))

    def cos_sin(pos):
        ang = pos.astype(np.float64)[:, None] * inv[None, :]
        ang = np.concatenate([ang, ang], axis=-1)
        return np.cos(ang), np.sin(ang)

    cr, sr = cos_sin(t // GRID_W)
    cc, sc = cos_sin(t % GRID_W)
    cos = np.concatenate([cr, cc], axis=-1)
    sin = np.concatenate([sr, sc], axis=-1)
    sign = np.where((np.arange(DH) % (DH // 2)) < DH // 4, -1.0, 1.0)
    sin = sin * sign[None, :]
    cos = np.concatenate([cos, cos], axis=-1).astype(np.float32)
    sin = np.concatenate([sin, sin], axis=-1).astype(np.float32)
    return jnp.asarray(cos), jnp.asarray(sin)


def _softmax_pv(segs, sink=None):
    m = None
    for s, _ in segs:
        sm = jnp.max(s, axis=-1, keepdims=True)
        m = sm if m is None else jnp.maximum(m, sm)
    if sink is not None:
        m = jnp.maximum(m, sink)
    den = None
    acc = None
    for s, v in segs:
        p = jnp.exp(s - m)
        ps = jnp.sum(p, axis=-1, keepdims=True)
        den = ps if den is None else den + ps
        pv = _dot(p.astype(BF16), v)
        acc = pv if acc is None else acc + pv
    if sink is not None:
        den = den + jnp.exp(sink - m)
    return acc / den


def _sink_col(sink_ref, g, rows_per_head):
    row = lax.broadcasted_iota(jnp.int32, (SW_G * rows_per_head, 1), 0)
    col = jnp.zeros((SW_G * rows_per_head, 1), F32)
    for r in range(SW_G):
        col = jnp.where(row // rows_per_head == r, sink_ref[g * SW_G + r], col)
    return col


def _ctx_attn_kernel(sink_ref, qna_ref, kna_ref, vna_ref, qsw_ref, ksw_ref, vsw_ref, ona_ref, osw_ref):
    lane = lax.broadcasted_iota(jnp.int32, (SEQ, 2 * DH), 1)
    for p in range(NA_H // 2):
        cols = slice(p * 2 * DH, (p + 1) * 2 * DH)
        q = qna_ref[:, cols]
        k = kna_ref[:, cols]
        v = vna_ref[:, cols]
        outs = []
        for half in range(2):
            mine = (lane < DH) if half == 0 else (lane >= DH)
            qm = jnp.where(mine, q, jnp.zeros_like(q))
            outs.append(_softmax_pv([(_dot_nt(qm, k), v)]))
        ona_ref[:, cols] = jnp.where(lane < DH, outs[0], outs[1]).astype(BF16)
    outs = []
    for g in range(SW_KV):
        kc = slice(g * DH, (g + 1) * DH)
        q = jnp.concatenate([qsw_ref[:, (g * SW_G + r) * DH:(g * SW_G + r + 1) * DH] for r in range(SW_G)], axis=0)
        s = _dot_nt(q, ksw_ref[:, kc])
        o = _softmax_pv([(s, vsw_ref[:, kc])], sink=_sink_col(sink_ref, g, SEQ))
        outs.extend(o[r * SEQ:(r + 1) * SEQ, :] for r in range(SW_G))
    osw_ref[...] = jnp.concatenate(outs, axis=1).astype(BF16)


def _ctx_attn(sink, qna, kna, vna, qsw, ksw, vsw):
    def spec(w):
        return pl.BlockSpec((SEQ, w), lambda b: (b, 0))

    return pl.pallas_call(
        _ctx_attn_kernel,
        grid=(BATCH,),
        in_specs=[pl.BlockSpec(memory_space=pltpu.SMEM), spec(NA_W), spec(NA_W), spec(NA_W),
                  spec(SWQ_W), spec(SWKV_W), spec(SWKV_W)],
        out_specs=[spec(NA_W), spec(SWQ_W)],
        out_shape=[jax.ShapeDtypeStruct((N_P, NA_W), BF16), jax.ShapeDtypeStruct((N_P, SWQ_W), BF16)],
        compiler_params=_cparams("parallel"),
        name="ctx_attn",
    )(sink, qna, kna, vna, qsw, ksw, vsw)


NA_QB = 4
NA_KR = 12
NA_NQB = ROWS // NA_QB


def _na_block_offset(case, i, j):
    if case == 0:
        valid, dr = j < NA_ROWS, j - i + NA_ROWS - 1
    elif case == 1:
        valid, dr = i <= j < i + NA_ROWS, j - i + NA_ROWS // 2 - 1
    else:
        valid, dr = j >= NA_KR - NA_ROWS, j - i + NA_ROWS - 1 - (NA_KR - NA_QB)
    return dr if valid else None


def _na_bias_kernel(rpb_ref, o_ref):
    h = pl.program_id(0)
    cq = lax.broadcasted_iota(jnp.int32, (GRID_W, GRID_W), 0)
    ck = lax.broadcasted_iota(jnp.int32, (GRID_W, GRID_W), 1)
    cstart = jnp.clip(cq - NA_COLS // 2, 0, GRID_W - NA_COLS)
    ok = (ck >= cstart) & (ck < cstart + NA_COLS)
    dc = jnp.clip(ck - cq + NA_COLS - 1, 0, 2 * NA_COLS - 2)
    ndc = 2 * NA_COLS - 1
    neg = jnp.full((GRID_W, GRID_W), NEG, F32)
    by_offset = []
    for dr in range(2 * NA_ROWS - 1):
        b = jnp.zeros((GRID_W, GRID_W), F32)
        for e in range(ndc):
            b = jnp.where(dc == e, rpb_ref[h, dr * ndc + e], b)
        by_offset.append(jnp.where(ok, b, NEG))
    for case in range(3):
        for i in range(NA_QB):
            blocks = []
            for j in range(NA_KR):
                dr = _na_block_offset(case, i, j)
                blocks.append(neg if dr is None else by_offset[dr])
            o_ref[case, i * GRID_W:(i + 1) * GRID_W, :] = jnp.concatenate(blocks, axis=1)


def _na_bias(rpb):
    nd = 2 * NA_ROWS - 1
    return pl.pallas_call(
        _na_bias_kernel,
        grid=(NA_H,),
        in_specs=[pl.BlockSpec(memory_space=pltpu.SMEM)],
        out_specs=pl.BlockSpec((3, None, NA_QB * GRID_W, NA_KR * GRID_W), lambda h: (0, h, 0, 0)),
        out_shape=jax.ShapeDtypeStruct((3, NA_H, NA_QB * GRID_W, NA_KR * GRID_W), F32),
        compiler_params=_cparams("parallel"),
        name="na_bias",
    )(rpb.reshape(NA_H, nd * (2 * NA_COLS - 1)))


def _na_kernel(q_ref, k_ref, v_ref, kc_ref, vc_ref, bias_ref, o_ref):
    r0 = pl.program_id(1) * NA_QB
    start = pl.multiple_of(jnp.clip(r0 - NA_ROWS // 2, 0, ROWS - NA_KR) * GRID_W, GRID_W)
    nq = NA_QB * GRID_W
    lane = lax.broadcasted_iota(jnp.int32, (nq, 2 * DH), 1)
    for p in range(NA_H // 2):
        cols = slice(p * 2 * DH, (p + 1) * 2 * DH)
        q = q_ref[:, cols]
        kl = k_ref[pl.ds(start, NA_KR * GRID_W), cols]
        vl = v_ref[pl.ds(start, NA_KR * GRID_W), cols]
        kc = kc_ref[:, cols]
        vc = vc_ref[:, cols]
        zero = jnp.zeros_like(q)
        qm = jnp.concatenate([jnp.where(lane < DH, q, zero), jnp.where(lane < DH, zero, q)], axis=0)
        s_loc = _dot_nt(qm, kl)
        s_ctx = _dot_nt(qm, kc)
        outs = []
        for half in range(2):
            rows = slice(half * nq, (half + 1) * nq)
            outs.append(_softmax_pv([(s_loc[rows] + bias_ref[2 * p + half], vl), (s_ctx[rows], vc)]))
        o_ref[:, cols] = jnp.where(lane < DH, outs[0], outs[1]).astype(BF16)


def _na_attn(q, k, v, kc, vc, bias):
    nq = NA_QB * GRID_W

    def bias_case(b, rb):
        return (jnp.where(rb == 0, 0, jnp.where(rb == NA_NQB - 1, 2, 1)), 0, 0, 0)

    return pl.pallas_call(
        _na_kernel,
        grid=(DEC_BATCH, NA_NQB),
        in_specs=[
            pl.BlockSpec((nq, NA_W), lambda b, rb: (b * NA_NQB + rb, 0)),
            pl.BlockSpec((DEC_SEQ, NA_W), lambda b, rb: (b, 0)),
            pl.BlockSpec((DEC_SEQ, NA_W), lambda b, rb: (b, 0)),
            pl.BlockSpec((None, PAST, NA_W), lambda b, rb: (b, 0, 0)),
            pl.BlockSpec((None, PAST, NA_W), lambda b, rb: (b, 0, 0)),
            pl.BlockSpec((None, NA_H, nq, NA_KR * GRID_W), bias_case),
        ],
        out_specs=pl.BlockSpec((nq, NA_W), lambda b, rb: (b * NA_NQB + rb, 0)),
        out_shape=jax.ShapeDtypeStruct((N_S, NA_W), BF16),
        compiler_params=_cparams("parallel", "arbitrary"),
        name="na_attn",
    )(q, k, v, kc, vc, bias)


def _sw_window_bias():
    q = np.arange(SW_G * ABLK)[:, None] % ABLK
    k = np.arange(3 * ABLK)[None, :]
    tables = [np.where(np.abs(q + lead - k) <= SW_WIN, 0.0, NEG) for lead in (0, ABLK, 2 * ABLK)]
    return jnp.asarray(np.stack(tables).astype(np.float32))


SW_QB = 4


def _sw_kernel(sink_ref, q_ref, k_ref, v_ref, kc_ref, vc_ref, wb_ref, o_ref):
    nk = 3 * ABLK
    nb = DEC_SEQ // ABLK
    for sub in range(SW_QB):
        j = pl.program_id(1) * SW_QB + sub
        rows = slice(sub * ABLK, (sub + 1) * ABLK)
        start = pl.multiple_of(jnp.clip((j - 1) * ABLK, 0, DEC_SEQ - nk), ABLK)
        wb = wb_ref[jnp.where(j == 0, 0, jnp.where(j == nb - 1, 2, 1))]
        outs = []
        for g in range(SW_KV):
            kcols = slice(g * DH, (g + 1) * DH)
            q = jnp.concatenate([q_ref[rows, (g * SW_G + r) * DH:(g * SW_G + r + 1) * DH] for r in range(SW_G)],
                                axis=0)
            kw = k_ref[pl.ds(start, nk), kcols]
            vw = v_ref[pl.ds(start, nk), kcols]
            s_w = _dot_nt(q, kw) + wb
            s_c = _dot_nt(q, kc_ref[:, kcols])
            o = _softmax_pv([(s_w, vw), (s_c, vc_ref[:, kcols])], sink=_sink_col(sink_ref, g, ABLK))
            outs.extend(o[r * ABLK:(r + 1) * ABLK, :] for r in range(SW_G))
        o_ref[rows, :] = jnp.concatenate(outs, axis=1).astype(BF16)


def _sw_attn(sink, q, k, v, kc, vc):
    nsteps = DEC_SEQ // (ABLK * SW_QB)
    return pl.pallas_call(
        _sw_kernel,
        grid=(DEC_BATCH, nsteps),
        in_specs=[
            pl.BlockSpec(memory_space=pltpu.SMEM),
            pl.BlockSpec((SW_QB * ABLK, SWQ_W), lambda b, j: (b * nsteps + j, 0)),
            pl.BlockSpec((DEC_SEQ, SWKV_W), lambda b, j: (b, 0)),
            pl.BlockSpec((DEC_SEQ, SWKV_W), lambda b, j: (b, 0)),
            pl.BlockSpec((None, PAST, SWKV_W), lambda b, j: (b, 0, 0)),
            pl.BlockSpec((None, PAST, SWKV_W), lambda b, j: (b, 0, 0)),
            pl.BlockSpec((3, SW_G * ABLK, 3 * ABLK), lambda b, j: (0, 0, 0)),
        ],
        out_specs=pl.BlockSpec((SW_QB * ABLK, SWQ_W), lambda b, j: (b * nsteps + j, 0)),
        out_shape=jax.ShapeDtypeStruct((N_S, SWQ_W), BF16),
        compiler_params=_cparams("parallel", "arbitrary"),
        name="sw_attn",
    )(sink, q, k, v, kc, vc, _sw_window_bias())


BT = 4096
NB = N_TOK // BT
TPB = BT // TM
CH = 256
CH_TAIL = 128
XPS = 2
NES = N_EXP // XPS
DT = 4
NDS = TPB // DT
XO_ROWS = 2 * BT + CH
ROW = 8
assert D == ROW * 128


def _route_dense(sel):
    e = lax.broadcasted_iota(jnp.int32, sel.shape, 0)
    el = e % EPG
    g = e // EPG

    def group_rot(a, k):
        return jnp.where(el >= k, pltpu.roll(a, k, 0), pltpu.roll(a, (k - EPG) % N_EXP, 0))

    rank = jnp.zeros_like(sel)
    for k in range(1, EPG):
        partner = group_rot(sel, k)
        ahead = jnp.logical_or(partner > sel, jnp.logical_and(el >= k, partner == sel))
        rank = rank + ahead.astype(F32)
    contrib = jnp.where(rank < 2.0, sel, 0.0)
    score = contrib
    for k in range(1, EPG):
        score = score + group_rot(contrib, k)
    best = None
    for k in range(1, N_GRP):
        other = pltpu.roll(score, k * EPG, 0)
        c = jnp.logical_or(score > other, jnp.logical_and(g < k, score == other))
        best = c if best is None else jnp.logical_and(best, c)
    return jnp.logical_and(best, rank == 0.0), jnp.logical_and(best, rank == 1.0)


def _pick(mask, vals):
    return jnp.sum(jnp.where(mask, vals, 0.0), axis=0, keepdims=True)


def _route_tile(xv, mod_ref, g_ref, rw_ref, rb_ref, h_ref, pos_ref, w_ref, cnt_ref, off_ref, meta, carry):
    i = pl.program_id(0)
    j = i % TPB
    r = _mod_row(i)
    h = _rms_mod(xv, g_ref[...], mod_ref[3, pl.ds(r, 1), :], mod_ref[4, pl.ds(r, 1), :])
    h_ref[...] = h.astype(BF16)
    h_hi = h.astype(BF16)
    h_lo = (h - h_hi.astype(F32)).astype(BF16)
    logits = _dot(h_lo, rw_ref[0]) + _dot(h_hi, rw_ref[1]) + _dot(h_hi, rw_ref[0])
    lt = logits.T[:N_EXP, :]
    m = jnp.max(lt, axis=0, keepdims=True)
    ex = jnp.exp(lt - m)
    pr = ex / jnp.sum(ex, axis=0, keepdims=True)
    se = pr + rb_ref[:N_EXP, :]
    top1, top2 = _route_dense(se)

    @pl.when(j == 0)
    def _():
        carry[...] = jnp.zeros_like(carry)

    member = jnp.logical_or(top1, top2).astype(F32)
    s_idx = lax.broadcasted_iota(jnp.int32, (TM, TM), 0)
    t_idx = lax.broadcasted_iota(jnp.int32, (TM, TM), 1)
    before = jnp.where(s_idx < t_idx, 1.0, 0.0).astype(BF16)
    seen = _dot(member.astype(BF16), before) + carry[:, 0:1]
    ids = lax.broadcasted_iota(jnp.int32, (N_EXP, TM), 0).astype(F32)
    p1 = _pick(top1, pr)
    p2 = _pick(top2, pr)
    den = p1 + p2
    w_ref[j, 0:1, :] = p1 / den
    w_ref[j, 1:2, :] = p2 / den
    meta[j, 0:1, :] = _pick(top1, ids)
    meta[j, 1:2, :] = _pick(top2, ids)
    meta[j, 2:3, :] = _pick(top1, seen)
    meta[j, 3:4, :] = _pick(top2, seen)
    carry[...] = carry[...] + jnp.sum(member, axis=1, keepdims=True)

    @pl.when(j == TPB - 1)
    def _():
        cnt = carry[...]
        offs = [jnp.zeros((1, 128), F32)]
        for e in range(1, N_EXP):
            offs.append(offs[-1] + cnt[e - 1:e, :])
        cnt_ref[...] = cnt
        off_ref[...] = jnp.concatenate(offs, axis=0)
        for jj in range(TPB):
            for k in range(2):
                eid = meta[jj, k:k + 1, :]
                pos = meta[jj, 2 + k:3 + k, :]
                for e in range(1, N_EXP):
                    pos = pos + jnp.where(eid == float(e), offs[e][:, 0:1], 0.0)
                pos_ref[jj, k:k + 1, :] = (pos * float(ROW)).astype(jnp.int32)


def _router_kernel(x_ref, *refs):
    _route_tile(x_ref[...], *refs)


def _odd_out_router_kernel(nap_ref, nas_ref, swp_ref, sws_ref, x_ref, wo_ref, mod_ref, g_ref, rw_ref, rb_ref,
                           x1_ref, *refs):
    i = pl.program_id(0)
    is_p = i < NPT
    ona = jnp.where(is_p, nap_ref[...], nas_ref[...])
    osw = jnp.where(is_p, swp_ref[...], sws_ref[...])
    y = _dot(ona, wo_ref[:NA_W, :]) + _dot(osw, wo_ref[NA_W:, :])
    x1 = x_ref[...] + mod_ref[2, pl.ds(_mod_row(i), 1), :] * y
    x1_ref[...] = x1
    _route_tile(x1, mod_ref, g_ref, rw_ref, rb_ref, *refs)


def _router(x, mod_l, g, rw2, rb_col, attn=None):
    blk = lambda i: (i // TPB, 0, 0, 0)
    in_specs = [
        pl.BlockSpec((TM, D), lambda i: (i, 0)),
        pl.BlockSpec((6, MOD_ROWS, D), lambda i: (0, 0, 0)),
        pl.BlockSpec((1, D), lambda i: (0, 0)),
        pl.BlockSpec((2, D, 128), lambda i: (0, 0, 0)),
        pl.BlockSpec((128, 1), lambda i: (0, 0)),
    ]
    out_specs = [
        pl.BlockSpec((TM, D), lambda i: (i, 0)),
        pl.BlockSpec((None, TPB, 2, TM), blk),
        pl.BlockSpec((None, TPB, 2, TM), blk),
        pl.BlockSpec((None, N_EXP, 128), lambda i: (i // TPB, 0, 0)),
        pl.BlockSpec((None, N_EXP, 128), lambda i: (i // TPB, 0, 0)),
    ]
    out_shape = [
        jax.ShapeDtypeStruct((N_TOK, D), BF16),
        jax.ShapeDtypeStruct((NB, TPB, 2, TM), jnp.int32),
        jax.ShapeDtypeStruct((NB, TPB, 2, TM), F32),
        jax.ShapeDtypeStruct((NB, N_EXP, 128), F32),
        jax.ShapeDtypeStruct((NB, N_EXP, 128), F32),
    ]
    args = (x, mod_l, g, rw2, rb_col)
    body = _router_kernel
    if attn is not None:
        nap, nas, swp, sws, w_out = attn
        in_specs = (_dual_specs(TM, NA_W, NPT) + _dual_specs(TM, SWQ_W, NPT) + in_specs[:1]
                    + [pl.BlockSpec((D, D), lambda i: (0, 0))] + in_specs[1:])
        out_specs = [pl.BlockSpec((TM, D), lambda i: (i, 0))] + out_specs
        out_shape = [jax.ShapeDtypeStruct((N_TOK, D), F32)] + out_shape
        args = (nap, nas, swp, sws, x, w_out, mod_l, g, rw2, rb_col)
        body = _odd_out_router_kernel
    return pl.pallas_call(
        body,
        grid=(NT,),
        in_specs=in_specs,
        out_specs=out_specs,
        out_shape=out_shape,
        scratch_shapes=[pltpu.VMEM((TPB, 4, TM), F32), pltpu.VMEM((N_EXP, 128), F32)],
        compiler_params=_cparams("arbitrary"),
        name="router" if attn is None else "odd_out_router",
    )(*args)


def _row(p):
    return pl.ds(pl.multiple_of(p, ROW), ROW)


def _expert_segment(n, base, xo, wg_ref, wu_ref, wd_ref):
    def ffn_rows(row0, rows):
        x = jnp.concatenate([xo[pl.ds(row0 * ROW + c, rows, stride=ROW), :] for c in range(ROW)], axis=1)
        xb = x.astype(BF16)
        a = _dot(xb, wg_ref[...])
        u = _dot(xb, wu_ref[...])
        hid = (a / (1.0 + jnp.exp(-a))) * u
        out = _dot(hid.astype(BF16), wd_ref[...])
        valid = lax.broadcasted_iota(jnp.int32, (rows, 1), 0) < base + n - row0
        res = jnp.where(valid, out, x)
        for c in range(ROW):
            xo[pl.ds(row0 * ROW + c, rows, stride=ROW), :] = res[:, c * 128:(c + 1) * 128]

    def chunk(jc, carry):
        ffn_rows(base + jc * CH, CH)
        return carry

    nfull = (n + CH - CH_TAIL - 1) // CH
    lax.fori_loop(0, nfull, chunk, 0)

    @pl.when(n > nfull * CH)
    def _():
        ffn_rows(base + nfull * CH, CH_TAIL)


def _experts_kernel(cnt_ref, off_ref, dpos_ref, pos_ref, w_ref, h_ref, wg_ref, wu_ref, wd_ref, x_ref, mod_ref,
                    fg_ref, *rest, final):
    if final:
        op_ref, os_ref, xo, stg = rest
    else:
        o_ref, xo, stg = rest
    b = pl.program_id(0)
    s = pl.program_id(1)

    @pl.when(s == 0)
    def _():
        xo[pl.ds(2 * BT * ROW, CH * ROW), :] = jnp.zeros((CH * ROW, 128), F32)

    @pl.when(s < NDS)
    def _():
        for dt in range(DT):
            hs = h_ref[dt * TM:(dt + 1) * TM, :].astype(F32)
            for c in range(ROW):
                stg[pl.ds(c, TM, stride=ROW), :] = hs[:, c * 128:(c + 1) * 128]
            for t in range(TM):
                v = stg[t * ROW:(t + 1) * ROW, :]
                xo[_row(dpos_ref[dt, 0, t]), :] = v
                xo[_row(dpos_ref[dt, 1, t]), :] = v

    @pl.when(jnp.logical_and(s >= NDS, s < NDS + NES))
    def _():
        for k in range(XPS):
            e = (s - NDS) * XPS + k
            _expert_segment(cnt_ref[b, e], off_ref[b, e], xo, wg_ref.at[k], wu_ref.at[k], wd_ref.at[k])

    @pl.when(s >= NDS + NES - 1)
    def _():
        j = s - (NDS + NES - 1)
        tile = b * TPB + j
        g2 = mod_ref[5, pl.ds(_mod_row(tile), 1), :]
        for t in range(TM):
            a = xo[_row(pos_ref[0, t]), :]
            u = xo[_row(pos_ref[1, t]), :]
            stg[t * ROW:(t + 1) * ROW, :] = w_ref[0, t] * a + w_ref[1, t] * u
        y = jnp.concatenate([stg[pl.ds(c, TM, stride=ROW), :] for c in range(ROW)], axis=1)
        res = x_ref[...] + g2 * y
        if final:
            ms = jnp.mean(res * res, axis=-1, keepdims=True)
            res = res * lax.rsqrt(ms + EPS) * fg_ref[...]

            @pl.when(tile < NPT)
            def _():
                op_ref[...] = res

            @pl.when(tile >= NPT)
            def _():
                os_ref[...] = res
        else:
            o_ref[...] = res


def _experts(cnt, off, pos, wts, h, wg, wu, wd, x, mod_l, fg, *, layer, final):
    def tile_of(b, s):
        return b * TPB + jnp.clip(s - (NDS + NES - 1), 0, TPB - 1)

    def expert_of(b, s, *_):
        return (layer, jnp.clip(s - NDS, 0, NES - 1), 0, 0)

    def dispatch_of(b, s, *_):
        return (b, jnp.minimum(s, NDS - 1), 0, 0)

    def combine_of(b, s, *_):
        return (b, jnp.clip(s - (NDS + NES - 1), 0, TPB - 1), 0, 0)

    smem_disp = pl.BlockSpec((None, DT, 2, TM), dispatch_of, memory_space=pltpu.SMEM)
    smem_blk = pl.BlockSpec((None, None, 2, TM), combine_of, memory_space=pltpu.SMEM)
    if final:
        out_specs = [pl.BlockSpec((TM, D), lambda b, s, *_: (jnp.minimum(tile_of(b, s), NPT - 1), 0)),
                     pl.BlockSpec((TM, D), lambda b, s, *_: (jnp.maximum(tile_of(b, s) - NPT, 0), 0))]
        out_shape = [jax.ShapeDtypeStruct((N_P, D), F32), jax.ShapeDtypeStruct((N_S, D), F32)]
    else:
        out_specs = pl.BlockSpec((TM, D), lambda b, s, *_: (tile_of(b, s), 0))
        out_shape = jax.ShapeDtypeStruct((N_TOK, D), F32)
    grid_spec = pltpu.PrefetchScalarGridSpec(
        num_scalar_prefetch=2,
        grid=(NB, NDS + NES + TPB - 1),
        in_specs=[
            smem_disp,
            smem_blk,
            smem_blk,
            pl.BlockSpec((DT * TM, D), lambda b, s, *_: (b * NDS + jnp.minimum(s, NDS - 1), 0)),
            pl.BlockSpec((None, XPS, D, D_EXP), expert_of),
            pl.BlockSpec((None, XPS, D, D_EXP), expert_of),
            pl.BlockSpec((None, XPS, D_EXP, D), expert_of),
            pl.BlockSpec((TM, D), lambda b, s, *_: (tile_of(b, s), 0)),
            pl.BlockSpec((6, MOD_ROWS, D), lambda b, s, *_: (0, 0, 0)),
            pl.BlockSpec((1, D), lambda b, s, *_: (0, 0)),
        ],
        out_specs=out_specs,
        scratch_shapes=[pltpu.VMEM((XO_ROWS * ROW, 128), F32), pltpu.VMEM((TM * ROW, 128), F32)],
    )
    return pl.pallas_call(
        functools.partial(_experts_kernel, final=final),
        grid_spec=grid_spec,
        out_shape=out_shape,
        compiler_params=_cparams("arbitrary", "arbitrary"),
        name="experts_final" if final else "experts",
    )(cnt, off, pos, pos, wts, h, wg, wu, wd, x, mod_l, fg)


def _moe_sparse(x, routed, mod_l, wg, wu, wd, fg, *, layer, final):
    h, pos, wts, cnt, off = routed
    cnt = cnt[:, :, 0].astype(jnp.int32)
    off = off[:, :, 0].astype(jnp.int32)
    return _experts(cnt, off, pos, wts, h, wg, wu, wd, x, mod_l, fg, layer=layer, final=final)


def kernel(x_prompt, x_sample, cache_na_k, cache_na_v, cache_sw_k, cache_sw_v, c, c_ctx, mod_w, mod_b, norm_mix_g, norm_ffn_g, ev_w_in, ev_pool_w, ev_pool_scale, ev_conv_w, ev_w_out, od_w_in, od_rpb, od_sink, od_w_out, router_w, router_b, moe_w_gate, moe_w_up, moe_w_down, final_norm_g):
    xp = x_prompt.reshape(N_P, D)
    xs = x_sample.reshape(N_S, D)
    cvec = jnp.concatenate([c_ctx[None, :], c, jnp.zeros((MOD_ROWS - 1 - DEC_BATCH, D), F32)], axis=0)
    mod = _modulation(cvec, mod_w, mod_b)

    rw_pad = jnp.pad(router_w, ((0, 0), (0, 128 - N_EXP)))
    rw_hi = rw_pad.astype(BF16)
    rw2 = jnp.stack([rw_hi, (rw_pad - rw_hi.astype(F32)).astype(BF16)])
    rb_col = jnp.pad(router_b, (0, 128 - N_EXP)).reshape(128, 1)

    conv_w = jnp.pad(ev_conv_w[0], ((0, 8 - ev_conv_w.shape[1]), (0, 0)))
    x, wg, wu, wd = _even_layer(xp, xs, mod[0], norm_mix_g[0:1], ev_w_in[0].astype(BF16),
                                ev_pool_w[0].astype(BF16), ev_pool_scale[0:1], conv_w, ev_w_out[0].astype(BF16),
                                moe_w_gate, moe_w_up, moe_w_down)
    fg = final_norm_g.reshape(1, D)
    routed = _router(x, mod[0], norm_ffn_g[0:1], rw2, rb_col)
    x = _moe_sparse(x, routed, mod[0], wg, wu, wd, fg, layer=0, final=False)

    cos, sin = _rope_tables()
    w_in = od_w_in[0].astype(BF16)
    g1 = norm_mix_g[1:2]
    qna_p, kna_p, vna_p, qsw_p, ksw_p, vsw_p, nak, nav, swk, swv = _odd_in(x, mod[1], g1, w_in, cos, sin, prompt=True)
    qna_s, kna_s, vna_s, qsw_s, ksw_s, vsw_s = _odd_in(x, mod[1], g1, w_in, cos, sin, prompt=False)
    sink = od_sink[0]
    ona_p, osw_p = _ctx_attn(sink, qna_p, kna_p, vna_p, qsw_p, ksw_p, vsw_p)
    bias = _na_bias(od_rpb[0])
    ona_s = _na_attn(qna_s, kna_s, vna_s,
                     cache_na_k[:, 0].reshape(DEC_BATCH, PAST, NA_W).astype(BF16),
                     cache_na_v[:, 0].reshape(DEC_BATCH, PAST, NA_W).astype(BF16), bias)
    osw_s = _sw_attn(sink, qsw_s, ksw_s, vsw_s,
                     cache_sw_k[:, 0].reshape(DEC_BATCH, PAST, SWKV_W).astype(BF16),
                     cache_sw_v[:, 0].reshape(DEC_BATCH, PAST, SWKV_W).astype(BF16))
    x, *routed = _router(x, mod[1], norm_ffn_g[1:2], rw2, rb_col,
                         attn=(ona_p, ona_s, osw_p, osw_s, od_w_out[0].astype(BF16)))
    y_prompt, y_sample = _moe_sparse(x, routed, mod[1], wg, wu, wd, fg, layer=1, final=True)
    y_prompt = y_prompt.reshape(BATCH, SEQ, D)
    y_sample = y_sample.reshape(DEC_BATCH, DEC_SEQ, D)
    new_na_k = nak.reshape(BATCH, 1, SEQ, NA_H, DH)
    new_na_v = nav.reshape(BATCH, 1, SEQ, NA_H, DH)
    new_sw_k = swk.reshape(BATCH, 1, SEQ, SW_KV, DH)
    new_sw_v = swv.reshape(BATCH, 1, SEQ, SW_KV, DH)
    return (y_prompt, y_sample, new_na_k, new_na_v, new_sw_k, new_sw_v)
```

```python
import functools

import jax
import jax.numpy as jnp
import numpy as np
from jax import lax
from jax.experimental import pallas as pl
from jax.experimental.pallas import tpu as pltpu

D = 1024
BATCH = 16
SEQ = 256
DEC_BATCH = 4
DEC_SEQ = 4096
PAST = 512
GRID_W = 64
ROWS = DEC_SEQ // GRID_W
DH = 64
POOL_W = 512
POOL_WINDOWS = (2, 4, 8, 16)
POOL_GW = 128
CONV_W = 512
EVEN_IN = POOL_W + 3 * CONV_W
NA_H = 8
NA_ROWS = 8
NA_COLS = 16
SW_H = 8
SW_KV = 2
SW_G = SW_H // SW_KV
SW_WIN = 128
ABLK = 128
NA_W = NA_H * DH
SWQ_W = SW_H * DH
SWKV_W = SW_KV * DH
ODD_IN = 3 * NA_W + SWQ_W + 2 * SWKV_W
N_EXP = 16
N_GRP = 4
EPG = 4
D_EXP = 512
EPS = 1e-6
NEG = -1e30
ROPE_BASE = 10000.0
QK_SCALE = DH ** -0.5
assert QK_SCALE == 0.125

N_P = BATCH * SEQ
N_S = DEC_BATCH * DEC_SEQ
N_TOK = N_P + N_S
MOD_ROWS = 8

TM = 256
NPT = N_P // TM
TPS = DEC_SEQ // TM
NT = N_TOK // TM
HALO = 8
EV_HALVES = 2
EV_TM = EV_HALVES * TM
CAST_STEPS = 32

F32 = jnp.float32
BF16 = jnp.bfloat16
VMEM_LIMIT = 62 * 1024 * 1024


def _cparams(*sem):
    return pltpu.CompilerParams(dimension_semantics=sem, vmem_limit_bytes=VMEM_LIMIT)


def _mod_row(i):
    return jnp.where(i < NPT, 0, 1 + (i - NPT) // TPS)


def _rms_mod(x, g, shift, scale):
    ms = jnp.mean(x * x, axis=-1, keepdims=True)
    y = x * lax.rsqrt(ms + EPS) * g
    return y * (1.0 + scale) + shift


def _dot(a, b):
    return jnp.dot(a, b, preferred_element_type=F32)


def _dot_nt(a, b):
    return lax.dot_general(a, b, (((1,), (1,)), ((), ())), preferred_element_type=F32)


def _mod_kernel(cv_ref, w_ref, b_ref, o_ref):
    cv = cv_ref[...]
    a = cv / (1.0 + jnp.exp(-cv))
    o_ref[...] = jnp.dot(a, w_ref[...], preferred_element_type=F32,
                         precision=lax.Precision.HIGHEST) + b_ref[...]


def _modulation(cvec, mod_w, mod_b):
    depth = mod_w.shape[0]
    return pl.pallas_call(
        _mod_kernel,
        grid=(depth, 6),
        in_specs=[
            pl.BlockSpec((MOD_ROWS, D), lambda l, j: (0, 0)),
            pl.BlockSpec((None, D, D), lambda l, j: (l, 0, j)),
            pl.BlockSpec((None, None, 1, D), lambda l, j: (l, j, 0, 0)),
        ],
        out_specs=pl.BlockSpec((None, None, MOD_ROWS, D), lambda l, j: (l, j, 0, 0)),
        out_shape=jax.ShapeDtypeStruct((depth, 6, MOD_ROWS, D), F32),
        compiler_params=_cparams("arbitrary", "arbitrary"),
        name="modulation",
    )(cvec, mod_w, mod_b.reshape(depth, 6, 1, D))


def _dual_specs(tm, width, npt):
    return [
        pl.BlockSpec((tm, width), lambda i: (jnp.minimum(i, npt - 1), 0)),
        pl.BlockSpec((tm, width), lambda i: (jnp.maximum(i - npt, 0), 0)),
    ]


def _even_kernel(xp_ref, xs_ref, xprev_ref, xnext_ref, mod_ref, g_ref, wi_ref, pw_ref, ps_ref, cw_ref,
                 wo_ref, wgf_ref, wuf_ref, wdf_ref, o_ref, wgb_ref, wub_ref, wdb_ref, pext, uext):
    i = pl.program_id(0)

    @pl.when(i < CAST_STEPS)
    def _():
        wgb_ref[...] = wgf_ref[...].astype(BF16)
        wub_ref[...] = wuf_ref[...].astype(BF16)
        wdb_ref[...] = wdf_ref[...].astype(BF16)

    npt = NPT // EV_HALVES
    r = _mod_row(i * EV_HALVES)
    is_p = i < npt
    seq_len = jnp.where(is_p, SEQ, DEC_SEQ)

    x = jnp.where(is_p, xp_ref[...], xs_ref[...])
    xe = jnp.concatenate([xprev_ref[...], x, xnext_ref[...]], axis=0)
    h = _rms_mod(xe, g_ref[...], mod_ref[0, pl.ds(r, 1), :], mod_ref[1, pl.ds(r, 1), :])
    ze_all = _dot(h.astype(BF16), wi_ref[...])
    row = lax.broadcasted_iota(jnp.int32, (TM + 2 * HALO, 1), 0)
    for hf in range(EV_HALVES):
        t0 = jnp.where(is_p, 0, ((i - npt) % (DEC_SEQ // EV_TM)) * EV_TM + hf * TM)
        first = t0 == 0
        last = t0 + TM == seq_len
        ze = ze_all[hf * TM:hf * TM + TM + 2 * HALO, :]
        outside = jnp.logical_or(jnp.logical_and(first, row < HALO), jnp.logical_and(last, row >= HALO + TM))
        ze = jnp.where(outside, 0.0, ze)
        pext[...] = ze[:, :POOL_W]
        uext[...] = ze[:, POOL_W + 2 * CONV_W:] * ze[:, POOL_W:POOL_W + CONV_W]
        p = ze[HALO:HALO + TM, :POOL_W]
        gb = ze[HALO:HALO + TM, POOL_W + CONV_W:POOL_W + 2 * CONV_W]

        t = t0 + lax.broadcasted_iota(jnp.int32, (TM, 1), 0)
        mixed = []
        for g, w in enumerate(POOL_WINDOWS):
            cols = slice(g * POOL_GW, (g + 1) * POOL_GW)
            acc = jnp.zeros((TM, POOL_GW), F32)
            for k in range(-(w // 2), w - (w // 2)):
                acc = acc + pext[pl.ds(HALO + k, TM), cols]
            lo = jnp.maximum(t - w // 2, 0)
            hi = jnp.minimum(t + (w - 1 - w // 2), seq_len - 1)
            cnt = (hi - lo + 1).astype(F32)
            d = acc / cnt - p[:, cols]
            mixed.append((_dot(d.astype(BF16), pw_ref[g]) * ps_ref[:, cols]).astype(BF16))
        conv = (uext[pl.ds(HALO - 1, TM), :] * cw_ref[0:1, :] + uext[pl.ds(HALO, TM), :] * cw_ref[1:2, :]
                + uext[pl.ds(HALO + 1, TM), :] * cw_ref[2:3, :])
        mixed.append((gb * conv).astype(BF16))
        y = _dot(jnp.concatenate(mixed, axis=1), wo_ref[...])
        rows = slice(hf * TM, (hf + 1) * TM)
        o_ref[rows, :] = x[rows, :] + mod_ref[2, pl.ds(r, 1), :] * y


def _even_layer(xp, xs, mod_l, g, w_in, pool_w, pool_scale, conv_w, w_out, wg, wu, wd):
    hb = EV_TM // HALO
    nhb = N_S // HALO
    npt = N_P // EV_TM
    nt = N_TOK // EV_TM
    wg2, wu2, wd2 = wg.reshape(-1, D_EXP), wu.reshape(-1, D_EXP), wd.reshape(-1, D)
    rows_in, rows_out = wg2.shape[0] // CAST_STEPS, wd2.shape[0] // CAST_STEPS
    assert rows_in * CAST_STEPS == wg2.shape[0] and rows_out * CAST_STEPS == wd2.shape[0] and CAST_STEPS <= nt
    cast_blk = lambda i: (jnp.minimum(i, CAST_STEPS - 1), 0)
    cast_specs = [pl.BlockSpec((rows_in, D_EXP), cast_blk), pl.BlockSpec((rows_in, D_EXP), cast_blk),
                  pl.BlockSpec((rows_out, D), cast_blk)]
    outs = pl.pallas_call(
        _even_kernel,
        grid=(nt,),
        in_specs=_dual_specs(EV_TM, D, npt) + [
            pl.BlockSpec((HALO, D), lambda i: (jnp.maximum((i - npt) * hb - 1, 0), 0)),
            pl.BlockSpec((HALO, D), lambda i: (jnp.clip((i - npt + 1) * hb, 0, nhb - 1), 0)),
            pl.BlockSpec((6, MOD_ROWS, D), lambda i: (0, 0, 0)),
            pl.BlockSpec((1, D), lambda i: (0, 0)),
            pl.BlockSpec((D, EVEN_IN), lambda i: (0, 0)),
            pl.BlockSpec((4, POOL_GW, POOL_GW), lambda i: (0, 0, 0)),
            pl.BlockSpec((1, POOL_W), lambda i: (0, 0)),
            pl.BlockSpec((8, CONV_W), lambda i: (0, 0)),
            pl.BlockSpec((D, D), lambda i: (0, 0)),
        ] + cast_specs,
        out_specs=[pl.BlockSpec((EV_TM, D), lambda i: (i, 0))] + cast_specs,
        out_shape=[jax.ShapeDtypeStruct((N_TOK, D), F32), jax.ShapeDtypeStruct(wg2.shape, BF16),
                   jax.ShapeDtypeStruct(wu2.shape, BF16), jax.ShapeDtypeStruct(wd2.shape, BF16)],
        scratch_shapes=[pltpu.VMEM((TM + 2 * HALO, POOL_W), F32),
                        pltpu.VMEM((TM + 2 * HALO, CONV_W), F32)],
        compiler_params=_cparams("arbitrary"),
        name="even_layer",
    )(xp, xs, xs, xs, mod_l, g, w_in, pool_w, pool_scale, conv_w, w_out, wg2, wu2, wd2)
    return outs[0], outs[1].reshape(wg.shape), outs[2].reshape(wu.shape), outs[3].reshape(wd.shape)


def _rope(x, cos, sin_signed):
    n = x.shape[1] // 128
    cosf = jnp.concatenate([cos] * n, axis=1) if n > 1 else cos
    sinf = jnp.concatenate([sin_signed] * n, axis=1) if n > 1 else sin_signed
    w = x.shape[1]
    lane = lax.broadcasted_iota(jnp.int32, x.shape, 1)
    up = pltpu.roll(x, w - DH // 4, 1)
    dn = pltpu.roll(x, DH // 4, 1)
    rot = jnp.where((lane % (DH // 2)) < DH // 4, up, dn)
    return x * cosf + rot * sinf


def _odd_in_kernel(x_ref, mod_ref, g_ref, w_ref, cos_ref, sin_ref, *out_refs, tm, tile0, rope, kv_f32):
    r = _mod_row((pl.program_id(0) + tile0) * (tm // TM))
    h = _rms_mod(x_ref[...], g_ref[...], mod_ref[0, pl.ds(r, 1), :], mod_ref[1, pl.ds(r, 1), :])
    z = _dot(h.astype(BF16), w_ref[...])
    qna_ref, kna_ref, vna_ref, qsw_ref, ksw_ref, vsw_ref = out_refs[:6]
    c0 = 0
    qna = z[:, 0:NA_W]
    kna = z[:, NA_W:2 * NA_W]
    vna = z[:, 2 * NA_W:3 * NA_W]
    c0 = 3 * NA_W
    qsw = z[:, c0:c0 + SWQ_W]
    ksw = z[:, c0 + SWQ_W:c0 + SWQ_W + SWKV_W]
    vsw = z[:, c0 + SWQ_W + SWKV_W:]
    if kv_f32:
        for ref, val in zip(out_refs[6:10], (kna, vna, ksw, vsw)):
            heads = val.shape[1] // DH
            for hh in range(heads):
                ref[pl.ds(hh, SEQ, stride=heads), :] = val[:, hh * DH:(hh + 1) * DH]
    if rope:
        cos = cos_ref[...]
        sin = sin_ref[...]
        qsw = _rope(qsw, cos, sin)
        ksw = _rope(ksw, cos, sin)
    qna_ref[...] = (qna * QK_SCALE).astype(BF16)
    kna_ref[...] = kna.astype(BF16)
    vna_ref[...] = vna.astype(BF16)
    qsw_ref[...] = (qsw * QK_SCALE).astype(BF16)
    ksw_ref[...] = ksw.astype(BF16)
    vsw_ref[...] = vsw.astype(BF16)


def _odd_in(x, mod_l, g, w, cos, sin, *, prompt):
    tm = TM if prompt else 2 * TM
    tile0 = 0 if prompt else N_P // tm
    nt = (N_P if prompt else N_S) // tm
    n = nt * tm
    widths = [NA_W, NA_W, NA_W, SWQ_W, SWKV_W, SWKV_W]
    out_shape = [jax.ShapeDtypeStruct((n, w_), BF16) for w_ in widths]
    out_specs = [pl.BlockSpec((tm, w_), lambda i: (i, 0)) for w_ in widths]
    if prompt:
        assert tm == SEQ
        for heads in (NA_H, NA_H, SW_KV, SW_KV):
            out_shape.append(jax.ShapeDtypeStruct((BATCH * SEQ * heads, DH), F32))
            out_specs.append(pl.BlockSpec((SEQ * heads, DH), lambda i: (i, 0)))
    return pl.pallas_call(
        functools.partial(_odd_in_kernel, tm=tm, tile0=tile0, rope=not prompt, kv_f32=prompt),
        grid=(nt,),
        in_specs=[
            pl.BlockSpec((tm, D), lambda i: (i + tile0, 0)),
            pl.BlockSpec((6, MOD_ROWS, D), lambda i: (0, 0, 0)),
            pl.BlockSpec((1, D), lambda i: (0, 0)),
            pl.BlockSpec((D, ODD_IN), lambda i: (0, 0)),
            pl.BlockSpec((tm, 128), lambda i: (i % (DEC_SEQ // tm), 0)),
            pl.BlockSpec((tm, 128), lambda i: (i % (DEC_SEQ // tm), 0)),
        ],
        out_specs=out_specs,
        out_shape=out_shape,
        compiler_params=_cparams("parallel"),
        name="odd_in_prompt" if prompt else "odd_in_latent",
    )(x, mod_l, g, w, cos, sin)


def _rope_tables():
    t = np.arange(DEC_SEQ)
    quarter = DH // 4
    inv = 1.0 / (ROPE_BASE ** (np.arange(quarter, dtype=np.float64) / quarter))

    def cos_sin(pos):
        ang = pos.astype(np.float64)[:, None] * inv[None, :]
        ang = np.concatenate([ang, ang], axis=-1)
        return np.cos(ang), np.sin(ang)

    cr, sr = cos_sin(t // GRID_W)
    cc, sc = cos_sin(t % GRID_W)
    cos = np.concatenate([cr, cc], axis=-1)
    sin = np.concatenate([sr, sc], axis=-1)
    sign = np.where((np.arange(DH) % (DH // 2)) < DH // 4, -1.0, 1.0)
    sin = sin * sign[None, :]
    cos = np.concatenate([cos, cos], axis=-1).astype(np.float32)
    sin = np.concatenate([sin, sin], axis=-1).astype(np.float32)
    return jnp.asarray(cos), jnp.asarray(sin)


def _softmax_pv(segs, sink=None):
    m = None
    for s, _ in segs:
        sm = jnp.max(s, axis=-1, keepdims=True)
        m = sm if m is None else jnp.maximum(m, sm)
    if sink is not None:
        m = jnp.maximum(m, sink)
    den = None
    acc = None
    for s, v in segs:
        p = jnp.exp(s - m)
        ps = jnp.sum(p, axis=-1, keepdims=True)
        den = ps if den is None else den + ps
        pv = _dot(p.astype(BF16), v)
        acc = pv if acc is None else acc + pv
    if sink is not None:
        den = den + jnp.exp(sink - m)
    return acc / den


def _sink_col(sink_ref, g, rows_per_head):
    row = lax.broadcasted_iota(jnp.int32, (SW_G * rows_per_head, 1), 0)
    col = jnp.zeros((SW_G * rows_per_head, 1), F32)
    for r in range(SW_G):
        col = jnp.where(row // rows_per_head == r, sink_ref[g * SW_G + r], col)
    return col


def _ctx_attn_kernel(sink_ref, qna_ref, kna_ref, vna_ref, qsw_ref, ksw_ref, vsw_ref, ona_ref, osw_ref):
    lane = lax.broadcasted_iota(jnp.int32, (SEQ, 2 * DH), 1)
    for p in range(NA_H // 2):
        cols = slice(p * 2 * DH, (p + 1) * 2 * DH)
        q = qna_ref[:, cols]
        k = kna_ref[:, cols]
        v = vna_ref[:, cols]
        outs = []
        for half in range(2):
            mine = (lane < DH) if half == 0 else (lane >= DH)
            qm = jnp.where(mine, q, jnp.zeros_like(q))
            outs.append(_softmax_pv([(_dot_nt(qm, k), v)]))
        ona_ref[:, cols] = jnp.where(lane < DH, outs[0], outs[1]).astype(BF16)
    outs = []
    for g in range(SW_KV):
        kc = slice(g * DH, (g + 1) * DH)
        q = jnp.concatenate([qsw_ref[:, (g * SW_G + r) * DH:(g * SW_G + r + 1) * DH] for r in range(SW_G)], axis=0)
        s = _dot_nt(q, ksw_ref[:, kc])
        o = _softmax_pv([(s, vsw_ref[:, kc])], sink=_sink_col(sink_ref, g, SEQ))
        outs.extend(o[r * SEQ:(r + 1) * SEQ, :] for r in range(SW_G))
    osw_ref[...] = jnp.concatenate(outs, axis=1).astype(BF16)


def _ctx_attn(sink, qna, kna, vna, qsw, ksw, vsw):
    def spec(w):
        return pl.BlockSpec((SEQ, w), lambda b: (b, 0))

    return pl.pallas_call(
        _ctx_attn_kernel,
        grid=(BATCH,),
        in_specs=[pl.BlockSpec(memory_space=pltpu.SMEM), spec(NA_W), spec(NA_W), spec(NA_W),
                  spec(SWQ_W), spec(SWKV_W), spec(SWKV_W)],
        out_specs=[spec(NA_W), spec(SWQ_W)],
        out_shape=[jax.ShapeDtypeStruct((N_P, NA_W), BF16), jax.ShapeDtypeStruct((N_P, SWQ_W), BF16)],
        compiler_params=_cparams("parallel"),
        name="ctx_attn",
    )(sink, qna, kna, vna, qsw, ksw, vsw)


NA_QB = 4
NA_KR = 12
NA_NQB = ROWS // NA_QB


def _na_block_offset(case, i, j):
    if case == 0:
        valid, dr = j < NA_ROWS, j - i + NA_ROWS - 1
    elif case == 1:
        valid, dr = i <= j < i + NA_ROWS, j - i + NA_ROWS // 2 - 1
    else:
        valid, dr = j >= NA_KR - NA_ROWS, j - i + NA_ROWS - 1 - (NA_KR - NA_QB)
    return dr if valid else None


def _na_bias_kernel(rpb_ref, o_ref):
    h = pl.program_id(0)
    cq = lax.broadcasted_iota(jnp.int32, (GRID_W, GRID_W), 0)
    ck = lax.broadcasted_iota(jnp.int32, (GRID_W, GRID_W), 1)
    cstart = jnp.clip(cq - NA_COLS // 2, 0, GRID_W - NA_COLS)
    ok = (ck >= cstart) & (ck < cstart + NA_COLS)
    dc = jnp.clip(ck - cq + NA_COLS - 1, 0, 2 * NA_COLS - 2)
    ndc = 2 * NA_COLS - 1
    neg = jnp.full((GRID_W, GRID_W), NEG, F32)
    by_offset = []
    for dr in range(2 * NA_ROWS - 1):
        b = jnp.zeros((GRID_W, GRID_W), F32)
        for e in range(ndc):
            b = jnp.where(dc == e, rpb_ref[h, dr * ndc + e], b)
        by_offset.append(jnp.where(ok, b, NEG))
    for case in range(3):
        for i in range(NA_QB):
            blocks = []
            for j in range(NA_KR):
                dr = _na_block_offset(case, i, j)
                blocks.append(neg if dr is None else by_offset[dr])
            o_ref[case, i * GRID_W:(i + 1) * GRID_W, :] = jnp.concatenate(blocks, axis=1)


def _na_bias(rpb):
    nd = 2 * NA_ROWS - 1
    return pl.pallas_call(
        _na_bias_kernel,
        grid=(NA_H,),
        in_specs=[pl.BlockSpec(memory_space=pltpu.SMEM)],
        out_specs=pl.BlockSpec((3, None, NA_QB * GRID_W, NA_KR * GRID_W), lambda h: (0, h, 0, 0)),
        out_shape=jax.ShapeDtypeStruct((3, NA_H, NA_QB * GRID_W, NA_KR * GRID_W), F32),
        compiler_params=_cparams("parallel"),
        name="na_bias",
    )(rpb.reshape(NA_H, nd * (2 * NA_COLS - 1)))


def _na_kernel(q_ref, k_ref, v_ref, kc_ref, vc_ref, bias_ref, o_ref):
    r0 = pl.program_id(1) * NA_QB
    start = pl.multiple_of(jnp.clip(r0 - NA_ROWS // 2, 0, ROWS - NA_KR) * GRID_W, GRID_W)
    nq = NA_QB * GRID_W
    lane = lax.broadcasted_iota(jnp.int32, (nq, 2 * DH), 1)
    for p in range(NA_H // 2):
        cols = slice(p * 2 * DH, (p + 1) * 2 * DH)
        q = q_ref[:, cols]
        kl = k_ref[pl.ds(start, NA_KR * GRID_W), cols]
        vl = v_ref[pl.ds(start, NA_KR * GRID_W), cols]
        kc = kc_ref[:, cols]
        vc = vc_ref[:, cols]
        zero = jnp.zeros_like(q)
        qm = jnp.concatenate([jnp.where(lane < DH, q, zero), jnp.where(lane < DH, zero, q)], axis=0)
        s_loc = _dot_nt(qm, kl)
        s_ctx = _dot_nt(qm, kc)
        outs = []
        for half in range(2):
            rows = slice(half * nq, (half + 1) * nq)
            outs.append(_softmax_pv([(s_loc[rows] + bias_ref[2 * p + half], vl), (s_ctx[rows], vc)]))
        o_ref[:, cols] = jnp.where(lane < DH, outs[0], outs[1]).astype(BF16)


def _na_attn(q, k, v, kc, vc, bias):
    nq = NA_QB * GRID_W

    def bias_case(b, rb):
        return (jnp.where(rb == 0, 0, jnp.where(rb == NA_NQB - 1, 2, 1)), 0, 0, 0)

    return pl.pallas_call(
        _na_kernel,
        grid=(DEC_BATCH, NA_NQB),
        in_specs=[
            pl.BlockSpec((nq, NA_W), lambda b, rb: (b * NA_NQB + rb, 0)),
            pl.BlockSpec((DEC_SEQ, NA_W), lambda b, rb: (b, 0)),
            pl.BlockSpec((DEC_SEQ, NA_W), lambda b, rb: (b, 0)),
            pl.BlockSpec((None, PAST, NA_W), lambda b, rb: (b, 0, 0)),
            pl.BlockSpec((None, PAST, NA_W), lambda b, rb: (b, 0, 0)),
            pl.BlockSpec((None, NA_H, nq, NA_KR * GRID_W), bias_case),
        ],
        out_specs=pl.BlockSpec((nq, NA_W), lambda b, rb: (b * NA_NQB + rb, 0)),
        out_shape=jax.ShapeDtypeStruct((N_S, NA_W), BF16),
        compiler_params=_cparams("parallel", "arbitrary"),
        name="na_attn",
    )(q, k, v, kc, vc, bias)


def _sw_window_bias():
    q = np.arange(SW_G * ABLK)[:, None] % ABLK
    k = np.arange(3 * ABLK)[None, :]
    tables = [np.where(np.abs(q + lead - k) <= SW_WIN, 0.0, NEG) for lead in (0, ABLK, 2 * ABLK)]
    return jnp.asarray(np.stack(tables).astype(np.float32))


SW_QB = 4


def _sw_kernel(sink_ref, q_ref, k_ref, v_ref, kc_ref, vc_ref, wb_ref, o_ref):
    nk = 3 * ABLK
    nb = DEC_SEQ // ABLK
    for sub in range(SW_QB):
        j = pl.program_id(1) * SW_QB + sub
        rows = slice(sub * ABLK, (sub + 1) * ABLK)
        start = pl.multiple_of(jnp.clip((j - 1) * ABLK, 0, DEC_SEQ - nk), ABLK)
        wb = wb_ref[jnp.where(j == 0, 0, jnp.where(j == nb - 1, 2, 1))]
        outs = []
        for g in range(SW_KV):
            kcols = slice(g * DH, (g + 1) * DH)
            q = jnp.concatenate([q_ref[rows, (g * SW_G + r) * DH:(g * SW_G + r + 1) * DH] for r in range(SW_G)],
                                axis=0)
            kw = k_ref[pl.ds(start, nk), kcols]
            vw = v_ref[pl.ds(start, nk), kcols]
            s_w = _dot_nt(q, kw) + wb
            s_c = _dot_nt(q, kc_ref[:, kcols])
            o = _softmax_pv([(s_w, vw), (s_c, vc_ref[:, kcols])], sink=_sink_col(sink_ref, g, ABLK))
            outs.extend(o[r * ABLK:(r + 1) * ABLK, :] for r in range(SW_G))
        o_ref[rows, :] = jnp.concatenate(outs, axis=1).astype(BF16)


def _sw_attn(sink, q, k, v, kc, vc):
    nsteps = DEC_SEQ // (ABLK * SW_QB)
    return pl.pallas_call(
        _sw_kernel,
        grid=(DEC_BATCH, nsteps),
        in_specs=[
            pl.BlockSpec(memory_space=pltpu.SMEM),
            pl.BlockSpec((SW_QB * ABLK, SWQ_W), lambda b, j: (b * nsteps + j, 0)),
            pl.BlockSpec((DEC_SEQ, SWKV_W), lambda b, j: (b, 0)),
            pl.BlockSpec((DEC_SEQ, SWKV_W), lambda b, j: (b, 0)),
            pl.BlockSpec((None, PAST, SWKV_W), lambda b, j: (b, 0, 0)),
            pl.BlockSpec((None, PAST, SWKV_W), lambda b, j: (b, 0, 0)),
            pl.BlockSpec((3, SW_G * ABLK, 3 * ABLK), lambda b, j: (0, 0, 0)),
        ],
        out_specs=pl.BlockSpec((SW_QB * ABLK, SWQ_W), lambda b, j: (b * nsteps + j, 0)),
        out_shape=jax.ShapeDtypeStruct((N_S, SWQ_W), BF16),
        compiler_params=_cparams("parallel", "arbitrary"),
        name="sw_attn",
    )(sink, q, k, v, kc, vc, _sw_window_bias())


BT = 4096
NB = N_TOK // BT
TPB = BT // TM
CH = 256
CH_TAIL = 128
XPS = 2
NES = N_EXP // XPS
DT = 2
NDS = TPB // DT
CT = 2
NCS = TPB // CT
assert NPT % CT == 0
XO_ROWS = 2 * BT + CH
ROW = 8
assert D == ROW * 128


def _route_dense(sel):
    e = lax.broadcasted_iota(jnp.int32, sel.shape, 0)
    el = e % EPG
    g = e // EPG

    def group_rot(a, k):
        return jnp.where(el >= k, pltpu.roll(a, k, 0), pltpu.roll(a, (k - EPG) % N_EXP, 0))

    rank = jnp.zeros_like(sel)
    for k in range(1, EPG):
        partner = group_rot(sel, k)
        ahead = jnp.logical_or(partner > sel, jnp.logical_and(el >= k, partner == sel))
        rank = rank + ahead.astype(F32)
    contrib = jnp.where(rank < 2.0, sel, 0.0)
    score = contrib
    for k in range(1, EPG):
        score = score + group_rot(contrib, k)
    best = None
    for k in range(1, N_GRP):
        other = pltpu.roll(score, k * EPG, 0)
        c = jnp.logical_or(score > other, jnp.logical_and(g < k, score == other))
        best = c if best is None else jnp.logical_and(best, c)
    return jnp.logical_and(best, rank == 0.0), jnp.logical_and(best, rank == 1.0)


def _pick(mask, vals):
    return jnp.sum(jnp.where(mask, vals, 0.0), axis=0, keepdims=True)


def _route_tile(xv, mod_ref, g_ref, rw_ref, rb_ref, h_ref, pos_ref, w_ref, cnt_ref, off_ref, meta, carry):
    i = pl.program_id(0)
    j = i % TPB
    r = _mod_row(i)
    h = _rms_mod(xv, g_ref[...], mod_ref[3, pl.ds(r, 1), :], mod_ref[4, pl.ds(r, 1), :])
    h_ref[...] = h.astype(BF16)
    h_hi = h.astype(BF16)
    h_lo = (h - h_hi.astype(F32)).astype(BF16)
    logits = _dot(h_lo, rw_ref[0]) + _dot(h_hi, rw_ref[1]) + _dot(h_hi, rw_ref[0])
    lt = logits.T[:N_EXP, :]
    m = jnp.max(lt, axis=0, keepdims=True)
    ex = jnp.exp(lt - m)
    pr = ex / jnp.sum(ex, axis=0, keepdims=True)
    se = pr + rb_ref[:N_EXP, :]
    top1, top2 = _route_dense(se)

    @pl.when(j == 0)
    def _():
        carry[...] = jnp.zeros_like(carry)

    member = jnp.logical_or(top1, top2).astype(F32)
    s_idx = lax.broadcasted_iota(jnp.int32, (TM, TM), 0)
    t_idx = lax.broadcasted_iota(jnp.int32, (TM, TM), 1)
    before = jnp.where(s_idx < t_idx, 1.0, 0.0).astype(BF16)
    seen = _dot(member.astype(BF16), before) + carry[:, 0:1]
    ids = lax.broadcasted_iota(jnp.int32, (N_EXP, TM), 0).astype(F32)
    p1 = _pick(top1, pr)
    p2 = _pick(top2, pr)
    den = p1 + p2
    w_ref[j, 0:1, :] = p1 / den
    w_ref[j, 1:2, :] = p2 / den
    meta[j, 0:1, :] = _pick(top1, ids)
    meta[j, 1:2, :] = _pick(top2, ids)
    meta[j, 2:3, :] = _pick(top1, seen)
    meta[j, 3:4, :] = _pick(top2, seen)
    carry[...] = carry[...] + jnp.sum(member, axis=1, keepdims=True)

    @pl.when(j == TPB - 1)
    def _():
        cnt = carry[...]
        offs = [jnp.zeros((1, 128), F32)]
        for e in range(1, N_EXP):
            offs.append(offs[-1] + cnt[e - 1:e, :])
        cnt_ref[...] = cnt
        off_ref[...] = jnp.concatenate(offs, axis=0)
        for jj in range(TPB):
            for k in range(2):
                eid = meta[jj, k:k + 1, :]
                pos = meta[jj, 2 + k:3 + k, :]
                for e in range(1, N_EXP):
                    pos = pos + jnp.where(eid == float(e), offs[e][:, 0:1], 0.0)
                pos_ref[jj, k:k + 1, :] = (pos * float(ROW)).astype(jnp.int32)


def _router_kernel(x_ref, *refs):
    _route_tile(x_ref[...], *refs)


def _odd_out_router_kernel(nap_ref, nas_ref, swp_ref, sws_ref, x_ref, wo_ref, mod_ref, g_ref, rw_ref, rb_ref,
                           x1_ref, *refs):
    i = pl.program_id(0)
    is_p = i < NPT
    ona = jnp.where(is_p, nap_ref[...], nas_ref[...])
    osw = jnp.where(is_p, swp_ref[...], sws_ref[...])
    y = _dot(ona, wo_ref[:NA_W, :]) + _dot(osw, wo_ref[NA_W:, :])
    x1 = x_ref[...] + mod_ref[2, pl.ds(_mod_row(i), 1), :] * y
    x1_ref[...] = x1
    _route_tile(x1, mod_ref, g_ref, rw_ref, rb_ref, *refs)


def _router(x, mod_l, g, rw2, rb_col, attn=None):
    blk = lambda i: (i // TPB, 0, 0, 0)
    in_specs = [
        pl.BlockSpec((TM, D), lambda i: (i, 0)),
        pl.BlockSpec((6, MOD_ROWS, D), lambda i: (0, 0, 0)),
        pl.BlockSpec((1, D), lambda i: (0, 0)),
        pl.BlockSpec((2, D, 128), lambda i: (0, 0, 0)),
        pl.BlockSpec((128, 1), lambda i: (0, 0)),
    ]
    out_specs = [
        pl.BlockSpec((TM, D), lambda i: (i, 0)),
        pl.BlockSpec((None, TPB, 2, TM), blk),
        pl.BlockSpec((None, TPB, 2, TM), blk),
        pl.BlockSpec((None, N_EXP, 128), lambda i: (i // TPB, 0, 0)),
        pl.BlockSpec((None, N_EXP, 128), lambda i: (i // TPB, 0, 0)),
    ]
    out_shape = [
        jax.ShapeDtypeStruct((N_TOK, D), BF16),
        jax.ShapeDtypeStruct((NB, TPB, 2, TM), jnp.int32),
        jax.ShapeDtypeStruct((NB, TPB, 2, TM), F32),
        jax.ShapeDtypeStruct((NB, N_EXP, 128), F32),
        jax.ShapeDtypeStruct((NB, N_EXP, 128), F32),
    ]
    args = (x, mod_l, g, rw2, rb_col)
    body = _router_kernel
    if attn is not None:
        nap, nas, swp, sws, w_out = attn
        in_specs = (_dual_specs(TM, NA_W, NPT) + _dual_specs(TM, SWQ_W, NPT) + in_specs[:1]
                    + [pl.BlockSpec((D, D), lambda i: (0, 0))] + in_specs[1:])
        out_specs = [pl.BlockSpec((TM, D), lambda i: (i, 0))] + out_specs
        out_shape = [jax.ShapeDtypeStruct((N_TOK, D), F32)] + out_shape
        args = (nap, nas, swp, sws, x, w_out, mod_l, g, rw2, rb_col)
        body = _odd_out_router_kernel
    return pl.pallas_call(
        body,
        grid=(NT,),
        in_specs=in_specs,
        out_specs=out_specs,
        out_shape=out_shape,
        scratch_shapes=[pltpu.VMEM((TPB, 4, TM), F32), pltpu.VMEM((N_EXP, 128), F32)],
        compiler_params=_cparams("arbitrary"),
        name="router" if attn is None else "odd_out_router",
    )(*args)


def _row(p):
    return pl.ds(pl.multiple_of(p, ROW), ROW)


def _expert_segment(n, base, xo, wg_ref, wu_ref, wd_ref):
    def ffn_rows(row0, rows):
        x = jnp.concatenate([xo[pl.ds(row0 * ROW + c, rows, stride=ROW), :] for c in range(ROW)], axis=1)
        xb = x.astype(BF16)
        a = _dot(xb, wg_ref[...])
        u = _dot(xb, wu_ref[...])
        hid = (a / (1.0 + jnp.exp(-a))) * u
        out = _dot(hid.astype(BF16), wd_ref[...])
        valid = lax.broadcasted_iota(jnp.int32, (rows, 1), 0) < base + n - row0
        res = jnp.where(valid, out, x)
        for c in range(ROW):
            xo[pl.ds(row0 * ROW + c, rows, stride=ROW), :] = res[:, c * 128:(c + 1) * 128]

    def chunk(jc, carry):
        ffn_rows(base + jc * CH, CH)
        return carry

    nfull = (n + CH - CH_TAIL - 1) // CH
    lax.fori_loop(0, nfull, chunk, 0)

    @pl.when(n > nfull * CH)
    def _():
        ffn_rows(base + nfull * CH, CH_TAIL)


def _experts_kernel(cnt_ref, off_ref, dpos_ref, pos_ref, w_ref, h_ref, wg_ref, wu_ref, wd_ref, x_ref, mod_ref,
                    fg_ref, *rest, final):
    if final:
        op_ref, os_ref, xo, stg = rest
    else:
        o_ref, xo, stg = rest
    b = pl.program_id(0)
    s = pl.program_id(1)

    @pl.when(s == 0)
    def _():
        xo[pl.ds(2 * BT * ROW, CH * ROW), :] = jnp.zeros((CH * ROW, 128), F32)

    @pl.when(s < NDS)
    def _():
        for dt in range(DT):
            hs = h_ref[dt * TM:(dt + 1) * TM, :].astype(F32)
            for c in range(ROW):
                stg[pl.ds(c, TM, stride=ROW), :] = hs[:, c * 128:(c + 1) * 128]
            for t in range(TM):
                v = stg[t * ROW:(t + 1) * ROW, :]
                xo[_row(dpos_ref[dt, 0, t]), :] = v
                xo[_row(dpos_ref[dt, 1, t]), :] = v

    @pl.when(jnp.logical_and(s >= NDS, s < NDS + NES))
    def _():
        for k in range(XPS):
            e = (s - NDS) * XPS + k
            _expert_segment(cnt_ref[b, e], off_ref[b, e], xo, wg_ref.at[k], wu_ref.at[k], wd_ref.at[k])

    @pl.when(s >= NDS + NES - 1)
    def _():
        step = b * NCS + s - (NDS + NES - 1)
        for ct in range(CT):
            tile = step * CT + ct
            rows = slice(ct * TM, (ct + 1) * TM)
            g2 = mod_ref[5, pl.ds(_mod_row(tile), 1), :]
            for t in range(TM):
                a = xo[_row(pos_ref[ct, 0, t]), :]
                u = xo[_row(pos_ref[ct, 1, t]), :]
                stg[t * ROW:(t + 1) * ROW, :] = w_ref[ct, 0, t] * a + w_ref[ct, 1, t] * u
            y = jnp.concatenate([stg[pl.ds(c, TM, stride=ROW), :] for c in range(ROW)], axis=1)
            res = x_ref[rows, :] + g2 * y
            if final:
                ms = jnp.mean(res * res, axis=-1, keepdims=True)
                res = res * lax.rsqrt(ms + EPS) * fg_ref[...]

                @pl.when(tile < NPT)
                def _():
                    op_ref[rows, :] = res

                @pl.when(tile >= NPT)
                def _():
                    os_ref[rows, :] = res
            else:
                o_ref[rows, :] = res


def _experts(cnt, off, pos, wts, h, wg, wu, wd, x, mod_l, fg, *, layer, final):
    def cstep_of(b, s):
        return b * NCS + jnp.clip(s - (NDS + NES - 1), 0, NCS - 1)

    def expert_of(b, s, *_):
        return (layer, jnp.clip(s - NDS, 0, NES - 1), 0, 0)

    def dispatch_of(b, s, *_):
        return (b, jnp.minimum(s, NDS - 1), 0, 0)

    def combine_of(b, s, *_):
        return (b, jnp.clip(s - (NDS + NES - 1), 0, NCS - 1), 0, 0)

    smem_disp = pl.BlockSpec((None, DT, 2, TM), dispatch_of, memory_space=pltpu.SMEM)
    smem_blk = pl.BlockSpec((None, CT, 2, TM), combine_of, memory_space=pltpu.SMEM)
    tm = CT * TM
    if final:
        out_specs = [pl.BlockSpec((tm, D), lambda b, s, *_: (jnp.minimum(cstep_of(b, s), N_P // tm - 1), 0)),
                     pl.BlockSpec((tm, D), lambda b, s, *_: (jnp.maximum(cstep_of(b, s) - N_P // tm, 0), 0))]
        out_shape = [jax.ShapeDtypeStruct((N_P, D), F32), jax.ShapeDtypeStruct((N_S, D), F32)]
    else:
        out_specs = pl.BlockSpec((tm, D), lambda b, s, *_: (cstep_of(b, s), 0))
        out_shape = jax.ShapeDtypeStruct((N_TOK, D), F32)
    grid_spec = pltpu.PrefetchScalarGridSpec(
        num_scalar_prefetch=2,
        grid=(NB, NDS + NES + NCS - 1),
        in_specs=[
            smem_disp,
            smem_blk,
            smem_blk,
            pl.BlockSpec((DT * TM, D), lambda b, s, *_: (b * NDS + jnp.minimum(s, NDS - 1), 0)),
            pl.BlockSpec((None, XPS, D, D_EXP), expert_of),
            pl.BlockSpec((None, XPS, D, D_EXP), expert_of),
            pl.BlockSpec((None, XPS, D_EXP, D), expert_of),
            pl.BlockSpec((tm, D), lambda b, s, *_: (cstep_of(b, s), 0)),
            pl.BlockSpec((6, MOD_ROWS, D), lambda b, s, *_: (0, 0, 0)),
            pl.BlockSpec((1, D), lambda b, s, *_: (0, 0)),
        ],
        out_specs=out_specs,
        scratch_shapes=[pltpu.VMEM((XO_ROWS * ROW, 128), F32), pltpu.VMEM((TM * ROW, 128), F32)],
    )
    return pl.pallas_call(
        functools.partial(_experts_kernel, final=final),
        grid_spec=grid_spec,
        out_shape=out_shape,
        compiler_params=_cparams("arbitrary", "arbitrary"),
        name="experts_final" if final else "experts",
    )(cnt, off, pos, pos, wts, h, wg, wu, wd, x, mod_l, fg)


def _moe_sparse(x, routed, mod_l, wg, wu, wd, fg, *, layer, final):
    h, pos, wts, cnt, off = routed
    cnt = cnt[:, :, 0].astype(jnp.int32)
    off = off[:, :, 0].astype(jnp.int32)
    return _experts(cnt, off, pos, wts, h, wg, wu, wd, x, mod_l, fg, layer=layer, final=final)


def kernel(x_prompt, x_sample, cache_na_k, cache_na_v, cache_sw_k, cache_sw_v, c, c_ctx, mod_w, mod_b, norm_mix_g, norm_ffn_g, ev_w_in, ev_pool_w, ev_pool_scale, ev_conv_w, ev_w_out, od_w_in, od_rpb, od_sink, od_w_out, router_w, router_b, moe_w_gate, moe_w_up, moe_w_down, final_norm_g):
    xp = x_prompt.reshape(N_P, D)
    xs = x_sample.reshape(N_S, D)
    cvec = jnp.concatenate([c_ctx[None, :], c, jnp.zeros((MOD_ROWS - 1 - DEC_BATCH, D), F32)], axis=0)
    mod = _modulation(cvec, mod_w, mod_b)

    rw_pad = jnp.pad(router_w, ((0, 0), (0, 128 - N_EXP)))
    rw_hi = rw_pad.astype(BF16)
    rw2 = jnp.stack([rw_hi, (rw_pad - rw_hi.astype(F32)).astype(BF16)])
    rb_col = jnp.pad(router_b, (0, 128 - N_EXP)).reshape(128, 1)

    conv_w = jnp.pad(ev_conv_w[0], ((0, 8 - ev_conv_w.shape[1]), (0, 0)))
    x, wg, wu, wd = _even_layer(xp, xs, mod[0], norm_mix_g[0:1], ev_w_in[0].astype(BF16),
                                ev_pool_w[0].astype(BF16), ev_pool_scale[0:1], conv_w, ev_w_out[0].astype(BF16),
                                moe_w_gate, moe_w_up, moe_w_down)
    fg = final_norm_g.reshape(1, D)
    routed = _router(x, mod[0], norm_ffn_g[0:1], rw2, rb_col)
    x = _moe_sparse(x, routed, mod[0], wg, wu, wd, fg, layer=0, final=False)

    cos, sin = _rope_tables()
    w_in = od_w_in[0].astype(BF16)
    g1 = norm_mix_g[1:2]
    qna_p, kna_p, vna_p, qsw_p, ksw_p, vsw_p, nak, nav, swk, swv = _odd_in(x, mod[1], g1, w_in, cos, sin, prompt=True)
    qna_s, kna_s, vna_s, qsw_s, ksw_s, vsw_s = _odd_in(x, mod[1], g1, w_in, cos, sin, prompt=False)
    sink = od_sink[0]
    ona_p, osw_p = _ctx_attn(sink, qna_p, kna_p, vna_p, qsw_p, ksw_p, vsw_p)
    bias = _na_bias(od_rpb[0])
    ona_s = _na_attn(qna_s, kna_s, vna_s,
                     cache_na_k[:, 0].reshape(DEC_BATCH, PAST, NA_W).astype(BF16),
                     cache_na_v[:, 0].reshape(DEC_BATCH, PAST, NA_W).astype(BF16), bias)
    osw_s = _sw_attn(sink, qsw_s, ksw_s, vsw_s,
                     cache_sw_k[:, 0].reshape(DEC_BATCH, PAST, SWKV_W).astype(BF16),
                     cache_sw_v[:, 0].reshape(DEC_BATCH, PAST, SWKV_W).astype(BF16))
    x, *routed = _router(x, mod[1], norm_ffn_g[1:2], rw2, rb_col,
                         attn=(ona_p, ona_s, osw_p, osw_s, od_w_out[0].astype(BF16)))
    y_prompt, y_sample = _moe_sparse(x, routed, mod[1], wg, wu, wd, fg, layer=1, final=True)
    y_prompt = y_prompt.reshape(BATCH, SEQ, D)
    y_sample = y_sample.reshape(DEC_BATCH, DEC_SEQ, D)
    new_na_k = nak.reshape(BATCH, 1, SEQ, NA_H, DH)
    new_na_v = nav.reshape(BATCH, 1, SEQ, NA_H, DH)
    new_sw_k = swk.reshape(BATCH, 1, SEQ, SW_KV, DH)
    new_sw_v = swv.reshape(BATCH, 1, SEQ, SW_KV, DH)
    return (y_prompt, y_sample, new_na_k, new_na_v, new_sw_k, new_sw_v)
```

```python
import functools

import jax
import jax.numpy as jnp
import numpy as np
from jax import lax
from jax.experimental import pallas as pl
from jax.experimental.pallas import tpu as pltpu

D = 1024
BATCH = 16
SEQ = 256
DEC_BATCH = 4
DEC_SEQ = 4096
PAST = 512
GRID_W = 64
ROWS = DEC_SEQ // GRID_W
DH = 64
POOL_W = 512
POOL_WINDOWS = (2, 4, 8, 16)
POOL_GW = 128
CONV_W = 512
EVEN_IN = POOL_W + 3 * CONV_W
NA_H = 8
NA_ROWS = 8
NA_COLS = 16
SW_H = 8
SW_KV = 2
SW_G = SW_H // SW_KV
SW_WIN = 128
ABLK = 128
NA_W = NA_H * DH
SWQ_W = SW_H * DH
SWKV_W = SW_KV * DH
ODD_IN = 3 * NA_W + SWQ_W + 2 * SWKV_W
N_EXP = 16
N_GRP = 4
EPG = 4
D_EXP = 512
EPS = 1e-6
NEG = -1e30
ROPE_BASE = 10000.0
QK_SCALE = DH ** -0.5
assert QK_SCALE == 0.125

N_P = BATCH * SEQ
N_S = DEC_BATCH * DEC_SEQ
N_TOK = N_P + N_S
MOD_ROWS = 8

TM = 256
NPT = N_P // TM
TPS = DEC_SEQ // TM
NT = N_TOK // TM
HALO = 8
EV_HALVES = 2
EV_TM = EV_HALVES * TM
CAST_STEPS = 32

F32 = jnp.float32
BF16 = jnp.bfloat16
VMEM_LIMIT = 62 * 1024 * 1024


def _cparams(*sem):
    return pltpu.CompilerParams(dimension_semantics=sem, vmem_limit_bytes=VMEM_LIMIT)


def _mod_row(i):
    return jnp.where(i < NPT, 0, 1 + (i - NPT) // TPS)


def _rms_mod(x, g, shift, scale):
    ms = jnp.mean(x * x, axis=-1, keepdims=True)
    y = x * lax.rsqrt(ms + EPS) * g
    return y * (1.0 + scale) + shift


def _dot(a, b):
    return jnp.dot(a, b, preferred_element_type=F32)


def _dot_nt(a, b):
    return lax.dot_general(a, b, (((1,), (1,)), ((), ())), preferred_element_type=F32)


def _mod_kernel(cv_ref, w_ref, b_ref, o_ref):
    cv = cv_ref[...]
    a = cv / (1.0 + jnp.exp(-cv))
    o_ref[...] = jnp.dot(a, w_ref[...], preferred_element_type=F32,
                         precision=lax.Precision.HIGHEST) + b_ref[...]


def _modulation(cvec, mod_w, mod_b):
    depth = mod_w.shape[0]
    return pl.pallas_call(
        _mod_kernel,
        grid=(depth, 6),
        in_specs=[
            pl.BlockSpec((MOD_ROWS, D), lambda l, j: (0, 0)),
            pl.BlockSpec((None, D, D), lambda l, j: (l, 0, j)),
            pl.BlockSpec((None, None, 1, D), lambda l, j: (l, j, 0, 0)),
        ],
        out_specs=pl.BlockSpec((None, None, MOD_ROWS, D), lambda l, j: (l, j, 0, 0)),
        out_shape=jax.ShapeDtypeStruct((depth, 6, MOD_ROWS, D), F32),
        compiler_params=_cparams("arbitrary", "arbitrary"),
        name="modulation",
    )(cvec, mod_w, mod_b.reshape(depth, 6, 1, D))


def _dual_specs(tm, width, npt):
    return [
        pl.BlockSpec((tm, width), lambda i: (jnp.minimum(i, npt - 1), 0)),
        pl.BlockSpec((tm, width), lambda i: (jnp.maximum(i - npt, 0), 0)),
    ]


def _even_kernel(xp_ref, xs_ref, xprev_ref, xnext_ref, mod_ref, g_ref, wi_ref, pw_ref, ps_ref, cw_ref,
                 wo_ref, wgf_ref, wuf_ref, wdf_ref, o_ref, wgb_ref, wub_ref, wdb_ref, pext, uext):
    i = pl.program_id(0)

    @pl.when(i < CAST_STEPS)
    def _():
        wgb_ref[...] = wgf_ref[...].astype(BF16)
        wub_ref[...] = wuf_ref[...].astype(BF16)
        wdb_ref[...] = wdf_ref[...].astype(BF16)

    npt = NPT // EV_HALVES
    r = _mod_row(i * EV_HALVES)
    is_p = i < npt
    seq_len = jnp.where(is_p, SEQ, DEC_SEQ)

    x = jnp.where(is_p, xp_ref[...], xs_ref[...])
    xe = jnp.concatenate([xprev_ref[...], x, xnext_ref[...]], axis=0)
    h = _rms_mod(xe, g_ref[...], mod_ref[0, pl.ds(r, 1), :], mod_ref[1, pl.ds(r, 1), :])
    ze_all = _dot(h.astype(BF16), wi_ref[...])
    row = lax.broadcasted_iota(jnp.int32, (TM + 2 * HALO, 1), 0)
    for hf in range(EV_HALVES):
        t0 = jnp.where(is_p, 0, ((i - npt) % (DEC_SEQ // EV_TM)) * EV_TM + hf * TM)
        first = t0 == 0
        last = t0 + TM == seq_len
        ze = ze_all[hf * TM:hf * TM + TM + 2 * HALO, :]
        outside = jnp.logical_or(jnp.logical_and(first, row < HALO), jnp.logical_and(last, row >= HALO + TM))
        ze = jnp.where(outside, 0.0, ze)
        pext[...] = ze[:, :POOL_W]
        uext[...] = ze[:, POOL_W + 2 * CONV_W:] * ze[:, POOL_W:POOL_W + CONV_W]
        p = ze[HALO:HALO + TM, :POOL_W]
        gb = ze[HALO:HALO + TM, POOL_W + CONV_W:POOL_W + 2 * CONV_W]

        t = t0 + lax.broadcasted_iota(jnp.int32, (TM, 1), 0)
        mixed = []
        for g, w in enumerate(POOL_WINDOWS):
            cols = slice(g * POOL_GW, (g + 1) * POOL_GW)
            acc = jnp.zeros((TM, POOL_GW), F32)
            for k in range(-(w // 2), w - (w // 2)):
                acc = acc + pext[pl.ds(HALO + k, TM), cols]
            lo = jnp.maximum(t - w // 2, 0)
            hi = jnp.minimum(t + (w - 1 - w // 2), seq_len - 1)
            cnt = (hi - lo + 1).astype(F32)
            d = acc / cnt - p[:, cols]
            mixed.append((_dot(d.astype(BF16), pw_ref[g]) * ps_ref[:, cols]).astype(BF16))
        conv = (uext[pl.ds(HALO - 1, TM), :] * cw_ref[0:1, :] + uext[pl.ds(HALO, TM), :] * cw_ref[1:2, :]
                + uext[pl.ds(HALO + 1, TM), :] * cw_ref[2:3, :])
        mixed.append((gb * conv).astype(BF16))
        y = _dot(jnp.concatenate(mixed, axis=1), wo_ref[...])
        rows = slice(hf * TM, (hf + 1) * TM)
        o_ref[rows, :] = x[rows, :] + mod_ref[2, pl.ds(r, 1), :] * y


def _even_layer(xp, xs, mod_l, g, w_in, pool_w, pool_scale, conv_w, w_out, wg, wu, wd):
    hb = EV_TM // HALO
    nhb = N_S // HALO
    npt = N_P // EV_TM
    nt = N_TOK // EV_TM
    wg2, wu2, wd2 = wg.reshape(-1, D_EXP), wu.reshape(-1, D_EXP), wd.reshape(-1, D)
    rows_in, rows_out = wg2.shape[0] // CAST_STEPS, wd2.shape[0] // CAST_STEPS
    assert rows_in * CAST_STEPS == wg2.shape[0] and rows_out * CAST_STEPS == wd2.shape[0] and CAST_STEPS <= nt
    cast_blk = lambda i: (jnp.minimum(i, CAST_STEPS - 1), 0)
    cast_specs = [pl.BlockSpec((rows_in, D_EXP), cast_blk), pl.BlockSpec((rows_in, D_EXP), cast_blk),
                  pl.BlockSpec((rows_out, D), cast_blk)]
    outs = pl.pallas_call(
        _even_kernel,
        grid=(nt,),
        in_specs=_dual_specs(EV_TM, D, npt) + [
            pl.BlockSpec((HALO, D), lambda i: (jnp.maximum((i - npt) * hb - 1, 0), 0)),
            pl.BlockSpec((HALO, D), lambda i: (jnp.clip((i - npt + 1) * hb, 0, nhb - 1), 0)),
            pl.BlockSpec((6, MOD_ROWS, D), lambda i: (0, 0, 0)),
            pl.BlockSpec((1, D), lambda i: (0, 0)),
            pl.BlockSpec((D, EVEN_IN), lambda i: (0, 0)),
            pl.BlockSpec((4, POOL_GW, POOL_GW), lambda i: (0, 0, 0)),
            pl.BlockSpec((1, POOL_W), lambda i: (0, 0)),
            pl.BlockSpec((8, CONV_W), lambda i: (0, 0)),
            pl.BlockSpec((D, D), lambda i: (0, 0)),
        ] + cast_specs,
        out_specs=[pl.BlockSpec((EV_TM, D), lambda i: (i, 0))] + cast_specs,
        out_shape=[jax.ShapeDtypeStruct((N_TOK, D), F32), jax.ShapeDtypeStruct(wg2.shape, BF16),
                   jax.ShapeDtypeStruct(wu2.shape, BF16), jax.ShapeDtypeStruct(wd2.shape, BF16)],
        scratch_shapes=[pltpu.VMEM((TM + 2 * HALO, POOL_W), F32),
                        pltpu.VMEM((TM + 2 * HALO, CONV_W), F32)],
        compiler_params=_cparams("arbitrary"),
        name="even_layer",
    )(xp, xs, xs, xs, mod_l, g, w_in, pool_w, pool_scale, conv_w, w_out, wg2, wu2, wd2)
    return outs[0], outs[1].reshape(wg.shape), outs[2].reshape(wu.shape), outs[3].reshape(wd.shape)


def _rope(x, cos, sin_signed):
    n = x.shape[1] // 128
    cosf = jnp.concatenate([cos] * n, axis=1) if n > 1 else cos
    sinf = jnp.concatenate([sin_signed] * n, axis=1) if n > 1 else sin_signed
    w = x.shape[1]
    lane = lax.broadcasted_iota(jnp.int32, x.shape, 1)
    up = pltpu.roll(x, w - DH // 4, 1)
    dn = pltpu.roll(x, DH // 4, 1)
    rot = jnp.where((lane % (DH // 2)) < DH // 4, up, dn)
    return x * cosf + rot * sinf


def _odd_in_kernel(x_ref, mod_ref, g_ref, w_ref, cos_ref, sin_ref, *out_refs, tm, tile0, rope, kv_f32):
    r = _mod_row((pl.program_id(0) + tile0) * (tm // TM))
    h = _rms_mod(x_ref[...], g_ref[...], mod_ref[0, pl.ds(r, 1), :], mod_ref[1, pl.ds(r, 1), :])
    z = _dot(h.astype(BF16), w_ref[...])
    qna_ref, kna_ref, vna_ref, qsw_ref, ksw_ref, vsw_ref = out_refs[:6]
    c0 = 0
    qna = z[:, 0:NA_W]
    kna = z[:, NA_W:2 * NA_W]
    vna = z[:, 2 * NA_W:3 * NA_W]
    c0 = 3 * NA_W
    qsw = z[:, c0:c0 + SWQ_W]
    ksw = z[:, c0 + SWQ_W:c0 + SWQ_W + SWKV_W]
    vsw = z[:, c0 + SWQ_W + SWKV_W:]
    if kv_f32:
        for ref, val in zip(out_refs[6:10], (kna, vna, ksw, vsw)):
            heads = val.shape[1] // DH
            for hh in range(heads):
                ref[pl.ds(hh, SEQ, stride=heads), :] = val[:, hh * DH:(hh + 1) * DH]
    if rope:
        cos = cos_ref[...]
        sin = sin_ref[...]
        qsw = _rope(qsw, cos, sin)
        ksw = _rope(ksw, cos, sin)
    qna_ref[...] = (qna * QK_SCALE).astype(BF16)
    kna_ref[...] = kna.astype(BF16)
    vna_ref[...] = vna.astype(BF16)
    qsw_ref[...] = (qsw * QK_SCALE).astype(BF16)
    ksw_ref[...] = ksw.astype(BF16)
    vsw_ref[...] = vsw.astype(BF16)


def _odd_in(x, mod_l, g, w, cos, sin, *, prompt):
    tm = TM if prompt else 2 * TM
    tile0 = 0 if prompt else N_P // tm
    nt = (N_P if prompt else N_S) // tm
    n = nt * tm
    widths = [NA_W, NA_W, NA_W, SWQ_W, SWKV_W, SWKV_W]
    out_shape = [jax.ShapeDtypeStruct((n, w_), BF16) for w_ in widths]
    out_specs = [pl.BlockSpec((tm, w_), lambda i: (i, 0)) for w_ in widths]
    if prompt:
        assert tm == SEQ
        for heads in (NA_H, NA_H, SW_KV, SW_KV):
            out_shape.append(jax.ShapeDtypeStruct((BATCH * SEQ * heads, DH), F32))
            out_specs.append(pl.BlockSpec((SEQ * heads, DH), lambda i: (i, 0)))
    return pl.pallas_call(
        functools.partial(_odd_in_kernel, tm=tm, tile0=tile0, rope=not prompt, kv_f32=prompt),
        grid=(nt,),
        in_specs=[
            pl.BlockSpec((tm, D), lambda i: (i + tile0, 0)),
            pl.BlockSpec((6, MOD_ROWS, D), lambda i: (0, 0, 0)),
            pl.BlockSpec((1, D), lambda i: (0, 0)),
            pl.BlockSpec((D, ODD_IN), lambda i: (0, 0)),
            pl.BlockSpec((tm, 128), lambda i: (i % (DEC_SEQ // tm), 0)),
            pl.BlockSpec((tm, 128), lambda i: (i % (DEC_SEQ // tm), 0)),
        ],
        out_specs=out_specs,
        out_shape=out_shape,
        compiler_params=_cparams("parallel"),
        name="odd_in_prompt" if prompt else "odd_in_latent",
    )(x, mod_l, g, w, cos, sin)


def _rope_tables():
    t = np.arange(DEC_SEQ)
    quarter = DH // 4
    inv = 1.0 / (ROPE_BASE ** (np.arange(quarter, dtype=np.float64) / quarter))

    def cos_sin(pos):
        ang = pos.astype(np.float64)[:, None] * inv[None, :]
        ang = np.concatenate([ang, ang], axis=-1)
        return np.cos(ang), np.sin(ang)

    cr, sr = cos_sin(t // GRID_W)
    cc, sc = cos_sin(t % GRID_W)
    cos = np.concatenate([cr, cc], axis=-1)
    sin = np.concatenate([sr, sc], axis=-1)
    sign = np.where((np.arange(DH) % (DH // 2)) < DH // 4, -1.0, 1.0)
    sin = sin * sign[None, :]
    cos = np.concatenate([cos, cos], axis=-1).astype(np.float32)
    sin = np.concatenate([sin, sin], axis=-1).astype(np.float32)
    return jnp.asarray(cos), jnp.asarray(sin)


def _softmax_pv(segs, sink=None):
    m = None
    for s, _ in segs:
        sm = jnp.max(s, axis=-1, keepdims=True)
        m = sm if m is None else jnp.maximum(m, sm)
    if sink is not None:
        m = jnp.maximum(m, sink)
    den = None
    acc = None
    for s, v in segs:
        p = jnp.exp(s - m)
        ps = jnp.sum(p, axis=-1, keepdims=True)
        den = ps if den is None else den + ps
        pv = _dot(p.astype(BF16), v)
        acc = pv if acc is None else acc + pv
    if sink is not None:
        den = den + jnp.exp(sink - m)
    return acc / den


def _sink_col(sink_ref, g, rows_per_head):
    row = lax.broadcasted_iota(jnp.int32, (SW_G * rows_per_head, 1), 0)
    col = jnp.zeros((SW_G * rows_per_head, 1), F32)
    for r in range(SW_G):
        col = jnp.where(row // rows_per_head == r, sink_ref[g * SW_G + r], col)
    return col


def _ctx_attn_kernel(sink_ref, qna_ref, kna_ref, vna_ref, qsw_ref, ksw_ref, vsw_ref, ona_ref, osw_ref):
    lane = lax.broadcasted_iota(jnp.int32, (SEQ, 2 * DH), 1)
    for p in range(NA_H // 2):
        cols = slice(p * 2 * DH, (p + 1) * 2 * DH)
        q = qna_ref[:, cols]
        k = kna_ref[:, cols]
        v = vna_ref[:, cols]
        outs = []
        for half in range(2):
            mine = (lane < DH) if half == 0 else (lane >= DH)
            qm = jnp.where(mine, q, jnp.zeros_like(q))
            outs.append(_softmax_pv([(_dot_nt(qm, k), v)]))
        ona_ref[:, cols] = jnp.where(lane < DH, outs[0], outs[1]).astype(BF16)
    outs = []
    for g in range(SW_KV):
        kc = slice(g * DH, (g + 1) * DH)
        q = jnp.concatenate([qsw_ref[:, (g * SW_G + r) * DH:(g * SW_G + r + 1) * DH] for r in range(SW_G)], axis=0)
        s = _dot_nt(q, ksw_ref[:, kc])
        o = _softmax_pv([(s, vsw_ref[:, kc])], sink=_sink_col(sink_ref, g, SEQ))
        outs.extend(o[r * SEQ:(r + 1) * SEQ, :] for r in range(SW_G))
    osw_ref[...] = jnp.concatenate(outs, axis=1).astype(BF16)


def _ctx_attn(sink, qna, kna, vna, qsw, ksw, vsw):
    def spec(w):
        return pl.BlockSpec((SEQ, w), lambda b: (b, 0))

    return pl.pallas_call(
        _ctx_attn_kernel,
        grid=(BATCH,),
        in_specs=[pl.BlockSpec(memory_space=pltpu.SMEM), spec(NA_W), spec(NA_W), spec(NA_W),
                  spec(SWQ_W), spec(SWKV_W), spec(SWKV_W)],
        out_specs=[spec(NA_W), spec(SWQ_W)],
        out_shape=[jax.ShapeDtypeStruct((N_P, NA_W), BF16), jax.ShapeDtypeStruct((N_P, SWQ_W), BF16)],
        compiler_params=_cparams("parallel"),
        name="ctx_attn",
    )(sink, qna, kna, vna, qsw, ksw, vsw)


NA_QB = 4
NA_KR = 12
NA_NQB = ROWS // NA_QB


def _na_block_offset(case, i, j):
    if case == 0:
        valid, dr = j < NA_ROWS, j - i + NA_ROWS - 1
    elif case == 1:
        valid, dr = i <= j < i + NA_ROWS, j - i + NA_ROWS // 2 - 1
    else:
        valid, dr = j >= NA_KR - NA_ROWS, j - i + NA_ROWS - 1 - (NA_KR - NA_QB)
    return dr if valid else None


def _na_bias_kernel(rpb_ref, o_ref):
    h = pl.program_id(0)
    cq = lax.broadcasted_iota(jnp.int32, (GRID_W, GRID_W), 0)
    ck = lax.broadcasted_iota(jnp.int32, (GRID_W, GRID_W), 1)
    cstart = jnp.clip(cq - NA_COLS // 2, 0, GRID_W - NA_COLS)
    ok = (ck >= cstart) & (ck < cstart + NA_COLS)
    dc = jnp.clip(ck - cq + NA_COLS - 1, 0, 2 * NA_COLS - 2)
    ndc = 2 * NA_COLS - 1
    neg = jnp.full((GRID_W, GRID_W), NEG, F32)
    by_offset = []
    for dr in range(2 * NA_ROWS - 1):
        b = jnp.zeros((GRID_W, GRID_W), F32)
        for e in range(ndc):
            b = jnp.where(dc == e, rpb_ref[h, dr * ndc + e], b)
        by_offset.append(jnp.where(ok, b, NEG))
    for case in range(3):
        for i in range(NA_QB):
            blocks = []
            for j in range(NA_KR):
                dr = _na_block_offset(case, i, j)
                blocks.append(neg if dr is None else by_offset[dr])
            o_ref[case, i * GRID_W:(i + 1) * GRID_W, :] = jnp.concatenate(blocks, axis=1)


def _na_bias(rpb):
    nd = 2 * NA_ROWS - 1
    return pl.pallas_call(
        _na_bias_kernel,
        grid=(NA_H,),
        in_specs=[pl.BlockSpec(memory_space=pltpu.SMEM)],
        out_specs=pl.BlockSpec((3, None, NA_QB * GRID_W, NA_KR * GRID_W), lambda h: (0, h, 0, 0)),
        out_shape=jax.ShapeDtypeStruct((3, NA_H, NA_QB * GRID_W, NA_KR * GRID_W), F32),
        compiler_params=_cparams("parallel"),
        name="na_bias",
    )(rpb.reshape(NA_H, nd * (2 * NA_COLS - 1)))


def _na_kernel(q_ref, k_ref, v_ref, kc_ref, vc_ref, bias_ref, o_ref):
    r0 = pl.program_id(1) * NA_QB
    start = pl.multiple_of(jnp.clip(r0 - NA_ROWS // 2, 0, ROWS - NA_KR) * GRID_W, GRID_W)
    nq = NA_QB * GRID_W
    lane = lax.broadcasted_iota(jnp.int32, (nq, 2 * DH), 1)
    for p in range(NA_H // 2):
        cols = slice(p * 2 * DH, (p + 1) * 2 * DH)
        q = q_ref[:, cols]
        kl = k_ref[pl.ds(start, NA_KR * GRID_W), cols]
        vl = v_ref[pl.ds(start, NA_KR * GRID_W), cols]
        kc = kc_ref[:, cols]
        vc = vc_ref[:, cols]
        zero = jnp.zeros_like(q)
        qm = jnp.concatenate([jnp.where(lane < DH, q, zero), jnp.where(lane < DH, zero, q)], axis=0)
        s_loc = _dot_nt(qm, kl)
        s_ctx = _dot_nt(qm, kc)
        outs = []
        for half in range(2):
            rows = slice(half * nq, (half + 1) * nq)
            outs.append(_softmax_pv([(s_loc[rows] + bias_ref[2 * p + half], vl), (s_ctx[rows], vc)]))
        o_ref[:, cols] = jnp.where(lane < DH, outs[0], outs[1]).astype(BF16)


def _na_attn(q, k, v, kc, vc, bias):
    nq = NA_QB * GRID_W

    def bias_case(b, rb):
        return (jnp.where(rb == 0, 0, jnp.where(rb == NA_NQB - 1, 2, 1)), 0, 0, 0)

    return pl.pallas_call(
        _na_kernel,
        grid=(DEC_BATCH, NA_NQB),
        in_specs=[
            pl.BlockSpec((nq, NA_W), lambda b, rb: (b * NA_NQB + rb, 0)),
            pl.BlockSpec((DEC_SEQ, NA_W), lambda b, rb: (b, 0)),
            pl.BlockSpec((DEC_SEQ, NA_W), lambda b, rb: (b, 0)),
            pl.BlockSpec((None, PAST, NA_W), lambda b, rb: (b, 0, 0)),
            pl.BlockSpec((None, PAST, NA_W), lambda b, rb: (b, 0, 0)),
            pl.BlockSpec((None, NA_H, nq, NA_KR * GRID_W), bias_case),
        ],
        out_specs=pl.BlockSpec((nq, NA_W), lambda b, rb: (b * NA_NQB + rb, 0)),
        out_shape=jax.ShapeDtypeStruct((N_S, NA_W), BF16),
        compiler_params=_cparams("parallel", "arbitrary"),
        name="na_attn",
    )(q, k, v, kc, vc, bias)


def _sw_window_bias():
    q = np.arange(SW_G * ABLK)[:, None] % ABLK
    k = np.arange(3 * ABLK)[None, :]
    tables = [np.where(np.abs(q + lead - k) <= SW_WIN, 0.0, NEG) for lead in (0, ABLK, 2 * ABLK)]
    return jnp.asarray(np.stack(tables).astype(np.float32))


SW_QB = 4


def _sw_kernel(sink_ref, q_ref, k_ref, v_ref, kc_ref, vc_ref, wb_ref, o_ref):
    nk = 3 * ABLK
    nb = DEC_SEQ // ABLK
    for sub in range(SW_QB):
        j = pl.program_id(1) * SW_QB + sub
        rows = slice(sub * ABLK, (sub + 1) * ABLK)
        start = pl.multiple_of(jnp.clip((j - 1) * ABLK, 0, DEC_SEQ - nk), ABLK)
        wb = wb_ref[jnp.where(j == 0, 0, jnp.where(j == nb - 1, 2, 1))]
        outs = []
        for g in range(SW_KV):
            kcols = slice(g * DH, (g + 1) * DH)
            q = jnp.concatenate([q_ref[rows, (g * SW_G + r) * DH:(g * SW_G + r + 1) * DH] for r in range(SW_G)],
                                axis=0)
            kw = k_ref[pl.ds(start, nk), kcols]
            vw = v_ref[pl.ds(start, nk), kcols]
            s_w = _dot_nt(q, kw) + wb
            s_c = _dot_nt(q, kc_ref[:, kcols])
            o = _softmax_pv([(s_w, vw), (s_c, vc_ref[:, kcols])], sink=_sink_col(sink_ref, g, ABLK))
            outs.extend(o[r * ABLK:(r + 1) * ABLK, :] for r in range(SW_G))
        o_ref[rows, :] = jnp.concatenate(outs, axis=1).astype(BF16)


def _sw_attn(sink, q, k, v, kc, vc):
    nsteps = DEC_SEQ // (ABLK * SW_QB)
    return pl.pallas_call(
        _sw_kernel,
        grid=(DEC_BATCH, nsteps),
        in_specs=[
            pl.BlockSpec(memory_space=pltpu.SMEM),
            pl.BlockSpec((SW_QB * ABLK, SWQ_W), lambda b, j: (b * nsteps + j, 0)),
            pl.BlockSpec((DEC_SEQ, SWKV_W), lambda b, j: (b, 0)),
            pl.BlockSpec((DEC_SEQ, SWKV_W), lambda b, j: (b, 0)),
            pl.BlockSpec((None, PAST, SWKV_W), lambda b, j: (b, 0, 0)),
            pl.BlockSpec((None, PAST, SWKV_W), lambda b, j: (b, 0, 0)),
            pl.BlockSpec((3, SW_G * ABLK, 3 * ABLK), lambda b, j: (0, 0, 0)),
        ],
        out_specs=pl.BlockSpec((SW_QB * ABLK, SWQ_W), lambda b, j: (b * nsteps + j, 0)),
        out_shape=jax.ShapeDtypeStruct((N_S, SWQ_W), BF16),
        compiler_params=_cparams("parallel", "arbitrary"),
        name="sw_attn",
    )(sink, q, k, v, kc, vc, _sw_window_bias())


BT = 4096
NB = N_TOK // BT
TPB = BT // TM
CH = 256
CH_TAIL = 128
XPS = 2
NES = N_EXP // XPS
DT = 2
NDS = TPB // DT
CT = 2
NCS = TPB // CT
assert NPT % CT == 0
RT = 2
assert TPB % RT == 0 and NPT % RT == 0
XO_ROWS = 2 * BT + CH
ROW = 8
assert D == ROW * 128


def _route_dense(sel):
    e = lax.broadcasted_iota(jnp.int32, sel.shape, 0)
    el = e % EPG
    g = e // EPG

    def group_rot(a, k):
        return jnp.where(el >= k, pltpu.roll(a, k, 0), pltpu.roll(a, (k - EPG) % N_EXP, 0))

    rank = jnp.zeros_like(sel)
    for k in range(1, EPG):
        partner = group_rot(sel, k)
        ahead = jnp.logical_or(partner > sel, jnp.logical_and(el >= k, partner == sel))
        rank = rank + ahead.astype(F32)
    contrib = jnp.where(rank < 2.0, sel, 0.0)
    score = contrib
    for k in range(1, EPG):
        score = score + group_rot(contrib, k)
    best = None
    for k in range(1, N_GRP):
        other = pltpu.roll(score, k * EPG, 0)
        c = jnp.logical_or(score > other, jnp.logical_and(g < k, score == other))
        best = c if best is None else jnp.logical_and(best, c)
    return jnp.logical_and(best, rank == 0.0), jnp.logical_and(best, rank == 1.0)


def _pick(mask, vals):
    return jnp.sum(jnp.where(mask, vals, 0.0), axis=0, keepdims=True)


def _route_tile(xv, sub, mod_ref, g_ref, rw_ref, rb_ref, h_ref, pos_ref, w_ref, cnt_ref, off_ref, meta, carry):
    i = pl.program_id(0) * RT + sub
    j = i % TPB
    r = _mod_row(i)
    h = _rms_mod(xv, g_ref[...], mod_ref[3, pl.ds(r, 1), :], mod_ref[4, pl.ds(r, 1), :])
    h_ref[sub * TM:(sub + 1) * TM, :] = h.astype(BF16)
    h_hi = h.astype(BF16)
    h_lo = (h - h_hi.astype(F32)).astype(BF16)
    logits = _dot(h_lo, rw_ref[0]) + _dot(h_hi, rw_ref[1]) + _dot(h_hi, rw_ref[0])
    lt = logits.T[:N_EXP, :]
    m = jnp.max(lt, axis=0, keepdims=True)
    ex = jnp.exp(lt - m)
    pr = ex / jnp.sum(ex, axis=0, keepdims=True)
    se = pr + rb_ref[:N_EXP, :]
    top1, top2 = _route_dense(se)

    @pl.when(j == 0)
    def _():
        carry[...] = jnp.zeros_like(carry)

    member = jnp.logical_or(top1, top2).astype(F32)
    s_idx = lax.broadcasted_iota(jnp.int32, (TM, TM), 0)
    t_idx = lax.broadcasted_iota(jnp.int32, (TM, TM), 1)
    before = jnp.where(s_idx < t_idx, 1.0, 0.0).astype(BF16)
    seen = _dot(member.astype(BF16), before) + carry[:, 0:1]
    ids = lax.broadcasted_iota(jnp.int32, (N_EXP, TM), 0).astype(F32)
    p1 = _pick(top1, pr)
    p2 = _pick(top2, pr)
    den = p1 + p2
    w_ref[j, 0:1, :] = p1 / den
    w_ref[j, 1:2, :] = p2 / den
    meta[j, 0:1, :] = _pick(top1, ids)
    meta[j, 1:2, :] = _pick(top2, ids)
    meta[j, 2:3, :] = _pick(top1, seen)
    meta[j, 3:4, :] = _pick(top2, seen)
    carry[...] = carry[...] + jnp.sum(member, axis=1, keepdims=True)

    @pl.when(j == TPB - 1)
    def _():
        cnt = carry[...]
        offs = [jnp.zeros((1, 128), F32)]
        for e in range(1, N_EXP):
            offs.append(offs[-1] + cnt[e - 1:e, :])
        cnt_ref[...] = cnt
        off_ref[...] = jnp.concatenate(offs, axis=0)
        for jj in range(TPB):
            for k in range(2):
                eid = meta[jj, k:k + 1, :]
                pos = meta[jj, 2 + k:3 + k, :]
                for e in range(1, N_EXP):
                    pos = pos + jnp.where(eid == float(e), offs[e][:, 0:1], 0.0)
                pos_ref[jj, k:k + 1, :] = (pos * float(ROW)).astype(jnp.int32)


def _router_kernel(x_ref, *refs):
    for sub in range(RT):
        _route_tile(x_ref[sub * TM:(sub + 1) * TM, :], sub, *refs)


def _odd_out_router_kernel(nap_ref, nas_ref, swp_ref, sws_ref, x_ref, wo_ref, mod_ref, g_ref, rw_ref, rb_ref,
                           x1_ref, *refs):
    i = pl.program_id(0)
    is_p = i < NPT // RT
    ona = jnp.where(is_p, nap_ref[...], nas_ref[...])
    osw = jnp.where(is_p, swp_ref[...], sws_ref[...])
    y = _dot(ona, wo_ref[:NA_W, :]) + _dot(osw, wo_ref[NA_W:, :])
    x1 = x_ref[...] + mod_ref[2, pl.ds(_mod_row(i * RT), 1), :] * y
    x1_ref[...] = x1
    for sub in range(RT):
        _route_tile(x1[sub * TM:(sub + 1) * TM, :], sub, mod_ref, g_ref, rw_ref, rb_ref, *refs)


def _router(x, mod_l, g, rw2, rb_col, attn=None):
    tm = RT * TM
    blk = lambda i: (i // (TPB // RT), 0, 0, 0)
    in_specs = [
        pl.BlockSpec((tm, D), lambda i: (i, 0)),
        pl.BlockSpec((6, MOD_ROWS, D), lambda i: (0, 0, 0)),
        pl.BlockSpec((1, D), lambda i: (0, 0)),
        pl.BlockSpec((2, D, 128), lambda i: (0, 0, 0)),
        pl.BlockSpec((128, 1), lambda i: (0, 0)),
    ]
    out_specs = [
        pl.BlockSpec((tm, D), lambda i: (i, 0)),
        pl.BlockSpec((None, TPB, 2, TM), blk),
        pl.BlockSpec((None, TPB, 2, TM), blk),
        pl.BlockSpec((None, N_EXP, 128), lambda i: (i // (TPB // RT), 0, 0)),
        pl.BlockSpec((None, N_EXP, 128), lambda i: (i // (TPB // RT), 0, 0)),
    ]
    out_shape = [
        jax.ShapeDtypeStruct((N_TOK, D), BF16),
        jax.ShapeDtypeStruct((NB, TPB, 2, TM), jnp.int32),
        jax.ShapeDtypeStruct((NB, TPB, 2, TM), F32),
        jax.ShapeDtypeStruct((NB, N_EXP, 128), F32),
        jax.ShapeDtypeStruct((NB, N_EXP, 128), F32),
    ]
    args = (x, mod_l, g, rw2, rb_col)
    body = _router_kernel
    if attn is not None:
        nap, nas, swp, sws, w_out = attn
        in_specs = (_dual_specs(tm, NA_W, N_P // tm) + _dual_specs(tm, SWQ_W, N_P // tm) + in_specs[:1]
                    + [pl.BlockSpec((D, D), lambda i: (0, 0))] + in_specs[1:])
        out_specs = [pl.BlockSpec((tm, D), lambda i: (i, 0))] + out_specs
        out_shape = [jax.ShapeDtypeStruct((N_TOK, D), F32)] + out_shape
        args = (nap, nas, swp, sws, x, w_out, mod_l, g, rw2, rb_col)
        body = _odd_out_router_kernel
    return pl.pallas_call(
        body,
        grid=(N_TOK // tm,),
        in_specs=in_specs,
        out_specs=out_specs,
        out_shape=out_shape,
        scratch_shapes=[pltpu.VMEM((TPB, 4, TM), F32), pltpu.VMEM((N_EXP, 128), F32)],
        compiler_params=_cparams("arbitrary"),
        name="router" if attn is None else "odd_out_router",
    )(*args)


def _row(p):
    return pl.ds(pl.multiple_of(p, ROW), ROW)


def _expert_segment(n, base, xo, wg_ref, wu_ref, wd_ref):
    def ffn_rows(row0, rows):
        x = jnp.concatenate([xo[pl.ds(row0 * ROW + c, rows, stride=ROW), :] for c in range(ROW)], axis=1)
        xb = x.astype(BF16)
        a = _dot(xb, wg_ref[...])
        u = _dot(xb, wu_ref[...])
        hid = (a / (1.0 + jnp.exp(-a))) * u
        out = _dot(hid.astype(BF16), wd_ref[...])
        valid = lax.broadcasted_iota(jnp.int32, (rows, 1), 0) < base + n - row0
        res = jnp.where(valid, out, x)
        for c in range(ROW):
            xo[pl.ds(row0 * ROW + c, rows, stride=ROW), :] = res[:, c * 128:(c + 1) * 128]

    def chunk(jc, carry):
        ffn_rows(base + jc * CH, CH)
        return carry

    nfull = (n + CH - CH_TAIL - 1) // CH
    lax.fori_loop(0, nfull, chunk, 0)

    @pl.when(n > nfull * CH)
    def _():
        ffn_rows(base + nfull * CH, CH_TAIL)


def _experts_kernel(cnt_ref, off_ref, dpos_ref, pos_ref, w_ref, h_ref, wg_ref, wu_ref, wd_ref, x_ref, mod_ref,
                    fg_ref, *rest, final):
    if final:
        op_ref, os_ref, xo, stg = rest
    else:
        o_ref, xo, stg = rest
    b = pl.program_id(0)
    s = pl.program_id(1)

    @pl.when(s == 0)
    def _():
        xo[pl.ds(2 * BT * ROW, CH * ROW), :] = jnp.zeros((CH * ROW, 128), F32)

    @pl.when(s < NDS)
    def _():
        for dt in range(DT):
            hs = h_ref[dt * TM:(dt + 1) * TM, :].astype(F32)
            for c in range(ROW):
                stg[pl.ds(c, TM, stride=ROW), :] = hs[:, c * 128:(c + 1) * 128]
            for t in range(TM):
                v = stg[t * ROW:(t + 1) * ROW, :]
                xo[_row(dpos_ref[dt, 0, t]), :] = v
                xo[_row(dpos_ref[dt, 1, t]), :] = v

    @pl.when(jnp.logical_and(s >= NDS, s < NDS + NES))
    def _():
        for k in range(XPS):
            e = (s - NDS) * XPS + k
            _expert_segment(cnt_ref[b, e], off_ref[b, e], xo, wg_ref.at[k], wu_ref.at[k], wd_ref.at[k])

    @pl.when(s >= NDS + NES - 1)
    def _():
        step = b * NCS + s - (NDS + NES - 1)
        for ct in range(CT):
            tile = step * CT + ct
            rows = slice(ct * TM, (ct + 1) * TM)
            g2 = mod_ref[5, pl.ds(_mod_row(tile), 1), :]
            for t in range(TM):
                a = xo[_row(pos_ref[ct, 0, t]), :]
                u = xo[_row(pos_ref[ct, 1, t]), :]
                stg[t * ROW:(t + 1) * ROW, :] = w_ref[ct, 0, t] * a + w_ref[ct, 1, t] * u
            y = jnp.concatenate([stg[pl.ds(c, TM, stride=ROW), :] for c in range(ROW)], axis=1)
            res = x_ref[rows, :] + g2 * y
            if final:
                ms = jnp.mean(res * res, axis=-1, keepdims=True)
                res = res * lax.rsqrt(ms + EPS) * fg_ref[...]

                @pl.when(tile < NPT)
                def _():
                    op_ref[rows, :] = res

                @pl.when(tile >= NPT)
                def _():
                    os_ref[rows, :] = res
            else:
                o_ref[rows, :] = res


def _experts(cnt, off, pos, wts, h, wg, wu, wd, x, mod_l, fg, *, layer, final):
    def cstep_of(b, s):
        return b * NCS + jnp.clip(s - (NDS + NES - 1), 0, NCS - 1)

    def expert_of(b, s, *_):
        return (layer, jnp.clip(s - NDS, 0, NES - 1), 0, 0)

    def dispatch_of(b, s, *_):
        return (b, jnp.minimum(s, NDS - 1), 0, 0)

    def combine_of(b, s, *_):
        return (b, jnp.clip(s - (NDS + NES - 1), 0, NCS - 1), 0, 0)

    smem_disp = pl.BlockSpec((None, DT, 2, TM), dispatch_of, memory_space=pltpu.SMEM)
    smem_blk = pl.BlockSpec((None, CT, 2, TM), combine_of, memory_space=pltpu.SMEM)
    tm = CT * TM
    if final:
        out_specs = [pl.BlockSpec((tm, D), lambda b, s, *_: (jnp.minimum(cstep_of(b, s), N_P // tm - 1), 0)),
                     pl.BlockSpec((tm, D), lambda b, s, *_: (jnp.maximum(cstep_of(b, s) - N_P // tm, 0), 0))]
        out_shape = [jax.ShapeDtypeStruct((N_P, D), F32), jax.ShapeDtypeStruct((N_S, D), F32)]
    else:
        out_specs = pl.BlockSpec((tm, D), lambda b, s, *_: (cstep_of(b, s), 0))
        out_shape = jax.ShapeDtypeStruct((N_TOK, D), F32)
    grid_spec = pltpu.PrefetchScalarGridSpec(
        num_scalar_prefetch=2,
        grid=(NB, NDS + NES + NCS - 1),
        in_specs=[
            smem_disp,
            smem_blk,
            smem_blk,
            pl.BlockSpec((DT * TM, D), lambda b, s, *_: (b * NDS + jnp.minimum(s, NDS - 1), 0)),
            pl.BlockSpec((None, XPS, D, D_EXP), expert_of),
            pl.BlockSpec((None, XPS, D, D_EXP), expert_of),
            pl.BlockSpec((None, XPS, D_EXP, D), expert_of),
            pl.BlockSpec((tm, D), lambda b, s, *_: (cstep_of(b, s), 0)),
            pl.BlockSpec((6, MOD_ROWS, D), lambda b, s, *_: (0, 0, 0)),
            pl.BlockSpec((1, D), lambda b, s, *_: (0, 0)),
        ],
        out_specs=out_specs,
        scratch_shapes=[pltpu.VMEM((XO_ROWS * ROW, 128), F32), pltpu.VMEM((TM * ROW, 128), F32)],
    )
    return pl.pallas_call(
        functools.partial(_experts_kernel, final=final),
        grid_spec=grid_spec,
        out_shape=out_shape,
        compiler_params=_cparams("arbitrary", "arbitrary"),
        name="experts_final" if final else "experts",
    )(cnt, off, pos, pos, wts, h, wg, wu, wd, x, mod_l, fg)


def _moe_sparse(x, routed, mod_l, wg, wu, wd, fg, *, layer, final):
    h, pos, wts, cnt, off = routed
    cnt = cnt[:, :, 0].astype(jnp.int32)
    off = off[:, :, 0].astype(jnp.int32)
    return _experts(cnt, off, pos, wts, h, wg, wu, wd, x, mod_l, fg, layer=layer, final=final)


def kernel(x_prompt, x_sample, cache_na_k, cache_na_v, cache_sw_k, cache_sw_v, c, c_ctx, mod_w, mod_b, norm_mix_g, norm_ffn_g, ev_w_in, ev_pool_w, ev_pool_scale, ev_conv_w, ev_w_out, od_w_in, od_rpb, od_sink, od_w_out, router_w, router_b, moe_w_gate, moe_w_up, moe_w_down, final_norm_g):
    xp = x_prompt.reshape(N_P, D)
    xs = x_sample.reshape(N_S, D)
    cvec = jnp.concatenate([c_ctx[None, :], c, jnp.zeros((MOD_ROWS - 1 - DEC_BATCH, D), F32)], axis=0)
    mod = _modulation(cvec, mod_w, mod_b)

    rw_pad = jnp.pad(router_w, ((0, 0), (0, 128 - N_EXP)))
    rw_hi = rw_pad.astype(BF16)
    rw2 = jnp.stack([rw_hi, (rw_pad - rw_hi.astype(F32)).astype(BF16)])
    rb_col = jnp.pad(router_b, (0, 128 - N_EXP)).reshape(128, 1)

    conv_w = jnp.pad(ev_conv_w[0], ((0, 8 - ev_conv_w.shape[1]), (0, 0)))
    x, wg, wu, wd = _even_layer(xp, xs, mod[0], norm_mix_g[0:1], ev_w_in[0].astype(BF16),
                                ev_pool_w[0].astype(BF16), ev_pool_scale[0:1], conv_w, ev_w_out[0].astype(BF16),
                                moe_w_gate, moe_w_up, moe_w_down)
    fg = final_norm_g.reshape(1, D)
    routed = _router(x, mod[0], norm_ffn_g[0:1], rw2, rb_col)
    x = _moe_sparse(x, routed, mod[0], wg, wu, wd, fg, layer=0, final=False)

    cos, sin = _rope_tables()
    w_in = od_w_in[0].astype(BF16)
    g1 = norm_mix_g[1:2]
    qna_p, kna_p, vna_p, qsw_p, ksw_p, vsw_p, nak, nav, swk, swv = _odd_in(x, mod[1], g1, w_in, cos, sin, prompt=True)
    qna_s, kna_s, vna_s, qsw_s, ksw_s, vsw_s = _odd_in(x, mod[1], g1, w_in, cos, sin, prompt=False)
    sink = od_sink[0]
    ona_p, osw_p = _ctx_attn(sink, qna_p, kna_p, vna_p, qsw_p, ksw_p, vsw_p)
    bias = _na_bias(od_rpb[0])
    ona_s = _na_attn(qna_s, kna_s, vna_s,
                     cache_na_k[:, 0].reshape(DEC_BATCH, PAST, NA_W).astype(BF16),
                     cache_na_v[:, 0].reshape(DEC_BATCH, PAST, NA_W).astype(BF16), bias)
    osw_s = _sw_attn(sink, qsw_s, ksw_s, vsw_s,
                     cache_sw_k[:, 0].reshape(DEC_BATCH, PAST, SWKV_W).astype(BF16),
                     cache_sw_v[:, 0].reshape(DEC_BATCH, PAST, SWKV_W).astype(BF16))
    x, *routed = _router(x, mod[1], norm_ffn_g[1:2], rw2, rb_col,
                         attn=(ona_p, ona_s, osw_p, osw_s, od_w_out[0].astype(BF16)))
    y_prompt, y_sample = _moe_sparse(x, routed, mod[1], wg, wu, wd, fg, layer=1, final=True)
    y_prompt = y_prompt.reshape(BATCH, SEQ, D)
    y_sample = y_sample.reshape(DEC_BATCH, DEC_SEQ, D)
    new_na_k = nak.reshape(BATCH, 1, SEQ, NA_H, DH)
    new_na_v = nav.reshape(BATCH, 1, SEQ, NA_H, DH)
    new_sw_k = swk.reshape(BATCH, 1, SEQ, SW_KV, DH)
    new_sw_v = swv.reshape(BATCH, 1, SEQ, SW_KV, DH)
    return (y_prompt, y_sample, new_na_k, new_na_v, new_sw_k, new_sw_v)
```

```python
import functools

import jax
import jax.numpy as jnp
import numpy as np
from jax import lax
from jax.experimental import pallas as pl
from jax.experimental.pallas import tpu as pltpu

D = 1024
BATCH = 16
SEQ = 256
DEC_BATCH = 4
DEC_SEQ = 4096
PAST = 512
GRID_W = 64
ROWS = DEC_SEQ // GRID_W
DH = 64
POOL_W = 512
POOL_WINDOWS = (2, 4, 8, 16)
POOL_GW = 128
CONV_W = 512
EVEN_IN = POOL_W + 3 * CONV_W
NA_H = 8
NA_ROWS = 8
NA_COLS = 16
SW_H = 8
SW_KV = 2
SW_G = SW_H // SW_KV
SW_WIN = 128
ABLK = 128
NA_W = NA_H * DH
SWQ_W = SW_H * DH
SWKV_W = SW_KV * DH
ODD_IN = 3 * NA_W + SWQ_W + 2 * SWKV_W
N_EXP = 16
N_GRP = 4
EPG = 4
D_EXP = 512
EPS = 1e-6
NEG = -1e30
ROPE_BASE = 10000.0
QK_SCALE = DH ** -0.5
assert QK_SCALE == 0.125

N_P = BATCH * SEQ
N_S = DEC_BATCH * DEC_SEQ
N_TOK = N_P + N_S
MOD_ROWS = 8

TM = 256
NPT = N_P // TM
TPS = DEC_SEQ // TM
NT = N_TOK // TM
HALO = 8
EV_HALVES = 2
EV_TM = EV_HALVES * TM
CAST_STEPS = 32

F32 = jnp.float32
BF16 = jnp.bfloat16
VMEM_LIMIT = 62 * 1024 * 1024


def _cparams(*sem):
    return pltpu.CompilerParams(dimension_semantics=sem, vmem_limit_bytes=VMEM_LIMIT)


def _mod_row(i):
    return jnp.where(i < NPT, 0, 1 + (i - NPT) // TPS)


def _rms_mod(x, g, shift, scale):
    ms = jnp.mean(x * x, axis=-1, keepdims=True)
    y = x * lax.rsqrt(ms + EPS) * g
    return y * (1.0 + scale) + shift


def _dot(a, b):
    return jnp.dot(a, b, preferred_element_type=F32)


def _dot_nt(a, b):
    return lax.dot_general(a, b, (((1,), (1,)), ((), ())), preferred_element_type=F32)


def _mod_kernel(cv_ref, w_ref, b_ref, o_ref):
    cv = cv_ref[...]
    a = cv / (1.0 + jnp.exp(-cv))
    o_ref[...] = jnp.dot(a, w_ref[...], preferred_element_type=F32,
                         precision=lax.Precision.HIGHEST) + b_ref[...]


def _modulation(cvec, mod_w, mod_b):
    depth = mod_w.shape[0]
    return pl.pallas_call(
        _mod_kernel,
        grid=(depth, 6),
        in_specs=[
            pl.BlockSpec((MOD_ROWS, D), lambda l, j: (0, 0)),
            pl.BlockSpec((None, D, D), lambda l, j: (l, 0, j)),
            pl.BlockSpec((None, None, 1, D), lambda l, j: (l, j, 0, 0)),
        ],
        out_specs=pl.BlockSpec((None, None, MOD_ROWS, D), lambda l, j: (l, j, 0, 0)),
        out_shape=jax.ShapeDtypeStruct((depth, 6, MOD_ROWS, D), F32),
        compiler_params=_cparams("arbitrary", "arbitrary"),
        name="modulation",
    )(cvec, mod_w, mod_b.reshape(depth, 6, 1, D))


def _dual_specs(tm, width, npt):
    return [
        pl.BlockSpec((tm, width), lambda i: (jnp.minimum(i, npt - 1), 0)),
        pl.BlockSpec((tm, width), lambda i: (jnp.maximum(i - npt, 0), 0)),
    ]


def _even_kernel(xp_ref, xs_ref, xprev_ref, xnext_ref, mod_ref, g_ref, wi_ref, pw_ref, ps_ref, cw_ref,
                 wo_ref, wgf_ref, wuf_ref, wdf_ref, o_ref, wgb_ref, wub_ref, wdb_ref, pext, uext):
    i = pl.program_id(0)

    @pl.when(i < CAST_STEPS)
    def _():
        wgb_ref[...] = wgf_ref[...].astype(BF16)
        wub_ref[...] = wuf_ref[...].astype(BF16)
        wdb_ref[...] = wdf_ref[...].astype(BF16)

    npt = NPT // EV_HALVES
    r = _mod_row(i * EV_HALVES)
    is_p = i < npt
    seq_len = jnp.where(is_p, SEQ, DEC_SEQ)

    x = jnp.where(is_p, xp_ref[...], xs_ref[...])
    xe = jnp.concatenate([xprev_ref[...], x, xnext_ref[...]], axis=0)
    h = _rms_mod(xe, g_ref[...], mod_ref[0, pl.ds(r, 1), :], mod_ref[1, pl.ds(r, 1), :])
    ze_all = _dot(h.astype(BF16), wi_ref[...])
    row = lax.broadcasted_iota(jnp.int32, (TM + 2 * HALO, 1), 0)
    for hf in range(EV_HALVES):
        t0 = jnp.where(is_p, 0, ((i - npt) % (DEC_SEQ // EV_TM)) * EV_TM + hf * TM)
        first = t0 == 0
        last = t0 + TM == seq_len
        ze = ze_all[hf * TM:hf * TM + TM + 2 * HALO, :]
        outside = jnp.logical_or(jnp.logical_and(first, row < HALO), jnp.logical_and(last, row >= HALO + TM))
        ze = jnp.where(outside, 0.0, ze)
        pext[...] = ze[:, :POOL_W]
        uext[...] = ze[:, POOL_W + 2 * CONV_W:] * ze[:, POOL_W:POOL_W + CONV_W]
        p = ze[HALO:HALO + TM, :POOL_W]
        gb = ze[HALO:HALO + TM, POOL_W + CONV_W:POOL_W + 2 * CONV_W]

        t = t0 + lax.broadcasted_iota(jnp.int32, (TM, 1), 0)
        mixed = []
        for g, w in enumerate(POOL_WINDOWS):
            cols = slice(g * POOL_GW, (g + 1) * POOL_GW)
            acc = jnp.zeros((TM, POOL_GW), F32)
            for k in range(-(w // 2), w - (w // 2)):
                acc = acc + pext[pl.ds(HALO + k, TM), cols]
            lo = jnp.maximum(t - w // 2, 0)
            hi = jnp.minimum(t + (w - 1 - w // 2), seq_len - 1)
            cnt = (hi - lo + 1).astype(F32)
            d = acc / cnt - p[:, cols]
            mixed.append((_dot(d.astype(BF16), pw_ref[g]) * ps_ref[:, cols]).astype(BF16))
        conv = (uext[pl.ds(HALO - 1, TM), :] * cw_ref[0:1, :] + uext[pl.ds(HALO, TM), :] * cw_ref[1:2, :]
                + uext[pl.ds(HALO + 1, TM), :] * cw_ref[2:3, :])
        mixed.append((gb * conv).astype(BF16))
        y = _dot(jnp.concatenate(mixed, axis=1), wo_ref[...])
        rows = slice(hf * TM, (hf + 1) * TM)
        o_ref[rows, :] = x[rows, :] + mod_ref[2, pl.ds(r, 1), :] * y


def _even_layer(xp, xs, mod_l, g, w_in, pool_w, pool_scale, conv_w, w_out, wg, wu, wd):
    hb = EV_TM // HALO
    nhb = N_S // HALO
    npt = N_P // EV_TM
    nt = N_TOK // EV_TM
    wg2, wu2, wd2 = wg.reshape(-1, D_EXP), wu.reshape(-1, D_EXP), wd.reshape(-1, D)
    rows_in, rows_out = wg2.shape[0] // CAST_STEPS, wd2.shape[0] // CAST_STEPS
    assert rows_in * CAST_STEPS == wg2.shape[0] and rows_out * CAST_STEPS == wd2.shape[0] and CAST_STEPS <= nt
    cast_blk = lambda i: (jnp.minimum(i, CAST_STEPS - 1), 0)
    cast_specs = [pl.BlockSpec((rows_in, D_EXP), cast_blk), pl.BlockSpec((rows_in, D_EXP), cast_blk),
                  pl.BlockSpec((rows_out, D), cast_blk)]
    outs = pl.pallas_call(
        _even_kernel,
        grid=(nt,),
        in_specs=_dual_specs(EV_TM, D, npt) + [
            pl.BlockSpec((HALO, D), lambda i: (jnp.maximum((i - npt) * hb - 1, 0), 0)),
            pl.BlockSpec((HALO, D), lambda i: (jnp.clip((i - npt + 1) * hb, 0, nhb - 1), 0)),
            pl.BlockSpec((6, MOD_ROWS, D), lambda i: (0, 0, 0)),
            pl.BlockSpec((1, D), lambda i: (0, 0)),
            pl.BlockSpec((D, EVEN_IN), lambda i: (0, 0)),
            pl.BlockSpec((4, POOL_GW, POOL_GW), lambda i: (0, 0, 0)),
            pl.BlockSpec((1, POOL_W), lambda i: (0, 0)),
            pl.BlockSpec((8, CONV_W), lambda i: (0, 0)),
            pl.BlockSpec((D, D), lambda i: (0, 0)),
        ] + cast_specs,
        out_specs=[pl.BlockSpec((EV_TM, D), lambda i: (i, 0))] + cast_specs,
        out_shape=[jax.ShapeDtypeStruct((N_TOK, D), F32), jax.ShapeDtypeStruct(wg2.shape, BF16),
                   jax.ShapeDtypeStruct(wu2.shape, BF16), jax.ShapeDtypeStruct(wd2.shape, BF16)],
        scratch_shapes=[pltpu.VMEM((TM + 2 * HALO, POOL_W), F32),
                        pltpu.VMEM((TM + 2 * HALO, CONV_W), F32)],
        compiler_params=_cparams("arbitrary"),
        name="even_layer",
    )(xp, xs, xs, xs, mod_l, g, w_in, pool_w, pool_scale, conv_w, w_out, wg2, wu2, wd2)
    return outs[0], outs[1].reshape(wg.shape), outs[2].reshape(wu.shape), outs[3].reshape(wd.shape)


def _rope(x, cos, sin_signed):
    n = x.shape[1] // 128
    cosf = jnp.concatenate([cos] * n, axis=1) if n > 1 else cos
    sinf = jnp.concatenate([sin_signed] * n, axis=1) if n > 1 else sin_signed
    w = x.shape[1]
    lane = lax.broadcasted_iota(jnp.int32, x.shape, 1)
    up = pltpu.roll(x, w - DH // 4, 1)
    dn = pltpu.roll(x, DH // 4, 1)
    rot = jnp.where((lane % (DH // 2)) < DH // 4, up, dn)
    return x * cosf + rot * sinf


def _odd_in_kernel(x_ref, mod_ref, g_ref, w_ref, cos_ref, sin_ref, *out_refs, tm, tile0, rope, kv_f32):
    r = _mod_row((pl.program_id(0) + tile0) * (tm // TM))
    h = _rms_mod(x_ref[...], g_ref[...], mod_ref[0, pl.ds(r, 1), :], mod_ref[1, pl.ds(r, 1), :])
    z = _dot(h.astype(BF16), w_ref[...])
    qna_ref, kna_ref, vna_ref, qsw_ref, ksw_ref, vsw_ref = out_refs[:6]
    c0 = 0
    qna = z[:, 0:NA_W]
    kna = z[:, NA_W:2 * NA_W]
    vna = z[:, 2 * NA_W:3 * NA_W]
    c0 = 3 * NA_W
    qsw = z[:, c0:c0 + SWQ_W]
    ksw = z[:, c0 + SWQ_W:c0 + SWQ_W + SWKV_W]
    vsw = z[:, c0 + SWQ_W + SWKV_W:]
    if kv_f32:
        for ref, val in zip(out_refs[6:10], (kna, vna, ksw, vsw)):
            heads = val.shape[1] // DH
            for hh in range(heads):
                ref[pl.ds(hh, SEQ, stride=heads), :] = val[:, hh * DH:(hh + 1) * DH]
    if rope:
        cos = cos_ref[...]
        sin = sin_ref[...]
        qsw = _rope(qsw, cos, sin)
        ksw = _rope(ksw, cos, sin)
    qna_ref[...] = (qna * QK_SCALE).astype(BF16)
    kna_ref[...] = kna.astype(BF16)
    vna_ref[...] = vna.astype(BF16)
    qsw_ref[...] = (qsw * QK_SCALE).astype(BF16)
    ksw_ref[...] = ksw.astype(BF16)
    vsw_ref[...] = vsw.astype(BF16)


def _odd_in(x, mod_l, g, w, cos, sin, *, prompt):
    tm = TM if prompt else 2 * TM
    tile0 = 0 if prompt else N_P // tm
    nt = (N_P if prompt else N_S) // tm
    n = nt * tm
    widths = [NA_W, NA_W, NA_W, SWQ_W, SWKV_W, SWKV_W]
    out_shape = [jax.ShapeDtypeStruct((n, w_), BF16) for w_ in widths]
    out_specs = [pl.BlockSpec((tm, w_), lambda i: (i, 0)) for w_ in widths]
    if prompt:
        assert tm == SEQ
        for heads in (NA_H, NA_H, SW_KV, SW_KV):
            out_shape.append(jax.ShapeDtypeStruct((BATCH * SEQ * heads, DH), F32))
            out_specs.append(pl.BlockSpec((SEQ * heads, DH), lambda i: (i, 0)))
    return pl.pallas_call(
        functools.partial(_odd_in_kernel, tm=tm, tile0=tile0, rope=not prompt, kv_f32=prompt),
        grid=(nt,),
        in_specs=[
            pl.BlockSpec((tm, D), lambda i: (i + tile0, 0)),
            pl.BlockSpec((6, MOD_ROWS, D), lambda i: (0, 0, 0)),
            pl.BlockSpec((1, D), lambda i: (0, 0)),
            pl.BlockSpec((D, ODD_IN), lambda i: (0, 0)),
            pl.BlockSpec((tm, 128), lambda i: (i % (DEC_SEQ // tm), 0)),
            pl.BlockSpec((tm, 128), lambda i: (i % (DEC_SEQ // tm), 0)),
        ],
        out_specs=out_specs,
        out_shape=out_shape,
        compiler_params=_cparams("parallel"),
        name="odd_in_prompt" if prompt else "odd_in_latent",
    )(x, mod_l, g, w, cos, sin)


def _rope_tables():
    t = np.arange(DEC_SEQ)
    quarter = DH // 4
    inv = 1.0 / (ROPE_BASE ** (np.arange(quarter, dtype=np.float64) / quarter))

    def cos_sin(pos):
        ang = pos.astype(np.float64)[:, None] * inv[None, :]
        ang = np.concatenate([ang, ang], axis=-1)
        return np.cos(ang), np.sin(ang)

    cr, sr = cos_sin(t // GRID_W)
    cc, sc = cos_sin(t % GRID_W)
    cos = np.concatenate([cr, cc], axis=-1)
    sin = np.concatenate([sr, sc], axis=-1)
    sign = np.where((np.arange(DH) % (DH // 2)) < DH // 4, -1.0, 1.0)
    sin = sin * sign[None, :]
    cos = np.concatenate([cos, cos], axis=-1).astype(np.float32)
    sin = np.concatenate([sin, sin], axis=-1).astype(np.float32)
    return jnp.asarray(cos), jnp.asarray(sin)


def _softmax_pv(segs, sink=None):
    m = None
    for s, _ in segs:
        sm = jnp.max(s, axis=-1, keepdims=True)
        m = sm if m is None else jnp.maximum(m, sm)
    if sink is not None:
        m = jnp.maximum(m, sink)
    den = None
    acc = None
    for s, v in segs:
        p = jnp.exp(s - m)
        ps = jnp.sum(p, axis=-1, keepdims=True)
        den = ps if den is None else den + ps
        pv = _dot(p.astype(BF16), v)
        acc = pv if acc is None else acc + pv
    if sink is not None:
        den = den + jnp.exp(sink - m)
    return acc / den


def _sink_col(sink_ref, g, rows_per_head):
    row = lax.broadcasted_iota(jnp.int32, (SW_G * rows_per_head, 1), 0)
    col = jnp.zeros((SW_G * rows_per_head, 1), F32)
    for r in range(SW_G):
        col = jnp.where(row // rows_per_head == r, sink_ref[g * SW_G + r], col)
    return col


def _ctx_attn_kernel(sink_ref, qna_ref, kna_ref, vna_ref, qsw_ref, ksw_ref, vsw_ref, ona_ref, osw_ref):
    lane = lax.broadcasted_iota(jnp.int32, (SEQ, 2 * DH), 1)
    for p in range(NA_H // 2):
        cols = slice(p * 2 * DH, (p + 1) * 2 * DH)
        q = qna_ref[:, cols]
        k = kna_ref[:, cols]
        v = vna_ref[:, cols]
        outs = []
        for half in range(2):
            mine = (lane < DH) if half == 0 else (lane >= DH)
            qm = jnp.where(mine, q, jnp.zeros_like(q))
            outs.append(_softmax_pv([(_dot_nt(qm, k), v)]))
        ona_ref[:, cols] = jnp.where(lane < DH, outs[0], outs[1]).astype(BF16)
    outs = []
    for g in range(SW_KV):
        kc = slice(g * DH, (g + 1) * DH)
        q = jnp.concatenate([qsw_ref[:, (g * SW_G + r) * DH:(g * SW_G + r + 1) * DH] for r in range(SW_G)], axis=0)
        s = _dot_nt(q, ksw_ref[:, kc])
        o = _softmax_pv([(s, vsw_ref[:, kc])], sink=_sink_col(sink_ref, g, SEQ))
        outs.extend(o[r * SEQ:(r + 1) * SEQ, :] for r in range(SW_G))
    osw_ref[...] = jnp.concatenate(outs, axis=1).astype(BF16)


def _ctx_attn(sink, qna, kna, vna, qsw, ksw, vsw):
    def spec(w):
        return pl.BlockSpec((SEQ, w), lambda b: (b, 0))

    return pl.pallas_call(
        _ctx_attn_kernel,
        grid=(BATCH,),
        in_specs=[pl.BlockSpec(memory_space=pltpu.SMEM), spec(NA_W), spec(NA_W), spec(NA_W),
                  spec(SWQ_W), spec(SWKV_W), spec(SWKV_W)],
        out_specs=[spec(NA_W), spec(SWQ_W)],
        out_shape=[jax.ShapeDtypeStruct((N_P, NA_W), BF16), jax.ShapeDtypeStruct((N_P, SWQ_W), BF16)],
        compiler_params=_cparams("parallel"),
        name="ctx_attn",
    )(sink, qna, kna, vna, qsw, ksw, vsw)


NA_QB = 4
NA_KR = 12
NA_NQB = ROWS // NA_QB


def _na_block_offset(case, i, j):
    if case == 0:
        valid, dr = j < NA_ROWS, j - i + NA_ROWS - 1
    elif case == 1:
        valid, dr = i <= j < i + NA_ROWS, j - i + NA_ROWS // 2 - 1
    else:
        valid, dr = j >= NA_KR - NA_ROWS, j - i + NA_ROWS - 1 - (NA_KR - NA_QB)
    return dr if valid else None


def _na_bias_kernel(rpb_ref, o_ref):
    h = pl.program_id(0)
    cq = lax.broadcasted_iota(jnp.int32, (GRID_W, GRID_W), 0)
    ck = lax.broadcasted_iota(jnp.int32, (GRID_W, GRID_W), 1)
    cstart = jnp.clip(cq - NA_COLS // 2, 0, GRID_W - NA_COLS)
    ok = (ck >= cstart) & (ck < cstart + NA_COLS)
    dc = jnp.clip(ck - cq + NA_COLS - 1, 0, 2 * NA_COLS - 2)
    ndc = 2 * NA_COLS - 1
    neg = jnp.full((GRID_W, GRID_W), NEG, F32)
    by_offset = []
    for dr in range(2 * NA_ROWS - 1):
        b = jnp.zeros((GRID_W, GRID_W), F32)
        for e in range(ndc):
            b = jnp.where(dc == e, rpb_ref[h, dr * ndc + e], b)
        by_offset.append(jnp.where(ok, b, NEG))
    for case in range(3):
        for i in range(NA_QB):
            blocks = []
            for j in range(NA_KR):
                dr = _na_block_offset(case, i, j)
                blocks.append(neg if dr is None else by_offset[dr])
            o_ref[case, i * GRID_W:(i + 1) * GRID_W, :] = jnp.concatenate(blocks, axis=1)


def _na_bias(rpb):
    nd = 2 * NA_ROWS - 1
    return pl.pallas_call(
        _na_bias_kernel,
        grid=(NA_H,),
        in_specs=[pl.BlockSpec(memory_space=pltpu.SMEM)],
        out_specs=pl.BlockSpec((3, None, NA_QB * GRID_W, NA_KR * GRID_W), lambda h: (0, h, 0, 0)),
        out_shape=jax.ShapeDtypeStruct((3, NA_H, NA_QB * GRID_W, NA_KR * GRID_W), F32),
        compiler_params=_cparams("parallel"),
        name="na_bias",
    )(rpb.reshape(NA_H, nd * (2 * NA_COLS - 1)))


def _na_kernel(q_ref, k_ref, v_ref, kc_ref, vc_ref, bias_ref, o_ref):
    r0 = pl.program_id(1) * NA_QB
    start = pl.multiple_of(jnp.clip(r0 - NA_ROWS // 2, 0, ROWS - NA_KR) * GRID_W, GRID_W)
    nq = NA_QB * GRID_W
    lane = lax.broadcasted_iota(jnp.int32, (nq, 2 * DH), 1)
    for p in range(NA_H // 2):
        cols = slice(p * 2 * DH, (p + 1) * 2 * DH)
        q = q_ref[:, cols]
        kl = k_ref[pl.ds(start, NA_KR * GRID_W), cols]
        vl = v_ref[pl.ds(start, NA_KR * GRID_W), cols]
        kc = kc_ref[:, cols]
        vc = vc_ref[:, cols]
        zero = jnp.zeros_like(q)
        qm = jnp.concatenate([jnp.where(lane < DH, q, zero), jnp.where(lane < DH, zero, q)], axis=0)
        s_loc = _dot_nt(qm, kl)
        s_ctx = _dot_nt(qm, kc)
        outs = []
        for half in range(2):
            rows = slice(half * nq, (half + 1) * nq)
            outs.append(_softmax_pv([(s_loc[rows] + bias_ref[2 * p + half], vl), (s_ctx[rows], vc)]))
        o_ref[:, cols] = jnp.where(lane < DH, outs[0], outs[1]).astype(BF16)


def _na_attn(q, k, v, kc, vc, bias):
    nq = NA_QB * GRID_W

    def bias_case(b, rb):
        return (jnp.where(rb == 0, 0, jnp.where(rb == NA_NQB - 1, 2, 1)), 0, 0, 0)

    return pl.pallas_call(
        _na_kernel,
        grid=(DEC_BATCH, NA_NQB),
        in_specs=[
            pl.BlockSpec((nq, NA_W), lambda b, rb: (b * NA_NQB + rb, 0)),
            pl.BlockSpec((DEC_SEQ, NA_W), lambda b, rb: (b, 0)),
            pl.BlockSpec((DEC_SEQ, NA_W), lambda b, rb: (b, 0)),
            pl.BlockSpec((None, PAST, NA_W), lambda b, rb: (b, 0, 0)),
            pl.BlockSpec((None, PAST, NA_W), lambda b, rb: (b, 0, 0)),
            pl.BlockSpec((None, NA_H, nq, NA_KR * GRID_W), bias_case),
        ],
        out_specs=pl.BlockSpec((nq, NA_W), lambda b, rb: (b * NA_NQB + rb, 0)),
        out_shape=jax.ShapeDtypeStruct((N_S, NA_W), BF16),
        compiler_params=_cparams("parallel", "arbitrary"),
        name="na_attn",
    )(q, k, v, kc, vc, bias)


def _sw_window_bias():
    q = np.arange(SW_G * ABLK)[:, None] % ABLK
    k = np.arange(3 * ABLK)[None, :]
    tables = [np.where(np.abs(q + lead - k) <= SW_WIN, 0.0, NEG) for lead in (0, ABLK, 2 * ABLK)]
    return jnp.asarray(np.stack(tables).astype(np.float32))


SW_QB = 4


def _sw_kernel(sink_ref, q_ref, k_ref, v_ref, kc_ref, vc_ref, wb_ref, o_ref):
    nk = 3 * ABLK
    nb = DEC_SEQ // ABLK
    for sub in range(SW_QB):
        j = pl.program_id(1) * SW_QB + sub
        rows = slice(sub * ABLK, (sub + 1) * ABLK)
        start = pl.multiple_of(jnp.clip((j - 1) * ABLK, 0, DEC_SEQ - nk), ABLK)
        wb = wb_ref[jnp.where(j == 0, 0, jnp.where(j == nb - 1, 2, 1))]
        outs = []
        for g in range(SW_KV):
            kcols = slice(g * DH, (g + 1) * DH)
            q = jnp.concatenate([q_ref[rows, (g * SW_G + r) * DH:(g * SW_G + r + 1) * DH] for r in range(SW_G)],
                                axis=0)
            kw = k_ref[pl.ds(start, nk), kcols]
            vw = v_ref[pl.ds(start, nk), kcols]
            s_w = _dot_nt(q, kw) + wb
            s_c = _dot_nt(q, kc_ref[:, kcols])
            o = _softmax_pv([(s_w, vw), (s_c, vc_ref[:, kcols])], sink=_sink_col(sink_ref, g, ABLK))
            outs.extend(o[r * ABLK:(r + 1) * ABLK, :] for r in range(SW_G))
        o_ref[rows, :] = jnp.concatenate(outs, axis=1).astype(BF16)


def _sw_attn(sink, q, k, v, kc, vc):
    nsteps = DEC_SEQ // (ABLK * SW_QB)
    return pl.pallas_call(
        _sw_kernel,
        grid=(DEC_BATCH, nsteps),
        in_specs=[
            pl.BlockSpec(memory_space=pltpu.SMEM),
            pl.BlockSpec((SW_QB * ABLK, SWQ_W), lambda b, j: (b * nsteps + j, 0)),
            pl.BlockSpec((DEC_SEQ, SWKV_W), lambda b, j: (b, 0)),
            pl.BlockSpec((DEC_SEQ, SWKV_W), lambda b, j: (b, 0)),
            pl.BlockSpec((None, PAST, SWKV_W), lambda b, j: (b, 0, 0)),
            pl.BlockSpec((None, PAST, SWKV_W), lambda b, j: (b, 0, 0)),
            pl.BlockSpec((3, SW_G * ABLK, 3 * ABLK), lambda b, j: (0, 0, 0)),
        ],
        out_specs=pl.BlockSpec((SW_QB * ABLK, SWQ_W), lambda b, j: (b * nsteps + j, 0)),
        out_shape=jax.ShapeDtypeStruct((N_S, SWQ_W), BF16),
        compiler_params=_cparams("parallel", "arbitrary"),
        name="sw_attn",
    )(sink, q, k, v, kc, vc, _sw_window_bias())


BT = 4096
NB = N_TOK // BT
TPB = BT // TM
CH = 256
CH_TAIL = 128
XPS = 2
NES = N_EXP // XPS
DT = 2
NDS = TPB // DT
CT = 2
NCS = TPB // CT
assert NPT % CT == 0
RT = 4
assert TPB % RT == 0 and NPT % RT == 0
XO_ROWS = 2 * BT + CH
ROW = 8
assert D == ROW * 128


def _route_dense(sel):
    e = lax.broadcasted_iota(jnp.int32, sel.shape, 0)
    el = e % EPG
    g = e // EPG

    def group_rot(a, k):
        return jnp.where(el >= k, pltpu.roll(a, k, 0), pltpu.roll(a, (k - EPG) % N_EXP, 0))

    rank = jnp.zeros_like(sel)
    for k in range(1, EPG):
        partner = group_rot(sel, k)
        ahead = jnp.logical_or(partner > sel, jnp.logical_and(el >= k, partner == sel))
        rank = rank + ahead.astype(F32)
    contrib = jnp.where(rank < 2.0, sel, 0.0)
    score = contrib
    for k in range(1, EPG):
        score = score + group_rot(contrib, k)
    best = None
    for k in range(1, N_GRP):
        other = pltpu.roll(score, k * EPG, 0)
        c = jnp.logical_or(score > other, jnp.logical_and(g < k, score == other))
        best = c if best is None else jnp.logical_and(best, c)
    return jnp.logical_and(best, rank == 0.0), jnp.logical_and(best, rank == 1.0)


def _pick(mask, vals):
    return jnp.sum(jnp.where(mask, vals, 0.0), axis=0, keepdims=True)


def _route_tile(xv, sub, mod_ref, g_ref, rw_ref, rb_ref, h_ref, pos_ref, w_ref, cnt_ref, off_ref, meta, carry):
    i = pl.program_id(0) * RT + sub
    j = i % TPB
    r = _mod_row(i)
    h = _rms_mod(xv, g_ref[...], mod_ref[3, pl.ds(r, 1), :], mod_ref[4, pl.ds(r, 1), :])
    h_ref[sub * TM:(sub + 1) * TM, :] = h.astype(BF16)
    h_hi = h.astype(BF16)
    h_lo = (h - h_hi.astype(F32)).astype(BF16)
    logits = _dot(h_lo, rw_ref[0]) + _dot(h_hi, rw_ref[1]) + _dot(h_hi, rw_ref[0])
    lt = logits.T[:N_EXP, :]
    m = jnp.max(lt, axis=0, keepdims=True)
    ex = jnp.exp(lt - m)
    pr = ex / jnp.sum(ex, axis=0, keepdims=True)
    se = pr + rb_ref[:N_EXP, :]
    top1, top2 = _route_dense(se)

    @pl.when(j == 0)
    def _():
        carry[...] = jnp.zeros_like(carry)

    member = jnp.logical_or(top1, top2).astype(F32)
    s_idx = lax.broadcasted_iota(jnp.int32, (TM, TM), 0)
    t_idx = lax.broadcasted_iota(jnp.int32, (TM, TM), 1)
    before = jnp.where(s_idx < t_idx, 1.0, 0.0).astype(BF16)
    seen = _dot(member.astype(BF16), before) + carry[:, 0:1]
    ids = lax.broadcasted_iota(jnp.int32, (N_EXP, TM), 0).astype(F32)
    p1 = _pick(top1, pr)
    p2 = _pick(top2, pr)
    den = p1 + p2
    w_ref[j, 0:1, :] = p1 / den
    w_ref[j, 1:2, :] = p2 / den
    meta[j, 0:1, :] = _pick(top1, ids)
    meta[j, 1:2, :] = _pick(top2, ids)
    meta[j, 2:3, :] = _pick(top1, seen)
    meta[j, 3:4, :] = _pick(top2, seen)
    carry[...] = carry[...] + jnp.sum(member, axis=1, keepdims=True)

    @pl.when(j == TPB - 1)
    def _():
        cnt = carry[...]
        offs = [jnp.zeros((1, 128), F32)]
        for e in range(1, N_EXP):
            offs.append(offs[-1] + cnt[e - 1:e, :])
        cnt_ref[...] = cnt
        off_ref[...] = jnp.concatenate(offs, axis=0)
        for jj in range(TPB):
            for k in range(2):
                eid = meta[jj, k:k + 1, :]
                pos = meta[jj, 2 + k:3 + k, :]
                for e in range(1, N_EXP):
                    pos = pos + jnp.where(eid == float(e), offs[e][:, 0:1], 0.0)
                pos_ref[jj, k:k + 1, :] = (pos * float(ROW)).astype(jnp.int32)


def _router_kernel(x_ref, *refs):
    for sub in range(RT):
        _route_tile(x_ref[sub * TM:(sub + 1) * TM, :], sub, *refs)


def _odd_out_router_kernel(nap_ref, nas_ref, swp_ref, sws_ref, x_ref, wo_ref, mod_ref, g_ref, rw_ref, rb_ref,
                           x1_ref, *refs):
    i = pl.program_id(0)
    is_p = i < NPT // RT
    ona = jnp.where(is_p, nap_ref[...], nas_ref[...])
    osw = jnp.where(is_p, swp_ref[...], sws_ref[...])
    y = _dot(ona, wo_ref[:NA_W, :]) + _dot(osw, wo_ref[NA_W:, :])
    x1 = x_ref[...] + mod_ref[2, pl.ds(_mod_row(i * RT), 1), :] * y
    x1_ref[...] = x1
    for sub in range(RT):
        _route_tile(x1[sub * TM:(sub + 1) * TM, :], sub, mod_ref, g_ref, rw_ref, rb_ref, *refs)


def _router(x, mod_l, g, rw2, rb_col, attn=None):
    tm = RT * TM
    blk = lambda i: (i // (TPB // RT), 0, 0, 0)
    in_specs = [
        pl.BlockSpec((tm, D), lambda i: (i, 0)),
        pl.BlockSpec((6, MOD_ROWS, D), lambda i: (0, 0, 0)),
        pl.BlockSpec((1, D), lambda i: (0, 0)),
        pl.BlockSpec((2, D, 128), lambda i: (0, 0, 0)),
        pl.BlockSpec((128, 1), lambda i: (0, 0)),
    ]
    out_specs = [
        pl.BlockSpec((tm, D), lambda i: (i, 0)),
        pl.BlockSpec((None, TPB, 2, TM), blk),
        pl.BlockSpec((None, TPB, 2, TM), blk),
        pl.BlockSpec((None, N_EXP, 128), lambda i: (i // (TPB // RT), 0, 0)),
        pl.BlockSpec((None, N_EXP, 128), lambda i: (i // (TPB // RT), 0, 0)),
    ]
    out_shape = [
        jax.ShapeDtypeStruct((N_TOK, D), BF16),
        jax.ShapeDtypeStruct((NB, TPB, 2, TM), jnp.int32),
        jax.ShapeDtypeStruct((NB, TPB, 2, TM), F32),
        jax.ShapeDtypeStruct((NB, N_EXP, 128), F32),
        jax.ShapeDtypeStruct((NB, N_EXP, 128), F32),
    ]
    args = (x, mod_l, g, rw2, rb_col)
    body = _router_kernel
    if attn is not None:
        nap, nas, swp, sws, w_out = attn
        in_specs = (_dual_specs(tm, NA_W, N_P // tm) + _dual_specs(tm, SWQ_W, N_P // tm) + in_specs[:1]
                    + [pl.BlockSpec((D, D), lambda i: (0, 0))] + in_specs[1:])
        out_specs = [pl.BlockSpec((tm, D), lambda i: (i, 0))] + out_specs
        out_shape = [jax.ShapeDtypeStruct((N_TOK, D), F32)] + out_shape
        args = (nap, nas, swp, sws, x, w_out, mod_l, g, rw2, rb_col)
        body = _odd_out_router_kernel
    return pl.pallas_call(
        body,
        grid=(N_TOK // tm,),
        in_specs=in_specs,
        out_specs=out_specs,
        out_shape=out_shape,
        scratch_shapes=[pltpu.VMEM((TPB, 4, TM), F32), pltpu.VMEM((N_EXP, 128), F32)],
        compiler_params=_cparams("arbitrary"),
        name="router" if attn is None else "odd_out_router",
    )(*args)


def _row(p):
    return pl.ds(pl.multiple_of(p, ROW), ROW)


def _expert_segment(n, base, xo, wg_ref, wu_ref, wd_ref):
    def ffn_rows(row0, rows):
        x = jnp.concatenate([xo[pl.ds(row0 * ROW + c, rows, stride=ROW), :] for c in range(ROW)], axis=1)
        xb = x.astype(BF16)
        a = _dot(xb, wg_ref[...])
        u = _dot(xb, wu_ref[...])
        hid = (a / (1.0 + jnp.exp(-a))) * u
        out = _dot(hid.astype(BF16), wd_ref[...])
        valid = lax.broadcasted_iota(jnp.int32, (rows, 1), 0) < base + n - row0
        res = jnp.where(valid, out, x)
        for c in range(ROW):
            xo[pl.ds(row0 * ROW + c, rows, stride=ROW), :] = res[:, c * 128:(c + 1) * 128]

    def chunk(jc, carry):
        ffn_rows(base + jc * CH, CH)
        return carry

    nfull = (n + CH - CH_TAIL - 1) // CH
    lax.fori_loop(0, nfull, chunk, 0)

    @pl.when(n > nfull * CH)
    def _():
        ffn_rows(base + nfull * CH, CH_TAIL)


def _experts_kernel(cnt_ref, off_ref, dpos_ref, pos_ref, w_ref, h_ref, wg_ref, wu_ref, wd_ref, x_ref, mod_ref,
                    fg_ref, *rest, final):
    if final:
        op_ref, os_ref, xo, stg = rest
    else:
        o_ref, xo, stg = rest
    b = pl.program_id(0)
    s = pl.program_id(1)

    @pl.when(s == 0)
    def _():
        xo[pl.ds(2 * BT * ROW, CH * ROW), :] = jnp.zeros((CH * ROW, 128), F32)

    @pl.when(s < NDS)
    def _():
        for dt in range(DT):
            hs = h_ref[dt * TM:(dt + 1) * TM, :].astype(F32)
            for c in range(ROW):
                stg[pl.ds(c, TM, stride=ROW), :] = hs[:, c * 128:(c + 1) * 128]
            for t in range(TM):
                v = stg[t * ROW:(t + 1) * ROW, :]
                xo[_row(dpos_ref[dt, 0, t]), :] = v
                xo[_row(dpos_ref[dt, 1, t]), :] = v

    @pl.when(jnp.logical_and(s >= NDS, s < NDS + NES))
    def _():
        for k in range(XPS):
            e = (s - NDS) * XPS + k
            _expert_segment(cnt_ref[b, e], off_ref[b, e], xo, wg_ref.at[k], wu_ref.at[k], wd_ref.at[k])

    @pl.when(s >= NDS + NES - 1)
    def _():
        step = b * NCS + s - (NDS + NES - 1)
        for ct in range(CT):
            tile = step * CT + ct
            rows = slice(ct * TM, (ct + 1) * TM)
            g2 = mod_ref[5, pl.ds(_mod_row(tile), 1), :]
            for t in range(TM):
                a = xo[_row(pos_ref[ct, 0, t]), :]
                u = xo[_row(pos_ref[ct, 1, t]), :]
                stg[t * ROW:(t + 1) * ROW, :] = w_ref[ct, 0, t] * a + w_ref[ct, 1, t] * u
            y = jnp.concatenate([stg[pl.ds(c, TM, stride=ROW), :] for c in range(ROW)], axis=1)
            res = x_ref[rows, :] + g2 * y
            if final:
                ms = jnp.mean(res * res, axis=-1, keepdims=True)
                res = res * lax.rsqrt(ms + EPS) * fg_ref[...]

                @pl.when(tile < NPT)
                def _():
                    op_ref[rows, :] = res

                @pl.when(tile >= NPT)
                def _():
                    os_ref[rows, :] = res
            else:
                o_ref[rows, :] = res


def _experts(cnt, off, pos, wts, h, wg, wu, wd, x, mod_l, fg, *, layer, final):
    def cstep_of(b, s):
        return b * NCS + jnp.clip(s - (NDS + NES - 1), 0, NCS - 1)

    def expert_of(b, s, *_):
        return (layer, jnp.clip(s - NDS, 0, NES - 1), 0, 0)

    def dispatch_of(b, s, *_):
        return (b, jnp.minimum(s, NDS - 1), 0, 0)

    def combine_of(b, s, *_):
        return (b, jnp.clip(s - (NDS + NES - 1), 0, NCS - 1), 0, 0)

    smem_disp = pl.BlockSpec((None, DT, 2, TM), dispatch_of, memory_space=pltpu.SMEM)
    smem_blk = pl.BlockSpec((None, CT, 2, TM), combine_of, memory_space=pltpu.SMEM)
    tm = CT * TM
    if final:
        out_specs = [pl.BlockSpec((tm, D), lambda b, s, *_: (jnp.minimum(cstep_of(b, s), N_P // tm - 1), 0)),
                     pl.BlockSpec((tm, D), lambda b, s, *_: (jnp.maximum(cstep_of(b, s) - N_P // tm, 0), 0))]
        out_shape = [jax.ShapeDtypeStruct((N_P, D), F32), jax.ShapeDtypeStruct((N_S, D), F32)]
    else:
        out_specs = pl.BlockSpec((tm, D), lambda b, s, *_: (cstep_of(b, s), 0))
        out_shape = jax.ShapeDtypeStruct((N_TOK, D), F32)
    grid_spec = pltpu.PrefetchScalarGridSpec(
        num_scalar_prefetch=2,
        grid=(NB, NDS + NES + NCS - 1),
        in_specs=[
            smem_disp,
            smem_blk,
            smem_blk,
            pl.BlockSpec((DT * TM, D), lambda b, s, *_: (b * NDS + jnp.minimum(s, NDS - 1), 0)),
            pl.BlockSpec((None, XPS, D, D_EXP), expert_of),
            pl.BlockSpec((None, XPS, D, D_EXP), expert_of),
            pl.BlockSpec((None, XPS, D_EXP, D), expert_of),
            pl.BlockSpec((tm, D), lambda b, s, *_: (cstep_of(b, s), 0)),
            pl.BlockSpec((6, MOD_ROWS, D), lambda b, s, *_: (0, 0, 0)),
            pl.BlockSpec((1, D), lambda b, s, *_: (0, 0)),
        ],
        out_specs=out_specs,
        scratch_shapes=[pltpu.VMEM((XO_ROWS * ROW, 128), F32), pltpu.VMEM((TM * ROW, 128), F32)],
    )
    return pl.pallas_call(
        functools.partial(_experts_kernel, final=final),
        grid_spec=grid_spec,
        out_shape=out_shape,
        compiler_params=_cparams("arbitrary", "arbitrary"),
        name="experts_final" if final else "experts",
    )(cnt, off, pos, pos, wts, h, wg, wu, wd, x, mod_l, fg)


def _moe_sparse(x, routed, mod_l, wg, wu, wd, fg, *, layer, final):
    h, pos, wts, cnt, off = routed
    cnt = cnt[:, :, 0].astype(jnp.int32)
    off = off[:, :, 0].astype(jnp.int32)
    return _experts(cnt, off, pos, wts, h, wg, wu, wd, x, mod_l, fg, layer=layer, final=final)


def kernel(x_prompt, x_sample, cache_na_k, cache_na_v, cache_sw_k, cache_sw_v, c, c_ctx, mod_w, mod_b, norm_mix_g, norm_ffn_g, ev_w_in, ev_pool_w, ev_pool_scale, ev_conv_w, ev_w_out, od_w_in, od_rpb, od_sink, od_w_out, router_w, router_b, moe_w_gate, moe_w_up, moe_w_down, final_norm_g):
    xp = x_prompt.reshape(N_P, D)
    xs = x_sample.reshape(N_S, D)
    cvec = jnp.concatenate([c_ctx[None, :], c, jnp.zeros((MOD_ROWS - 1 - DEC_BATCH, D), F32)], axis=0)
    mod = _modulation(cvec, mod_w, mod_b)

    rw_pad = jnp.pad(router_w, ((0, 0), (0, 128 - N_EXP)))
    rw_hi = rw_pad.astype(BF16)
    rw2 = jnp.stack([rw_hi, (rw_pad - rw_hi.astype(F32)).astype(BF16)])
    rb_col = jnp.pad(router_b, (0, 128 - N_EXP)).reshape(128, 1)

    conv_w = jnp.pad(ev_conv_w[0], ((0, 8 - ev_conv_w.shape[1]), (0, 0)))
    x, wg, wu, wd = _even_layer(xp, xs, mod[0], norm_mix_g[0:1], ev_w_in[0].astype(BF16),
                                ev_pool_w[0].astype(BF16), ev_pool_scale[0:1], conv_w, ev_w_out[0].astype(BF16),
                                moe_w_gate, moe_w_up, moe_w_down)
    fg = final_norm_g.reshape(1, D)
    routed = _router(x, mod[0], norm_ffn_g[0:1], rw2, rb_col)
    x = _moe_sparse(x, routed, mod[0], wg, wu, wd, fg, layer=0, final=False)

    cos, sin = _rope_tables()
    w_in = od_w_in[0].astype(BF16)
    g1 = norm_mix_g[1:2]
    qna_p, kna_p, vna_p, qsw_p, ksw_p, vsw_p, nak, nav, swk, swv = _odd_in(x, mod[1], g1, w_in, cos, sin, prompt=True)
    qna_s, kna_s, vna_s, qsw_s, ksw_s, vsw_s = _odd_in(x, mod[1], g1, w_in, cos, sin, prompt=False)
    sink = od_sink[0]
    ona_p, osw_p = _ctx_attn(sink, qna_p, kna_p, vna_p, qsw_p, ksw_p, vsw_p)
    bias = _na_bias(od_rpb[0])
    ona_s = _na_attn(qna_s, kna_s, vna_s,
                     cache_na_k[:, 0].reshape(DEC_BATCH, PAST, NA_W).astype(BF16),
                     cache_na_v[:, 0].reshape(DEC_BATCH, PAST, NA_W).astype(BF16), bias)
    osw_s = _sw_attn(sink, qsw_s, ksw_s, vsw_s,
                     cache_sw_k[:, 0].reshape(DEC_BATCH, PAST, SWKV_W).astype(BF16),
                     cache_sw_v[:, 0].reshape(DEC_BATCH, PAST, SWKV_W).astype(BF16))
    x, *routed = _router(x, mod[1], norm_ffn_g[1:2], rw2, rb_col,
                         attn=(ona_p, ona_s, osw_p, osw_s, od_w_out[0].astype(BF16)))
    y_prompt, y_sample = _moe_sparse(x, routed, mod[1], wg, wu, wd, fg, layer=1, final=True)
    y_prompt = y_prompt.reshape(BATCH, SEQ, D)
    y_sample = y_sample.reshape(DEC_BATCH, DEC_SEQ, D)
    new_na_k = nak.reshape(BATCH, 1, SEQ, NA_H, DH)
    new_na_v = nav.reshape(BATCH, 1, SEQ, NA_H, DH)
    new_sw_k = swk.reshape(BATCH, 1, SEQ, SW_KV, DH)
    new_sw_v = swv.reshape(BATCH, 1, SEQ, SW_KV, DH)
    return (y_prompt, y_sample, new_na_k, new_na_v, new_sw_k, new_sw_v)
```
